```python
import jax, jax.numpy as jnp
from jax import lax
import numpy as np

D_MODEL = 1024
BATCH = 8
SEQ = 2048
DEPTH = 2
DEC_BATCH = 128
DEC_SEQ = 8
PAST_LEN = 16384
PAGE_SIZE = 128

N_MIXERS = 2
N_A_LAYERS = (DEPTH + N_MIXERS - 1) // N_MIXERS
N_B_LAYERS = DEPTH // N_MIXERS

N_MEM = 256
X_WIDTH = D_MODEL // 4
X_HEADS = 4
X_HEAD_DIM = X_WIDTH // X_HEADS

GDN_WIDTH = 3 * D_MODEL // 4
GDN_DK = 128
GDN_DV = 128
GDN_HEADS = GDN_WIDTH // GDN_DV
QKV_WIDTH = GDN_HEADS * (2 * GDN_DK + GDN_DV)
GDN_CONV = 4
GDN_CHUNK = 64

SC_WIDTH = 3 * D_MODEL // 4
SC_CONV = 3

BRANCH_WIDTH = GDN_WIDTH + X_WIDTH
COLS_A = QKV_WIDTH + 2 * GDN_HEADS + BRANCH_WIDTH + X_WIDTH
COLS_B = 3 * SC_WIDTH + BRANCH_WIDTH + X_WIDTH
EPS = 1e-6

kernel_name = 'hybrid_gdn_shortconv_memxattn_step'


def _split(t, sizes):
    return jnp.split(t, [int(s) for s in np.cumsum(sizes)[:-1]], axis=-1)


def rmsnorm(x, g):
    xf = x.astype(jnp.float32)
    r = lax.rsqrt(jnp.mean(xf * xf, axis=-1, keepdims=True) + EPS)
    return (xf * r * g.astype(jnp.float32)).astype(x.dtype)


def l2norm(x):
    return x * lax.rsqrt(jnp.sum(x * x, axis=-1, keepdims=True) + EPS)


def causal_dwconv(x, buf, w):
    width = w.shape[0]
    L = x.shape[1]
    xp = jnp.concatenate([buf.astype(x.dtype), x], axis=1)
    y = sum(xp[:, j:j + L] * w[j].astype(x.dtype) for j in range(width))
    return y, xp[:, xp.shape[1] - (width - 1):]


def gated_delta_rule(q, k, v, g, beta, S0):
    Bn, L, H, DK = q.shape
    C = min(GDN_CHUNK, L)
    n = -(-L // C)
    pad = n * C - L

    def blocks(t):
        t = jnp.pad(t, [(0, 0), (0, pad)] + [(0, 0)] * (t.ndim - 2))
        t = t.reshape((Bn, n, C) + t.shape[2:])
        return jnp.moveaxis(t, 3, 2)

    q, k, v, g, beta = (blocks(t) for t in (q, k, v, g, beta))
    G = jnp.cumsum(g, axis=-1)
    tril = jnp.tril(jnp.ones((C, C), bool))
    decay = jnp.exp(jnp.where(tril, G[..., :, None] - G[..., None, :], -jnp.inf))
    kb = k * beta[..., None]
    A = jnp.tril(jnp.einsum('bnhid,bnhjd->bnhij', kb, k) * decay, -1)
    rhs = jnp.concatenate([kb * jnp.exp(G)[..., None], v * beta[..., None]], axis=-1)
    wu = lax.linalg.triangular_solve(A + jnp.eye(C, dtype=A.dtype), rhs, left_side=True,
                                     lower=True, unit_diagonal=True)
    w, u = wu[..., :DK], wu[..., DK:]
    Aqk = jnp.einsum('bnhid,bnhjd->bnhij', q, k) * decay
    qd = q * jnp.exp(G)[..., None]
    Gl = G[..., -1]
    kd = k * jnp.exp(Gl[..., None] - G)[..., None]
    xs = tuple(jnp.moveaxis(t, 1, 0) for t in (w, u, qd, kd, Aqk, Gl))

    def step(S, inp):
        w_c, u_c, qd_c, kd_c, aqk_c, gl_c = inp
        v_new = u_c - jnp.einsum('bhcd,bhde->bhce', w_c, S)
        o_c = (jnp.einsum('bhcd,bhde->bhce', qd_c, S)
               + jnp.einsum('bhij,bhje->bhie', aqk_c, v_new))
        S = S * jnp.exp(gl_c)[..., None, None] + jnp.einsum('bhcd,bhce->bhde', kd_c, v_new)
        return S, o_c

    S, o = lax.scan(step, S0, xs)
    o = jnp.transpose(o, (1, 0, 3, 2, 4)).reshape(Bn, n * C, H, -1)[:, :L]
    return o, S


def mem_attend(xq, mk, mv):
    s = jnp.einsum('blhd,bmhd->bhlm', xq, mk.astype(xq.dtype)).astype(jnp.float32)
    p = jax.nn.softmax(s * (X_HEAD_DIM ** -0.5), axis=-1).astype(xq.dtype)
    return jnp.einsum('bhlm,bmhd->blhd', p, mv.astype(xq.dtype))


def gdn_branch(h, w_in, conv_w, a_log, dt_bias, o_norm_g, S0, conv0):
    Bn, L, _ = h.shape
    proj = h @ w_in.astype(h.dtype)
    qkv, b, a, gate, xq = _split(proj, [QKV_WIDTH, GDN_HEADS, GDN_HEADS, BRANCH_WIDTH, X_WIDTH])
    qkv, conv_new = causal_dwconv(qkv, conv0, conv_w)
    qkv = jax.nn.silu(qkv).astype(jnp.float32)
    q, k, v = _split(qkv, [GDN_HEADS * GDN_DK, GDN_HEADS * GDN_DK, GDN_HEADS * GDN_DV])
    q = l2norm(q.reshape(Bn, L, GDN_HEADS, GDN_DK)) * (GDN_DK ** -0.5)
    k = l2norm(k.reshape(Bn, L, GDN_HEADS, GDN_DK))
    v = v.reshape(Bn, L, GDN_HEADS, GDN_DV)
    beta = jax.nn.sigmoid(b.astype(jnp.float32))
    g = -jnp.exp(a_log.astype(jnp.float32)) * jax.nn.softplus(
        a.astype(jnp.float32) + dt_bias.astype(jnp.float32))
    o, S = gated_delta_rule(q, k, v, g, beta, S0.astype(jnp.float32))
    o = rmsnorm(o, o_norm_g).reshape(Bn, L, GDN_WIDTH).astype(h.dtype)
    return o, gate, xq, S, conv_new


def sconv_branch(h, w_in, conv_w, conv0):
    proj = h @ w_in.astype(h.dtype)
    bg, cg, xs, gate, xq = _split(proj, [SC_WIDTH, SC_WIDTH, SC_WIDTH, BRANCH_WIDTH, X_WIDTH])
    y, conv_new = causal_dwconv(cg * xs, conv0, conv_w)
    return bg * y, gate, xq, conv_new


def trunk(x, mem_k, mem_v, gdn_S, gdn_conv, sc_conv, norm_g, w_in_a, conv_w_a, a_log,
          dt_bias, o_norm_g, w_in_b, conv_w_b, w_out, final_norm_g):
    Bn, L, _ = x.shape
    S_new, gconv_new, sconv_new = [], [], []
    for i in range(DEPTH):
        h = rmsnorm(x, norm_g[i])
        j = i // N_MIXERS
        if i % N_MIXERS == 0:
            tok, gate, xq, S, cbuf = gdn_branch(h, w_in_a[j], conv_w_a[j], a_log[j], dt_bias[j],
                                               o_norm_g[j], gdn_S[j], gdn_conv[j])
            S_new.append(S)
            gconv_new.append(cbuf)
        else:
            tok, gate, xq, cbuf = sconv_branch(h, w_in_b[j], conv_w_b[j], sc_conv[j])
            sconv_new.append(cbuf)
        xo = mem_attend(xq.reshape(Bn, L, X_HEADS, X_HEAD_DIM), mem_k[i], mem_v[i])
        br = jnp.concatenate([tok, xo.reshape(Bn, L, X_WIDTH)], axis=-1) * jax.nn.silu(gate)
        x = x + br @ w_out[i].astype(x.dtype)
    y = rmsnorm(x, final_norm_g)
    return y, jnp.stack(S_new), jnp.stack(gconv_new), jnp.stack(sconv_new)


def setup_inputs(seed: int = 0) -> dict:
    key = jax.random.key(seed)
    ks = jax.random.split(key, 24)

    def nrm(k, shape, scale):
        return jax.random.normal(k, shape, jnp.float32) * scale

    dt = jnp.exp(jax.random.uniform(ks[11], (N_A_LAYERS, GDN_HEADS), jnp.float32,
                                    np.log(1e-3), np.log(1e-1)))
    return {
        'x_prompt': nrm(ks[0], (BATCH, SEQ, D_MODEL), 1.0),
        'x_sample': nrm(ks[1], (DEC_BATCH, DEC_SEQ, D_MODEL), 1.0),
        'mem_prompt': nrm(ks[2], (BATCH, N_MEM, D_MODEL), 1.0),
        'state_gdn': nrm(ks[3], (N_A_LAYERS, DEC_BATCH, GDN_HEADS, GDN_DK, GDN_DV), 0.05),
        'state_gdn_conv': nrm(ks[4], (N_A_LAYERS, DEC_BATCH, GDN_CONV - 1, QKV_WIDTH), 1.0),
        'state_sconv': nrm(ks[5], (N_B_LAYERS, DEC_BATCH, SC_CONV - 1, SC_WIDTH), 1.0),
        'cache_mem_k': nrm(ks[6], (DEPTH, DEC_BATCH, N_MEM, X_HEADS, X_HEAD_DIM), 1.0),
        'cache_mem_v': nrm(ks[7], (DEPTH, DEC_BATCH, N_MEM, X_HEADS, X_HEAD_DIM), 1.0),
        'norm_g': 1.0 + nrm(ks[8], (DEPTH, D_MODEL), 0.02),
        'w_in_a': nrm(ks[9], (N_A_LAYERS, D_MODEL, COLS_A), D_MODEL ** -0.5),
        'conv_w_a': nrm(ks[10], (N_A_LAYERS, GDN_CONV, QKV_WIDTH), GDN_CONV ** -0.5),
        'a_log': jnp.log(jax.random.uniform(ks[12], (N_A_LAYERS, GDN_HEADS), jnp.float32, 1.0, 16.0)),
        'dt_bias': dt + jnp.log(-jnp.expm1(-dt)),
        'o_norm_g': 1.0 + nrm(ks[13], (N_A_LAYERS, GDN_DV), 0.02),
        'w_in_b': nrm(ks[14], (N_B_LAYERS, D_MODEL, COLS_B), D_MODEL ** -0.5),
        'conv_w_b': nrm(ks[15], (N_B_LAYERS, SC_CONV, SC_WIDTH), SC_CONV ** -0.5),
        'mem_norm_g': 1.0 + nrm(ks[16], (D_MODEL,), 0.02),
        'w_mem_kv': nrm(ks[17], (DEPTH, D_MODEL, 2 * X_WIDTH), D_MODEL ** -0.5),
        'w_out': nrm(ks[18], (DEPTH, BRANCH_WIDTH, D_MODEL), BRANCH_WIDTH ** -0.5),
        'final_norm_g': 1.0 + nrm(ks[19], (D_MODEL,), 0.02),
    }


def reference(x_prompt, x_sample, mem_prompt, state_gdn, state_gdn_conv, state_sconv,
              cache_mem_k, cache_mem_v, norm_g, w_in_a, conv_w_a, a_log, dt_bias, o_norm_g,
              w_in_b, conv_w_b, mem_norm_g, w_mem_kv, w_out, final_norm_g):
    Bp = x_prompt.shape[0]
    n_mem = mem_prompt.shape[1]
    mem_n = rmsnorm(mem_prompt, mem_norm_g)
    mkv = jnp.einsum('bmd,lde->lbme', mem_n, w_mem_kv.astype(mem_n.dtype))
    mem_k_p = mkv[..., :X_WIDTH].reshape(DEPTH, Bp, n_mem, X_HEADS, X_HEAD_DIM)
    mem_v_p = mkv[..., X_WIDTH:].reshape(DEPTH, Bp, n_mem, X_HEADS, X_HEAD_DIM)
    S0_p = jnp.zeros((N_A_LAYERS, Bp, GDN_HEADS, GDN_DK, GDN_DV), jnp.float32)
    gc0_p = jnp.zeros((N_A_LAYERS, Bp, GDN_CONV - 1, QKV_WIDTH), x_prompt.dtype)
    sc0_p = jnp.zeros((N_B_LAYERS, Bp, SC_CONV - 1, SC_WIDTH), x_prompt.dtype)
    y_prompt, S_p, gc_p, sc_p = trunk(x_prompt, mem_k_p, mem_v_p, S0_p, gc0_p, sc0_p,
                                      norm_g, w_in_a, conv_w_a, a_log, dt_bias, o_norm_g,
                                      w_in_b, conv_w_b, w_out, final_norm_g)
    y_sample, S_s, gc_s, sc_s = trunk(x_sample, cache_mem_k, cache_mem_v, state_gdn,
                                      state_gdn_conv, state_sconv,
                                      norm_g, w_in_a, conv_w_a, a_log, dt_bias, o_norm_g,
                                      w_in_b, conv_w_b, w_out, final_norm_g)
    return (y_prompt, y_sample, S_p, gc_p, sc_p, mem_k_p, mem_v_p, S_s, gc_s, sc_s)
```

```python
import functools

import jax
import jax.numpy as jnp
from jax import lax
from jax.experimental import pallas as pl
from jax.experimental.pallas import tpu as pltpu

F32 = jnp.float32
BF16 = jnp.bfloat16

D_MODEL = 1024
N_MEM = 256
X_WIDTH = 256
X_HEADS = 4
X_HEAD_DIM = 64
GDN_HEADS = 6
GDN_DK = 128
GDN_DV = 128
GDN_WIDTH = GDN_HEADS * GDN_DV
QK_WIDTH = GDN_HEADS * GDN_DK
QKV_WIDTH = 2 * QK_WIDTH + GDN_WIDTH
GDN_CHUNK = 64
SC_WIDTH = 768
BRANCH_WIDTH = 1024
EPS = 1e-6

LANES = 128
BA_PAD = LANES
VMEM_LIMIT = 56 * 1024 * 1024

HIGHEST = lax.Precision.HIGHEST


def _cparams(*sem):
    return pltpu.CompilerParams(dimension_semantics=sem, vmem_limit_bytes=VMEM_LIMIT)


def _rms_rows(x, g):
    r = lax.rsqrt(jnp.mean(x * x, axis=-1, keepdims=True) + EPS)
    return x * r * g


def _silu(x):
    return x * (1.0 / (1.0 + jnp.exp(-x)))


def _dot(a, b):
    return jnp.dot(a.astype(BF16), b.astype(BF16), preferred_element_type=F32)


def _dot_nt(a, b):
    return lax.dot_general(a.astype(BF16), b.astype(BF16), (((1,), (1,)), ((), ())),
                           preferred_element_type=F32)


def _dot_tn(a, b):
    return lax.dot_general(a.astype(BF16), b.astype(BF16), (((0,), (0,)), ((), ())),
                           preferred_element_type=F32)


def _dot_f32(a, b):
    return jnp.dot(a, b, preferred_element_type=F32, precision=HIGHEST)


def _memkv_kernel(m_ref, g_ref, w_ref, k_ref, v_ref):
    h = _rms_rows(m_ref[...], g_ref[...])
    kv = _dot(h, w_ref[...])
    k_ref[...] = kv[:, :X_WIDTH]
    v_ref[...] = kv[:, X_WIDTH:]


def _memkv(mem2d, g, w_bf16, tm=512):
    rows = mem2d.shape[0]
    depth = w_bf16.shape[0]
    out = jax.ShapeDtypeStruct((depth, rows, X_WIDTH), F32)
    return pl.pallas_call(
        _memkv_kernel,
        grid=(depth, rows // tm),
        in_specs=[
            pl.BlockSpec((tm, D_MODEL), lambda l, i: (i, 0)),
            pl.BlockSpec((1, D_MODEL), lambda l, i: (0, 0)),
            pl.BlockSpec((None, D_MODEL, 2 * X_WIDTH), lambda l, i: (l, 0, 0)),
        ],
        out_specs=[
            pl.BlockSpec((None, tm, X_WIDTH), lambda l, i: (l, i, 0)),
            pl.BlockSpec((None, tm, X_WIDTH), lambda l, i: (l, i, 0)),
        ],
        out_shape=[out, out],
        compiler_params=_cparams("arbitrary", "arbitrary"),
        name="memkv",
    )(mem2d, g, w_bf16)


def _causal_conv(scr, hist_ref, hist_out, cw_ref, val, bt, lt):
    width = cw_ref.shape[0]
    h0 = 8 - (width - 1)

    @pl.when(pl.program_id(1) == 0)
    def _():
        scr[:, h0:8, :] = hist_ref[...]

    scr[:, 8:8 + lt, :] = val
    y = scr[:, h0:h0 + lt, :] * cw_ref[0:1, :]
    for j in range(1, width):
        y = y + scr[:, h0 + j:h0 + j + lt, :] * cw_ref[j:j + 1, :]
    new_hist = scr[:, lt + h0:lt + 8, :]
    hist_out[...] = new_hist
    scr[:, h0:8, :] = new_hist
    return y


def _inproj_gdn_kernel(x_ref, g_ref, w_ref, hist_ref, cw_ref,
                       qkv_out, gate_out, xq_out, ba_out, hist_out, scr, *, bt, lt):
    tm = bt * lt
    h = _rms_rows(x_ref[...].reshape(tm, D_MODEL), g_ref[...]).astype(BF16)
    c0 = QKV_WIDTH
    c1 = c0 + BRANCH_WIDTH
    c2 = c1 + X_WIDTH
    qkv = jnp.dot(h, w_ref[:, :c0], preferred_element_type=F32)
    y = _causal_conv(scr, hist_ref, hist_out, cw_ref, qkv.reshape(bt, lt, c0), bt, lt)
    qkv_out[...] = _silu(y)
    gate_out[...] = jnp.dot(h, w_ref[:, c0:c1], preferred_element_type=F32).reshape(bt, lt, -1)
    xq_out[...] = jnp.dot(h, w_ref[:, c1:c2], preferred_element_type=F32).reshape(bt, lt, -1)
    ba_out[...] = jnp.dot(h, w_ref[:, c2:], preferred_element_type=F32).reshape(bt, lt, -1)


def _inproj_sconv_kernel(x_ref, g_ref, w_ref, hist_ref, cw_ref,
                         tok_out, gate_out, xq_out, hist_out, scr, *, bt, lt):
    tm = bt * lt
    h = _rms_rows(x_ref[...].reshape(tm, D_MODEL), g_ref[...]).astype(BF16)
    c0 = SC_WIDTH
    c3 = 3 * SC_WIDTH
    c4 = c3 + BRANCH_WIDTH
    bg = jnp.dot(h, w_ref[:, :c0], preferred_element_type=F32)
    cg = jnp.dot(h, w_ref[:, c0:2 * c0], preferred_element_type=F32)
    xs = jnp.dot(h, w_ref[:, 2 * c0:c3], preferred_element_type=F32)
    y = _causal_conv(scr, hist_ref, hist_out, cw_ref, (cg * xs).reshape(bt, lt, c0), bt, lt)
    tok_out[...] = bg.reshape(bt, lt, c0) * y
    gate_out[...] = jnp.dot(h, w_ref[:, c3:c4], preferred_element_type=F32).reshape(bt, lt, -1)
    xq_out[...] = jnp.dot(h, w_ref[:, c4:], preferred_element_type=F32).reshape(bt, lt, -1)


def _inproj(x, g, w_bf16, hist, cw, *, mixer, bt, lt):
    bn, seq, _ = x.shape
    ncols = w_bf16.shape[1]
    width, cc = cw.shape
    grid = (bn // bt, seq // lt)

    def act(n):
        return (pl.BlockSpec((bt, lt, n), lambda b, l: (b, l, 0)),
                jax.ShapeDtypeStruct((bn, seq, n), F32))

    hist_spec = pl.BlockSpec((bt, width - 1, cc), lambda b, l: (b, 0, 0))
    in_specs = [
        pl.BlockSpec((bt, lt, D_MODEL), lambda b, l: (b, l, 0)),
        pl.BlockSpec((1, D_MODEL), lambda b, l: (0, 0)),
        pl.BlockSpec((D_MODEL, ncols), lambda b, l: (0, 0)),
        hist_spec,
        pl.BlockSpec((width, cc), lambda b, l: (0, 0)),
    ]
    if mixer == "gdn":
        body = functools.partial(_inproj_gdn_kernel, bt=bt, lt=lt)
        outs = [act(QKV_WIDTH), act(BRANCH_WIDTH), act(X_WIDTH), act(BA_PAD)]
    else:
        body = functools.partial(_inproj_sconv_kernel, bt=bt, lt=lt)
        outs = [act(SC_WIDTH), act(BRANCH_WIDTH), act(X_WIDTH)]
    outs.append((hist_spec, jax.ShapeDtypeStruct(hist.shape, F32)))
    return pl.pallas_call(
        body,
        grid=grid,
        in_specs=in_specs,
        out_specs=[o[0] for o in outs],
        out_shape=[o[1] for o in outs],
        scratch_shapes=[pltpu.VMEM((bt, lt + 8, cc), F32)],
        compiler_params=_cparams("arbitrary", "arbitrary"),
        name="inproj_" + mixer,
    )(x, g, w_bf16, hist, cw)


def _unit_lower_inverse(a, eye, chunk):
    t = eye - a
    p = a
    span = 2
    while span < chunk:
        p = _dot_f32(p, p)
        t = t + _dot_f32(t, p)
        span *= 2
    return t


def _gdn_kernel(qkv_ref, ba_ref, s0_ref, alog_ref, dtb_ref, og_ref, o_ref, s_ref,
                *, chunk, nchunks):
    @pl.when(pl.program_id(1) == 0)
    def _():
        s_ref[...] = s0_ref[...]

    row = lax.broadcasted_iota(jnp.int32, (chunk, chunk), 0)
    col = lax.broadcasted_iota(jnp.int32, (chunk, chunk), 1)
    tril = row >= col
    eye = (row == col).astype(F32)
    ones_tril = tril.astype(F32)
    neg_a = -jnp.exp(alog_ref[...])
    dtb = dtb_ref[...]
    og = og_ref[...]

    def chunk_step(c, carry):
        r0 = pl.multiple_of(c * chunk, chunk)
        ba = ba_ref[pl.ds(r0, chunk), :]
        beta_all = 1.0 / (1.0 + jnp.exp(-ba))
        z = ba + dtb
        softplus = jnp.maximum(z, 0.0) + jnp.log1p(jnp.exp(-jnp.abs(z)))
        g_all = neg_a * softplus
        gcum = _dot_f32(ones_tril, g_all)
        gcum_t = gcum.T
        for h in range(GDN_HEADS):
            q = qkv_ref[pl.ds(r0, chunk), h * GDN_DK:(h + 1) * GDN_DK]
            k = qkv_ref[pl.ds(r0, chunk), QK_WIDTH + h * GDN_DK:QK_WIDTH + (h + 1) * GDN_DK]
            v = qkv_ref[pl.ds(r0, chunk), 2 * QK_WIDTH + h * GDN_DV:2 * QK_WIDTH + (h + 1) * GDN_DV]
            q = q * lax.rsqrt(jnp.sum(q * q, axis=-1, keepdims=True) + EPS) * (GDN_DK ** -0.5)
            k = k * lax.rsqrt(jnp.sum(k * k, axis=-1, keepdims=True) + EPS)
            gc = gcum[:, GDN_HEADS + h:GDN_HEADS + h + 1]
            gr = gcum_t[GDN_HEADS + h:GDN_HEADS + h + 1, :]
            gl = gcum[chunk - 1:chunk, GDN_HEADS + h:GDN_HEADS + h + 1]
            beta = beta_all[:, h:h + 1]
            decay = jnp.exp(jnp.where(tril, gc - gr, -jnp.inf))
            eg = jnp.exp(gc)
            kb = k * beta
            a = jnp.where(row > col, _dot_nt(kb, k) * decay, 0.0)
            t = _unit_lower_inverse(a, eye, chunk)
            wu = _dot_f32(t, jnp.concatenate([kb * eg, v * beta], axis=-1))
            w = wu[:, :GDN_DK]
            u = wu[:, GDN_DK:]
            aqk = _dot_nt(q, k) * decay
            qd = q * eg
            kd = k * jnp.exp(gl - gc)
            s = s_ref[h]
            v_new = u - _dot(w, s)
            o = _dot(qd, s) + _dot(aqk, v_new)
            s_ref[h] = s * jnp.exp(gl) + _dot_tn(kd, v_new)
            o = o * lax.rsqrt(jnp.mean(o * o, axis=-1, keepdims=True) + EPS) * og
            o_ref[pl.ds(r0, chunk), h * GDN_DV:(h + 1) * GDN_DV] = o
        return carry

    lax.fori_loop(0, nchunks, chunk_step, 0)


def _gdn(qkv, ba, s0, alog_pad, dtb_pad, og, *, chunk, lt):
    bn, seq, _ = qkv.shape
    s_spec = pl.BlockSpec((None, GDN_HEADS, GDN_DK, GDN_DV), lambda b, l: (b, 0, 0, 0))
    vec = pl.BlockSpec((1, LANES), lambda b, l: (0, 0))
    return pl.pallas_call(
        functools.partial(_gdn_kernel, chunk=chunk, nchunks=lt // chunk),
        grid=(bn, seq // lt),
        in_specs=[
            pl.BlockSpec((None, lt, QKV_WIDTH), lambda b, l: (b, l, 0)),
            pl.BlockSpec((None, lt, BA_PAD), lambda b, l: (b, l, 0)),
            s_spec, vec, vec, vec,
        ],
        out_specs=[
            pl.BlockSpec((None, lt, GDN_WIDTH), lambda b, l: (b, l, 0)),
            s_spec,
        ],
        out_shape=[
            jax.ShapeDtypeStruct((bn, seq, GDN_WIDTH), F32),
            jax.ShapeDtypeStruct(s0.shape, F32),
        ],
        compiler_params=_cparams("arbitrary", "arbitrary"),
        name="gdn",
    )(qkv, ba, s0, alog_pad, dtb_pad, og)


def _attn_out_kernel(tok_ref, xq_ref, gate_ref, x_ref, mk_ref, mv_ref, w_ref, fg_ref,
                     y_ref, xo_scr, *, bt, lt, final):
    lane_head = lax.broadcasted_iota(jnp.int32, (1, X_WIDTH), 1) // X_HEAD_DIM
    scale = X_HEAD_DIM ** -0.5

    def one_seq(b, carry):
        q = xq_ref[b]
        qx = jnp.concatenate([jnp.where(lane_head == h, q, 0.0) for h in range(X_HEADS)], axis=0)
        s = _dot_nt(qx, mk_ref[b]) * scale
        e = jnp.exp(s - jnp.max(s, axis=-1, keepdims=True))
        p = e / jnp.sum(e, axis=-1, keepdims=True)
        o4 = _dot(p, mv_ref[b])
        xo = jnp.where(lane_head == 0, o4[0:lt], 0.0)
        for h in range(1, X_HEADS):
            xo = xo + jnp.where(lane_head == h, o4[h * lt:(h + 1) * lt], 0.0)
        xo_scr[b] = xo
        return carry

    lax.fori_loop(0, bt, one_seq, 0)

    tm = bt * lt
    sg = _silu(gate_ref[...].reshape(tm, BRANCH_WIDTH))
    br_tok = tok_ref[...].reshape(tm, GDN_WIDTH) * sg[:, :GDN_WIDTH]
    br_x = xo_scr[...].reshape(tm, X_WIDTH) * sg[:, GDN_WIDTH:]
    y = (x_ref[...].reshape(tm, D_MODEL)
         + _dot(br_tok, w_ref[:GDN_WIDTH, :]) + _dot(br_x, w_ref[GDN_WIDTH:, :]))
    if final:
        y = _rms_rows(y, fg_ref[...])
    y_ref[...] = y.reshape(bt, lt, D_MODEL)


def _attn_out(tok, xq, gate, x, mk, mv, w_bf16, fg, *, layer, bt, lt, final):
    bn, seq, _ = x.shape

    def act(n):
        return pl.BlockSpec((bt, lt, n), lambda b, l: (b, l, 0))

    mem = pl.BlockSpec((None, bt, N_MEM, X_WIDTH), lambda b, l: (layer, b, 0, 0))
    return pl.pallas_call(
        functools.partial(_attn_out_kernel, bt=bt, lt=lt, final=final),
        grid=(bn // bt, seq // lt),
        in_specs=[
            act(GDN_WIDTH), act(X_WIDTH), act(BRANCH_WIDTH), act(D_MODEL), mem, mem,
            pl.BlockSpec((BRANCH_WIDTH, D_MODEL), lambda b, l: (0, 0)),
            pl.BlockSpec((1, D_MODEL), lambda b, l: (0, 0)),
        ],
        out_specs=act(D_MODEL),
        out_shape=jax.ShapeDtypeStruct(x.shape, F32),
        scratch_shapes=[pltpu.VMEM((bt, lt, X_WIDTH), F32)],
        compiler_params=_cparams("arbitrary", "arbitrary"),
        name="attn_out",
    )(tok, xq, gate, x, mk, mv, w_bf16, fg)


def _trunk(x, mem_k, mem_v, gdn_s, gdn_conv, sc_conv, p, *, bt, lt, chunk, gdn_lt):
    qkv, gate, xq, ba, gconv_new = _inproj(
        x, p["norm_g"][0:1], p["w_in_a"], gdn_conv, p["conv_w_a"], mixer="gdn", bt=bt, lt=lt)
    tok, s_new = _gdn(qkv, ba, gdn_s, p["alog_pad"], p["dtb_pad"], p["o_norm_g"],
                      chunk=chunk, lt=gdn_lt)
    x = _attn_out(tok, xq, gate, x, mem_k, mem_v, p["w_out"][0], p["final_norm_g"],
                  layer=0, bt=bt, lt=lt, final=False)
    tok, gate, xq, sconv_new = _inproj(
        x, p["norm_g"][1:2], p["w_in_b"], sc_conv, p["conv_w_b"], mixer="sconv", bt=bt, lt=lt)
    y = _attn_out(tok, xq, gate, x, mem_k, mem_v, p["w_out"][1], p["final_norm_g"],
                  layer=1, bt=bt, lt=lt, final=True)
    return y, s_new[None], gconv_new[None], sconv_new[None]


def kernel(x_prompt, x_sample, mem_prompt, state_gdn, state_gdn_conv, state_sconv, cache_mem_k, cache_mem_v, norm_g, w_in_a, conv_w_a, a_log, dt_bias, o_norm_g, w_in_b, conv_w_b, mem_norm_g, w_mem_kv, w_out, final_norm_g):
    bp, seq, _ = x_prompt.shape
    n_mem = mem_prompt.shape[1]

    wa = w_in_a[0]
    c_b = QKV_WIDTH
    c_g = QKV_WIDTH + 2 * GDN_HEADS
    wa = jnp.concatenate(
        [wa[:, :c_b], wa[:, c_g:], wa[:, c_b:c_g],
         jnp.zeros((D_MODEL, BA_PAD - 2 * GDN_HEADS), wa.dtype)], axis=1).astype(BF16)
    pad_lo = jnp.zeros((GDN_HEADS,), F32)
    pad_hi = jnp.zeros((LANES - 2 * GDN_HEADS,), F32)
    params = {
        "norm_g": norm_g,
        "w_in_a": wa,
        "conv_w_a": conv_w_a[0],
        "alog_pad": jnp.concatenate([pad_lo, a_log[0], pad_hi])[None],
        "dtb_pad": jnp.concatenate([pad_lo, dt_bias[0], pad_hi])[None],
        "o_norm_g": o_norm_g,
        "w_in_b": w_in_b[0].astype(BF16),
        "conv_w_b": conv_w_b[0],
        "w_out": w_out.astype(BF16),
        "final_norm_g": final_norm_g[None],
    }

    mem_k2, mem_v2 = _memkv(mem_prompt.reshape(bp * n_mem, D_MODEL), mem_norm_g[None],
                            w_mem_kv.astype(BF16))
    mem_k_p = mem_k2.reshape(-1, bp, n_mem, X_HEADS, X_HEAD_DIM)
    mem_v_p = mem_v2.reshape(-1, bp, n_mem, X_HEADS, X_HEAD_DIM)

    s0_p = jnp.zeros((bp,) + state_gdn.shape[2:], F32)
    gc0_p = jnp.zeros((bp,) + state_gdn_conv.shape[2:], F32)
    sc0_p = jnp.zeros((bp,) + state_sconv.shape[2:], F32)
    y_p, s_p, gc_p, sc_p = _trunk(x_prompt, mem_k2.reshape(-1, bp, n_mem, X_WIDTH),
                                  mem_v2.reshape(-1, bp, n_mem, X_WIDTH), s0_p, gc0_p, sc0_p,
                                  params, bt=1, lt=256, chunk=GDN_CHUNK, gdn_lt=256)
    bs, dec_seq, _ = x_sample.shape
    y_s, s_s, gc_s, sc_s = _trunk(x_sample, cache_mem_k.reshape(-1, bs, n_mem, X_WIDTH),
                                  cache_mem_v.reshape(-1, bs, n_mem, X_WIDTH), state_gdn[0],
                                  state_gdn_conv[0], state_sconv[0], params,
                                  bt=32, lt=dec_seq, chunk=dec_seq, gdn_lt=dec_seq)
    return (y_p, y_s, s_p, gc_p, sc_p, mem_k_p, mem_v_p, s_s, gc_s, sc_s)
```

```python
import functools

import jax
import jax.numpy as jnp
from jax import lax
from jax.experimental import pallas as pl
from jax.experimental.pallas import tpu as pltpu

F32 = jnp.float32
BF16 = jnp.bfloat16

D_MODEL = 1024
N_MEM = 256
X_WIDTH = 256
X_HEADS = 4
X_HEAD_DIM = 64
GDN_HEADS = 6
GDN_DK = 128
GDN_DV = 128
GDN_WIDTH = GDN_HEADS * GDN_DV
QK_WIDTH = GDN_HEADS * GDN_DK
QKV_WIDTH = 2 * QK_WIDTH + GDN_WIDTH
GDN_CHUNK = 64
SC_WIDTH = 768
BRANCH_WIDTH = 1024
EPS = 1e-6

LANES = 128
BA_PAD = LANES
VMEM_LIMIT = 56 * 1024 * 1024

HIGHEST = lax.Precision.HIGHEST


def _cparams(*sem):
    return pltpu.CompilerParams(dimension_semantics=sem, vmem_limit_bytes=VMEM_LIMIT)


def _rms_rows(x, g):
    r = lax.rsqrt(jnp.mean(x * x, axis=-1, keepdims=True) + EPS)
    return x * r * g


def _silu(x):
    return x * (1.0 / (1.0 + jnp.exp(-x)))


def _dot(a, b):
    return jnp.dot(a.astype(BF16), b.astype(BF16), preferred_element_type=F32)


def _dot_nt(a, b):
    return lax.dot_general(a.astype(BF16), b.astype(BF16), (((1,), (1,)), ((), ())),
                           preferred_element_type=F32)


def _dot_tn(a, b):
    return lax.dot_general(a.astype(BF16), b.astype(BF16), (((0,), (0,)), ((), ())),
                           preferred_element_type=F32)


def _dot_f32(a, b):
    return jnp.dot(a, b, preferred_element_type=F32, precision=HIGHEST)


def _memkv_kernel(m_ref, g_ref, w_ref, k_ref, v_ref):
    h = _rms_rows(m_ref[...], g_ref[...])
    kv = _dot(h, w_ref[...])
    k_ref[...] = kv[:, :X_WIDTH]
    v_ref[...] = kv[:, X_WIDTH:]


def _memkv(mem2d, g, w_bf16, tm=512):
    rows = mem2d.shape[0]
    depth = w_bf16.shape[0]
    out = jax.ShapeDtypeStruct((depth, rows, X_WIDTH), F32)
    return pl.pallas_call(
        _memkv_kernel,
        grid=(depth, rows // tm),
        in_specs=[
            pl.BlockSpec((tm, D_MODEL), lambda l, i: (i, 0)),
            pl.BlockSpec((1, D_MODEL), lambda l, i: (0, 0)),
            pl.BlockSpec((None, D_MODEL, 2 * X_WIDTH), lambda l, i: (l, 0, 0)),
        ],
        out_specs=[
            pl.BlockSpec((None, tm, X_WIDTH), lambda l, i: (l, i, 0)),
            pl.BlockSpec((None, tm, X_WIDTH), lambda l, i: (l, i, 0)),
        ],
        out_shape=[out, out],
        compiler_params=_cparams("arbitrary", "arbitrary"),
        name="memkv",
    )(mem2d, g, w_bf16)


def _causal_conv(scr, hist_ref, hist_out, cw_ref, val, bt, lt):
    width = cw_ref.shape[0]
    h0 = 8 - (width - 1)

    @pl.when(pl.program_id(1) == 0)
    def _():
        scr[:, h0:8, :] = hist_ref[...]

    scr[:, 8:8 + lt, :] = val
    y = scr[:, h0:h0 + lt, :] * cw_ref[0:1, :]
    for j in range(1, width):
        y = y + scr[:, h0 + j:h0 + j + lt, :] * cw_ref[j:j + 1, :]
    new_hist = scr[:, lt + h0:lt + 8, :]
    hist_out[...] = new_hist
    scr[:, h0:8, :] = new_hist
    return y


def _inproj_gdn_kernel(x_ref, g_ref, w_ref, hist_ref, cw_ref,
                       qkv_out, gate_out, xq_out, ba_out, hist_out, scr, *, bt, lt):
    tm = bt * lt
    h = _rms_rows(x_ref[...].reshape(tm, D_MODEL), g_ref[...]).astype(BF16)
    c0 = QKV_WIDTH
    c1 = c0 + BRANCH_WIDTH
    c2 = c1 + X_WIDTH
    qkv = jnp.dot(h, w_ref[:, :c0], preferred_element_type=F32)
    y = _causal_conv(scr, hist_ref, hist_out, cw_ref, qkv.reshape(bt, lt, c0), bt, lt)
    qkv_out[...] = _silu(y)
    gate_out[...] = jnp.dot(h, w_ref[:, c0:c1], preferred_element_type=F32).reshape(bt, lt, -1)
    xq_out[...] = jnp.dot(h, w_ref[:, c1:c2], preferred_element_type=F32).reshape(bt, lt, -1)
    ba_out[...] = jnp.dot(h, w_ref[:, c2:], preferred_element_type=F32).reshape(bt, lt, -1)


def _inproj_sconv_kernel(x_ref, g_ref, w_ref, hist_ref, cw_ref,
                         tok_out, gate_out, xq_out, hist_out, scr, *, bt, lt):
    tm = bt * lt
    h = _rms_rows(x_ref[...].reshape(tm, D_MODEL), g_ref[...]).astype(BF16)
    c0 = SC_WIDTH
    c3 = 3 * SC_WIDTH
    c4 = c3 + BRANCH_WIDTH
    bg = jnp.dot(h, w_ref[:, :c0], preferred_element_type=F32)
    cg = jnp.dot(h, w_ref[:, c0:2 * c0], preferred_element_type=F32)
    xs = jnp.dot(h, w_ref[:, 2 * c0:c3], preferred_element_type=F32)
    y = _causal_conv(scr, hist_ref, hist_out, cw_ref, (cg * xs).reshape(bt, lt, c0), bt, lt)
    tok_out[...] = bg.reshape(bt, lt, c0) * y
    gate_out[...] = jnp.dot(h, w_ref[:, c3:c4], preferred_element_type=F32).reshape(bt, lt, -1)
    xq_out[...] = jnp.dot(h, w_ref[:, c4:], preferred_element_type=F32).reshape(bt, lt, -1)


def _inproj(x, g, w_bf16, hist, cw, *, mixer, bt, lt):
    bn, seq, _ = x.shape
    ncols = w_bf16.shape[1]
    width, cc = cw.shape
    grid = (bn // bt, seq // lt)

    def act(n):
        return (pl.BlockSpec((bt, lt, n), lambda b, l: (b, l, 0)),
                jax.ShapeDtypeStruct((bn, seq, n), F32))

    hist_spec = pl.BlockSpec((bt, width - 1, cc), lambda b, l: (b, 0, 0))
    in_specs = [
        pl.BlockSpec((bt, lt, D_MODEL), lambda b, l: (b, l, 0)),
        pl.BlockSpec((1, D_MODEL), lambda b, l: (0, 0)),
        pl.BlockSpec((D_MODEL, ncols), lambda b, l: (0, 0)),
        hist_spec,
        pl.BlockSpec((width, cc), lambda b, l: (0, 0)),
    ]
    if mixer == "gdn":
        body = functools.partial(_inproj_gdn_kernel, bt=bt, lt=lt)
        outs = [act(QKV_WIDTH), act(BRANCH_WIDTH), act(X_WIDTH), act(BA_PAD)]
    else:
        body = functools.partial(_inproj_sconv_kernel, bt=bt, lt=lt)
        outs = [act(SC_WIDTH), act(BRANCH_WIDTH), act(X_WIDTH)]
    outs.append((hist_spec, jax.ShapeDtypeStruct(hist.shape, F32)))
    return pl.pallas_call(
        body,
        grid=grid,
        in_specs=in_specs,
        out_specs=[o[0] for o in outs],
        out_shape=[o[1] for o in outs],
        scratch_shapes=[pltpu.VMEM((bt, lt + 8, cc), F32)],
        compiler_params=_cparams("arbitrary", "arbitrary"),
        name="inproj_" + mixer,
    )(x, g, w_bf16, hist, cw)


def _neg_strict_parts_of_inverses(a_list, chunk):
    n = [-a for a in a_list]
    p = a_list
    span = 2
    while span < chunk:
        p = [_dot(x, x) for x in p]
        n = [ni + pi + _dot(ni, pi) for ni, pi in zip(n, p)]
        span *= 2
    return n


def _gdn_kernel(qkv_ref, ba_ref, s0_ref, alog_ref, dtb_ref, og_ref, o_ref, s_ref,
                *, chunk, nchunks):
    @pl.when(pl.program_id(1) == 0)
    def _():
        s_ref[...] = s0_ref[...]

    heads = range(GDN_HEADS)
    row = lax.broadcasted_iota(jnp.int32, (chunk, chunk), 0)
    col = lax.broadcasted_iota(jnp.int32, (chunk, chunk), 1)
    tril = row >= col
    strict = row > col
    ones_tril = tril.astype(F32)
    neg_a = -jnp.exp(alog_ref[...])
    dtb = dtb_ref[...]
    og = og_ref[...]

    def lanes(base, h, n):
        return slice(base + h * n, base + (h + 1) * n)

    def chunk_step(c, carry):
        rows = pl.ds(pl.multiple_of(c * chunk, chunk), chunk)
        ba = ba_ref[rows, :]
        beta_all = 1.0 / (1.0 + jnp.exp(-ba))
        z = ba + dtb
        softplus = jnp.maximum(z, 0.0) + jnp.log1p(jnp.exp(-jnp.abs(z)))
        g_all = neg_a * softplus
        gcum = _dot_f32(ones_tril, g_all)
        gcum_t = gcum.T
        q = [qkv_ref[rows, lanes(0, h, GDN_DK)] for h in heads]
        k = [qkv_ref[rows, lanes(QK_WIDTH, h, GDN_DK)] for h in heads]
        v = [qkv_ref[rows, lanes(2 * QK_WIDTH, h, GDN_DV)] for h in heads]
        q = [x * (lax.rsqrt(jnp.sum(x * x, axis=-1, keepdims=True) + EPS) * (GDN_DK ** -0.5))
             for x in q]
        k = [x * lax.rsqrt(jnp.sum(x * x, axis=-1, keepdims=True) + EPS) for x in k]
        gc = [jnp.broadcast_to(gcum[:, GDN_HEADS + h:GDN_HEADS + h + 1], (chunk, GDN_DK))
              for h in heads]
        gr = [gcum_t[GDN_HEADS + h:GDN_HEADS + h + 1, :] for h in heads]
        gl = [gcum[chunk - 1:chunk, GDN_HEADS + h:GDN_HEADS + h + 1] for h in heads]
        beta = [jnp.broadcast_to(beta_all[:, h:h + 1], (chunk, GDN_DK)) for h in heads]
        decay = [jnp.exp(jnp.where(tril, gc[h][:, :chunk] - gr[h], -jnp.inf)) for h in heads]
        eg = [jnp.exp(x) for x in gc]
        kb = [k[h] * beta[h] for h in heads]
        kkqk = [_dot_nt(jnp.concatenate([kb[h], q[h]], axis=0), k[h]) for h in heads]
        a = [jnp.where(strict, kkqk[h][:chunk] * decay[h], 0.0) for h in heads]
        aqk = [kkqk[h][chunk:] * decay[h] for h in heads]
        n = _neg_strict_parts_of_inverses(a, chunk)
        rhs = [jnp.concatenate([kb[h] * eg[h], v[h] * beta[h]], axis=-1) for h in heads]
        wu = [rhs[h] + _dot(n[h], rhs[h]) for h in heads]
        kd = [k[h] * jnp.exp(gl[h] - gc[h]) for h in heads]
        s = [s_ref[h] for h in heads]
        ws_qs = [_dot(jnp.concatenate([wu[h][:, :GDN_DK], q[h] * eg[h]], axis=0), s[h])
                 for h in heads]
        v_new = [wu[h][:, GDN_DK:] - ws_qs[h][:chunk] for h in heads]
        o = [ws_qs[h][chunk:] + _dot(aqk[h], v_new[h]) for h in heads]
        for h in heads:
            s_ref[h] = s[h] * jnp.exp(gl[h]) + _dot_tn(kd[h], v_new[h])
        for h in heads:
            oh = o[h]
            oh = oh * lax.rsqrt(jnp.mean(oh * oh, axis=-1, keepdims=True) + EPS) * og
            o_ref[rows, lanes(0, h, GDN_DV)] = oh
        return carry

    lax.fori_loop(0, nchunks, chunk_step, 0)


def _gdn(qkv, ba, s0, alog_pad, dtb_pad, og, *, chunk, lt):
    bn, seq, _ = qkv.shape
    s_spec = pl.BlockSpec((None, GDN_HEADS, GDN_DK, GDN_DV), lambda b, l: (b, 0, 0, 0))
    vec = pl.BlockSpec((1, LANES), lambda b, l: (0, 0))
    return pl.pallas_call(
        functools.partial(_gdn_kernel, chunk=chunk, nchunks=lt // chunk),
        grid=(bn, seq // lt),
        in_specs=[
            pl.BlockSpec((None, lt, QKV_WIDTH), lambda b, l: (b, l, 0)),
            pl.BlockSpec((None, lt, BA_PAD), lambda b, l: (b, l, 0)),
            s_spec, vec, vec, vec,
        ],
        out_specs=[
            pl.BlockSpec((None, lt, GDN_WIDTH), lambda b, l: (b, l, 0)),
            s_spec,
        ],
        out_shape=[
            jax.ShapeDtypeStruct((bn, seq, GDN_WIDTH), F32),
            jax.ShapeDtypeStruct(s0.shape, F32),
        ],
        compiler_params=_cparams("arbitrary", "arbitrary"),
        name="gdn",
    )(qkv, ba, s0, alog_pad, dtb_pad, og)


def _attn_out_kernel(tok_ref, xq_ref, gate_ref, x_ref, mk_ref, mv_ref, w_ref, fg_ref,
                     y_ref, xo_scr, *, bt, lt, final):
    lane_head = lax.broadcasted_iota(jnp.int32, (1, X_WIDTH), 1) // X_HEAD_DIM
    scale = X_HEAD_DIM ** -0.5

    def one_seq(b, carry):
        q = xq_ref[b]
        qx = jnp.concatenate([jnp.where(lane_head == h, q, 0.0) for h in range(X_HEADS)], axis=0)
        s = _dot_nt(qx, mk_ref[b]) * scale
        e = jnp.exp(s - jnp.max(s, axis=-1, keepdims=True))
        p = e / jnp.sum(e, axis=-1, keepdims=True)
        o4 = _dot(p, mv_ref[b])
        xo = jnp.where(lane_head == 0, o4[0:lt], 0.0)
        for h in range(1, X_HEADS):
            xo = xo + jnp.where(lane_head == h, o4[h * lt:(h + 1) * lt], 0.0)
        xo_scr[b] = xo
        return carry

    lax.fori_loop(0, bt, one_seq, 0)

    tm = bt * lt
    sg = _silu(gate_ref[...].reshape(tm, BRANCH_WIDTH))
    br_tok = tok_ref[...].reshape(tm, GDN_WIDTH) * sg[:, :GDN_WIDTH]
    br_x = xo_scr[...].reshape(tm, X_WIDTH) * sg[:, GDN_WIDTH:]
    y = (x_ref[...].reshape(tm, D_MODEL)
         + _dot(br_tok, w_ref[:GDN_WIDTH, :]) + _dot(br_x, w_ref[GDN_WIDTH:, :]))
    if final:
        y = _rms_rows(y, fg_ref[...])
    y_ref[...] = y.reshape(bt, lt, D_MODEL)


def _attn_out(tok, xq, gate, x, mk, mv, w_bf16, fg, *, layer, bt, lt, final):
    bn, seq, _ = x.shape

    def act(n):
        return pl.BlockSpec((bt, lt, n), lambda b, l: (b, l, 0))

    mem = pl.BlockSpec((None, bt, N_MEM, X_WIDTH), lambda b, l: (layer, b, 0, 0))
    return pl.pallas_call(
        functools.partial(_attn_out_kernel, bt=bt, lt=lt, final=final),
        grid=(bn // bt, seq // lt),
        in_specs=[
            act(GDN_WIDTH), act(X_WIDTH), act(BRANCH_WIDTH), act(D_MODEL), mem, mem,
            pl.BlockSpec((BRANCH_WIDTH, D_MODEL), lambda b, l: (0, 0)),
            pl.BlockSpec((1, D_MODEL), lambda b, l: (0, 0)),
        ],
        out_specs=act(D_MODEL),
        out_shape=jax.ShapeDtypeStruct(x.shape, F32),
        scratch_shapes=[pltpu.VMEM((bt, lt, X_WIDTH), F32)],
        compiler_params=_cparams("arbitrary", "arbitrary"),
        name="attn_out",
    )(tok, xq, gate, x, mk, mv, w_bf16, fg)


def _trunk(x, mem_k, mem_v, gdn_s, gdn_conv, sc_conv, p, *, bt, lt, chunk, gdn_lt):
    qkv, gate, xq, ba, gconv_new = _inproj(
        x, p["norm_g"][0:1], p["w_in_a"], gdn_conv, p["conv_w_a"], mixer="gdn", bt=bt, lt=lt)
    tok, s_new = _gdn(qkv, ba, gdn_s, p["alog_pad"], p["dtb_pad"], p["o_norm_g"],
                      chunk=chunk, lt=gdn_lt)
    x = _attn_out(tok, xq, gate, x, mem_k, mem_v, p["w_out"][0], p["final_norm_g"],
                  layer=0, bt=bt, lt=lt, final=False)
    tok, gate, xq, sconv_new = _inproj(
        x, p["norm_g"][1:2], p["w_in_b"], sc_conv, p["conv_w_b"], mixer="sconv", bt=bt, lt=lt)
    y = _attn_out(tok, xq, gate, x, mem_k, mem_v, p["w_out"][1], p["final_norm_g"],
                  layer=1, bt=bt, lt=lt, final=True)
    return y, s_new[None], gconv_new[None], sconv_new[None]


def kernel(x_prompt, x_sample, mem_prompt, state_gdn, state_gdn_conv, state_sconv, cache_mem_k, cache_mem_v, norm_g, w_in_a, conv_w_a, a_log, dt_bias, o_norm_g, w_in_b, conv_w_b, mem_norm_g, w_mem_kv, w_out, final_norm_g):
    bp, seq, _ = x_prompt.shape
    n_mem = mem_prompt.shape[1]

    wa = w_in_a[0]
    c_b = QKV_WIDTH
    c_g = QKV_WIDTH + 2 * GDN_HEADS
    wa = jnp.concatenate(
        [wa[:, :c_b], wa[:, c_g:], wa[:, c_b:c_g],
         jnp.zeros((D_MODEL, BA_PAD - 2 * GDN_HEADS), wa.dtype)], axis=1).astype(BF16)
    pad_lo = jnp.zeros((GDN_HEADS,), F32)
    pad_hi = jnp.zeros((LANES - 2 * GDN_HEADS,), F32)
    params = {
        "norm_g": norm_g,
        "w_in_a": wa,
        "conv_w_a": conv_w_a[0],
        "alog_pad": jnp.concatenate([pad_lo, a_log[0], pad_hi])[None],
        "dtb_pad": jnp.concatenate([pad_lo, dt_bias[0], pad_hi])[None],
        "o_norm_g": o_norm_g,
        "w_in_b": w_in_b[0].astype(BF16),
        "conv_w_b": conv_w_b[0],
        "w_out": w_out.astype(BF16),
        "final_norm_g": final_norm_g[None],
    }

    mem_k2, mem_v2 = _memkv(mem_prompt.reshape(bp * n_mem, D_MODEL), mem_norm_g[None],
                            w_mem_kv.astype(BF16))
    mem_k_p = mem_k2.reshape(-1, bp, n_mem, X_HEADS, X_HEAD_DIM)
    mem_v_p = mem_v2.reshape(-1, bp, n_mem, X_HEADS, X_HEAD_DIM)

    s0_p = jnp.zeros((bp,) + state_gdn.shape[2:], F32)
    gc0_p = jnp.zeros((bp,) + state_gdn_conv.shape[2:], F32)
    sc0_p = jnp.zeros((bp,) + state_sconv.shape[2:], F32)
    y_p, s_p, gc_p, sc_p = _trunk(x_prompt, mem_k2.reshape(-1, bp, n_mem, X_WIDTH),
                                  mem_v2.reshape(-1, bp, n_mem, X_WIDTH), s0_p, gc0_p, sc0_p,
                                  params, bt=1, lt=256, chunk=GDN_CHUNK, gdn_lt=256)
    bs, dec_seq, _ = x_sample.shape
    y_s, s_s, gc_s, sc_s = _trunk(x_sample, cache_mem_k.reshape(-1, bs, n_mem, X_WIDTH),
                                  cache_mem_v.reshape(-1, bs, n_mem, X_WIDTH), state_gdn[0],
                                  state_gdn_conv[0], state_sconv[0], params,
                                  bt=32, lt=dec_seq, chunk=dec_seq, gdn_lt=dec_seq)
    return (y_p, y_s, s_p, gc_p, sc_p, mem_k_p, mem_v_p, s_s, gc_s, sc_s)
```

```python
import functools

import jax
import jax.numpy as jnp
from jax import lax
from jax.experimental import pallas as pl
from jax.experimental.pallas import tpu as pltpu

F32 = jnp.float32
BF16 = jnp.bfloat16

D_MODEL = 1024
N_MEM = 256
X_WIDTH = 256
X_HEADS = 4
X_HEAD_DIM = 64
GDN_HEADS = 6
GDN_DK = 128
GDN_DV = 128
GDN_WIDTH = GDN_HEADS * GDN_DV
QK_WIDTH = GDN_HEADS * GDN_DK
QKV_WIDTH = 2 * QK_WIDTH + GDN_WIDTH
GDN_CHUNK = 64
SC_WIDTH = 768
BRANCH_WIDTH = 1024
EPS = 1e-6

LANES = 128
SUBLANES = 8
COL_CHUNK = 256
BA_PAD = LANES
VMEM_LIMIT = 56 * 1024 * 1024

HIGHEST = lax.Precision.HIGHEST


def _cparams(*sem):
    return pltpu.CompilerParams(dimension_semantics=sem, vmem_limit_bytes=VMEM_LIMIT)


def _rms_rows(x, g):
    r = lax.rsqrt(jnp.mean(x * x, axis=-1, keepdims=True) + EPS)
    return x * r * g


def _silu(x):
    return x * (1.0 / (1.0 + jnp.exp(-x)))


def _dot(a, b):
    return jnp.dot(a.astype(BF16), b.astype(BF16), preferred_element_type=F32)


def _dot_nt(a, b):
    return lax.dot_general(a.astype(BF16), b.astype(BF16), (((1,), (1,)), ((), ())),
                           preferred_element_type=F32)


def _dot_tn(a, b):
    return lax.dot_general(a.astype(BF16), b.astype(BF16), (((0,), (0,)), ((), ())),
                           preferred_element_type=F32)


def _dot_f32(a, b):
    return jnp.dot(a, b, preferred_element_type=F32, precision=HIGHEST)


def _col_chunks(n):
    return [slice(c, min(c + COL_CHUNK, n)) for c in range(0, n, COL_CHUNK)]


def _memkv_kernel(m_ref, g_ref, w_ref, kt_ref, vt_ref):
    h = _rms_rows(m_ref[...], g_ref[...])
    kv = _dot(h, w_ref[...])
    kt_ref[...] = kv[:, :X_WIDTH].T
    vt_ref[...] = kv[:, X_WIDTH:].T


def _memkv(mem, g, w_bf16):
    bn, n_mem, _ = mem.shape
    depth = w_bf16.shape[0]
    out = jax.ShapeDtypeStruct((depth, bn, X_WIDTH, n_mem), F32)
    out_spec = pl.BlockSpec((None, None, X_WIDTH, n_mem), lambda l, b: (l, b, 0, 0))
    return pl.pallas_call(
        _memkv_kernel,
        grid=(depth, bn),
        in_specs=[
            pl.BlockSpec((None, n_mem, D_MODEL), lambda l, b: (b, 0, 0)),
            pl.BlockSpec((1, D_MODEL), lambda l, b: (0, 0)),
            pl.BlockSpec((None, D_MODEL, 2 * X_WIDTH), lambda l, b: (l, 0, 0)),
        ],
        out_specs=[out_spec, out_spec],
        out_shape=[out, out],
        compiler_params=_cparams("arbitrary", "arbitrary"),
        name="memkv",
    )(mem, g, w_bf16)


def _init_conv_carry(carry, hist_ref):
    @pl.when(pl.program_id(1) == 0)
    def _():
        carry[:, SUBLANES - hist_ref.shape[1]:, :] = hist_ref[...]


def _causal_conv_cols(x, cols, carry, hist_out, cw_ref):
    width = cw_ref.shape[0]
    bt, lt, n = x.shape
    g = lt // SUBLANES
    xe = jnp.concatenate([carry[:, :, cols], x], axis=1).reshape(bt, g + 1, SUBLANES, n)
    sub = lax.broadcasted_iota(jnp.int32, (1, 1, SUBLANES, n), 2)
    y = x.reshape(bt, g, SUBLANES, n) * cw_ref[width - 1:width, cols]
    for j in range(width - 1):
        s = width - 1 - j
        r = pltpu.roll(xe, s, axis=2)
        y = y + jnp.where(sub < s, r[:, :g], r[:, 1:]) * cw_ref[j:j + 1, cols]
    carry[:, :, cols] = x[:, lt - SUBLANES:, :]
    hist_out[:, :, cols] = x[:, lt - (width - 1):, :]
    return y.reshape(bt, lt, n)


def _inproj_gdn_kernel(x_ref, g_ref, w_ref, qkv_out, gate_out, xq_out, ba_out, *, bt, lt):
    tm = bt * lt
    h = _rms_rows(x_ref[...].reshape(tm, D_MODEL), g_ref[...]).astype(BF16)
    base = 0
    for out in (qkv_out, gate_out, xq_out, ba_out):
        n = out.shape[-1]
        for cols in _col_chunks(n):
            wcols = slice(base + cols.start, base + cols.stop)
            out[:, :, cols] = jnp.dot(h, w_ref[:, wcols],
                                      preferred_element_type=F32).reshape(bt, lt, -1)
        base += n


def _inproj_sconv_kernel(x_ref, g_ref, w_ref, hist_ref, cw_ref,
                         tok_out, gate_out, xq_out, hist_out, carry, *, bt, lt):
    tm = bt * lt
    h = _rms_rows(x_ref[...].reshape(tm, D_MODEL), g_ref[...]).astype(BF16)
    _init_conv_carry(carry, hist_ref)

    def proj(cols, base):
        cols = slice(base + cols.start, base + cols.stop)
        return jnp.dot(h, w_ref[:, cols], preferred_element_type=F32).reshape(bt, lt, -1)

    for cols in _col_chunks(SC_WIDTH):
        cx = proj(cols, SC_WIDTH) * proj(cols, 2 * SC_WIDTH)
        y = _causal_conv_cols(cx, cols, carry, hist_out, cw_ref)
        tok_out[:, :, cols] = proj(cols, 0) * y
    base = 3 * SC_WIDTH
    for out in (gate_out, xq_out):
        n = out.shape[-1]
        for cols in _col_chunks(n):
            out[:, :, cols] = proj(cols, base)
        base += n


def _inproj(x, g, w_bf16, hist, cw, *, mixer, bt, lt):
    bn, seq, _ = x.shape
    ncols = w_bf16.shape[1]

    def act(n):
        return (pl.BlockSpec((bt, lt, n), lambda b, l: (b, l, 0)),
                jax.ShapeDtypeStruct((bn, seq, n), F32))

    in_specs = [
        pl.BlockSpec((bt, lt, D_MODEL), lambda b, l: (b, l, 0)),
        pl.BlockSpec((1, D_MODEL), lambda b, l: (0, 0)),
        pl.BlockSpec((D_MODEL, ncols), lambda b, l: (0, 0)),
    ]
    if mixer == "gdn":
        body = functools.partial(_inproj_gdn_kernel, bt=bt, lt=lt)
        args = (x, g, w_bf16)
        outs = [act(QKV_WIDTH), act(BRANCH_WIDTH), act(X_WIDTH), act(BA_PAD)]
        scratch = []
    else:
        width, cc = cw.shape
        hist_spec = pl.BlockSpec((bt, width - 1, cc), lambda b, l: (b, 0, 0))
        body = functools.partial(_inproj_sconv_kernel, bt=bt, lt=lt)
        args = (x, g, w_bf16, hist, cw)
        in_specs += [hist_spec, pl.BlockSpec((width, cc), lambda b, l: (0, 0))]
        outs = [act(SC_WIDTH), act(BRANCH_WIDTH), act(X_WIDTH),
                (hist_spec, jax.ShapeDtypeStruct(hist.shape, F32))]
        scratch = [pltpu.VMEM((bt, SUBLANES, cc), F32)]
    return pl.pallas_call(
        body,
        grid=(bn // bt, seq // lt),
        in_specs=in_specs,
        out_specs=[o[0] for o in outs],
        out_shape=[o[1] for o in outs],
        scratch_shapes=scratch,
        compiler_params=_cparams("arbitrary", "arbitrary"),
        name="inproj_" + mixer,
    )(*args)


def _neg_strict_parts_of_inverses(a_list, chunk):
    n = [-a for a in a_list]
    p = a_list
    span = 2
    while span < chunk:
        p = [_dot(x, x) for x in p]
        n = [ni + pi + _dot(ni, pi) for ni, pi in zip(n, p)]
        span *= 2
    return n


def _gdn_kernel(qkv_ref, ba_ref, s0_ref, hist_ref, cw_ref, alog_ref, dtb_ref, og_ref,
                o_ref, s_ref, hist_out, carry, *, chunk, nchunks):
    @pl.when(pl.program_id(1) == 0)
    def _():
        s_ref[...] = s0_ref[...]

    _init_conv_carry(carry, hist_ref)

    heads = range(GDN_HEADS)
    row = lax.broadcasted_iota(jnp.int32, (chunk, chunk), 0)
    col = lax.broadcasted_iota(jnp.int32, (chunk, chunk), 1)
    tril = row >= col
    strict = row > col
    ones_tril = tril.astype(F32)
    neg_a = -jnp.exp(alog_ref[...])
    dtb = dtb_ref[...]
    og = og_ref[...]

    def lanes(base, h, n):
        return slice(base + h * n, base + (h + 1) * n)

    def chunk_step(c, _):
        rows = pl.ds(pl.multiple_of(c * chunk, chunk), chunk)

        def conv_silu(cols):
            y = _causal_conv_cols(qkv_ref[rows, cols][None], cols, carry, hist_out, cw_ref)
            return _silu(y[0])

        ba = ba_ref[rows, :]
        beta_all = 1.0 / (1.0 + jnp.exp(-ba))
        z = ba + dtb
        softplus = jnp.maximum(z, 0.0) + jnp.log1p(jnp.exp(-jnp.abs(z)))
        g_all = neg_a * softplus
        gcum = _dot_f32(ones_tril, g_all)
        gcum_t = gcum.T
        q = [conv_silu(lanes(0, h, GDN_DK)) for h in heads]
        k = [conv_silu(lanes(QK_WIDTH, h, GDN_DK)) for h in heads]
        v = [conv_silu(lanes(2 * QK_WIDTH, h, GDN_DV)) for h in heads]
        q = [x * (lax.rsqrt(jnp.sum(x * x, axis=-1, keepdims=True) + EPS) * (GDN_DK ** -0.5))
             for x in q]
        k = [x * lax.rsqrt(jnp.sum(x * x, axis=-1, keepdims=True) + EPS) for x in k]
        gc = [jnp.broadcast_to(gcum[:, GDN_HEADS + h:GDN_HEADS + h + 1], (chunk, GDN_DK))
              for h in heads]
        gr = [gcum_t[GDN_HEADS + h:GDN_HEADS + h + 1, :] for h in heads]
        gl = [gcum[chunk - 1:chunk, GDN_HEADS + h:GDN_HEADS + h + 1] for h in heads]
        beta = [jnp.broadcast_to(beta_all[:, h:h + 1], (chunk, GDN_DK)) for h in heads]
        decay = [jnp.exp(jnp.where(tril, gc[h][:, :chunk] - gr[h], -jnp.inf)) for h in heads]
        eg = [jnp.exp(x) for x in gc]
        kb = [k[h] * beta[h] for h in heads]
        kkqk = [_dot_nt(jnp.concatenate([kb[h], q[h]], axis=0), k[h]) for h in heads]
        a = [jnp.where(strict, kkqk[h][:chunk] * decay[h], 0.0) for h in heads]
        aqk = [kkqk[h][chunk:] * decay[h] for h in heads]
        n = _neg_strict_parts_of_inverses(a, chunk)
        rhs = [jnp.concatenate([kb[h] * eg[h], v[h] * beta[h]], axis=-1) for h in heads]
        wu = [rhs[h] + _dot(n[h], rhs[h]) for h in heads]
        kd = [k[h] * jnp.exp(gl[h] - gc[h]) for h in heads]
        s = [s_ref[h] for h in heads]
        ws_qs = [_dot(jnp.concatenate([wu[h][:, :GDN_DK], q[h] * eg[h]], axis=0), s[h])
                 for h in heads]
        v_new = [wu[h][:, GDN_DK:] - ws_qs[h][:chunk] for h in heads]
        o = [ws_qs[h][chunk:] + _dot(aqk[h], v_new[h]) for h in heads]
        for h in heads:
            s_ref[h] = s[h] * jnp.exp(gl[h]) + _dot_tn(kd[h], v_new[h])
        for h in heads:
            oh = o[h]
            oh = oh * lax.rsqrt(jnp.mean(oh * oh, axis=-1, keepdims=True) + EPS) * og
            o_ref[rows, lanes(0, h, GDN_DV)] = oh
        return 0

    lax.fori_loop(0, nchunks, chunk_step, 0)


def _gdn(qkv, ba, s0, hist, cw, alog_pad, dtb_pad, og, *, chunk, lt):
    bn, seq, _ = qkv.shape
    width, cc = cw.shape
    s_spec = pl.BlockSpec((None, GDN_HEADS, GDN_DK, GDN_DV), lambda b, l: (b, 0, 0, 0))
    hist_spec = pl.BlockSpec((1, width - 1, cc), lambda b, l: (b, 0, 0))
    vec = pl.BlockSpec((1, LANES), lambda b, l: (0, 0))
    return pl.pallas_call(
        functools.partial(_gdn_kernel, chunk=chunk, nchunks=lt // chunk),
        grid=(bn, seq // lt),
        in_specs=[
            pl.BlockSpec((None, lt, QKV_WIDTH), lambda b, l: (b, l, 0)),
            pl.BlockSpec((None, lt, BA_PAD), lambda b, l: (b, l, 0)),
            s_spec, hist_spec,
            pl.BlockSpec((width, cc), lambda b, l: (0, 0)),
            vec, vec, vec,
        ],
        out_specs=[
            pl.BlockSpec((None, lt, GDN_WIDTH), lambda b, l: (b, l, 0)),
            s_spec, hist_spec,
        ],
        out_shape=[
            jax.ShapeDtypeStruct((bn, seq, GDN_WIDTH), F32),
            jax.ShapeDtypeStruct(s0.shape, F32),
            jax.ShapeDtypeStruct(hist.shape, F32),
        ],
        scratch_shapes=[pltpu.VMEM((1, SUBLANES, cc), F32)],
        compiler_params=_cparams("arbitrary", "arbitrary"),
        name="gdn",
    )(qkv, ba, s0, hist, cw, alog_pad, dtb_pad, og)


def _attn_out_kernel(tok_ref, xq_ref, gate_ref, x_ref, kt_ref, vt_ref, w_ref, fg_ref,
                     y_ref, xo_scr, *, bt, lt, final):
    lane_head = lax.broadcasted_iota(jnp.int32, (1, X_WIDTH), 1) // X_HEAD_DIM
    scale = X_HEAD_DIM ** -0.5

    def one_seq(b, carry):
        q = xq_ref[b]
        qx = jnp.concatenate([jnp.where(lane_head == h, q, 0.0) for h in range(X_HEADS)], axis=0)
        s = _dot(qx, kt_ref[b]) * scale
        e = jnp.exp(s - jnp.max(s, axis=-1, keepdims=True))
        p = e / jnp.sum(e, axis=-1, keepdims=True)
        o4 = _dot_nt(p, vt_ref[b])
        xo = jnp.where(lane_head == 0, o4[0:lt], 0.0)
        for h in range(1, X_HEADS):
            xo = xo + jnp.where(lane_head == h, o4[h * lt:(h + 1) * lt], 0.0)
        xo_scr[b] = xo
        return carry

    lax.fori_loop(0, bt, one_seq, 0)

    tm = bt * lt
    sg = _silu(gate_ref[...].reshape(tm, BRANCH_WIDTH))
    br_tok = tok_ref[...].reshape(tm, GDN_WIDTH) * sg[:, :GDN_WIDTH]
    br_x = xo_scr[...].reshape(tm, X_WIDTH) * sg[:, GDN_WIDTH:]
    y = (x_ref[...].reshape(tm, D_MODEL)
         + _dot(br_tok, w_ref[:GDN_WIDTH, :]) + _dot(br_x, w_ref[GDN_WIDTH:, :]))
    if final:
        y = _rms_rows(y, fg_ref[...])
    y_ref[...] = y.reshape(bt, lt, D_MODEL)


def _attn_out(tok, xq, gate, x, mk, mv, w_bf16, fg, *, layer, bt, lt, final):
    bn, seq, _ = x.shape

    def act(n):
        return pl.BlockSpec((bt, lt, n), lambda b, l: (b, l, 0))

    mem = pl.BlockSpec((None, bt, X_WIDTH, N_MEM), lambda b, l: (layer, b, 0, 0))
    return pl.pallas_call(
        functools.partial(_attn_out_kernel, bt=bt, lt=lt, final=final),
        grid=(bn // bt, seq // lt),
        in_specs=[
            act(GDN_WIDTH), act(X_WIDTH), act(BRANCH_WIDTH), act(D_MODEL), mem, mem,
            pl.BlockSpec((BRANCH_WIDTH, D_MODEL), lambda b, l: (0, 0)),
            pl.BlockSpec((1, D_MODEL), lambda b, l: (0, 0)),
        ],
        out_specs=act(D_MODEL),
        out_shape=jax.ShapeDtypeStruct(x.shape, F32),
        scratch_shapes=[pltpu.VMEM((bt, lt, X_WIDTH), F32)],
        compiler_params=_cparams("arbitrary", "arbitrary"),
        name="attn_out",
    )(tok, xq, gate, x, mk, mv, w_bf16, fg)


def _trunk(x, mem_k, mem_v, gdn_s, gdn_conv, sc_conv, p, *, bt, lt, chunk, gdn_lt):
    qkv, gate, xq, ba = _inproj(x, p["norm_g"][0:1], p["w_in_a"], None, None,
                                mixer="gdn", bt=bt, lt=lt)
    tok, s_new, gconv_new = _gdn(qkv, ba, gdn_s, gdn_conv, p["conv_w_a"], p["alog_pad"],
                                 p["dtb_pad"], p["o_norm_g"], chunk=chunk, lt=gdn_lt)
    x = _attn_out(tok, xq, gate, x, mem_k, mem_v, p["w_out"][0], p["final_norm_g"],
                  layer=0, bt=bt, lt=lt, final=False)
    tok, gate, xq, sconv_new = _inproj(
        x, p["norm_g"][1:2], p["w_in_b"], sc_conv, p["conv_w_b"], mixer="sconv", bt=bt, lt=lt)
    y = _attn_out(tok, xq, gate, x, mem_k, mem_v, p["w_out"][1], p["final_norm_g"],
                  layer=1, bt=bt, lt=lt, final=True)
    return y, s_new[None], gconv_new[None], sconv_new[None]


def kernel(x_prompt, x_sample, mem_prompt, state_gdn, state_gdn_conv, state_sconv, cache_mem_k, cache_mem_v, norm_g, w_in_a, conv_w_a, a_log, dt_bias, o_norm_g, w_in_b, conv_w_b, mem_norm_g, w_mem_kv, w_out, final_norm_g):
    bp = x_prompt.shape[0]

    wa = w_in_a[0]
    c_b = QKV_WIDTH
    c_g = QKV_WIDTH + 2 * GDN_HEADS
    wa = jnp.concatenate(
        [wa[:, :c_b], wa[:, c_g:], wa[:, c_b:c_g],
         jnp.zeros((D_MODEL, BA_PAD - 2 * GDN_HEADS), wa.dtype)], axis=1).astype(BF16)
    pad_lo = jnp.zeros((GDN_HEADS,), F32)
    pad_hi = jnp.zeros((LANES - 2 * GDN_HEADS,), F32)
    params = {
        "norm_g": norm_g,
        "w_in_a": wa,
        "conv_w_a": conv_w_a[0],
        "alog_pad": jnp.concatenate([pad_lo, a_log[0], pad_hi])[None],
        "dtb_pad": jnp.concatenate([pad_lo, dt_bias[0], pad_hi])[None],
        "o_norm_g": o_norm_g,
        "w_in_b": w_in_b[0].astype(BF16),
        "conv_w_b": conv_w_b[0],
        "w_out": w_out.astype(BF16),
        "final_norm_g": final_norm_g[None],
    }

    def to_cache(t):
        t = t.reshape(t.shape[0], t.shape[1], X_HEADS, X_HEAD_DIM, t.shape[3])
        return jnp.transpose(t, (0, 1, 4, 2, 3))

    def from_cache(t):
        t = jnp.transpose(t, (0, 1, 3, 4, 2))
        return t.reshape(t.shape[0], t.shape[1], X_WIDTH, t.shape[4])

    mem_kt, mem_vt = _memkv(mem_prompt, mem_norm_g[None], w_mem_kv.astype(BF16))
    mem_k_p = to_cache(mem_kt)
    mem_v_p = to_cache(mem_vt)

    s0_p = jnp.zeros((bp,) + state_gdn.shape[2:], F32)
    gc0_p = jnp.zeros((bp,) + state_gdn_conv.shape[2:], F32)
    sc0_p = jnp.zeros((bp,) + state_sconv.shape[2:], F32)
    y_p, s_p, gc_p, sc_p = _trunk(x_prompt, mem_kt, mem_vt, s0_p, gc0_p, sc0_p,
                                  params, bt=1, lt=256, chunk=GDN_CHUNK, gdn_lt=256)
    dec_seq = x_sample.shape[1]
    y_s, s_s, gc_s, sc_s = _trunk(x_sample, from_cache(cache_mem_k), from_cache(cache_mem_v),
                                  state_gdn[0], state_gdn_conv[0], state_sconv[0], params,
                                  bt=32, lt=dec_seq, chunk=dec_seq, gdn_lt=dec_seq)
    return (y_p, y_s, s_p, gc_p, sc_p, mem_k_p, mem_v_p, s_s, gc_s, sc_s)
```

```python
import functools

import jax
import jax.numpy as jnp
from jax import lax
from jax.experimental import pallas as pl
from jax.experimental.pallas import tpu as pltpu

F32 = jnp.float32
BF16 = jnp.bfloat16

D_MODEL = 1024
N_MEM = 256
X_WIDTH = 256
X_HEADS = 4
X_HEAD_DIM = 64
GDN_HEADS = 6
GDN_DK = 128
GDN_DV = 128
GDN_WIDTH = GDN_HEADS * GDN_DV
QK_WIDTH = GDN_HEADS * GDN_DK
QKV_WIDTH = 2 * QK_WIDTH + GDN_WIDTH
GDN_CHUNK = 64
SC_WIDTH = 768
BRANCH_WIDTH = 1024
EPS = 1e-6

LANES = 128
SUBLANES = 8
COL_CHUNK = 256
BA_PAD = LANES
VMEM_LIMIT = 56 * 1024 * 1024

HIGHEST = lax.Precision.HIGHEST


def _cparams(*sem):
    return pltpu.CompilerParams(dimension_semantics=sem, vmem_limit_bytes=VMEM_LIMIT)


def _rms_rows(x, g):
    r = lax.rsqrt(jnp.mean(x * x, axis=-1, keepdims=True) + EPS)
    return x * r * g


def _silu(x):
    return x * (1.0 / (1.0 + jnp.exp(-x)))


def _dot(a, b):
    return jnp.dot(a.astype(BF16), b.astype(BF16), preferred_element_type=F32)


def _dot_nt(a, b):
    return lax.dot_general(a.astype(BF16), b.astype(BF16), (((1,), (1,)), ((), ())),
                           preferred_element_type=F32)


def _dot_tn(a, b):
    return lax.dot_general(a.astype(BF16), b.astype(BF16), (((0,), (0,)), ((), ())),
                           preferred_element_type=F32)


def _dot_f32(a, b):
    return jnp.dot(a, b, preferred_element_type=F32, precision=HIGHEST)


def _col_chunks(n):
    return [slice(c, min(c + COL_CHUNK, n)) for c in range(0, n, COL_CHUNK)]


def _memkv_kernel(m_ref, g_ref, w_ref, kt_ref, vt_ref):
    h = _rms_rows(m_ref[...], g_ref[...])
    kv = _dot(h, w_ref[...])
    kt_ref[...] = kv[:, :X_WIDTH].T
    vt_ref[...] = kv[:, X_WIDTH:].T


def _memkv(mem, g, w_bf16):
    bn, n_mem, _ = mem.shape
    depth = w_bf16.shape[0]
    out = jax.ShapeDtypeStruct((depth, bn, X_WIDTH, n_mem), F32)
    out_spec = pl.BlockSpec((None, None, X_WIDTH, n_mem), lambda l, b: (l, b, 0, 0))
    return pl.pallas_call(
        _memkv_kernel,
        grid=(depth, bn),
        in_specs=[
            pl.BlockSpec((None, n_mem, D_MODEL), lambda l, b: (b, 0, 0)),
            pl.BlockSpec((1, D_MODEL), lambda l, b: (0, 0)),
            pl.BlockSpec((None, D_MODEL, 2 * X_WIDTH), lambda l, b: (l, 0, 0)),
        ],
        out_specs=[out_spec, out_spec],
        out_shape=[out, out],
        compiler_params=_cparams("arbitrary", "arbitrary"),
        name="memkv",
    )(mem, g, w_bf16)


def _init_conv_carry(carry, hist_ref):
    @pl.when(pl.program_id(1) == 0)
    def _():
        carry[:, SUBLANES - hist_ref.shape[1]:, :] = hist_ref[...]


def _causal_conv_cols(x, cols, carry, hist_out, cw_ref):
    width = cw_ref.shape[0]
    bt, lt, n = x.shape
    g = lt // SUBLANES
    xe = jnp.concatenate([carry[:, :, cols], x], axis=1).reshape(bt, g + 1, SUBLANES, n)
    sub = lax.broadcasted_iota(jnp.int32, (1, 1, SUBLANES, n), 2)
    y = x.reshape(bt, g, SUBLANES, n) * cw_ref[width - 1:width, cols]
    for j in range(width - 1):
        s = width - 1 - j
        r = pltpu.roll(xe, s, axis=2)
        y = y + jnp.where(sub < s, r[:, :g], r[:, 1:]) * cw_ref[j:j + 1, cols]
    carry[:, :, cols] = x[:, lt - SUBLANES:, :]
    if hist_out is not None:
        hist_out[:, :, cols] = x[:, lt - (width - 1):, :]
    return y.reshape(bt, lt, n)


def _inproj_gdn_kernel(x_ref, g_ref, w_ref, qkv_out, gate_out, xq_out, ba_out, hist_out,
                       *, bt, lt):
    tm = bt * lt
    h = _rms_rows(x_ref[...].reshape(tm, D_MODEL), g_ref[...]).astype(BF16)
    n_hist = hist_out.shape[1]
    base = 0
    for out in (qkv_out, gate_out, xq_out, ba_out):
        n = out.shape[-1]
        for cols in _col_chunks(n):
            wcols = slice(base + cols.start, base + cols.stop)
            y = jnp.dot(h, w_ref[:, wcols], preferred_element_type=F32).reshape(bt, lt, -1)
            out[:, :, cols] = y.astype(out.dtype)
            if out is qkv_out:
                hist_out[:, :, cols] = y[:, lt - n_hist:, :]
        base += n


def _inproj_sconv_kernel(x_ref, g_ref, w_ref, hist_ref, cw_ref,
                         tok_out, gate_out, xq_out, hist_out, carry, *, bt, lt):
    tm = bt * lt
    h = _rms_rows(x_ref[...].reshape(tm, D_MODEL), g_ref[...]).astype(BF16)
    _init_conv_carry(carry, hist_ref)

    def proj(cols, base):
        cols = slice(base + cols.start, base + cols.stop)
        return jnp.dot(h, w_ref[:, cols], preferred_element_type=F32).reshape(bt, lt, -1)

    for cols in _col_chunks(SC_WIDTH):
        cx = proj(cols, SC_WIDTH) * proj(cols, 2 * SC_WIDTH)
        y = _causal_conv_cols(cx, cols, carry, hist_out, cw_ref)
        tok_out[:, :, cols] = (proj(cols, 0) * y).astype(tok_out.dtype)
    base = 3 * SC_WIDTH
    for out in (gate_out, xq_out):
        n = out.shape[-1]
        for cols in _col_chunks(n):
            out[:, :, cols] = proj(cols, base).astype(out.dtype)
        base += n


def _inproj(x, g, w_bf16, hist, cw, *, mixer, bt, lt, act_dtype):
    bn, seq, _ = x.shape
    ncols = w_bf16.shape[1]
    width, cc = cw.shape

    def act(n, dtype=act_dtype):
        return (pl.BlockSpec((bt, lt, n), lambda b, l: (b, l, 0)),
                jax.ShapeDtypeStruct((bn, seq, n), dtype))

    hist_spec = pl.BlockSpec((bt, width - 1, cc), lambda b, l: (b, 0, 0))
    hist_out = (hist_spec, jax.ShapeDtypeStruct((bn, width - 1, cc), F32))
    in_specs = [
        pl.BlockSpec((bt, lt, D_MODEL), lambda b, l: (b, l, 0)),
        pl.BlockSpec((1, D_MODEL), lambda b, l: (0, 0)),
        pl.BlockSpec((D_MODEL, ncols), lambda b, l: (0, 0)),
    ]
    if mixer == "gdn":
        body = functools.partial(_inproj_gdn_kernel, bt=bt, lt=lt)
        args = (x, g, w_bf16)
        outs = [act(QKV_WIDTH), act(BRANCH_WIDTH), act(X_WIDTH), act(BA_PAD, F32), hist_out]
        scratch = []
    else:
        body = functools.partial(_inproj_sconv_kernel, bt=bt, lt=lt)
        args = (x, g, w_bf16, hist, cw)
        in_specs += [hist_spec, pl.BlockSpec((width, cc), lambda b, l: (0, 0))]
        outs = [act(SC_WIDTH), act(BRANCH_WIDTH), act(X_WIDTH), hist_out]
        scratch = [pltpu.VMEM((bt, SUBLANES, cc), F32)]
    return pl.pallas_call(
        body,
        grid=(bn // bt, seq // lt),
        in_specs=in_specs,
        out_specs=[o[0] for o in outs],
        out_shape=[o[1] for o in outs],
        scratch_shapes=scratch,
        compiler_params=_cparams("arbitrary", "arbitrary"),
        name="inproj_" + mixer,
    )(*args)


def _neg_strict_parts_of_inverses(a_list, chunk):
    n = [-a for a in a_list]
    p = a_list
    span = 2
    while span < chunk:
        p = [_dot(x, x) for x in p]
        n = [ni + pi + _dot(ni, pi) for ni, pi in zip(n, p)]
        span *= 2
    return n


def _gdn_kernel(qkv_ref, ba_ref, s0_ref, hist_ref, cw_ref, alog_ref, dtb_ref, og_ref,
                o_ref, s_ref, carry, *, chunk, nchunks):
    @pl.when(pl.program_id(1) == 0)
    def _():
        s_ref[...] = s0_ref[...]

    _init_conv_carry(carry, hist_ref)

    heads = range(GDN_HEADS)
    row = lax.broadcasted_iota(jnp.int32, (chunk, chunk), 0)
    col = lax.broadcasted_iota(jnp.int32, (chunk, chunk), 1)
    tril = row >= col
    strict = row > col
    ones_tril = tril.astype(F32)
    neg_a = -jnp.exp(alog_ref[...])
    dtb = dtb_ref[...]
    og = og_ref[...]

    def lanes(base, h, n):
        return slice(base + h * n, base + (h + 1) * n)

    def chunk_step(c, _):
        rows = pl.ds(pl.multiple_of(c * chunk, chunk), chunk)

        def conv_silu(cols):
            x = qkv_ref[rows, cols].astype(F32)
            y = _causal_conv_cols(x[None], cols, carry, None, cw_ref)
            return _silu(y[0])

        ba = ba_ref[rows, :]
        beta_all = 1.0 / (1.0 + jnp.exp(-ba))
        z = ba + dtb
        softplus = jnp.maximum(z, 0.0) + jnp.log1p(jnp.exp(-jnp.abs(z)))
        g_all = neg_a * softplus
        gcum = _dot_f32(ones_tril, g_all)
        gcum_t = gcum.T
        q = [conv_silu(lanes(0, h, GDN_DK)) for h in heads]
        k = [conv_silu(lanes(QK_WIDTH, h, GDN_DK)) for h in heads]
        v = [conv_silu(lanes(2 * QK_WIDTH, h, GDN_DV)) for h in heads]
        q = [x * (lax.rsqrt(jnp.sum(x * x, axis=-1, keepdims=True) + EPS) * (GDN_DK ** -0.5))
             for x in q]
        k = [x * lax.rsqrt(jnp.sum(x * x, axis=-1, keepdims=True) + EPS) for x in k]
        gc = [jnp.broadcast_to(gcum[:, GDN_HEADS + h:GDN_HEADS + h + 1], (chunk, GDN_DK))
              for h in heads]
        gr = [gcum_t[GDN_HEADS + h:GDN_HEADS + h + 1, :] for h in heads]
        gl = [gcum[chunk - 1:chunk, GDN_HEADS + h:GDN_HEADS + h + 1] for h in heads]
        beta = [jnp.broadcast_to(beta_all[:, h:h + 1], (chunk, GDN_DK)) for h in heads]
        decay = [jnp.exp(jnp.where(tril, gc[h][:, :chunk] - gr[h], -jnp.inf)) for h in heads]
        eg = [jnp.exp(x) for x in gc]
        kb = [k[h] * beta[h] for h in heads]
        kkqk = [_dot_nt(jnp.concatenate([kb[h], q[h]], axis=0), k[h]) for h in heads]
        a = [jnp.where(strict, kkqk[h][:chunk] * decay[h], 0.0) for h in heads]
        aqk = [kkqk[h][chunk:] * decay[h] for h in heads]
        n = _neg_strict_parts_of_inverses(a, chunk)
        rhs = [jnp.concatenate([kb[h] * eg[h], v[h] * beta[h]], axis=-1) for h in heads]
        wu = [rhs[h] + _dot(n[h], rhs[h]) for h in heads]
        kd = [k[h] * jnp.exp(gl[h] - gc[h]) for h in heads]
        s = [s_ref[h] for h in heads]
        ws_qs = [_dot(jnp.concatenate([wu[h][:, :GDN_DK], q[h] * eg[h]], axis=0), s[h])
                 for h in heads]
        v_new = [wu[h][:, GDN_DK:] - ws_qs[h][:chunk] for h in heads]
        o = [ws_qs[h][chunk:] + _dot(aqk[h], v_new[h]) for h in heads]
        for h in heads:
            s_ref[h] = s[h] * jnp.exp(gl[h]) + _dot_tn(kd[h], v_new[h])
        for h in heads:
            oh = o[h]
            oh = oh * lax.rsqrt(jnp.mean(oh * oh, axis=-1, keepdims=True) + EPS) * og
            o_ref[rows, lanes(0, h, GDN_DV)] = oh.astype(o_ref.dtype)
        return 0

    lax.fori_loop(0, nchunks, chunk_step, 0)


def _gdn(qkv, ba, s0, hist, cw, alog_pad, dtb_pad, og, *, chunk, lt):
    bn, seq, _ = qkv.shape
    width, cc = cw.shape
    s_spec = pl.BlockSpec((None, GDN_HEADS, GDN_DK, GDN_DV), lambda b, l: (b, 0, 0, 0))
    hist_spec = pl.BlockSpec((1, width - 1, cc), lambda b, l: (b, 0, 0))
    vec = pl.BlockSpec((1, LANES), lambda b, l: (0, 0))
    return pl.pallas_call(
        functools.partial(_gdn_kernel, chunk=chunk, nchunks=lt // chunk),
        grid=(bn, seq // lt),
        in_specs=[
            pl.BlockSpec((None, lt, QKV_WIDTH), lambda b, l: (b, l, 0)),
            pl.BlockSpec((None, lt, BA_PAD), lambda b, l: (b, l, 0)),
            s_spec, hist_spec,
            pl.BlockSpec((width, cc), lambda b, l: (0, 0)),
            vec, vec, vec,
        ],
        out_specs=[
            pl.BlockSpec((None, lt, GDN_WIDTH), lambda b, l: (b, l, 0)),
            s_spec,
        ],
        out_shape=[
            jax.ShapeDtypeStruct((bn, seq, GDN_WIDTH), qkv.dtype),
            jax.ShapeDtypeStruct(s0.shape, F32),
        ],
        scratch_shapes=[pltpu.VMEM((1, SUBLANES, cc), F32)],
        compiler_params=_cparams("arbitrary", "arbitrary"),
        name="gdn",
    )(qkv, ba, s0, hist, cw, alog_pad, dtb_pad, og)


def _attn_out_kernel(tok_ref, xq_ref, gate_ref, x_ref, kt_ref, vt_ref, w_ref, fg_ref,
                     y_ref, xo_scr, *, bt, lt, final):
    lane_head = lax.broadcasted_iota(jnp.int32, (1, X_WIDTH), 1) // X_HEAD_DIM
    scale = X_HEAD_DIM ** -0.5

    def one_seq(b, carry):
        q = xq_ref[b]
        qx = jnp.concatenate([jnp.where(lane_head == h, q, jnp.zeros_like(q))
                              for h in range(X_HEADS)], axis=0)
        s = _dot(qx, kt_ref[b]) * scale
        e = jnp.exp(s - jnp.max(s, axis=-1, keepdims=True))
        p = e / jnp.sum(e, axis=-1, keepdims=True)
        o4 = _dot_nt(p, vt_ref[b])
        xo = jnp.where(lane_head == 0, o4[0:lt], 0.0)
        for h in range(1, X_HEADS):
            xo = xo + jnp.where(lane_head == h, o4[h * lt:(h + 1) * lt], 0.0)
        xo_scr[b] = xo
        return carry

    lax.fori_loop(0, bt, one_seq, 0)

    tm = bt * lt
    sg = _silu(gate_ref[...].astype(F32).reshape(tm, BRANCH_WIDTH))
    br_tok = tok_ref[...].astype(F32).reshape(tm, GDN_WIDTH) * sg[:, :GDN_WIDTH]
    br_x = xo_scr[...].reshape(tm, X_WIDTH) * sg[:, GDN_WIDTH:]
    y = (x_ref[...].reshape(tm, D_MODEL)
         + _dot(br_tok, w_ref[:GDN_WIDTH, :]) + _dot(br_x, w_ref[GDN_WIDTH:, :]))
    if final:
        y = _rms_rows(y, fg_ref[...])
    y_ref[...] = y.reshape(bt, lt, D_MODEL)


def _attn_out(tok, xq, gate, x, mk, mv, w_bf16, fg, *, layer, bt, lt, final):
    bn, seq, _ = x.shape

    def act(n):
        return pl.BlockSpec((bt, lt, n), lambda b, l: (b, l, 0))

    mem = pl.BlockSpec((None, bt, X_WIDTH, N_MEM), lambda b, l: (layer, b, 0, 0))
    return pl.pallas_call(
        functools.partial(_attn_out_kernel, bt=bt, lt=lt, final=final),
        grid=(bn // bt, seq // lt),
        in_specs=[
            act(GDN_WIDTH), act(X_WIDTH), act(BRANCH_WIDTH), act(D_MODEL), mem, mem,
            pl.BlockSpec((BRANCH_WIDTH, D_MODEL), lambda b, l: (0, 0)),
            pl.BlockSpec((1, D_MODEL), lambda b, l: (0, 0)),
        ],
        out_specs=act(D_MODEL),
        out_shape=jax.ShapeDtypeStruct(x.shape, F32),
        scratch_shapes=[pltpu.VMEM((bt, lt, X_WIDTH), F32)],
        compiler_params=_cparams("arbitrary", "arbitrary"),
        name="attn_out",
    )(tok, xq, gate, x, mk, mv, w_bf16, fg)


def _trunk(x, mem_k, mem_v, gdn_s, gdn_conv, sc_conv, p, *, bt, lt, chunk, gdn_lt, act_dtype):
    qkv, gate, xq, ba, gconv_new = _inproj(
        x, p["norm_g"][0:1], p["w_in_a"], None, p["conv_w_a"], mixer="gdn", bt=bt, lt=lt,
        act_dtype=act_dtype)
    tok, s_new = _gdn(qkv, ba, gdn_s, gdn_conv, p["conv_w_a"], p["alog_pad"],
                      p["dtb_pad"], p["o_norm_g"], chunk=chunk, lt=gdn_lt)
    x = _attn_out(tok, xq, gate, x, mem_k, mem_v, p["w_out"][0], p["final_norm_g"],
                  layer=0, bt=bt, lt=lt, final=False)
    tok, gate, xq, sconv_new = _inproj(
        x, p["norm_g"][1:2], p["w_in_b"], sc_conv, p["conv_w_b"], mixer="sconv", bt=bt, lt=lt,
        act_dtype=act_dtype)
    y = _attn_out(tok, xq, gate, x, mem_k, mem_v, p["w_out"][1], p["final_norm_g"],
                  layer=1, bt=bt, lt=lt, final=True)
    return y, s_new[None], gconv_new[None], sconv_new[None]


def kernel(x_prompt, x_sample, mem_prompt, state_gdn, state_gdn_conv, state_sconv, cache_mem_k, cache_mem_v, norm_g, w_in_a, conv_w_a, a_log, dt_bias, o_norm_g, w_in_b, conv_w_b, mem_norm_g, w_mem_kv, w_out, final_norm_g):
    bp = x_prompt.shape[0]

    wa = w_in_a[0]
    c_b = QKV_WIDTH
    c_g = QKV_WIDTH + 2 * GDN_HEADS
    wa = jnp.concatenate(
        [wa[:, :c_b], wa[:, c_g:], wa[:, c_b:c_g],
         jnp.zeros((D_MODEL, BA_PAD - 2 * GDN_HEADS), wa.dtype)], axis=1).astype(BF16)
    pad_lo = jnp.zeros((GDN_HEADS,), F32)
    pad_hi = jnp.zeros((LANES - 2 * GDN_HEADS,), F32)
    params = {
        "norm_g": norm_g,
        "w_in_a": wa,
        "conv_w_a": conv_w_a[0],
        "alog_pad": jnp.concatenate([pad_lo, a_log[0], pad_hi])[None],
        "dtb_pad": jnp.concatenate([pad_lo, dt_bias[0], pad_hi])[None],
        "o_norm_g": o_norm_g,
        "w_in_b": w_in_b[0].astype(BF16),
        "conv_w_b": conv_w_b[0],
        "w_out": w_out.astype(BF16),
        "final_norm_g": final_norm_g[None],
    }

    def to_cache(t):
        t = t.reshape(t.shape[0], t.shape[1], X_HEADS, X_HEAD_DIM, t.shape[3])
        return jnp.transpose(t, (0, 1, 4, 2, 3))

    def from_cache(t):
        t = jnp.transpose(t, (0, 1, 3, 4, 2))
        return t.reshape(t.shape[0], t.shape[1], X_WIDTH, t.shape[4])

    mem_kt, mem_vt = _memkv(mem_prompt, mem_norm_g[None], w_mem_kv.astype(BF16))
    mem_k_p = to_cache(mem_kt)
    mem_v_p = to_cache(mem_vt)

    s0_p = jnp.zeros((bp,) + state_gdn.shape[2:], F32)
    gc0_p = jnp.zeros((bp,) + state_gdn_conv.shape[2:], F32)
    sc0_p = jnp.zeros((bp,) + state_sconv.shape[2:], F32)
    y_p, s_p, gc_p, sc_p = _trunk(x_prompt, mem_kt, mem_vt, s0_p, gc0_p, sc0_p,
                                  params, bt=1, lt=256, chunk=GDN_CHUNK, gdn_lt=256,
                                  act_dtype=BF16)
    dec_seq = x_sample.shape[1]
    y_s, s_s, gc_s, sc_s = _trunk(x_sample, from_cache(cache_mem_k), from_cache(cache_mem_v),
                                  state_gdn[0], state_gdn_conv[0], state_sconv[0], params,
                                  bt=32, lt=dec_seq, chunk=dec_seq, gdn_lt=dec_seq,
                                  act_dtype=F32)
    return (y_p, y_s, s_p, gc_p, sc_p, mem_k_p, mem_v_p, s_s, gc_s, sc_s)
```

```python
import functools

import jax
import jax.numpy as jnp
from jax import lax
from jax.experimental import pallas as pl
from jax.experimental.pallas import tpu as pltpu

F32 = jnp.float32
BF16 = jnp.bfloat16

D_MODEL = 1024
N_MEM = 256
X_WIDTH = 256
X_HEADS = 4
X_HEAD_DIM = 64
GDN_HEADS = 6
GDN_DK = 128
GDN_DV = 128
GDN_WIDTH = GDN_HEADS * GDN_DV
QK_WIDTH = GDN_HEADS * GDN_DK
QKV_WIDTH = 2 * QK_WIDTH + GDN_WIDTH
GDN_CHUNK = 64
SC_WIDTH = 768
BRANCH_WIDTH = 1024
EPS = 1e-6

LANES = 128
SUBLANES = 8
COL_CHUNK = 256
BA_PAD = LANES
VMEM_LIMIT = 56 * 1024 * 1024

HIGHEST = lax.Precision.HIGHEST


def _cparams(*sem):
    return pltpu.CompilerParams(dimension_semantics=sem, vmem_limit_bytes=VMEM_LIMIT)


def _rms_rows(x, g):
    r = lax.rsqrt(jnp.mean(x * x, axis=-1, keepdims=True) + EPS)
    return x * r * g


def _silu(x):
    return x * (1.0 / (1.0 + jnp.exp(-x)))


def _dot(a, b):
    return jnp.dot(a.astype(BF16), b.astype(BF16), preferred_element_type=F32)


def _dot_nt(a, b):
    return lax.dot_general(a.astype(BF16), b.astype(BF16), (((1,), (1,)), ((), ())),
                           preferred_element_type=F32)


def _dot_tn(a, b):
    return lax.dot_general(a.astype(BF16), b.astype(BF16), (((0,), (0,)), ((), ())),
                           preferred_element_type=F32)


def _dot_f32(a, b):
    return jnp.dot(a, b, preferred_element_type=F32, precision=HIGHEST)


def _col_chunks(n):
    return [slice(c, min(c + COL_CHUNK, n)) for c in range(0, n, COL_CHUNK)]


def _memkv_kernel(m_ref, g_ref, w_ref, kt_ref, vt_ref):
    h = _rms_rows(m_ref[...], g_ref[...])
    kv = _dot(h, w_ref[...])
    kt_ref[...] = kv[:, :X_WIDTH].T
    vt_ref[...] = kv[:, X_WIDTH:].T


def _memkv(mem, g, w_bf16):
    bn, n_mem, _ = mem.shape
    depth = w_bf16.shape[0]
    out = jax.ShapeDtypeStruct((depth, bn, X_WIDTH, n_mem), F32)
    out_spec = pl.BlockSpec((None, None, X_WIDTH, n_mem), lambda l, b: (l, b, 0, 0))
    return pl.pallas_call(
        _memkv_kernel,
        grid=(depth, bn),
        in_specs=[
            pl.BlockSpec((None, n_mem, D_MODEL), lambda l, b: (b, 0, 0)),
            pl.BlockSpec((1, D_MODEL), lambda l, b: (0, 0)),
            pl.BlockSpec((None, D_MODEL, 2 * X_WIDTH), lambda l, b: (l, 0, 0)),
        ],
        out_specs=[out_spec, out_spec],
        out_shape=[out, out],
        compiler_params=_cparams("arbitrary", "arbitrary"),
        name="memkv",
    )(mem, g, w_bf16)


def _init_conv_carry(carry, hist_ref):
    @pl.when(pl.program_id(1) == 0)
    def _():
        carry[:, SUBLANES - hist_ref.shape[1]:, :] = hist_ref[...]


def _causal_conv_cols(x, cols, carry, hist_out, cw_ref, seqs=slice(None)):
    width = cw_ref.shape[0]
    bt, lt, n = x.shape
    g = lt // SUBLANES
    xe = jnp.concatenate([carry[seqs, :, cols], x], axis=1).reshape(bt, g + 1, SUBLANES, n)
    sub = lax.broadcasted_iota(jnp.int32, (1, 1, SUBLANES, n), 2)
    y = x.reshape(bt, g, SUBLANES, n) * cw_ref[width - 1:width, cols]
    for j in range(width - 1):
        s = width - 1 - j
        r = pltpu.roll(xe, s, axis=2)
        y = y + jnp.where(sub < s, r[:, :g], r[:, 1:]) * cw_ref[j:j + 1, cols]
    carry[seqs, :, cols] = x[:, lt - SUBLANES:, :]
    if hist_out is not None:
        hist_out[seqs, :, cols] = x[:, lt - (width - 1):, :]
    return y.reshape(bt, lt, n)


def _inproj_gdn_kernel(x_ref, g_ref, w_ref, qkv_out, gate_out, xq_out, ba_out, hist_out,
                       *, bt, lt):
    tm = bt * lt
    h = _rms_rows(x_ref[...].reshape(tm, D_MODEL), g_ref[...]).astype(BF16)
    n_hist = hist_out.shape[1]
    base = 0
    for out in (qkv_out, gate_out, xq_out, ba_out):
        n = out.shape[-1]
        for cols in _col_chunks(n):
            wcols = slice(base + cols.start, base + cols.stop)
            y = jnp.dot(h, w_ref[:, wcols], preferred_element_type=F32).reshape(bt, lt, -1)
            out[:, :, cols] = y.astype(out.dtype)
            if out is qkv_out:
                hist_out[:, :, cols] = y[:, lt - n_hist:, :]
        base += n


def _inproj_sconv_kernel(x_ref, g_ref, w_ref, hist_ref, cw_ref,
                         tok_out, gate_out, xq_out, hist_out, carry, *, bt, lt):
    tm = bt * lt
    h = _rms_rows(x_ref[...].reshape(tm, D_MODEL), g_ref[...]).astype(BF16)
    _init_conv_carry(carry, hist_ref)

    def proj(cols, base):
        cols = slice(base + cols.start, base + cols.stop)
        return jnp.dot(h, w_ref[:, cols], preferred_element_type=F32).reshape(bt, lt, -1)

    for cols in _col_chunks(SC_WIDTH):
        cx = proj(cols, SC_WIDTH) * proj(cols, 2 * SC_WIDTH)
        y = _causal_conv_cols(cx, cols, carry, hist_out, cw_ref)
        tok_out[:, :, cols] = (proj(cols, 0) * y).astype(tok_out.dtype)
    base = 3 * SC_WIDTH
    for out in (gate_out, xq_out):
        n = out.shape[-1]
        for cols in _col_chunks(n):
            out[:, :, cols] = proj(cols, base).astype(out.dtype)
        base += n


def _inproj(x, g, w_bf16, hist, cw, *, mixer, bt, lt, act_dtype):
    bn, seq, _ = x.shape
    ncols = w_bf16.shape[1]
    width, cc = cw.shape

    def act(n, dtype=act_dtype):
        return (pl.BlockSpec((bt, lt, n), lambda b, l: (b, l, 0)),
                jax.ShapeDtypeStruct((bn, seq, n), dtype))

    hist_spec = pl.BlockSpec((bt, width - 1, cc), lambda b, l: (b, 0, 0))
    hist_out = (hist_spec, jax.ShapeDtypeStruct((bn, width - 1, cc), F32))
    in_specs = [
        pl.BlockSpec((bt, lt, D_MODEL), lambda b, l: (b, l, 0)),
        pl.BlockSpec((1, D_MODEL), lambda b, l: (0, 0)),
        pl.BlockSpec((D_MODEL, ncols), lambda b, l: (0, 0)),
    ]
    if mixer == "gdn":
        body = functools.partial(_inproj_gdn_kernel, bt=bt, lt=lt)
        args = (x, g, w_bf16)
        outs = [act(QKV_WIDTH), act(BRANCH_WIDTH), act(X_WIDTH), act(BA_PAD, F32), hist_out]
        scratch = []
    else:
        body = functools.partial(_inproj_sconv_kernel, bt=bt, lt=lt)
        args = (x, g, w_bf16, hist, cw)
        in_specs += [hist_spec, pl.BlockSpec((width, cc), lambda b, l: (0, 0))]
        outs = [act(SC_WIDTH), act(BRANCH_WIDTH), act(X_WIDTH), hist_out]
        scratch = [pltpu.VMEM((bt, SUBLANES, cc), F32)]
    return pl.pallas_call(
        body,
        grid=(bn // bt, seq // lt),
        in_specs=in_specs,
        out_specs=[o[0] for o in outs],
        out_shape=[o[1] for o in outs],
        scratch_shapes=scratch,
        compiler_params=_cparams("arbitrary", "arbitrary"),
        name="inproj_" + mixer,
    )(*args)


UNITS_PER_WAVE = 2


def _round_robin(gens):
    gens = list(gens)
    while gens:
        alive = []
        for g in gens:
            try:
                next(g)
                alive.append(g)
            except StopIteration:
                pass
        gens = alive


def _gdn_kernel(qkv_ref, ba_ref, s0_ref, hist_ref, cw_ref, alog_ref, dtb_ref, og_ref,
                o_ref, s_ref, carry, *, chunk, group):
    bt, lt, _ = qkv_ref.shape
    nchunks = lt // chunk

    @pl.when(pl.program_id(1) == 0)
    def _():
        s_ref[...] = s0_ref[...]

    _init_conv_carry(carry, hist_ref)

    heads = range(GDN_HEADS)
    groups = [tuple(range(g, g + group)) for g in range(0, GDN_HEADS, group)]
    gw = group * chunk
    row = lax.broadcasted_iota(jnp.int32, (chunk, gw), 0)
    col = lax.broadcasted_iota(jnp.int32, (chunk, gw), 1) % chunk
    tril = row >= col
    strict = row > col
    lane_blk = lax.broadcasted_iota(jnp.int32, (1, gw), 1) // chunk
    ones_tril = (lax.broadcasted_iota(jnp.int32, (chunk, chunk), 0)
                 >= lax.broadcasted_iota(jnp.int32, (chunk, chunk), 1)).astype(F32)
    neg_a = -jnp.exp(alog_ref[...])
    dtb = dtb_ref[...]
    og = og_ref[...]

    def rows(c):
        return slice(c * chunk, (c + 1) * chunk)

    def lanes(base, h, n):
        return slice(base + h * n, base + (h + 1) * n)

    def cat(xs, axis):
        return xs[0] if len(xs) == 1 else jnp.concatenate(xs, axis=axis)

    def pick(xs):
        out = xs[0]
        for j in range(1, len(xs)):
            out = jnp.where(lane_blk == j, xs[j], out)
        return out

    def blockdiag(xs):
        z = jnp.zeros_like(xs[0])
        return cat([cat([x if i == j else z for i in range(len(xs))], 1)
                    for j, x in enumerate(xs)], 0)

    def blockdiag_of_lane_blocks(x):
        if group == 1:
            return x
        return cat([jnp.where(lane_blk == j, x, 0.0) for j in range(group)], 0)

    def conv_silu(u, cols):
        b, c = u
        x = qkv_ref[b, rows(c), cols].astype(F32)
        y = _causal_conv_cols(x[None], cols, carry, None, cw_ref, seqs=slice(b, b + 1))
        return _silu(y[0])

    def phase1(units, res):
        uh = [(u, h) for u in units for h in heads]
        ug = [(u, g) for u in units for g in range(len(groups))]
        gcum, gcum_t, beta_all = {}, {}, {}
        for u in units:
            ba = ba_ref[u[0], rows(u[1]), :]
            beta_all[u] = 1.0 / (1.0 + jnp.exp(-ba))
            z = ba + dtb
            softplus = jnp.maximum(z, 0.0) + jnp.log1p(jnp.exp(-jnp.abs(z)))
            gcum[u] = _dot_f32(ones_tril, neg_a * softplus)
        q, k, v = {}, {}, {}
        for u in units:
            for h in heads:
                q[u, h] = conv_silu(u, lanes(0, h, GDN_DK))
                k[u, h] = conv_silu(u, lanes(QK_WIDTH, h, GDN_DK))
                v[u, h] = conv_silu(u, lanes(2 * QK_WIDTH, h, GDN_DV))
            yield
        for key in uh:
            x = q[key]
            q[key] = x * (lax.rsqrt(jnp.sum(x * x, axis=-1, keepdims=True) + EPS)
                          * (GDN_DK ** -0.5))
            x = k[key]
            k[key] = x * lax.rsqrt(jnp.sum(x * x, axis=-1, keepdims=True) + EPS)
        yield
        for u in units:
            gcum_t[u] = cat([gcum[u]] * group, 0).T
        gc = {(u, h): jnp.broadcast_to(gcum[u][:, GDN_HEADS + h:GDN_HEADS + h + 1],
                                       (chunk, GDN_DK)) for u, h in uh}
        gl = {(u, h): gcum[u][chunk - 1:chunk, GDN_HEADS + h:GDN_HEADS + h + 1]
              for u, h in uh}
        beta = {(u, h): jnp.broadcast_to(beta_all[u][:, h:h + 1], (chunk, GDN_DK))
                for u, h in uh}
        eg = {key: jnp.exp(gc[key]) for key in uh}
        kb = {key: k[key] * beta[key] for key in uh}
        kkqk = {}
        for u, g in ug:
            grp = groups[g]
            kkqk[u, g] = _dot_nt(
                cat([cat([kb[u, h] for h in grp], 1), cat([q[u, h] for h in grp], 1)], 0),
                blockdiag([k[u, h] for h in grp]))
        yield
        n, p, aqk = {}, {}, {}
        for u, g in ug:
            grp = groups[g]
            g_col = pick([gc[u, h][:, :gw] for h in grp])
            g_row = pick([gcum_t[u][GDN_HEADS + h:GDN_HEADS + h + 1, :] for h in grp])
            decay = jnp.exp(jnp.where(tril, g_col - g_row, -jnp.inf))
            a = jnp.where(strict, kkqk[u, g][:chunk] * decay, 0.0)
            aqk[u, g] = (kkqk[u, g][chunk:] * decay).astype(BF16)
            n[u, g] = -a
            p[u, g] = _dot(a, blockdiag_of_lane_blocks(a))
        yield
        span = 2
        while 2 * span < chunk:
            for key in ug:
                both = _dot(cat([n[key], p[key]], 0), blockdiag_of_lane_blocks(p[key]))
                n[key] = n[key] + p[key] + both[:chunk]
                p[key] = both[chunk:]
            span *= 2
            yield
        for key in ug:
            n[key] = n[key] + p[key] + _dot(n[key], blockdiag_of_lane_blocks(p[key]))
        yield
        wu = {}
        for u, h in uh:
            rhs = jnp.concatenate([kb[u, h] * eg[u, h], v[u, h] * beta[u, h]], axis=-1)
            wu[u, h] = rhs + _dot(n[u, h // group][:, lanes(0, h % group, chunk)], rhs)
        yield
        for u in units:
            res[u] = dict(
                w_qd=[jnp.concatenate([wu[u, h][:, :GDN_DK], q[u, h] * eg[u, h]],
                                      axis=0).astype(BF16) for h in heads],
                u=[wu[u, h][:, GDN_DK:] for h in heads],
                kd=[(k[u, h] * jnp.exp(gl[u, h] - gc[u, h])).astype(BF16) for h in heads],
                aqk=[aqk[u, g] for g in range(len(groups))],
                egl=[jnp.exp(gl[u, h]) for h in heads])
        yield

    def phase2(units, res):
        for c in sorted({c for _, c in units}):
            us = [u for u in units if u[1] == c]
            uh = [(u, h) for u in us for h in heads]
            s = {(u, h): s_ref[u[0], h] for u, h in uh}
            ws_qs = {(u, h): _dot(res[u]["w_qd"][h], s[u, h]) for u, h in uh}
            yield
            v_new = {(u, h): res[u]["u"][h] - ws_qs[u, h][:chunk] for u, h in uh}
            o_grp = {(u, g): _dot(res[u]["aqk"][g], blockdiag([v_new[u, h] for h in grp]))
                     for u in us for g, grp in enumerate(groups)}
            for u, h in uh:
                s_ref[u[0], h] = (s[u, h] * res[u]["egl"][h]
                                  + _dot_tn(res[u]["kd"][h], v_new[u, h]))
            yield
            for u, h in uh:
                oh = ws_qs[u, h][chunk:] + o_grp[u, h // group][:, lanes(0, h % group, GDN_DV)]
                oh = oh * lax.rsqrt(jnp.mean(oh * oh, axis=-1, keepdims=True) + EPS) * og
                o_ref[u[0], rows(c), lanes(0, h, GDN_DV)] = oh.astype(o_ref.dtype)
            yield

    units = [(b, c) for b in range(bt) for c in range(nchunks)]
    waves = [units[i:i + UNITS_PER_WAVE] for i in range(0, len(units), UNITS_PER_WAVE)]
    res = {}
    prev = None
    for wave in waves:
        gens = [phase1(wave, res)]
        if prev is not None:
            gens.append(phase2(prev, res))
        _round_robin(gens)
        prev = wave
    _round_robin([phase2(prev, res)])


def _gdn(qkv, ba, s0, hist, cw, alog_pad, dtb_pad, og, *, chunk, bt, lt):
    group = 2 if 2 * chunk == LANES else 1
    bn, seq, _ = qkv.shape
    width, cc = cw.shape
    s_spec = pl.BlockSpec((bt, GDN_HEADS, GDN_DK, GDN_DV), lambda b, l: (b, 0, 0, 0))
    hist_spec = pl.BlockSpec((bt, width - 1, cc), lambda b, l: (b, 0, 0))
    vec = pl.BlockSpec((1, LANES), lambda b, l: (0, 0))
    return pl.pallas_call(
        functools.partial(_gdn_kernel, chunk=chunk, group=group),
        grid=(bn // bt, seq // lt),
        in_specs=[
            pl.BlockSpec((bt, lt, QKV_WIDTH), lambda b, l: (b, l, 0)),
            pl.BlockSpec((bt, lt, BA_PAD), lambda b, l: (b, l, 0)),
            s_spec, hist_spec,
            pl.BlockSpec((width, cc), lambda b, l: (0, 0)),
            vec, vec, vec,
        ],
        out_specs=[
            pl.BlockSpec((bt, lt, GDN_WIDTH), lambda b, l: (b, l, 0)),
            s_spec,
        ],
        out_shape=[
            jax.ShapeDtypeStruct((bn, seq, GDN_WIDTH), qkv.dtype),
            jax.ShapeDtypeStruct(s0.shape, F32),
        ],
        scratch_shapes=[pltpu.VMEM((bt, SUBLANES, cc), F32)],
        compiler_params=_cparams("arbitrary", "arbitrary"),
        name="gdn",
    )(qkv, ba, s0, hist, cw, alog_pad, dtb_pad, og)


def _attn_out_kernel(tok_ref, xq_ref, gate_ref, x_ref, kt_ref, vt_ref, w_ref, fg_ref,
                     y_ref, xo_scr, *, bt, lt, final):
    lane_head = lax.broadcasted_iota(jnp.int32, (1, X_WIDTH), 1) // X_HEAD_DIM
    scale = X_HEAD_DIM ** -0.5

    def one_seq(b, carry):
        q = xq_ref[b]
        qx = jnp.concatenate([jnp.where(lane_head == h, q, jnp.zeros_like(q))
                              for h in range(X_HEADS)], axis=0)
        s = _dot(qx, kt_ref[b]) * scale
        e = jnp.exp(s - jnp.max(s, axis=-1, keepdims=True))
        p = e / jnp.sum(e, axis=-1, keepdims=True)
        o4 = _dot_nt(p, vt_ref[b])
        xo = jnp.where(lane_head == 0, o4[0:lt], 0.0)
        for h in range(1, X_HEADS):
            xo = xo + jnp.where(lane_head == h, o4[h * lt:(h + 1) * lt], 0.0)
        xo_scr[b] = xo
        return carry

    lax.fori_loop(0, bt, one_seq, 0)

    tm = bt * lt
    sg = _silu(gate_ref[...].astype(F32).reshape(tm, BRANCH_WIDTH))
    br_tok = tok_ref[...].astype(F32).reshape(tm, GDN_WIDTH) * sg[:, :GDN_WIDTH]
    br_x = xo_scr[...].reshape(tm, X_WIDTH) * sg[:, GDN_WIDTH:]
    y = (x_ref[...].reshape(tm, D_MODEL)
         + _dot(br_tok, w_ref[:GDN_WIDTH, :]) + _dot(br_x, w_ref[GDN_WIDTH:, :]))
    if final:
        y = _rms_rows(y, fg_ref[...])
    y_ref[...] = y.reshape(bt, lt, D_MODEL)


def _attn_out(tok, xq, gate, x, mk, mv, w_bf16, fg, *, layer, bt, lt, final):
    bn, seq, _ = x.shape

    def act(n):
        return pl.BlockSpec((bt, lt, n), lambda b, l: (b, l, 0))

    mem = pl.BlockSpec((None, bt, X_WIDTH, N_MEM), lambda b, l: (layer, b, 0, 0))
    return pl.pallas_call(
        functools.partial(_attn_out_kernel, bt=bt, lt=lt, final=final),
        grid=(bn // bt, seq // lt),
        in_specs=[
            act(GDN_WIDTH), act(X_WIDTH), act(BRANCH_WIDTH), act(D_MODEL), mem, mem,
            pl.BlockSpec((BRANCH_WIDTH, D_MODEL), lambda b, l: (0, 0)),
            pl.BlockSpec((1, D_MODEL), lambda b, l: (0, 0)),
        ],
        out_specs=act(D_MODEL),
        out_shape=jax.ShapeDtypeStruct(x.shape, F32),
        scratch_shapes=[pltpu.VMEM((bt, lt, X_WIDTH), F32)],
        compiler_params=_cparams("arbitrary", "arbitrary"),
        name="attn_out",
    )(tok, xq, gate, x, mk, mv, w_bf16, fg)


def _trunk(x, mem_k, mem_v, gdn_s, gdn_conv, sc_conv, p, *, bt, lt, chunk, gdn_bt, gdn_lt,
           act_dtype):
    qkv, gate, xq, ba, gconv_new = _inproj(
        x, p["norm_g"][0:1], p["w_in_a"], None, p["conv_w_a"], mixer="gdn", bt=bt, lt=lt,
        act_dtype=act_dtype)
    tok, s_new = _gdn(qkv, ba, gdn_s, gdn_conv, p["conv_w_a"], p["alog_pad"],
                      p["dtb_pad"], p["o_norm_g"], chunk=chunk, bt=gdn_bt, lt=gdn_lt)
    x = _attn_out(tok, xq, gate, x, mem_k, mem_v, p["w_out"][0], p["final_norm_g"],
                  layer=0, bt=bt, lt=lt, final=False)
    tok, gate, xq, sconv_new = _inproj(
        x, p["norm_g"][1:2], p["w_in_b"], sc_conv, p["conv_w_b"], mixer="sconv", bt=bt, lt=lt,
        act_dtype=act_dtype)
    y = _attn_out(tok, xq, gate, x, mem_k, mem_v, p["w_out"][1], p["final_norm_g"],
                  layer=1, bt=bt, lt=lt, final=True)
    return y, s_new[None], gconv_new[None], sconv_new[None]


def kernel(x_prompt, x_sample, mem_prompt, state_gdn, state_gdn_conv, state_sconv, cache_mem_k, cache_mem_v, norm_g, w_in_a, conv_w_a, a_log, dt_bias, o_norm_g, w_in_b, conv_w_b, mem_norm_g, w_mem_kv, w_out, final_norm_g):
    bp = x_prompt.shape[0]

    wa = w_in_a[0]
    c_b = QKV_WIDTH
    c_g = QKV_WIDTH + 2 * GDN_HEADS
    wa = jnp.concatenate(
        [wa[:, :c_b], wa[:, c_g:], wa[:, c_b:c_g],
         jnp.zeros((D_MODEL, BA_PAD - 2 * GDN_HEADS), wa.dtype)], axis=1).astype(BF16)
    pad_lo = jnp.zeros((GDN_HEADS,), F32)
    pad_hi = jnp.zeros((LANES - 2 * GDN_HEADS,), F32)
    params = {
        "norm_g": norm_g,
        "w_in_a": wa,
        "conv_w_a": conv_w_a[0],
        "alog_pad": jnp.concatenate([pad_lo, a_log[0], pad_hi])[None],
        "dtb_pad": jnp.concatenate([pad_lo, dt_bias[0], pad_hi])[None],
        "o_norm_g": o_norm_g,
        "w_in_b": w_in_b[0].astype(BF16),
        "conv_w_b": conv_w_b[0],
        "w_out": w_out.astype(BF16),
        "final_norm_g": final_norm_g[None],
    }

    def to_cache(t):
        t = t.reshape(t.shape[0], t.shape[1], X_HEADS, X_HEAD_DIM, t.shape[3])
        return jnp.transpose(t, (0, 1, 4, 2, 3))

    def from_cache(t):
        t = jnp.transpose(t, (0, 1, 3, 4, 2))
        return t.reshape(t.shape[0], t.shape[1], X_WIDTH, t.shape[4])

    mem_kt, mem_vt = _memkv(mem_prompt, mem_norm_g[None], w_mem_kv.astype(BF16))
    mem_k_p = to_cache(mem_kt)
    mem_v_p = to_cache(mem_vt)

    s0_p = jnp.zeros((bp,) + state_gdn.shape[2:], F32)
    gc0_p = jnp.zeros((bp,) + state_gdn_conv.shape[2:], F32)
    sc0_p = jnp.zeros((bp,) + state_sconv.shape[2:], F32)
    y_p, s_p, gc_p, sc_p = _trunk(x_prompt, mem_kt, mem_vt, s0_p, gc0_p, sc0_p,
                                  params, bt=1, lt=256, chunk=GDN_CHUNK, gdn_bt=1, gdn_lt=256,
                                  act_dtype=BF16)
    dec_seq = x_sample.shape[1]
    y_s, s_s, gc_s, sc_s = _trunk(x_sample, from_cache(cache_mem_k), from_cache(cache_mem_v),
                                  state_gdn[0], state_gdn_conv[0], state_sconv[0], params,
                                  bt=32, lt=dec_seq, chunk=dec_seq, gdn_bt=4, gdn_lt=dec_seq,
                                  act_dtype=F32)
    return (y_p, y_s, s_p, gc_p, sc_p, mem_k_p, mem_v_p, s_s, gc_s, sc_s)
```

```python
import functools

import jax
import jax.numpy as jnp
from jax import lax
from jax.experimental import pallas as pl
from jax.experimental.pallas import tpu as pltpu

F32 = jnp.float32
BF16 = jnp.bfloat16

D_MODEL = 1024
N_MEM = 256
X_WIDTH = 256
X_HEADS = 4
X_HEAD_DIM = 64
GDN_HEADS = 6
GDN_DK = 128
GDN_DV = 128
GDN_WIDTH = GDN_HEADS * GDN_DV
QK_WIDTH = GDN_HEADS * GDN_DK
QKV_WIDTH = 2 * QK_WIDTH + GDN_WIDTH
GDN_CHUNK = 64
SC_WIDTH = 768
BRANCH_WIDTH = 1024
EPS = 1e-6

LANES = 128
SUBLANES = 8
COL_CHUNK = 256
BA_PAD = LANES
VMEM_LIMIT = 56 * 1024 * 1024

HIGHEST = lax.Precision.HIGHEST


def _cparams(*sem):
    return pltpu.CompilerParams(dimension_semantics=sem, vmem_limit_bytes=VMEM_LIMIT)


def _rms_rows(x, g):
    r = lax.rsqrt(jnp.mean(x * x, axis=-1, keepdims=True) + EPS)
    return x * r * g


def _silu(x):
    h = 0.5 * x
    return h + h * jnp.tanh(h)


def _dot(a, b):
    return jnp.dot(a.astype(BF16), b.astype(BF16), preferred_element_type=F32)


def _dot_nt(a, b):
    return lax.dot_general(a.astype(BF16), b.astype(BF16), (((1,), (1,)), ((), ())),
                           preferred_element_type=F32)


def _dot_tn(a, b):
    return lax.dot_general(a.astype(BF16), b.astype(BF16), (((0,), (0,)), ((), ())),
                           preferred_element_type=F32)


def _dot_f32(a, b):
    return jnp.dot(a, b, preferred_element_type=F32, precision=HIGHEST)


def _col_chunks(n):
    return [slice(c, min(c + COL_CHUNK, n)) for c in range(0, n, COL_CHUNK)]


def _round_robin(gens):
    gens = list(gens)
    while gens:
        alive = []
        for g in gens:
            try:
                next(g)
                alive.append(g)
            except StopIteration:
                pass
        gens = alive


def _delayed(gen, rounds):
    for _ in range(rounds):
        yield
    yield from gen


def _memkv_kernel(m_ref, g_ref, w_ref, kt_ref, vt_ref):
    h = _rms_rows(m_ref[...], g_ref[...])
    kv = _dot(h, w_ref[...])
    kt_ref[...] = kv[:, :X_WIDTH].T
    vt_ref[...] = kv[:, X_WIDTH:].T


def _memkv(mem, g, w_bf16):
    bn, n_mem, _ = mem.shape
    depth = w_bf16.shape[0]
    out = jax.ShapeDtypeStruct((depth, bn, X_WIDTH, n_mem), F32)
    out_spec = pl.BlockSpec((None, None, X_WIDTH, n_mem), lambda l, b: (l, b, 0, 0))
    return pl.pallas_call(
        _memkv_kernel,
        grid=(depth, bn),
        in_specs=[
            pl.BlockSpec((None, n_mem, D_MODEL), lambda l, b: (b, 0, 0)),
            pl.BlockSpec((1, D_MODEL), lambda l, b: (0, 0)),
            pl.BlockSpec((None, D_MODEL, 2 * X_WIDTH), lambda l, b: (l, 0, 0)),
        ],
        out_specs=[out_spec, out_spec],
        out_shape=[out, out],
        compiler_params=_cparams("arbitrary", "arbitrary"),
        name="memkv",
    )(mem, g, w_bf16)


def _init_conv_carry(carry, hist_ref):
    @pl.when(pl.program_id(1) == 0)
    def _():
        carry[:, SUBLANES - hist_ref.shape[1]:, :] = hist_ref[...]


def _causal_conv_cols(x, cols, carry, hist_out, cw_ref, seqs=slice(None)):
    width = cw_ref.shape[0]
    bt, lt, n = x.shape
    g = lt // SUBLANES
    xe = jnp.concatenate([carry[seqs, :, cols], x], axis=1).reshape(bt, g + 1, SUBLANES, n)
    sub = lax.broadcasted_iota(jnp.int32, (1, 1, SUBLANES, n), 2)
    y = x.reshape(bt, g, SUBLANES, n) * cw_ref[width - 1:width, cols]
    for j in range(width - 1):
        s = width - 1 - j
        r = pltpu.roll(xe, s, axis=2)
        y = y + jnp.where(sub < s, r[:, :g], r[:, 1:]) * cw_ref[j:j + 1, cols]
    carry[seqs, :, cols] = x[:, lt - SUBLANES:, :]
    if hist_out is not None:
        hist_out[seqs, :, cols] = x[:, lt - (width - 1):, :]
    return y.reshape(bt, lt, n)


def _inproj_gdn_kernel(x_ref, g_ref, w_ref, qkv_out, gate_out, xq_out, ba_out, hist_out,
                       *, bt, lt):
    tm = bt * lt
    h = _rms_rows(x_ref[...].reshape(tm, D_MODEL), g_ref[...]).astype(BF16)
    n_hist = hist_out.shape[1]
    base = 0
    for out in (qkv_out, gate_out, xq_out, ba_out):
        n = out.shape[-1]
        for cols in _col_chunks(n):
            wcols = slice(base + cols.start, base + cols.stop)
            y = jnp.dot(h, w_ref[:, wcols], preferred_element_type=F32).reshape(bt, lt, -1)
            out[:, :, cols] = y.astype(out.dtype)
            if out is qkv_out:
                hist_out[:, :, cols] = y[:, lt - n_hist:, :]
        base += n


def _inproj_sconv_kernel(x_ref, g_ref, w_ref, hist_ref, cw_ref,
                         tok_out, gate_out, xq_out, hist_out, carry, *, bt, lt):
    tm = bt * lt
    h = _rms_rows(x_ref[...].reshape(tm, D_MODEL), g_ref[...]).astype(BF16)
    _init_conv_carry(carry, hist_ref)

    def proj(cols, base):
        cols = slice(base + cols.start, base + cols.stop)
        return jnp.dot(h, w_ref[:, cols], preferred_element_type=F32).reshape(bt, lt, -1)

    for cols in _col_chunks(SC_WIDTH):
        cx = proj(cols, SC_WIDTH) * proj(cols, 2 * SC_WIDTH)
        y = _causal_conv_cols(cx, cols, carry, hist_out, cw_ref)
        tok_out[:, :, cols] = (proj(cols, 0) * y).astype(tok_out.dtype)
    base = 3 * SC_WIDTH
    for out in (gate_out, xq_out):
        n = out.shape[-1]
        for cols in _col_chunks(n):
            out[:, :, cols] = proj(cols, base).astype(out.dtype)
        base += n


def _inproj(x, g, w_bf16, hist, cw, *, mixer, bt, lt, act_dtype):
    bn, seq, _ = x.shape
    ncols = w_bf16.shape[1]
    width, cc = cw.shape

    def act(n, dtype=act_dtype):
        return (pl.BlockSpec((bt, lt, n), lambda b, l: (b, l, 0)),
                jax.ShapeDtypeStruct((bn, seq, n), dtype))

    hist_spec = pl.BlockSpec((bt, width - 1, cc), lambda b, l: (b, 0, 0))
    hist_out = (hist_spec, jax.ShapeDtypeStruct((bn, width - 1, cc), F32))
    in_specs = [
        pl.BlockSpec((bt, lt, D_MODEL), lambda b, l: (b, l, 0)),
        pl.BlockSpec((1, D_MODEL), lambda b, l: (0, 0)),
        pl.BlockSpec((D_MODEL, ncols), lambda b, l: (0, 0)),
    ]
    if mixer == "gdn":
        body = functools.partial(_inproj_gdn_kernel, bt=bt, lt=lt)
        args = (x, g, w_bf16)
        outs = [act(QKV_WIDTH), act(BRANCH_WIDTH), act(X_WIDTH), act(BA_PAD, F32), hist_out]
        scratch = []
    else:
        body = functools.partial(_inproj_sconv_kernel, bt=bt, lt=lt)
        args = (x, g, w_bf16, hist, cw)
        in_specs += [hist_spec, pl.BlockSpec((width, cc), lambda b, l: (0, 0))]
        outs = [act(SC_WIDTH), act(BRANCH_WIDTH), act(X_WIDTH), hist_out]
        scratch = [pltpu.VMEM((bt, SUBLANES, cc), F32)]
    return pl.pallas_call(
        body,
        grid=(bn // bt, seq // lt),
        in_specs=in_specs,
        out_specs=[o[0] for o in outs],
        out_shape=[o[1] for o in outs],
        scratch_shapes=scratch,
        compiler_params=_cparams("arbitrary", "arbitrary"),
        name="inproj_" + mixer,
    )(*args)


UNITS_PER_WAVE = 2


def _gdn_kernel(qkv_ref, ba_ref, s0_ref, hist_ref, cw_ref, alog_ref, dtb_ref, og_ref,
                o_ref, s_ref, carry, *, chunk, group):
    bt, lt, _ = qkv_ref.shape
    nchunks = lt // chunk

    @pl.when(pl.program_id(1) == 0)
    def _():
        s_ref[...] = s0_ref[...]

    _init_conv_carry(carry, hist_ref)

    heads = range(GDN_HEADS)
    groups = [tuple(range(g, g + group)) for g in range(0, GDN_HEADS, group)]
    gw = group * chunk
    row = lax.broadcasted_iota(jnp.int32, (chunk, gw), 0)
    col = lax.broadcasted_iota(jnp.int32, (chunk, gw), 1) % chunk
    tril = row >= col
    strict = row > col
    lane_blk = lax.broadcasted_iota(jnp.int32, (1, gw), 1) // chunk
    ones_tril = (lax.broadcasted_iota(jnp.int32, (chunk, chunk), 0)
                 >= lax.broadcasted_iota(jnp.int32, (chunk, chunk), 1)).astype(F32)
    neg_a = -jnp.exp(alog_ref[...])
    dtb = dtb_ref[...]
    og = og_ref[...]

    def rows(c):
        return slice(c * chunk, (c + 1) * chunk)

    def lanes(base, h, n):
        return slice(base + h * n, base + (h + 1) * n)

    def cat(xs, axis):
        return xs[0] if len(xs) == 1 else jnp.concatenate(xs, axis=axis)

    def pick(xs):
        out = xs[0]
        for j in range(1, len(xs)):
            out = jnp.where(lane_blk == j, xs[j], out)
        return out

    def blockdiag(xs):
        z = jnp.zeros_like(xs[0])
        return cat([cat([x if i == j else z for i in range(len(xs))], 1)
                    for j, x in enumerate(xs)], 0)

    def blockdiag_of_lane_blocks(x):
        if group == 1:
            return x
        return cat([jnp.where(lane_blk == j, x, 0.0) for j in range(group)], 0)

    def conv_silu(u, cols):
        b, c = u
        x = qkv_ref[b, rows(c), cols].astype(F32)
        y = _causal_conv_cols(x[None], cols, carry, None, cw_ref, seqs=slice(b, b + 1))
        return _silu(y[0])

    def phase1(units, res):
        uh = [(u, h) for u in units for h in heads]
        ug = [(u, g) for u in units for g in range(len(groups))]
        gcum, gcum_t, beta_all = {}, {}, {}
        for u in units:
            ba = ba_ref[u[0], rows(u[1]), :]
            beta_all[u] = 1.0 / (1.0 + jnp.exp(-ba))
            z = ba + dtb
            softplus = jnp.maximum(z, 0.0) + jnp.log1p(jnp.exp(-jnp.abs(z)))
            gcum[u] = _dot_f32(ones_tril, neg_a * softplus)
        q, k, v = {}, {}, {}
        for u in units:
            for h in heads:
                q[u, h] = conv_silu(u, lanes(0, h, GDN_DK))
                k[u, h] = conv_silu(u, lanes(QK_WIDTH, h, GDN_DK))
                v[u, h] = conv_silu(u, lanes(2 * QK_WIDTH, h, GDN_DV))
            yield
        for key in uh:
            x = q[key]
            q[key] = x * (lax.rsqrt(jnp.sum(x * x, axis=-1, keepdims=True) + EPS)
                          * (GDN_DK ** -0.5))
            x = k[key]
            k[key] = x * lax.rsqrt(jnp.sum(x * x, axis=-1, keepdims=True) + EPS)
        yield
        for u in units:
            gcum_t[u] = cat([gcum[u]] * group, 0).T
        gc = {(u, h): jnp.broadcast_to(gcum[u][:, GDN_HEADS + h:GDN_HEADS + h + 1],
                                       (chunk, GDN_DK)) for u, h in uh}
        gl = {(u, h): gcum[u][chunk - 1:chunk, GDN_HEADS + h:GDN_HEADS + h + 1]
              for u, h in uh}
        beta = {(u, h): jnp.broadcast_to(beta_all[u][:, h:h + 1], (chunk, GDN_DK))
                for u, h in uh}
        eg = {key: jnp.exp(gc[key]) for key in uh}
        kb = {key: k[key] * beta[key] for key in uh}
        kkqk = {}
        for u, g in ug:
            grp = groups[g]
            kkqk[u, g] = _dot_nt(
                cat([cat([kb[u, h] for h in grp], 1), cat([q[u, h] for h in grp], 1)], 0),
                blockdiag([k[u, h] for h in grp]))
        yield
        n, p, aqk = {}, {}, {}
        for u, g in ug:
            grp = groups[g]
            g_col = pick([gc[u, h][:, :gw] for h in grp])
            g_row = pick([gcum_t[u][GDN_HEADS + h:GDN_HEADS + h + 1, :] for h in grp])
            decay = jnp.exp(jnp.where(tril, g_col - g_row, -jnp.inf))
            a = jnp.where(strict, kkqk[u, g][:chunk] * decay, 0.0)
            aqk[u, g] = (kkqk[u, g][chunk:] * decay).astype(BF16)
            n[u, g] = -a
            p[u, g] = _dot(a, blockdiag_of_lane_blocks(a))
        yield
        span = 2
        while 2 * span < chunk:
            for key in ug:
                both = _dot(cat([n[key], p[key]], 0), blockdiag_of_lane_blocks(p[key]))
                n[key] = n[key] + p[key] + both[:chunk]
                p[key] = both[chunk:]
            span *= 2
            yield
        for key in ug:
            n[key] = n[key] + p[key] + _dot(n[key], blockdiag_of_lane_blocks(p[key]))
        yield
        wu = {}
        for u, h in uh:
            rhs = jnp.concatenate([kb[u, h] * eg[u, h], v[u, h] * beta[u, h]], axis=-1)
            wu[u, h] = rhs + _dot(n[u, h // group][:, lanes(0, h % group, chunk)], rhs)
        yield
        for u in units:
            res[u] = dict(
                w_qd=[jnp.concatenate([wu[u, h][:, :GDN_DK], q[u, h] * eg[u, h]],
                                      axis=0).astype(BF16) for h in heads],
                u=[wu[u, h][:, GDN_DK:] for h in heads],
                kd=[(k[u, h] * jnp.exp(gl[u, h] - gc[u, h])).astype(BF16) for h in heads],
                aqk=[aqk[u, g] for g in range(len(groups))],
                egl=[jnp.exp(gl[u, h]) for h in heads])
        yield

    def phase2(units, res):
        for c in sorted({c for _, c in units}):
            us = [u for u in units if u[1] == c]
            uh = [(u, h) for u in us for h in heads]
            s = {(u, h): s_ref[u[0], h] for u, h in uh}
            ws_qs = {(u, h): _dot(res[u]["w_qd"][h], s[u, h]) for u, h in uh}
            yield
            v_new = {(u, h): res[u]["u"][h] - ws_qs[u, h][:chunk] for u, h in uh}
            o_grp = {(u, g): _dot(res[u]["aqk"][g], blockdiag([v_new[u, h] for h in grp]))
                     for u in us for g, grp in enumerate(groups)}
            for u, h in uh:
                s_ref[u[0], h] = (s[u, h] * res[u]["egl"][h]
                                  + _dot_tn(res[u]["kd"][h], v_new[u, h]))
            yield
            for u, h in uh:
                oh = ws_qs[u, h][chunk:] + o_grp[u, h // group][:, lanes(0, h % group, GDN_DV)]
                oh = oh * lax.rsqrt(jnp.mean(oh * oh, axis=-1, keepdims=True) + EPS) * og
                o_ref[u[0], rows(c), lanes(0, h, GDN_DV)] = oh.astype(o_ref.dtype)
            yield

    units = [(b, c) for b in range(bt) for c in range(nchunks)]
    waves = [units[i:i + UNITS_PER_WAVE] for i in range(0, len(units), UNITS_PER_WAVE)]
    res = {}
    prev = None
    for wave in waves:
        gens = [phase1(wave, res)]
        if prev is not None:
            gens.append(phase2(prev, res))
        _round_robin(gens)
        prev = wave
    _round_robin([phase2(prev, res)])


def _gdn(qkv, ba, s0, hist, cw, alog_pad, dtb_pad, og, *, chunk, bt, lt):
    group = 2 if 2 * chunk == LANES else 1
    bn, seq, _ = qkv.shape
    width, cc = cw.shape
    s_spec = pl.BlockSpec((bt, GDN_HEADS, GDN_DK, GDN_DV), lambda b, l: (b, 0, 0, 0))
    hist_spec = pl.BlockSpec((bt, width - 1, cc), lambda b, l: (b, 0, 0))
    vec = pl.BlockSpec((1, LANES), lambda b, l: (0, 0))
    return pl.pallas_call(
        functools.partial(_gdn_kernel, chunk=chunk, group=group),
        grid=(bn // bt, seq // lt),
        in_specs=[
            pl.BlockSpec((bt, lt, QKV_WIDTH), lambda b, l: (b, l, 0)),
            pl.BlockSpec((bt, lt, BA_PAD), lambda b, l: (b, l, 0)),
            s_spec, hist_spec,
            pl.BlockSpec((width, cc), lambda b, l: (0, 0)),
            vec, vec, vec,
        ],
        out_specs=[
            pl.BlockSpec((bt, lt, GDN_WIDTH), lambda b, l: (b, l, 0)),
            s_spec,
        ],
        out_shape=[
            jax.ShapeDtypeStruct((bn, seq, GDN_WIDTH), qkv.dtype),
            jax.ShapeDtypeStruct(s0.shape, F32),
        ],
        scratch_shapes=[pltpu.VMEM((bt, SUBLANES, cc), F32)],
        compiler_params=_cparams("arbitrary", "arbitrary"),
        name="gdn",
    )(qkv, ba, s0, hist, cw, alog_pad, dtb_pad, og)


ATTN_ROWS = 128
ATTN_SEQS = 4


def _attn_out_kernel(tok_ref, xq_ref, gate_ref, x_ref, kt_ref, vt_ref, w_ref, fg_ref,
                     y_ref, xo_scr, *, bt, lt, final):
    lane_head = lax.broadcasted_iota(jnp.int32, (1, X_WIDTH), 1) // X_HEAD_DIM
    scale = X_HEAD_DIM ** -0.5
    tm = bt * lt

    def attend(b, rs):
        r = rs.stop - rs.start
        q = xq_ref[b, rs, :]
        q = q * jnp.asarray(scale, q.dtype)
        qx = jnp.concatenate([jnp.where(lane_head == h, q, jnp.zeros_like(q))
                              for h in range(X_HEADS)], axis=0)
        s = _dot(qx, kt_ref[b])
        yield
        e = jnp.exp(s - jnp.max(s, axis=-1, keepdims=True))
        p = e / jnp.sum(e, axis=-1, keepdims=True)
        o4 = _dot_nt(p, vt_ref[b])
        yield
        xo = jnp.where(lane_head == 0, o4[0:r], 0.0)
        for h in range(1, X_HEADS):
            xo = xo + jnp.where(lane_head == h, o4[h * r:(h + 1) * r], 0.0)
        xo_scr[b, rs, :] = xo
        yield

    acc = {}

    def tok_proj():
        sg = _silu(gate_ref[:, :, :GDN_WIDTH].astype(F32).reshape(tm, GDN_WIDTH))
        br = (tok_ref[...].astype(F32).reshape(tm, GDN_WIDTH) * sg).astype(BF16)
        for cols in _col_chunks(D_MODEL):
            acc[cols.start] = (x_ref[:, :, cols].reshape(tm, -1)
                               + _dot(br, w_ref[:GDN_WIDTH, cols]))
            yield

    if bt == 1:
        blocks = [slice(r, r + ATTN_ROWS) for r in range(0, lt, ATTN_ROWS)]
        _round_robin([_delayed(attend(0, rs), i) for i, rs in enumerate(blocks)] + [tok_proj()])
    else:
        def seq_group(i, carry):
            _round_robin([attend(i * ATTN_SEQS + j, slice(0, lt)) for j in range(ATTN_SEQS)])
            return carry

        lax.fori_loop(0, bt // ATTN_SEQS, seq_group, 0)
        _round_robin([tok_proj()])

    sg_x = _silu(gate_ref[:, :, GDN_WIDTH:].astype(F32).reshape(tm, X_WIDTH))
    br_x = (xo_scr[...].reshape(tm, X_WIDTH) * sg_x).astype(BF16)
    y = [acc[cols.start] + _dot(br_x, w_ref[GDN_WIDTH:, cols]) for cols in _col_chunks(D_MODEL)]
    if final:
        ms = sum(jnp.sum(c * c, axis=-1, keepdims=True) for c in y) * (1.0 / D_MODEL)
        r = lax.rsqrt(ms + EPS)
        y = [c * r * fg_ref[:, cols] for c, cols in zip(y, _col_chunks(D_MODEL))]
    for c, cols in zip(y, _col_chunks(D_MODEL)):
        y_ref[:, :, cols] = c.reshape(bt, lt, -1)


def _attn_out(tok, xq, gate, x, mk, mv, w_bf16, fg, *, layer, bt, lt, final):
    bn, seq, _ = x.shape

    def act(n):
        return pl.BlockSpec((bt, lt, n), lambda b, l: (b, l, 0))

    mem = pl.BlockSpec((None, bt, X_WIDTH, N_MEM), lambda b, l: (layer, b, 0, 0))
    return pl.pallas_call(
        functools.partial(_attn_out_kernel, bt=bt, lt=lt, final=final),
        grid=(bn // bt, seq // lt),
        in_specs=[
            act(GDN_WIDTH), act(X_WIDTH), act(BRANCH_WIDTH), act(D_MODEL), mem, mem,
            pl.BlockSpec((BRANCH_WIDTH, D_MODEL), lambda b, l: (0, 0)),
            pl.BlockSpec((1, D_MODEL), lambda b, l: (0, 0)),
        ],
        out_specs=act(D_MODEL),
        out_shape=jax.ShapeDtypeStruct(x.shape, F32),
        scratch_shapes=[pltpu.VMEM((bt, lt, X_WIDTH), F32)],
        compiler_params=_cparams("arbitrary", "arbitrary"),
        name="attn_out",
    )(tok, xq, gate, x, mk, mv, w_bf16, fg)


def _trunk(x, mem_k, mem_v, gdn_s, gdn_conv, sc_conv, p, *, bt, lt, chunk, gdn_bt, gdn_lt,
           act_dtype):
    qkv, gate, xq, ba, gconv_new = _inproj(
        x, p["norm_g"][0:1], p["w_in_a"], None, p["conv_w_a"], mixer="gdn", bt=bt, lt=lt,
        act_dtype=act_dtype)
    tok, s_new = _gdn(qkv, ba, gdn_s, gdn_conv, p["conv_w_a"], p["alog_pad"],
                      p["dtb_pad"], p["o_norm_g"], chunk=chunk, bt=gdn_bt, lt=gdn_lt)
    x = _attn_out(tok, xq, gate, x, mem_k, mem_v, p["w_out"][0], p["final_norm_g"],
                  layer=0, bt=bt, lt=lt, final=False)
    tok, gate, xq, sconv_new = _inproj(
        x, p["norm_g"][1:2], p["w_in_b"], sc_conv, p["conv_w_b"], mixer="sconv", bt=bt, lt=lt,
        act_dtype=act_dtype)
    y = _attn_out(tok, xq, gate, x, mem_k, mem_v, p["w_out"][1], p["final_norm_g"],
                  layer=1, bt=bt, lt=lt, final=True)
    return y, s_new[None], gconv_new[None], sconv_new[None]


def kernel(x_prompt, x_sample, mem_prompt, state_gdn, state_gdn_conv, state_sconv, cache_mem_k, cache_mem_v, norm_g, w_in_a, conv_w_a, a_log, dt_bias, o_norm_g, w_in_b, conv_w_b, mem_norm_g, w_mem_kv, w_out, final_norm_g):
    bp = x_prompt.shape[0]

    wa = w_in_a[0]
    c_b = QKV_WIDTH
    c_g = QKV_WIDTH + 2 * GDN_HEADS
    wa = jnp.concatenate(
        [wa[:, :c_b], wa[:, c_g:], wa[:, c_b:c_g],
         jnp.zeros((D_MODEL, BA_PAD - 2 * GDN_HEADS), wa.dtype)], axis=1).astype(BF16)
    pad_lo = jnp.zeros((GDN_HEADS,), F32)
    pad_hi = jnp.zeros((LANES - 2 * GDN_HEADS,), F32)
    params = {
        "norm_g": norm_g,
        "w_in_a": wa,
        "conv_w_a": conv_w_a[0],
        "alog_pad": jnp.concatenate([pad_lo, a_log[0], pad_hi])[None],
        "dtb_pad": jnp.concatenate([pad_lo, dt_bias[0], pad_hi])[None],
        "o_norm_g": o_norm_g,
        "w_in_b": w_in_b[0].astype(BF16),
        "conv_w_b": conv_w_b[0],
        "w_out": w_out.astype(BF16),
        "final_norm_g": final_norm_g[None],
    }

    def to_cache(t):
        t = t.reshape(t.shape[0], t.shape[1], X_HEADS, X_HEAD_DIM, t.shape[3])
        return jnp.transpose(t, (0, 1, 4, 2, 3))

    def from_cache(t):
        t = jnp.transpose(t, (0, 1, 3, 4, 2))
        return t.reshape(t.shape[0], t.shape[1], X_WIDTH, t.shape[4])

    mem_kt, mem_vt = _memkv(mem_prompt, mem_norm_g[None], w_mem_kv.astype(BF16))
    mem_k_p = to_cache(mem_kt)
    mem_v_p = to_cache(mem_vt)

    s0_p = jnp.zeros((bp,) + state_gdn.shape[2:], F32)
    gc0_p = jnp.zeros((bp,) + state_gdn_conv.shape[2:], F32)
    sc0_p = jnp.zeros((bp,) + state_sconv.shape[2:], F32)
    y_p, s_p, gc_p, sc_p = _trunk(x_prompt, mem_kt, mem_vt, s0_p, gc0_p, sc0_p,
                                  params, bt=1, lt=256, chunk=GDN_CHUNK, gdn_bt=1, gdn_lt=256,
                                  act_dtype=BF16)
    dec_seq = x_sample.shape[1]
    y_s, s_s, gc_s, sc_s = _trunk(x_sample, from_cache(cache_mem_k), from_cache(cache_mem_v),
                                  state_gdn[0], state_gdn_conv[0], state_sconv[0], params,
                                  bt=32, lt=dec_seq, chunk=dec_seq, gdn_bt=4, gdn_lt=dec_seq,
                                  act_dtype=F32)
    return (y_p, y_s, s_p, gc_p, sc_p, mem_k_p, mem_v_p, s_s, gc_s, sc_s)
```

```python
import functools

import jax
import jax.numpy as jnp
from jax import lax
from jax.experimental import pallas as pl
from jax.experimental.pallas import tpu as pltpu

F32 = jnp.float32
BF16 = jnp.bfloat16

D_MODEL = 1024
N_MEM = 256
X_WIDTH = 256
X_HEADS = 4
X_HEAD_DIM = 64
GDN_HEADS = 6
GDN_DK = 128
GDN_DV = 128
GDN_WIDTH = GDN_HEADS * GDN_DV
QK_WIDTH = GDN_HEADS * GDN_DK
QKV_WIDTH = 2 * QK_WIDTH + GDN_WIDTH
GDN_CHUNK = 64
SC_WIDTH = 768
BRANCH_WIDTH = 1024
EPS = 1e-6

LANES = 128
SUBLANES = 8
COL_CHUNK = 256
BA_PAD = LANES
VMEM_LIMIT = 56 * 1024 * 1024

HIGHEST = lax.Precision.HIGHEST


def _cparams(*sem):
    return pltpu.CompilerParams(dimension_semantics=sem, vmem_limit_bytes=VMEM_LIMIT)


def _rms_rows(x, g):
    r = lax.rsqrt(jnp.mean(x * x, axis=-1, keepdims=True) + EPS)
    return x * r * g


def _silu(x):
    h = 0.5 * x
    return h + h * jnp.tanh(h)


def _dot(a, b):
    return jnp.dot(a.astype(BF16), b.astype(BF16), preferred_element_type=F32)


def _dot_nt(a, b):
    return lax.dot_general(a.astype(BF16), b.astype(BF16), (((1,), (1,)), ((), ())),
                           preferred_element_type=F32)


def _dot_tn(a, b):
    return lax.dot_general(a.astype(BF16), b.astype(BF16), (((0,), (0,)), ((), ())),
                           preferred_element_type=F32)


def _dot_f32(a, b):
    return jnp.dot(a, b, preferred_element_type=F32, precision=HIGHEST)


def _col_chunks(n):
    return [slice(c, min(c + COL_CHUNK, n)) for c in range(0, n, COL_CHUNK)]


def _round_robin(gens):
    gens = list(gens)
    while gens:
        alive = []
        for g in gens:
            try:
                next(g)
                alive.append(g)
            except StopIteration:
                pass
        gens = alive


def _delayed(gen, rounds):
    for _ in range(rounds):
        yield
    yield from gen


def _memkv_kernel(m_ref, g_ref, w_ref, kt_ref, vt_ref):
    h = _rms_rows(m_ref[...], g_ref[...])
    kv = _dot(h, w_ref[...])
    kt_ref[...] = kv[:, :X_WIDTH].T
    vt_ref[...] = kv[:, X_WIDTH:].T


def _memkv(mem, g, w_bf16):
    bn, n_mem, _ = mem.shape
    depth = w_bf16.shape[0]
    out = jax.ShapeDtypeStruct((depth, bn, X_WIDTH, n_mem), F32)
    out_spec = pl.BlockSpec((None, None, X_WIDTH, n_mem), lambda l, b: (l, b, 0, 0))
    return pl.pallas_call(
        _memkv_kernel,
        grid=(depth, bn),
        in_specs=[
            pl.BlockSpec((None, n_mem, D_MODEL), lambda l, b: (b, 0, 0)),
            pl.BlockSpec((1, D_MODEL), lambda l, b: (0, 0)),
            pl.BlockSpec((None, D_MODEL, 2 * X_WIDTH), lambda l, b: (l, 0, 0)),
        ],
        out_specs=[out_spec, out_spec],
        out_shape=[out, out],
        compiler_params=_cparams("arbitrary", "arbitrary"),
        name="memkv",
    )(mem, g, w_bf16)


def _init_conv_carry(carry, hist_ref):
    @pl.when(pl.program_id(1) == 0)
    def _():
        carry[:, SUBLANES - hist_ref.shape[1]:, :] = hist_ref[...]


def _causal_conv_cols(x, cols, carry, hist_out, cw_ref, seqs=slice(None)):
    width = cw_ref.shape[0]
    bt, lt, n = x.shape
    g = lt // SUBLANES
    xe = jnp.concatenate([carry[seqs, :, cols], x], axis=1).reshape(bt, g + 1, SUBLANES, n)
    sub = lax.broadcasted_iota(jnp.int32, (1, 1, SUBLANES, n), 2)
    y = x.reshape(bt, g, SUBLANES, n) * cw_ref[width - 1:width, cols]
    for j in range(width - 1):
        s = width - 1 - j
        r = pltpu.roll(xe, s, axis=2)
        y = y + jnp.where(sub < s, r[:, :g], r[:, 1:]) * cw_ref[j:j + 1, cols]
    carry[seqs, :, cols] = x[:, lt - SUBLANES:, :]
    if hist_out is not None:
        hist_out[seqs, :, cols] = x[:, lt - (width - 1):, :]
    return y.reshape(bt, lt, n)


def _inproj_kernel(x_ref, g_ref, w_ref, *rest, mixer, bt, lt):
    if mixer == "gdn":
        mix_out, *plain_outs, hist_out = rest
    else:
        hist_ref, cw_ref, mix_out, *plain_outs, hist_out, carry = rest
        _init_conv_carry(carry, hist_ref)
    tm = bt * lt
    h = _rms_rows(x_ref[...].reshape(tm, D_MODEL), g_ref[...]).astype(BF16)

    def proj(cols, base=0):
        cols = slice(base + cols.start, base + cols.stop)
        return jnp.dot(h, w_ref[:, cols], preferred_element_type=F32).reshape(bt, lt, -1)

    def mixer_chunk(cols):
        if mixer == "gdn":
            y = proj(cols)
            hist_out[:, :, cols] = y[:, lt - hist_out.shape[1]:, :]
        else:
            gate_b, pre = proj(cols), proj(cols, SC_WIDTH) * proj(cols, 2 * SC_WIDTH)
            yield
            y = gate_b * _causal_conv_cols(pre, cols, carry, hist_out, cw_ref)
        mix_out[:, :, cols] = y.astype(mix_out.dtype)
        yield

    def plain_chunk(out, cols, base):
        out[:, :, cols] = proj(cols, base).astype(out.dtype)
        yield

    mix_width = mix_out.shape[-1]
    gens = [mixer_chunk(cols) for cols in _col_chunks(mix_width)]
    base = mix_width if mixer == "gdn" else 3 * mix_width
    for out in plain_outs:
        gens += [plain_chunk(out, cols, base) for cols in _col_chunks(out.shape[-1])]
        base += out.shape[-1]
    _round_robin([_delayed(g, i) for i, g in enumerate(gens)])


def _inproj(x, g, w_bf16, hist, cw, *, mixer, bt, lt, act_dtype):
    bn, seq, _ = x.shape
    ncols = w_bf16.shape[1]
    width, cc = cw.shape

    def act(n, dtype=act_dtype):
        return (pl.BlockSpec((bt, lt, n), lambda b, l: (b, l, 0)),
                jax.ShapeDtypeStruct((bn, seq, n), dtype))

    hist_spec = pl.BlockSpec((bt, width - 1, cc), lambda b, l: (b, 0, 0))
    args = [x, g, w_bf16]
    in_specs = [
        pl.BlockSpec((bt, lt, D_MODEL), lambda b, l: (b, l, 0)),
        pl.BlockSpec((1, D_MODEL), lambda b, l: (0, 0)),
        pl.BlockSpec((D_MODEL, ncols), lambda b, l: (0, 0)),
    ]
    outs = [act(cc), act(BRANCH_WIDTH), act(X_WIDTH)]
    scratch = []
    if mixer == "gdn":
        outs.append(act(BA_PAD, F32))
    else:
        args += [hist, cw]
        in_specs += [hist_spec, pl.BlockSpec((width, cc), lambda b, l: (0, 0))]
        scratch.append(pltpu.VMEM((bt, SUBLANES, cc), F32))
    outs.append((hist_spec, jax.ShapeDtypeStruct((bn, width - 1, cc), F32)))
    return pl.pallas_call(
        functools.partial(_inproj_kernel, mixer=mixer, bt=bt, lt=lt),
        grid=(bn // bt, seq // lt),
        in_specs=in_specs,
        out_specs=[o[0] for o in outs],
        out_shape=[o[1] for o in outs],
        scratch_shapes=scratch,
        compiler_params=_cparams("arbitrary", "arbitrary"),
        name="inproj_" + mixer,
    )(*args)


UNITS_PER_WAVE = 2


def _gdn_kernel(qkv_ref, ba_ref, s0_ref, hist_ref, cw_ref, alog_ref, dtb_ref, og_ref,
                o_ref, s_ref, carry, *, chunk, group):
    bt, lt, _ = qkv_ref.shape
    nchunks = lt // chunk

    @pl.when(pl.program_id(1) == 0)
    def _():
        s_ref[...] = s0_ref[...]

    _init_conv_carry(carry, hist_ref)

    heads = range(GDN_HEADS)
    groups = [tuple(range(g, g + group)) for g in range(0, GDN_HEADS, group)]
    gw = group * chunk
    row = lax.broadcasted_iota(jnp.int32, (chunk, gw), 0)
    col = lax.broadcasted_iota(jnp.int32, (chunk, gw), 1) % chunk
    tril = row >= col
    strict = row > col
    lane_blk = lax.broadcasted_iota(jnp.int32, (1, gw), 1) // chunk
    ones_tril = (lax.broadcasted_iota(jnp.int32, (chunk, chunk), 0)
                 >= lax.broadcasted_iota(jnp.int32, (chunk, chunk), 1)).astype(F32)
    neg_a = -jnp.exp(alog_ref[...])
    dtb = dtb_ref[...]
    og = og_ref[...]

    def rows(c):
        return slice(c * chunk, (c + 1) * chunk)

    def lanes(base, h, n):
        return slice(base + h * n, base + (h + 1) * n)

    def cat(xs, axis):
        return xs[0] if len(xs) == 1 else jnp.concatenate(xs, axis=axis)

    def pick(xs):
        out = xs[0]
        for j in range(1, len(xs)):
            out = jnp.where(lane_blk == j, xs[j], out)
        return out

    def blockdiag(xs):
        z = jnp.zeros_like(xs[0])
        return cat([cat([x if i == j else z for i in range(len(xs))], 1)
                    for j, x in enumerate(xs)], 0)

    def blockdiag_of_lane_blocks(x):
        if group == 1:
            return x
        return cat([jnp.where(lane_blk == j, x, 0.0) for j in range(group)], 0)

    def conv_silu(u, cols):
        b, c = u
        x = qkv_ref[b, rows(c), cols].astype(F32)
        y = _causal_conv_cols(x[None], cols, carry, None, cw_ref, seqs=slice(b, b + 1))
        return _silu(y[0])

    def phase1(units, res):
        uh = [(u, h) for u in units for h in heads]
        ug = [(u, g) for u in units for g in range(len(groups))]
        gcum, gcum_t, beta_all = {}, {}, {}
        for u in units:
            ba = ba_ref[u[0], rows(u[1]), :]
            beta_all[u] = 1.0 / (1.0 + jnp.exp(-ba))
            z = ba + dtb
            softplus = jnp.maximum(z, 0.0) + jnp.log1p(jnp.exp(-jnp.abs(z)))
            gcum[u] = _dot_f32(ones_tril, neg_a * softplus)
        q, k, v = {}, {}, {}
        for u in units:
            for h in heads:
                q[u, h] = conv_silu(u, lanes(0, h, GDN_DK))
                k[u, h] = conv_silu(u, lanes(QK_WIDTH, h, GDN_DK))
                v[u, h] = conv_silu(u, lanes(2 * QK_WIDTH, h, GDN_DV))
            yield
        for key in uh:
            x = q[key]
            q[key] = x * (lax.rsqrt(jnp.sum(x * x, axis=-1, keepdims=True) + EPS)
                          * (GDN_DK ** -0.5))
            x = k[key]
            k[key] = x * lax.rsqrt(jnp.sum(x * x, axis=-1, keepdims=True) + EPS)
        yield
        for u in units:
            gcum_t[u] = cat([gcum[u]] * group, 0).T
        gc = {(u, h): jnp.broadcast_to(gcum[u][:, GDN_HEADS + h:GDN_HEADS + h + 1],
                                       (chunk, GDN_DK)) for u, h in uh}
        gl = {(u, h): gcum[u][chunk - 1:chunk, GDN_HEADS + h:GDN_HEADS + h + 1]
              for u, h in uh}
        beta = {(u, h): jnp.broadcast_to(beta_all[u][:, h:h + 1], (chunk, GDN_DK))
                for u, h in uh}
        eg = {key: jnp.exp(gc[key]) for key in uh}
        kb = {key: k[key] * beta[key] for key in uh}
        kkqk = {}
        for u, g in ug:
            grp = groups[g]
            kkqk[u, g] = _dot_nt(
                cat([cat([kb[u, h] for h in grp], 1), cat([q[u, h] for h in grp], 1)], 0),
                blockdiag([k[u, h] for h in grp]))
        yield
        n, p, aqk = {}, {}, {}
        for u, g in ug:
            grp = groups[g]
            g_col = pick([gc[u, h][:, :gw] for h in grp])
            g_row = pick([gcum_t[u][GDN_HEADS + h:GDN_HEADS + h + 1, :] for h in grp])
            decay = jnp.exp(jnp.where(tril, g_col - g_row, -jnp.inf))
            a = jnp.where(strict, kkqk[u, g][:chunk] * decay, 0.0)
            aqk[u, g] = (kkqk[u, g][chunk:] * decay).astype(BF16)
            n[u, g] = -a
            p[u, g] = _dot(a, blockdiag_of_lane_blocks(a))
        yield
        span = 2
        while 2 * span < chunk:
            for key in ug:
                both = _dot(cat([n[key], p[key]], 0), blockdiag_of_lane_blocks(p[key]))
                n[key] = n[key] + p[key] + both[:chunk]
                p[key] = both[chunk:]
            span *= 2
            yield
        for key in ug:
            n[key] = n[key] + p[key] + _dot(n[key], blockdiag_of_lane_blocks(p[key]))
        yield
        wu = {}
        for u, h in uh:
            rhs = jnp.concatenate([kb[u, h] * eg[u, h], v[u, h] * beta[u, h]], axis=-1)
            wu[u, h] = rhs + _dot(n[u, h // group][:, lanes(0, h % group, chunk)], rhs)
        yield
        for u in units:
            res[u] = dict(
                w_qd=[jnp.concatenate([wu[u, h][:, :GDN_DK], q[u, h] * eg[u, h]],
                                      axis=0).astype(BF16) for h in heads],
                u=[wu[u, h][:, GDN_DK:] for h in heads],
                kd=[(k[u, h] * jnp.exp(gl[u, h] - gc[u, h])).astype(BF16) for h in heads],
                aqk=[aqk[u, g] for g in range(len(groups))],
                egl=[jnp.exp(gl[u, h]) for h in heads])
        yield

    def phase2(units, res):
        for c in sorted({c for _, c in units}):
            us = [u for u in units if u[1] == c]
            uh = [(u, h) for u in us for h in heads]
            s = {(u, h): s_ref[u[0], h] for u, h in uh}
            ws_qs = {(u, h): _dot(res[u]["w_qd"][h], s[u, h]) for u, h in uh}
            yield
            v_new = {(u, h): res[u]["u"][h] - ws_qs[u, h][:chunk] for u, h in uh}
            o_grp = {(u, g): _dot(res[u]["aqk"][g], blockdiag([v_new[u, h] for h in grp]))
                     for u in us for g, grp in enumerate(groups)}
            for u, h in uh:
                s_ref[u[0], h] = (s[u, h] * res[u]["egl"][h]
                                  + _dot_tn(res[u]["kd"][h], v_new[u, h]))
            yield
            for u, h in uh:
                oh = ws_qs[u, h][chunk:] + o_grp[u, h // group][:, lanes(0, h % group, GDN_DV)]
                oh = oh * lax.rsqrt(jnp.mean(oh * oh, axis=-1, keepdims=True) + EPS) * og
                o_ref[u[0], rows(c), lanes(0, h, GDN_DV)] = oh.astype(o_ref.dtype)
            yield

    units = [(b, c) for b in range(bt) for c in range(nchunks)]
    waves = [units[i:i + UNITS_PER_WAVE] for i in range(0, len(units), UNITS_PER_WAVE)]
    res = {}
    prev = None
    for wave in waves:
        gens = [phase1(wave, res)]
        if prev is not None:
            gens.append(phase2(prev, res))
        _round_robin(gens)
        prev = wave
    _round_robin([phase2(prev, res)])


def _gdn(qkv, ba, s0, hist, cw, alog_pad, dtb_pad, og, *, chunk, bt, lt):
    group = 2 if 2 * chunk == LANES else 1
    bn, seq, _ = qkv.shape
    width, cc = cw.shape
    s_spec = pl.BlockSpec((bt, GDN_HEADS, GDN_DK, GDN_DV), lambda b, l: (b, 0, 0, 0))
    hist_spec = pl.BlockSpec((bt, width - 1, cc), lambda b, l: (b, 0, 0))
    vec = pl.BlockSpec((1, LANES), lambda b, l: (0, 0))
    return pl.pallas_call(
        functools.partial(_gdn_kernel, chunk=chunk, group=group),
        grid=(bn // bt, seq // lt),
        in_specs=[
            pl.BlockSpec((bt, lt, QKV_WIDTH), lambda b, l: (b, l, 0)),
            pl.BlockSpec((bt, lt, BA_PAD), lambda b, l: (b, l, 0)),
            s_spec, hist_spec,
            pl.BlockSpec((width, cc), lambda b, l: (0, 0)),
            vec, vec, vec,
        ],
        out_specs=[
            pl.BlockSpec((bt, lt, GDN_WIDTH), lambda b, l: (b, l, 0)),
            s_spec,
        ],
        out_shape=[
            jax.ShapeDtypeStruct((bn, seq, GDN_WIDTH), qkv.dtype),
            jax.ShapeDtypeStruct(s0.shape, F32),
        ],
        scratch_shapes=[pltpu.VMEM((bt, SUBLANES, cc), F32)],
        compiler_params=_cparams("arbitrary", "arbitrary"),
        name="gdn",
    )(qkv, ba, s0, hist, cw, alog_pad, dtb_pad, og)


ATTN_ROWS = 128
ATTN_SEQS = 4


def _attn_out_kernel(tok_ref, xq_ref, gate_ref, x_ref, kt_ref, vt_ref, w_ref, fg_ref,
                     y_ref, xo_scr, *, bt, lt, final):
    lane_head = lax.broadcasted_iota(jnp.int32, (1, X_WIDTH), 1) // X_HEAD_DIM
    scale = X_HEAD_DIM ** -0.5
    tm = bt * lt

    def attend(b, rs):
        r = rs.stop - rs.start
        q = xq_ref[b, rs, :]
        q = q * jnp.asarray(scale, q.dtype)
        qx = jnp.concatenate([jnp.where(lane_head == h, q, jnp.zeros_like(q))
                              for h in range(X_HEADS)], axis=0)
        s = _dot(qx, kt_ref[b])
        yield
        e = jnp.exp(s - jnp.max(s, axis=-1, keepdims=True))
        p = e / jnp.sum(e, axis=-1, keepdims=True)
        o4 = _dot_nt(p, vt_ref[b])
        yield
        xo = jnp.where(lane_head == 0, o4[0:r], 0.0)
        for h in range(1, X_HEADS):
            xo = xo + jnp.where(lane_head == h, o4[h * r:(h + 1) * r], 0.0)
        xo_scr[b, rs, :] = xo
        yield

    acc = {}

    def tok_proj():
        sg = _silu(gate_ref[:, :, :GDN_WIDTH].astype(F32).reshape(tm, GDN_WIDTH))
        br = (tok_ref[...].astype(F32).reshape(tm, GDN_WIDTH) * sg).astype(BF16)
        for cols in _col_chunks(D_MODEL):
            acc[cols.start] = (x_ref[:, :, cols].reshape(tm, -1)
                               + _dot(br, w_ref[:GDN_WIDTH, cols]))
            yield

    if bt == 1:
        blocks = [slice(r, r + ATTN_ROWS) for r in range(0, lt, ATTN_ROWS)]
        _round_robin([_delayed(attend(0, rs), i) for i, rs in enumerate(blocks)] + [tok_proj()])
    else:
        def seq_group(i, carry):
            _round_robin([attend(i * ATTN_SEQS + j, slice(0, lt)) for j in range(ATTN_SEQS)])
            return carry

        lax.fori_loop(0, bt // ATTN_SEQS, seq_group, 0)
        _round_robin([tok_proj()])

    sg_x = _silu(gate_ref[:, :, GDN_WIDTH:].astype(F32).reshape(tm, X_WIDTH))
    br_x = (xo_scr[...].reshape(tm, X_WIDTH) * sg_x).astype(BF16)
    y = [acc[cols.start] + _dot(br_x, w_ref[GDN_WIDTH:, cols]) for cols in _col_chunks(D_MODEL)]
    if final:
        ms = sum(jnp.sum(c * c, axis=-1, keepdims=True) for c in y) * (1.0 / D_MODEL)
        r = lax.rsqrt(ms + EPS)
        y = [c * r * fg_ref[:, cols] for c, cols in zip(y, _col_chunks(D_MODEL))]
    for c, cols in zip(y, _col_chunks(D_MODEL)):
        y_ref[:, :, cols] = c.reshape(bt, lt, -1)


def _attn_out(tok, xq, gate, x, mk, mv, w_bf16, fg, *, layer, bt, lt, final):
    bn, seq, _ = x.shape

    def act(n):
        return pl.BlockSpec((bt, lt, n), lambda b, l: (b, l, 0))

    mem = pl.BlockSpec((None, bt, X_WIDTH, N_MEM), lambda b, l: (layer, b, 0, 0))
    return pl.pallas_call(
        functools.partial(_attn_out_kernel, bt=bt, lt=lt, final=final),
        grid=(bn // bt, seq // lt),
        in_specs=[
            act(GDN_WIDTH), act(X_WIDTH), act(BRANCH_WIDTH), act(D_MODEL), mem, mem,
            pl.BlockSpec((BRANCH_WIDTH, D_MODEL), lambda b, l: (0, 0)),
            pl.BlockSpec((1, D_MODEL), lambda b, l: (0, 0)),
        ],
        out_specs=act(D_MODEL),
        out_shape=jax.ShapeDtypeStruct(x.shape, F32),
        scratch_shapes=[pltpu.VMEM((bt, lt, X_WIDTH), F32)],
        compiler_params=_cparams("arbitrary", "arbitrary"),
        name="attn_out",
    )(tok, xq, gate, x, mk, mv, w_bf16, fg)


def _trunk(x, mem_k, mem_v, gdn_s, gdn_conv, sc_conv, p, *, bt, lt, chunk, gdn_bt, gdn_lt,
           act_dtype):
    qkv, gate, xq, ba, gconv_new = _inproj(
        x, p["norm_g"][0:1], p["w_in_a"], None, p["conv_w_a"], mixer="gdn", bt=bt, lt=lt,
        act_dtype=act_dtype)
    tok, s_new = _gdn(qkv, ba, gdn_s, gdn_conv, p["conv_w_a"], p["alog_pad"],
                      p["dtb_pad"], p["o_norm_g"], chunk=chunk, bt=gdn_bt, lt=gdn_lt)
    x = _attn_out(tok, xq, gate, x, mem_k, mem_v, p["w_out"][0], p["final_norm_g"],
                  layer=0, bt=bt, lt=lt, final=False)
    tok, gate, xq, sconv_new = _inproj(
        x, p["norm_g"][1:2], p["w_in_b"], sc_conv, p["conv_w_b"], mixer="sconv", bt=bt, lt=lt,
        act_dtype=act_dtype)
    y = _attn_out(tok, xq, gate, x, mem_k, mem_v, p["w_out"][1], p["final_norm_g"],
                  layer=1, bt=bt, lt=lt, final=True)
    return y, s_new[None], gconv_new[None], sconv_new[None]


def kernel(x_prompt, x_sample, mem_prompt, state_gdn, state_gdn_conv, state_sconv, cache_mem_k, cache_mem_v, norm_g, w_in_a, conv_w_a, a_log, dt_bias, o_norm_g, w_in_b, conv_w_b, mem_norm_g, w_mem_kv, w_out, final_norm_g):
    bp = x_prompt.shape[0]

    wa = w_in_a[0]
    c_b = QKV_WIDTH
    c_g = QKV_WIDTH + 2 * GDN_HEADS
    wa = jnp.concatenate(
        [wa[:, :c_b], wa[:, c_g:], wa[:, c_b:c_g],
         jnp.zeros((D_MODEL, BA_PAD - 2 * GDN_HEADS), wa.dtype)], axis=1).astype(BF16)
    pad_lo = jnp.zeros((GDN_HEADS,), F32)
    pad_hi = jnp.zeros((LANES - 2 * GDN_HEADS,), F32)
    params = {
        "norm_g": norm_g,
        "w_in_a": wa,
        "conv_w_a": conv_w_a[0],
        "alog_pad": jnp.concatenate([pad_lo, a_log[0], pad_hi])[None],
        "dtb_pad": jnp.concatenate([pad_lo, dt_bias[0], pad_hi])[None],
        "o_norm_g": o_norm_g,
        "w_in_b": w_in_b[0].astype(BF16),
        "conv_w_b": conv_w_b[0],
        "w_out": w_out.astype(BF16),
        "final_norm_g": final_norm_g[None],
    }

    def to_cache(t):
        t = t.reshape(t.shape[0], t.shape[1], X_HEADS, X_HEAD_DIM, t.shape[3])
        return jnp.transpose(t, (0, 1, 4, 2, 3))

    def from_cache(t):
        t = jnp.transpose(t, (0, 1, 3, 4, 2))
        return t.reshape(t.shape[0], t.shape[1], X_WIDTH, t.shape[4])

    mem_kt, mem_vt = _memkv(mem_prompt, mem_norm_g[None], w_mem_kv.astype(BF16))
    mem_k_p = to_cache(mem_kt)
    mem_v_p = to_cache(mem_vt)

    s0_p = jnp.zeros((bp,) + state_gdn.shape[2:], F32)
    gc0_p = jnp.zeros((bp,) + state_gdn_conv.shape[2:], F32)
    sc0_p = jnp.zeros((bp,) + state_sconv.shape[2:], F32)
    y_p, s_p, gc_p, sc_p = _trunk(x_prompt, mem_kt, mem_vt, s0_p, gc0_p, sc0_p,
                                  params, bt=1, lt=512, chunk=GDN_CHUNK, gdn_bt=1, gdn_lt=512,
                                  act_dtype=BF16)
    dec_seq = x_sample.shape[1]
    y_s, s_s, gc_s, sc_s = _trunk(x_sample, from_cache(cache_mem_k), from_cache(cache_mem_v),
                                  state_gdn[0], state_gdn_conv[0], state_sconv[0], params,
                                  bt=32, lt=dec_seq, chunk=dec_seq, gdn_bt=4, gdn_lt=dec_seq,
                                  act_dtype=F32)
    return (y_p, y_s, s_p, gc_p, sc_p, mem_k_p, mem_v_p, s_s, gc_s, sc_s)
```

```python
import functools

import jax
import jax.numpy as jnp
from jax import lax
from jax.experimental import pallas as pl
from jax.experimental.pallas import tpu as pltpu

F32 = jnp.float32
BF16 = jnp.bfloat16

D_MODEL = 1024
N_MEM = 256
X_WIDTH = 256
X_HEADS = 4
X_HEAD_DIM = 64
GDN_HEADS = 6
GDN_DK = 128
GDN_DV = 128
GDN_WIDTH = GDN_HEADS * GDN_DV
QK_WIDTH = GDN_HEADS * GDN_DK
QKV_WIDTH = 2 * QK_WIDTH + GDN_WIDTH
GDN_CHUNK = 64
SC_WIDTH = 768
BRANCH_WIDTH = 1024
EPS = 1e-6

LANES = 128
SUBLANES = 8
COL_CHUNK = 256
BA_PAD = LANES
VMEM_LIMIT = 56 * 1024 * 1024

HIGHEST = lax.Precision.HIGHEST


def _cparams(*sem):
    return pltpu.CompilerParams(dimension_semantics=sem, vmem_limit_bytes=VMEM_LIMIT)


def _rms_rows(x, g):
    r = lax.rsqrt(jnp.mean(x * x, axis=-1, keepdims=True) + EPS)
    return x * r * g


def _silu(x):
    h = 0.5 * x
    return h + h * jnp.tanh(h)


def _dot(a, b):
    return jnp.dot(a.astype(BF16), b.astype(BF16), preferred_element_type=F32)


def _dot_nt(a, b):
    return lax.dot_general(a.astype(BF16), b.astype(BF16), (((1,), (1,)), ((), ())),
                           preferred_element_type=F32)


def _dot_tn(a, b):
    return lax.dot_general(a.astype(BF16), b.astype(BF16), (((0,), (0,)), ((), ())),
                           preferred_element_type=F32)


def _dot_f32(a, b):
    return jnp.dot(a, b, preferred_element_type=F32, precision=HIGHEST)


def _col_chunks(n):
    return [slice(c, min(c + COL_CHUNK, n)) for c in range(0, n, COL_CHUNK)]


def _round_robin(gens):
    gens = list(gens)
    while gens:
        alive = []
        for g in gens:
            try:
                next(g)
                alive.append(g)
            except StopIteration:
                pass
        gens = alive


def _delayed(gen, rounds):
    for _ in range(rounds):
        yield
    yield from gen


def _memkv_kernel(m_ref, g_ref, w_ref, kt_ref, vt_ref):
    h = _rms_rows(m_ref[...], g_ref[...])
    kv = _dot(h, w_ref[...])
    kt_ref[...] = kv[:, :X_WIDTH].T
    vt_ref[...] = kv[:, X_WIDTH:].T


def _memkv(mem, g, w_bf16):
    bn, n_mem, _ = mem.shape
    depth = w_bf16.shape[0]
    out = jax.ShapeDtypeStruct((depth, bn, X_WIDTH, n_mem), F32)
    out_spec = pl.BlockSpec((None, None, X_WIDTH, n_mem), lambda l, b: (l, b, 0, 0))
    return pl.pallas_call(
        _memkv_kernel,
        grid=(depth, bn),
        in_specs=[
            pl.BlockSpec((None, n_mem, D_MODEL), lambda l, b: (b, 0, 0)),
            pl.BlockSpec((1, D_MODEL), lambda l, b: (0, 0)),
            pl.BlockSpec((None, D_MODEL, 2 * X_WIDTH), lambda l, b: (l, 0, 0)),
        ],
        out_specs=[out_spec, out_spec],
        out_shape=[out, out],
        compiler_params=_cparams("arbitrary", "arbitrary"),
        name="memkv",
    )(mem, g, w_bf16)


def _init_conv_carry(carry, hist_ref):
    @pl.when(pl.program_id(1) == 0)
    def _():
        carry[:, SUBLANES - hist_ref.shape[1]:, :] = hist_ref[...]


def _causal_conv_cols(x, cols, carry, hist_out, cw_ref, seqs=slice(None)):
    width = cw_ref.shape[0]
    bt, lt, n = x.shape
    g = lt // SUBLANES
    xe = jnp.concatenate([carry[seqs, :, cols], x], axis=1).reshape(bt, g + 1, SUBLANES, n)
    sub = lax.broadcasted_iota(jnp.int32, (1, 1, SUBLANES, n), 2)
    y = x.reshape(bt, g, SUBLANES, n) * cw_ref[width - 1:width, cols]
    for j in range(width - 1):
        s = width - 1 - j
        r = pltpu.roll(xe, s, axis=2)
        y = y + jnp.where(sub < s, r[:, :g], r[:, 1:]) * cw_ref[j:j + 1, cols]
    carry[seqs, :, cols] = x[:, lt - SUBLANES:, :]
    if hist_out is not None:
        hist_out[seqs, :, cols] = x[:, lt - (width - 1):, :]
    return y.reshape(bt, lt, n)


def _inproj_kernel(x_ref, g_ref, w_ref, *rest, mixer, bt, lt):
    if mixer == "gdn":
        mix_out, *plain_outs, hist_out = rest
    else:
        hist_ref, cw_ref, mix_out, *plain_outs, hist_out, carry = rest
        _init_conv_carry(carry, hist_ref)
    tm = bt * lt
    h = _rms_rows(x_ref[...].reshape(tm, D_MODEL), g_ref[...]).astype(BF16)

    def proj(cols, base=0):
        cols = slice(base + cols.start, base + cols.stop)
        return jnp.dot(h, w_ref[:, cols], preferred_element_type=F32).reshape(bt, lt, -1)

    def mixer_chunk(cols):
        if mixer == "gdn":
            y = proj(cols)
            hist_out[:, :, cols] = y[:, lt - hist_out.shape[1]:, :]
        else:
            gate_b, pre = proj(cols), proj(cols, SC_WIDTH) * proj(cols, 2 * SC_WIDTH)
            yield
            y = gate_b * _causal_conv_cols(pre, cols, carry, hist_out, cw_ref)
        mix_out[:, :, cols] = y.astype(mix_out.dtype)
        yield

    def plain_chunk(out, cols, base):
        out[:, :, cols] = proj(cols, base).astype(out.dtype)
        yield

    mix_width = mix_out.shape[-1]
    gens = [mixer_chunk(cols) for cols in _col_chunks(mix_width)]
    base = mix_width if mixer == "gdn" else 3 * mix_width
    for out in plain_outs:
        gens += [plain_chunk(out, cols, base) for cols in _col_chunks(out.shape[-1])]
        base += out.shape[-1]
    _round_robin([_delayed(g, i) for i, g in enumerate(gens)])


def _inproj(x, g, w_bf16, hist, cw, *, mixer, bt, lt, act_dtype):
    bn, seq, _ = x.shape
    ncols = w_bf16.shape[1]
    width, cc = cw.shape

    def act(n, dtype=act_dtype):
        return (pl.BlockSpec((bt, lt, n), lambda b, l: (b, l, 0)),
                jax.ShapeDtypeStruct((bn, seq, n), dtype))

    hist_spec = pl.BlockSpec((bt, width - 1, cc), lambda b, l: (b, 0, 0))
    args = [x, g, w_bf16]
    in_specs = [
        pl.BlockSpec((bt, lt, D_MODEL), lambda b, l: (b, l, 0)),
        pl.BlockSpec((1, D_MODEL), lambda b, l: (0, 0)),
        pl.BlockSpec((D_MODEL, ncols), lambda b, l: (0, 0)),
    ]
    outs = [act(cc), act(BRANCH_WIDTH), act(X_WIDTH)]
    scratch = []
    if mixer == "gdn":
        outs.append(act(BA_PAD, F32))
    else:
        args += [hist, cw]
        in_specs += [hist_spec, pl.BlockSpec((width, cc), lambda b, l: (0, 0))]
        scratch.append(pltpu.VMEM((bt, SUBLANES, cc), F32))
    outs.append((hist_spec, jax.ShapeDtypeStruct((bn, width - 1, cc), F32)))
    return pl.pallas_call(
        functools.partial(_inproj_kernel, mixer=mixer, bt=bt, lt=lt),
        grid=(bn // bt, seq // lt),
        in_specs=in_specs,
        out_specs=[o[0] for o in outs],
        out_shape=[o[1] for o in outs],
        scratch_shapes=scratch,
        compiler_params=_cparams("arbitrary", "arbitrary"),
        name="inproj_" + mixer,
    )(*args)


UNITS_PER_WAVE = 4


def _gdn_kernel(qkv_ref, ba_ref, s0_ref, hist_ref, cw_ref, alog_ref, dtb_ref, og_ref,
                o_ref, s_ref, carry, *, chunk, group):
    bt, lt, _ = qkv_ref.shape
    nchunks = lt // chunk

    @pl.when(pl.program_id(1) == 0)
    def _():
        s_ref[...] = s0_ref[...]

    _init_conv_carry(carry, hist_ref)

    heads = range(GDN_HEADS)
    groups = [tuple(range(g, g + group)) for g in range(0, GDN_HEADS, group)]
    gw = group * chunk
    row = lax.broadcasted_iota(jnp.int32, (chunk, gw), 0)
    col = lax.broadcasted_iota(jnp.int32, (chunk, gw), 1) % chunk
    tril = row >= col
    strict = row > col
    lane_blk = lax.broadcasted_iota(jnp.int32, (1, gw), 1) // chunk
    ones_tril = (lax.broadcasted_iota(jnp.int32, (chunk, chunk), 0)
                 >= lax.broadcasted_iota(jnp.int32, (chunk, chunk), 1)).astype(F32)
    neg_a = -jnp.exp(alog_ref[...])
    dtb = dtb_ref[...]
    og = og_ref[...]

    def rows(c):
        return slice(c * chunk, (c + 1) * chunk)

    def lanes(base, h, n):
        return slice(base + h * n, base + (h + 1) * n)

    def cat(xs, axis):
        return xs[0] if len(xs) == 1 else jnp.concatenate(xs, axis=axis)

    def pick(xs):
        out = xs[0]
        for j in range(1, len(xs)):
            out = jnp.where(lane_blk == j, xs[j], out)
        return out

    def blockdiag(xs):
        z = jnp.zeros_like(xs[0])
        return cat([cat([x if i == j else z for i in range(len(xs))], 1)
                    for j, x in enumerate(xs)], 0)

    def blockdiag_of_lane_blocks(x):
        if group == 1:
            return x
        return cat([jnp.where(lane_blk == j, x, 0.0) for j in range(group)], 0)

    def conv_silu(u, cols):
        b, c = u
        x = qkv_ref[b, rows(c), cols].astype(F32)
        y = _causal_conv_cols(x[None], cols, carry, None, cw_ref, seqs=slice(b, b + 1))
        return _silu(y[0])

    def phase1(units, res):
        uh = [(u, h) for u in units for h in heads]
        ug = [(u, g) for u in units for g in range(len(groups))]
        gcum, gcum_t, beta_all = {}, {}, {}
        for u in units:
            ba = ba_ref[u[0], rows(u[1]), :]
            beta_all[u] = 1.0 / (1.0 + jnp.exp(-ba))
            z = ba + dtb
            softplus = jnp.maximum(z, 0.0) + jnp.log1p(jnp.exp(-jnp.abs(z)))
            gcum[u] = _dot_f32(ones_tril, neg_a * softplus)
        q, k, v = {}, {}, {}
        for u in units:
            for h in heads:
                q[u, h] = conv_silu(u, lanes(0, h, GDN_DK))
                k[u, h] = conv_silu(u, lanes(QK_WIDTH, h, GDN_DK))
                v[u, h] = conv_silu(u, lanes(2 * QK_WIDTH, h, GDN_DV))
            yield
        for key in uh:
            x = q[key]
            q[key] = x * (lax.rsqrt(jnp.sum(x * x, axis=-1, keepdims=True) + EPS)
                          * (GDN_DK ** -0.5))
            x = k[key]
            k[key] = x * lax.rsqrt(jnp.sum(x * x, axis=-1, keepdims=True) + EPS)
        yield
        for u in units:
            gcum_t[u] = cat([gcum[u]] * group, 0).T
        gc = {(u, h): jnp.broadcast_to(gcum[u][:, GDN_HEADS + h:GDN_HEADS + h + 1],
                                       (chunk, GDN_DK)) for u, h in uh}
        gl = {(u, h): gcum[u][chunk - 1:chunk, GDN_HEADS + h:GDN_HEADS + h + 1]
              for u, h in uh}
        beta = {(u, h): jnp.broadcast_to(beta_all[u][:, h:h + 1], (chunk, GDN_DK))
                for u, h in uh}
        eg = {key: jnp.exp(gc[key]) for key in uh}
        kb = {key: k[key] * beta[key] for key in uh}
        kkqk = {}
        for u, g in ug:
            grp = groups[g]
            kkqk[u, g] = _dot_nt(
                cat([cat([kb[u, h] for h in grp], 1), cat([q[u, h] for h in grp], 1)], 0),
                blockdiag([k[u, h] for h in grp]))
        yield
        n, p, aqk = {}, {}, {}
        for u, g in ug:
            grp = groups[g]
            g_col = pick([gc[u, h][:, :gw] for h in grp])
            g_row = pick([gcum_t[u][GDN_HEADS + h:GDN_HEADS + h + 1, :] for h in grp])
            decay = jnp.exp(jnp.where(tril, g_col - g_row, -jnp.inf))
            a = jnp.where(strict, kkqk[u, g][:chunk] * decay, 0.0)
            aqk[u, g] = (kkqk[u, g][chunk:] * decay).astype(BF16)
            n[u, g] = -a
            p[u, g] = _dot(a, blockdiag_of_lane_blocks(a))
        yield
        span = 2
        while 2 * span < chunk:
            for key in ug:
                both = _dot(cat([n[key], p[key]], 0), blockdiag_of_lane_blocks(p[key]))
                n[key] = n[key] + p[key] + both[:chunk]
                p[key] = both[chunk:]
            span *= 2
            yield
        for key in ug:
            n[key] = n[key] + p[key] + _dot(n[key], blockdiag_of_lane_blocks(p[key]))
        yield
        wu = {}
        for u, h in uh:
            rhs = jnp.concatenate([kb[u, h] * eg[u, h], v[u, h] * beta[u, h]], axis=-1)
            wu[u, h] = rhs + _dot(n[u, h // group][:, lanes(0, h % group, chunk)], rhs)
        yield
        for u in units:
            res[u] = dict(
                w_qd=[jnp.concatenate([wu[u, h][:, :GDN_DK], q[u, h] * eg[u, h]],
                                      axis=0).astype(BF16) for h in heads],
                u=[wu[u, h][:, GDN_DK:] for h in heads],
                kd=[(k[u, h] * jnp.exp(gl[u, h] - gc[u, h])).astype(BF16) for h in heads],
                aqk=[aqk[u, g] for g in range(len(groups))],
                egl=[jnp.exp(gl[u, h]) for h in heads])
        yield

    def phase2(units, res):
        for c in sorted({c for _, c in units}):
            us = [u for u in units if u[1] == c]
            uh = [(u, h) for u in us for h in heads]
            s = {(u, h): s_ref[u[0], h] for u, h in uh}
            ws_qs = {(u, h): _dot(res[u]["w_qd"][h], s[u, h]) for u, h in uh}
            yield
            v_new = {(u, h): res[u]["u"][h] - ws_qs[u, h][:chunk] for u, h in uh}
            o_grp = {(u, g): _dot(res[u]["aqk"][g], blockdiag([v_new[u, h] for h in grp]))
                     for u in us for g, grp in enumerate(groups)}
            for u, h in uh:
                s_ref[u[0], h] = (s[u, h] * res[u]["egl"][h]
                                  + _dot_tn(res[u]["kd"][h], v_new[u, h]))
            yield
            for u, h in uh:
                oh = ws_qs[u, h][chunk:] + o_grp[u, h // group][:, lanes(0, h % group, GDN_DV)]
                oh = oh * lax.rsqrt(jnp.mean(oh * oh, axis=-1, keepdims=True) + EPS) * og
                o_ref[u[0], rows(c), lanes(0, h, GDN_DV)] = oh.astype(o_ref.dtype)
            yield

    units = [(b, c) for b in range(bt) for c in range(nchunks)]
    waves = [units[i:i + UNITS_PER_WAVE] for i in range(0, len(units), UNITS_PER_WAVE)]
    res = {}
    prev = None
    for wave in waves:
        gens = [phase1(wave, res)]
        if prev is not None:
            gens.append(phase2(prev, res))
        _round_robin(gens)
        prev = wave
    _round_robin([phase2(prev, res)])


def _gdn(qkv, ba, s0, hist, cw, alog_pad, dtb_pad, og, *, chunk, bt, lt):
    group = 2 if 2 * chunk == LANES else 1
    bn, seq, _ = qkv.shape
    width, cc = cw.shape
    s_spec = pl.BlockSpec((bt, GDN_HEADS, GDN_DK, GDN_DV), lambda b, l: (b, 0, 0, 0))
    hist_spec = pl.BlockSpec((bt, width - 1, cc), lambda b, l: (b, 0, 0))
    vec = pl.BlockSpec((1, LANES), lambda b, l: (0, 0))
    return pl.pallas_call(
        functools.partial(_gdn_kernel, chunk=chunk, group=group),
        grid=(bn // bt, seq // lt),
        in_specs=[
            pl.BlockSpec((bt, lt, QKV_WIDTH), lambda b, l: (b, l, 0)),
            pl.BlockSpec((bt, lt, BA_PAD), lambda b, l: (b, l, 0)),
            s_spec, hist_spec,
            pl.BlockSpec((width, cc), lambda b, l: (0, 0)),
            vec, vec, vec,
        ],
        out_specs=[
            pl.BlockSpec((bt, lt, GDN_WIDTH), lambda b, l: (b, l, 0)),
            s_spec,
        ],
        out_shape=[
            jax.ShapeDtypeStruct((bn, seq, GDN_WIDTH), qkv.dtype),
            jax.ShapeDtypeStruct(s0.shape, F32),
        ],
        scratch_shapes=[pltpu.VMEM((bt, SUBLANES, cc), F32)],
        compiler_params=_cparams("arbitrary", "arbitrary"),
        name="gdn",
    )(qkv, ba, s0, hist, cw, alog_pad, dtb_pad, og)


ATTN_ROWS = 128
ATTN_SEQS = 4


def _attn_out_kernel(tok_ref, xq_ref, gate_ref, x_ref, kt_ref, vt_ref, w_ref, fg_ref,
                     y_ref, xo_scr, *, bt, lt, final):
    lane_head = lax.broadcasted_iota(jnp.int32, (1, X_WIDTH), 1) // X_HEAD_DIM
    scale = X_HEAD_DIM ** -0.5
    tm = bt * lt

    def attend(b, rs):
        r = rs.stop - rs.start
        q = xq_ref[b, rs, :]
        q = q * jnp.asarray(scale, q.dtype)
        qx = jnp.concatenate([jnp.where(lane_head == h, q, jnp.zeros_like(q))
                              for h in range(X_HEADS)], axis=0)
        s = _dot(qx, kt_ref[b])
        yield
        e = jnp.exp(s - jnp.max(s, axis=-1, keepdims=True))
        p = e / jnp.sum(e, axis=-1, keepdims=True)
        o4 = _dot_nt(p, vt_ref[b])
        yield
        xo = jnp.where(lane_head == 0, o4[0:r], 0.0)
        for h in range(1, X_HEADS):
            xo = xo + jnp.where(lane_head == h, o4[h * r:(h + 1) * r], 0.0)
        xo_scr[b, rs, :] = xo
        yield

    acc = {}

    def tok_proj():
        sg = _silu(gate_ref[:, :, :GDN_WIDTH].astype(F32).reshape(tm, GDN_WIDTH))
        br = (tok_ref[...].astype(F32).reshape(tm, GDN_WIDTH) * sg).astype(BF16)
        for cols in _col_chunks(D_MODEL):
            acc[cols.start] = (x_ref[:, :, cols].reshape(tm, -1)
                               + _dot(br, w_ref[:GDN_WIDTH, cols]))
            yield

    if bt == 1:
        blocks = [slice(r, r + ATTN_ROWS) for r in range(0, lt, ATTN_ROWS)]
        _round_robin([_delayed(attend(0, rs), i) for i, rs in enumerate(blocks)] + [tok_proj()])
    else:
        def seq_group(i, carry):
            _round_robin([attend(i * ATTN_SEQS + j, slice(0, lt)) for j in range(ATTN_SEQS)])
            return carry

        lax.fori_loop(0, bt // ATTN_SEQS, seq_group, 0)
        _round_robin([tok_proj()])

    sg_x = _silu(gate_ref[:, :, GDN_WIDTH:].astype(F32).reshape(tm, X_WIDTH))
    br_x = (xo_scr[...].reshape(tm, X_WIDTH) * sg_x).astype(BF16)
    y = [acc[cols.start] + _dot(br_x, w_ref[GDN_WIDTH:, cols]) for cols in _col_chunks(D_MODEL)]
    if final:
        ms = sum(jnp.sum(c * c, axis=-1, keepdims=True) for c in y) * (1.0 / D_MODEL)
        r = lax.rsqrt(ms + EPS)
        y = [c * r * fg_ref[:, cols] for c, cols in zip(y, _col_chunks(D_MODEL))]
    for c, cols in zip(y, _col_chunks(D_MODEL)):
        y_ref[:, :, cols] = c.reshape(bt, lt, -1)


def _attn_out(tok, xq, gate, x, mk, mv, w_bf16, fg, *, layer, bt, lt, final):
    bn, seq, _ = x.shape

    def act(n):
        return pl.BlockSpec((bt, lt, n), lambda b, l: (b, l, 0))

    mem = pl.BlockSpec((None, bt, X_WIDTH, N_MEM), lambda b, l: (layer, b, 0, 0))
    return pl.pallas_call(
        functools.partial(_attn_out_kernel, bt=bt, lt=lt, final=final),
        grid=(bn // bt, seq // lt),
        in_specs=[
            act(GDN_WIDTH), act(X_WIDTH), act(BRANCH_WIDTH), act(D_MODEL), mem, mem,
            pl.BlockSpec((BRANCH_WIDTH, D_MODEL), lambda b, l: (0, 0)),
            pl.BlockSpec((1, D_MODEL), lambda b, l: (0, 0)),
        ],
        out_specs=act(D_MODEL),
        out_shape=jax.ShapeDtypeStruct(x.shape, F32),
        scratch_shapes=[pltpu.VMEM((bt, lt, X_WIDTH), F32)],
        compiler_params=_cparams("arbitrary", "arbitrary"),
        name="attn_out",
    )(tok, xq, gate, x, mk, mv, w_bf16, fg)


def _trunk(x, mem_k, mem_v, gdn_s, gdn_conv, sc_conv, p, *, bt, lt, chunk, gdn_bt, gdn_lt,
           act_dtype):
    qkv, gate, xq, ba, gconv_new = _inproj(
        x, p["norm_g"][0:1], p["w_in_a"], None, p["conv_w_a"], mixer="gdn", bt=bt, lt=lt,
        act_dtype=act_dtype)
    tok, s_new = _gdn(qkv, ba, gdn_s, gdn_conv, p["conv_w_a"], p["alog_pad"],
                      p["dtb_pad"], p["o_norm_g"], chunk=chunk, bt=gdn_bt, lt=gdn_lt)
    x = _attn_out(tok, xq, gate, x, mem_k, mem_v, p["w_out"][0], p["final_norm_g"],
                  layer=0, bt=bt, lt=lt, final=False)
    tok, gate, xq, sconv_new = _inproj(
        x, p["norm_g"][1:2], p["w_in_b"], sc_conv, p["conv_w_b"], mixer="sconv", bt=bt, lt=lt,
        act_dtype=act_dtype)
    y = _attn_out(tok, xq, gate, x, mem_k, mem_v, p["w_out"][1], p["final_norm_g"],
                  layer=1, bt=bt, lt=lt, final=True)
    return y, s_new[None], gconv_new[None], sconv_new[None]


def kernel(x_prompt, x_sample, mem_prompt, state_gdn, state_gdn_conv, state_sconv, cache_mem_k, cache_mem_v, norm_g, w_in_a, conv_w_a, a_log, dt_bias, o_norm_g, w_in_b, conv_w_b, mem_norm_g, w_mem_kv, w_out, final_norm_g):
    bp = x_prompt.shape[0]

    wa = w_in_a[0]
    c_b = QKV_WIDTH
    c_g = QKV_WIDTH + 2 * GDN_HEADS
    wa = jnp.concatenate(
        [wa[:, :c_b], wa[:, c_g:], wa[:, c_b:c_g],
         jnp.zeros((D_MODEL, BA_PAD - 2 * GDN_HEADS), wa.dtype)], axis=1).astype(BF16)
    pad_lo = jnp.zeros((GDN_HEADS,), F32)
    pad_hi = jnp.zeros((LANES - 2 * GDN_HEADS,), F32)
    params = {
        "norm_g": norm_g,
        "w_in_a": wa,
        "conv_w_a": conv_w_a[0],
        "alog_pad": jnp.concatenate([pad_lo, a_log[0], pad_hi])[None],
        "dtb_pad": jnp.concatenate([pad_lo, dt_bias[0], pad_hi])[None],
        "o_norm_g": o_norm_g,
        "w_in_b": w_in_b[0].astype(BF16),
        "conv_w_b": conv_w_b[0],
        "w_out": w_out.astype(BF16),
        "final_norm_g": final_norm_g[None],
    }

    def to_cache(t):
        t = t.reshape(t.shape[0], t.shape[1], X_HEADS, X_HEAD_DIM, t.shape[3])
        return jnp.transpose(t, (0, 1, 4, 2, 3))

    def from_cache(t):
        t = jnp.transpose(t, (0, 1, 3, 4, 2))
        return t.reshape(t.shape[0], t.shape[1], X_WIDTH, t.shape[4])

    mem_kt, mem_vt = _memkv(mem_prompt, mem_norm_g[None], w_mem_kv.astype(BF16))
    mem_k_p = to_cache(mem_kt)
    mem_v_p = to_cache(mem_vt)

    s0_p = jnp.zeros((bp,) + state_gdn.shape[2:], F32)
    gc0_p = jnp.zeros((bp,) + state_gdn_conv.shape[2:], F32)
    sc0_p = jnp.zeros((bp,) + state_sconv.shape[2:], F32)
    y_p, s_p, gc_p, sc_p = _trunk(x_prompt, mem_kt, mem_vt, s0_p, gc0_p, sc0_p,
                                  params, bt=1, lt=512, chunk=GDN_CHUNK, gdn_bt=1, gdn_lt=512,
                                  act_dtype=BF16)
    dec_seq = x_sample.shape[1]
    y_s, s_s, gc_s, sc_s = _trunk(x_sample, from_cache(cache_mem_k), from_cache(cache_mem_v),
                                  state_gdn[0], state_gdn_conv[0], state_sconv[0], params,
                                  bt=32, lt=dec_seq, chunk=dec_seq, gdn_bt=8, gdn_lt=dec_seq,
                                  act_dtype=F32)
    return (y_p, y_s, s_p, gc_p, sc_p, mem_k_p, mem_v_p, s_s, gc_s, sc_s)
```

```python
import functools

import jax
import jax.numpy as jnp
from jax import lax
from jax.experimental import pallas as pl
from jax.experimental.pallas import tpu as pltpu

F32 = jnp.float32
BF16 = jnp.bfloat16

D_MODEL = 1024
N_MEM = 256
X_WIDTH = 256
X_HEADS = 4
X_HEAD_DIM = 64
GDN_HEADS = 6
GDN_DK = 128
GDN_DV = 128
GDN_WIDTH = GDN_HEADS * GDN_DV
QK_WIDTH = GDN_HEADS * GDN_DK
QKV_WIDTH = 2 * QK_WIDTH + GDN_WIDTH
GDN_CHUNK = 64
SC_WIDTH = 768
BRANCH_WIDTH = 1024
EPS = 1e-6

LANES = 128
SUBLANES = 8
COL_CHUNK = 256
BA_PAD = LANES
VMEM_LIMIT = 56 * 1024 * 1024

HIGHEST = lax.Precision.HIGHEST


def _cparams(*sem):
    return pltpu.CompilerParams(dimension_semantics=sem, vmem_limit_bytes=VMEM_LIMIT)


def _rms_rows(x, g):
    r = lax.rsqrt(jnp.mean(x * x, axis=-1, keepdims=True) + EPS)
    return x * r * g


def _silu(x):
    h = 0.5 * x
    return h + h * jnp.tanh(h)


def _dot(a, b):
    return jnp.dot(a.astype(BF16), b.astype(BF16), preferred_element_type=F32)


def _dot_nt(a, b):
    return lax.dot_general(a.astype(BF16), b.astype(BF16), (((1,), (1,)), ((), ())),
                           preferred_element_type=F32)


def _dot_tn(a, b):
    return lax.dot_general(a.astype(BF16), b.astype(BF16), (((0,), (0,)), ((), ())),
                           preferred_element_type=F32)


def _dot_f32(a, b):
    return jnp.dot(a, b, preferred_element_type=F32, precision=HIGHEST)


def _col_chunks(n):
    return [slice(c, min(c + COL_CHUNK, n)) for c in range(0, n, COL_CHUNK)]


def _round_robin(gens):
    gens = list(gens)
    while gens:
        alive = []
        for g in gens:
            try:
                next(g)
                alive.append(g)
            except StopIteration:
                pass
        gens = alive


def _delayed(gen, rounds):
    for _ in range(rounds):
        yield
    yield from gen


def _memkv_kernel(m_ref, g_ref, w_ref, kt_ref, vt_ref):
    h = _rms_rows(m_ref[...], g_ref[...])
    kv = _dot(h, w_ref[...])
    kt_ref[...] = kv[:, :X_WIDTH].T
    vt_ref[...] = kv[:, X_WIDTH:].T


def _memkv(mem, g, w_bf16):
    bn, n_mem, _ = mem.shape
    depth = w_bf16.shape[0]
    out = jax.ShapeDtypeStruct((depth, bn, X_WIDTH, n_mem), F32)
    out_spec = pl.BlockSpec((None, None, X_WIDTH, n_mem), lambda l, b: (l, b, 0, 0))
    return pl.pallas_call(
        _memkv_kernel,
        grid=(depth, bn),
        in_specs=[
            pl.BlockSpec((None, n_mem, D_MODEL), lambda l, b: (b, 0, 0)),
            pl.BlockSpec((1, D_MODEL), lambda l, b: (0, 0)),
            pl.BlockSpec((None, D_MODEL, 2 * X_WIDTH), lambda l, b: (l, 0, 0)),
        ],
        out_specs=[out_spec, out_spec],
        out_shape=[out, out],
        compiler_params=_cparams("arbitrary", "arbitrary"),
        name="memkv",
    )(mem, g, w_bf16)


def _init_conv_carry(carry, hist_ref):
    @pl.when(pl.program_id(1) == 0)
    def _():
        carry[:, SUBLANES - hist_ref.shape[1]:, :] = hist_ref[...]


def _causal_conv_cols(x, cols, carry, hist_out, cw_ref, seqs=slice(None)):
    width = cw_ref.shape[0]
    bt, lt, n = x.shape
    g = lt // SUBLANES
    xe = jnp.concatenate([carry[seqs, :, cols], x], axis=1).reshape(bt, g + 1, SUBLANES, n)
    sub = lax.broadcasted_iota(jnp.int32, (1, 1, SUBLANES, n), 2)
    y = x.reshape(bt, g, SUBLANES, n) * cw_ref[width - 1:width, cols]
    for j in range(width - 1):
        s = width - 1 - j
        r = pltpu.roll(xe, s, axis=2)
        y = y + jnp.where(sub < s, r[:, :g], r[:, 1:]) * cw_ref[j:j + 1, cols]
    carry[seqs, :, cols] = x[:, lt - SUBLANES:, :]
    if hist_out is not None:
        hist_out[seqs, :, cols] = x[:, lt - (width - 1):, :]
    return y.reshape(bt, lt, n)


def _inproj_kernel(x_ref, g_ref, w_ref, *rest, mixer, bt, lt, w_transposed):
    if mixer == "gdn":
        mix_out, *plain_outs, hist_out = rest
    else:
        hist_ref, cw_ref, mix_out, *plain_outs, hist_out, carry = rest
        _init_conv_carry(carry, hist_ref)
    tm = bt * lt
    h = _rms_rows(x_ref[...].reshape(tm, D_MODEL), g_ref[...]).astype(BF16)

    def proj(cols, base=0):
        cols = slice(base + cols.start, base + cols.stop)
        if w_transposed:
            y = _dot_nt(h, w_ref[cols, :])
        else:
            y = jnp.dot(h, w_ref[:, cols], preferred_element_type=F32)
        return y.reshape(bt, lt, -1)

    def mixer_chunk(cols):
        if mixer == "gdn":
            y = proj(cols)
            hist_out[:, :, cols] = y[:, lt - hist_out.shape[1]:, :]
        else:
            gate_b, pre = proj(cols), proj(cols, SC_WIDTH) * proj(cols, 2 * SC_WIDTH)
            yield
            y = gate_b * _causal_conv_cols(pre, cols, carry, hist_out, cw_ref)
        mix_out[:, :, cols] = y.astype(mix_out.dtype)
        yield

    def plain_chunk(out, cols, base):
        out[:, :, cols] = proj(cols, base).astype(out.dtype)
        yield

    mix_width = mix_out.shape[-1]
    gens = [mixer_chunk(cols) for cols in _col_chunks(mix_width)]
    base = mix_width if mixer == "gdn" else 3 * mix_width
    for out in plain_outs:
        gens += [plain_chunk(out, cols, base) for cols in _col_chunks(out.shape[-1])]
        base += out.shape[-1]
    _round_robin([_delayed(g, i) for i, g in enumerate(gens)])


def _inproj(x, g, w_bf16, hist, cw, *, mixer, bt, lt, act_dtype):
    bn, seq, _ = x.shape
    w_transposed = mixer == "gdn"
    width, cc = cw.shape

    def act(n, dtype=act_dtype):
        return (pl.BlockSpec((bt, lt, n), lambda b, l: (b, l, 0)),
                jax.ShapeDtypeStruct((bn, seq, n), dtype))

    hist_spec = pl.BlockSpec((bt, width - 1, cc), lambda b, l: (b, 0, 0))
    args = [x, g, w_bf16]
    in_specs = [
        pl.BlockSpec((bt, lt, D_MODEL), lambda b, l: (b, l, 0)),
        pl.BlockSpec((1, D_MODEL), lambda b, l: (0, 0)),
        pl.BlockSpec(w_bf16.shape, lambda b, l: (0, 0)),
    ]
    outs = [act(cc), act(BRANCH_WIDTH), act(X_WIDTH)]
    scratch = []
    if mixer == "gdn":
        outs.append(act(BA_PAD, F32))
    else:
        args += [hist, cw]
        in_specs += [hist_spec, pl.BlockSpec((width, cc), lambda b, l: (0, 0))]
        scratch.append(pltpu.VMEM((bt, SUBLANES, cc), F32))
    outs.append((hist_spec, jax.ShapeDtypeStruct((bn, width - 1, cc), F32)))
    return pl.pallas_call(
        functools.partial(_inproj_kernel, mixer=mixer, bt=bt, lt=lt,
                          w_transposed=w_transposed),
        grid=(bn // bt, seq // lt),
        in_specs=in_specs,
        out_specs=[o[0] for o in outs],
        out_shape=[o[1] for o in outs],
        scratch_shapes=scratch,
        compiler_params=_cparams("arbitrary", "arbitrary"),
        name="inproj_" + mixer,
    )(*args)


UNITS_PER_WAVE = 4


def _gdn_kernel(qkv_ref, ba_ref, s0_ref, hist_ref, cw_ref, alog_ref, dtb_ref, og_ref,
                o_ref, s_ref, carry, *, chunk, group):
    bt, lt, _ = qkv_ref.shape
    nchunks = lt // chunk

    @pl.when(pl.program_id(1) == 0)
    def _():
        s_ref[...] = s0_ref[...]

    _init_conv_carry(carry, hist_ref)

    heads = range(GDN_HEADS)
    groups = [tuple(range(g, g + group)) for g in range(0, GDN_HEADS, group)]
    gw = group * chunk
    row = lax.broadcasted_iota(jnp.int32, (chunk, gw), 0)
    col = lax.broadcasted_iota(jnp.int32, (chunk, gw), 1) % chunk
    tril = row >= col
    strict = row > col
    lane_blk = lax.broadcasted_iota(jnp.int32, (1, gw), 1) // chunk
    ones_tril = (lax.broadcasted_iota(jnp.int32, (chunk, chunk), 0)
                 >= lax.broadcasted_iota(jnp.int32, (chunk, chunk), 1)).astype(F32)
    neg_a = -jnp.exp(alog_ref[...])
    dtb = dtb_ref[...]
    og = og_ref[...]

    def rows(c):
        return slice(c * chunk, (c + 1) * chunk)

    def lanes(base, h, n):
        return slice(base + h * n, base + (h + 1) * n)

    def cat(xs, axis):
        return xs[0] if len(xs) == 1 else jnp.concatenate(xs, axis=axis)

    def pick(xs):
        out = xs[0]
        for j in range(1, len(xs)):
            out = jnp.where(lane_blk == j, xs[j], out)
        return out

    def blockdiag(xs):
        z = jnp.zeros_like(xs[0])
        return cat([cat([x if i == j else z for i in range(len(xs))], 1)
                    for j, x in enumerate(xs)], 0)

    def blockdiag_of_lane_blocks(x):
        if group == 1:
            return x
        return cat([jnp.where(lane_blk == j, x, 0.0) for j in range(group)], 0)

    def conv_silu(u, cols):
        b, c = u
        x = qkv_ref[b, rows(c), cols].astype(F32)
        y = _causal_conv_cols(x[None], cols, carry, None, cw_ref, seqs=slice(b, b + 1))
        return _silu(y[0])

    def phase1(units, res):
        uh = [(u, h) for u in units for h in heads]
        ug = [(u, g) for u in units for g in range(len(groups))]
        gcum, gcum_t, beta_all = {}, {}, {}
        for u in units:
            ba = ba_ref[u[0], rows(u[1]), :]
            beta_all[u] = 1.0 / (1.0 + jnp.exp(-ba))
            z = ba + dtb
            softplus = jnp.maximum(z, 0.0) + jnp.log1p(jnp.exp(-jnp.abs(z)))
            gcum[u] = _dot_f32(ones_tril, neg_a * softplus)
        q, k, v = {}, {}, {}
        for u in units:
            for h in heads:
                q[u, h] = conv_silu(u, lanes(0, h, GDN_DK))
                k[u, h] = conv_silu(u, lanes(QK_WIDTH, h, GDN_DK))
                v[u, h] = conv_silu(u, lanes(2 * QK_WIDTH, h, GDN_DV))
            yield
        for key in uh:
            x = q[key]
            q[key] = x * (lax.rsqrt(jnp.sum(x * x, axis=-1, keepdims=True) + EPS)
                          * (GDN_DK ** -0.5))
            x = k[key]
            k[key] = x * lax.rsqrt(jnp.sum(x * x, axis=-1, keepdims=True) + EPS)
        yield
        for u in units:
            gcum_t[u] = cat([gcum[u]] * group, 0).T
        gc = {(u, h): jnp.broadcast_to(gcum[u][:, GDN_HEADS + h:GDN_HEADS + h + 1],
                                       (chunk, GDN_DK)) for u, h in uh}
        gl = {(u, h): gcum[u][chunk - 1:chunk, GDN_HEADS + h:GDN_HEADS + h + 1]
              for u, h in uh}
        beta = {(u, h): jnp.broadcast_to(beta_all[u][:, h:h + 1], (chunk, GDN_DK))
                for u, h in uh}
        eg = {key: jnp.exp(gc[key]) for key in uh}
        kb = {key: k[key] * beta[key] for key in uh}
        kkqk = {}
        for u, g in ug:
            grp = groups[g]
            kkqk[u, g] = _dot_nt(
                cat([cat([kb[u, h] for h in grp], 1), cat([q[u, h] for h in grp], 1)], 0),
                blockdiag([k[u, h] for h in grp]))
        yield
        n, p, aqk = {}, {}, {}
        for u, g in ug:
            grp = groups[g]
            g_col = pick([gc[u, h][:, :gw] for h in grp])
            g_row = pick([gcum_t[u][GDN_HEADS + h:GDN_HEADS + h + 1, :] for h in grp])
            decay = jnp.exp(jnp.where(tril, g_col - g_row, -jnp.inf))
            a = jnp.where(strict, kkqk[u, g][:chunk] * decay, 0.0)
            aqk[u, g] = (kkqk[u, g][chunk:] * decay).astype(BF16)
            n[u, g] = -a
            p[u, g] = _dot(a, blockdiag_of_lane_blocks(a))
        yield
        span = 2
        while 2 * span < chunk:
            for key in ug:
                both = _dot(cat([n[key], p[key]], 0), blockdiag_of_lane_blocks(p[key]))
                n[key] = n[key] + p[key] + both[:chunk]
                p[key] = both[chunk:]
            span *= 2
            yield
        for key in ug:
            n[key] = n[key] + p[key] + _dot(n[key], blockdiag_of_lane_blocks(p[key]))
        yield
        wu = {}
        for u, h in uh:
            rhs = jnp.concatenate([kb[u, h] * eg[u, h], v[u, h] * beta[u, h]], axis=-1)
            wu[u, h] = rhs + _dot(n[u, h // group][:, lanes(0, h % group, chunk)], rhs)
        yield
        for u in units:
            res[u] = dict(
                w_qd=[jnp.concatenate([wu[u, h][:, :GDN_DK], q[u, h] * eg[u, h]],
                                      axis=0).astype(BF16) for h in heads],
                u=[wu[u, h][:, GDN_DK:] for h in heads],
                kd=[(k[u, h] * jnp.exp(gl[u, h] - gc[u, h])).astype(BF16) for h in heads],
                aqk=[aqk[u, g] for g in range(len(groups))],
                egl=[jnp.exp(gl[u, h]) for h in heads])
        yield

    def phase2(units, res):
        for c in sorted({c for _, c in units}):
            us = [u for u in units if u[1] == c]
            uh = [(u, h) for u in us for h in heads]
            s = {(u, h): s_ref[u[0], h] for u, h in uh}
            ws_qs = {(u, h): _dot(res[u]["w_qd"][h], s[u, h]) for u, h in uh}
            yield
            v_new = {(u, h): res[u]["u"][h] - ws_qs[u, h][:chunk] for u, h in uh}
            o_grp = {(u, g): _dot(res[u]["aqk"][g], blockdiag([v_new[u, h] for h in grp]))
                     for u in us for g, grp in enumerate(groups)}
            for u, h in uh:
                s_ref[u[0], h] = (s[u, h] * res[u]["egl"][h]
                                  + _dot_tn(res[u]["kd"][h], v_new[u, h]))
            yield
            for u, h in uh:
                oh = ws_qs[u, h][chunk:] + o_grp[u, h // group][:, lanes(0, h % group, GDN_DV)]
                oh = oh * lax.rsqrt(jnp.mean(oh * oh, axis=-1, keepdims=True) + EPS) * og
                o_ref[u[0], rows(c), lanes(0, h, GDN_DV)] = oh.astype(o_ref.dtype)
            yield

    units = [(b, c) for b in range(bt) for c in range(nchunks)]
    waves = [units[i:i + UNITS_PER_WAVE] for i in range(0, len(units), UNITS_PER_WAVE)]
    res = {}
    prev = None
    for wave in waves:
        gens = [phase1(wave, res)]
        if prev is not None:
            gens.append(phase2(prev, res))
        _round_robin(gens)
        prev = wave
    _round_robin([phase2(prev, res)])


def _gdn(qkv, ba, s0, hist, cw, alog_pad, dtb_pad, og, *, chunk, bt, lt):
    group = 2 if 2 * chunk == LANES else 1
    bn, seq, _ = qkv.shape
    width, cc = cw.shape
    s_spec = pl.BlockSpec((bt, GDN_HEADS, GDN_DK, GDN_DV), lambda b, l: (b, 0, 0, 0))
    hist_spec = pl.BlockSpec((bt, width - 1, cc), lambda b, l: (b, 0, 0))
    vec = pl.BlockSpec((1, LANES), lambda b, l: (0, 0))
    return pl.pallas_call(
        functools.partial(_gdn_kernel, chunk=chunk, group=group),
        grid=(bn // bt, seq // lt),
        in_specs=[
            pl.BlockSpec((bt, lt, QKV_WIDTH), lambda b, l: (b, l, 0)),
            pl.BlockSpec((bt, lt, BA_PAD), lambda b, l: (b, l, 0)),
            s_spec, hist_spec,
            pl.BlockSpec((width, cc), lambda b, l: (0, 0)),
            vec, vec, vec,
        ],
        out_specs=[
            pl.BlockSpec((bt, lt, GDN_WIDTH), lambda b, l: (b, l, 0)),
            s_spec,
        ],
        out_shape=[
            jax.ShapeDtypeStruct((bn, seq, GDN_WIDTH), qkv.dtype),
            jax.ShapeDtypeStruct(s0.shape, F32),
        ],
        scratch_shapes=[pltpu.VMEM((bt, SUBLANES, cc), F32)],
        compiler_params=_cparams("arbitrary", "arbitrary"),
        name="gdn",
    )(qkv, ba, s0, hist, cw, alog_pad, dtb_pad, og)


ATTN_ROWS = 128
ATTN_SEQS = 4


def _attn_out_kernel(tok_ref, xq_ref, gate_ref, x_ref, kt_ref, vt_ref, w_ref, fg_ref,
                     y_ref, xo_scr, *, bt, lt, final):
    lane_head = lax.broadcasted_iota(jnp.int32, (1, X_WIDTH), 1) // X_HEAD_DIM
    scale = X_HEAD_DIM ** -0.5
    tm = bt * lt

    def attend(b, rs):
        r = rs.stop - rs.start
        q = xq_ref[b, rs, :]
        q = q * jnp.asarray(scale, q.dtype)
        qx = jnp.concatenate([jnp.where(lane_head == h, q, jnp.zeros_like(q))
                              for h in range(X_HEADS)], axis=0)
        s = _dot(qx, kt_ref[b])
        yield
        e = jnp.exp(s - jnp.max(s, axis=-1, keepdims=True))
        p = e / jnp.sum(e, axis=-1, keepdims=True)
        o4 = _dot_nt(p, vt_ref[b])
        yield
        xo = jnp.where(lane_head == 0, o4[0:r], 0.0)
        for h in range(1, X_HEADS):
            xo = xo + jnp.where(lane_head == h, o4[h * r:(h + 1) * r], 0.0)
        xo_scr[b, rs, :] = xo
        yield

    acc = {}

    def tok_proj():
        sg = _silu(gate_ref[:, :, :GDN_WIDTH].astype(F32).reshape(tm, GDN_WIDTH))
        br = (tok_ref[...].astype(F32).reshape(tm, GDN_WIDTH) * sg).astype(BF16)
        for cols in _col_chunks(D_MODEL):
            acc[cols.start] = (x_ref[:, :, cols].reshape(tm, -1)
                               + _dot(br, w_ref[:GDN_WIDTH, cols]))
            yield

    if bt == 1:
        blocks = [slice(r, r + ATTN_ROWS) for r in range(0, lt, ATTN_ROWS)]
        _round_robin([_delayed(attend(0, rs), i) for i, rs in enumerate(blocks)] + [tok_proj()])
    else:
        def seq_group(i, carry):
            _round_robin([attend(i * ATTN_SEQS + j, slice(0, lt)) for j in range(ATTN_SEQS)])
            return carry

        lax.fori_loop(0, bt // ATTN_SEQS, seq_group, 0)
        _round_robin([tok_proj()])

    sg_x = _silu(gate_ref[:, :, GDN_WIDTH:].astype(F32).reshape(tm, X_WIDTH))
    br_x = (xo_scr[...].reshape(tm, X_WIDTH) * sg_x).astype(BF16)
    y = [acc[cols.start] + _dot(br_x, w_ref[GDN_WIDTH:, cols]) for cols in _col_chunks(D_MODEL)]
    if final:
        ms = sum(jnp.sum(c * c, axis=-1, keepdims=True) for c in y) * (1.0 / D_MODEL)
        r = lax.rsqrt(ms + EPS)
        y = [c * r * fg_ref[:, cols] for c, cols in zip(y, _col_chunks(D_MODEL))]
    for c, cols in zip(y, _col_chunks(D_MODEL)):
        y_ref[:, :, cols] = c.reshape(bt, lt, -1)


def _attn_out(tok, xq, gate, x, mk, mv, w_bf16, fg, *, layer, bt, lt, final):
    bn, seq, _ = x.shape

    def act(n):
        return pl.BlockSpec((bt, lt, n), lambda b, l: (b, l, 0))

    mem = pl.BlockSpec((None, bt, X_WIDTH, N_MEM), lambda b, l: (layer, b, 0, 0))
    return pl.pallas_call(
        functools.partial(_attn_out_kernel, bt=bt, lt=lt, final=final),
        grid=(bn // bt, seq // lt),
        in_specs=[
            act(GDN_WIDTH), act(X_WIDTH), act(BRANCH_WIDTH), act(D_MODEL), mem, mem,
            pl.BlockSpec((BRANCH_WIDTH, D_MODEL), lambda b, l: (0, 0)),
            pl.BlockSpec((1, D_MODEL), lambda b, l: (0, 0)),
        ],
        out_specs=act(D_MODEL),
        out_shape=jax.ShapeDtypeStruct(x.shape, F32),
        scratch_shapes=[pltpu.VMEM((bt, lt, X_WIDTH), F32)],
        compiler_params=_cparams("arbitrary", "arbitrary"),
        name="attn_out",
    )(tok, xq, gate, x, mk, mv, w_bf16, fg)


def _trunk(x, mem_k, mem_v, gdn_s, gdn_conv, sc_conv, p, *, bt, lt, chunk, gdn_bt, gdn_lt,
           act_dtype):
    qkv, gate, xq, ba, gconv_new = _inproj(
        x, p["norm_g"][0:1], p["w_in_a"], None, p["conv_w_a"], mixer="gdn", bt=bt, lt=lt,
        act_dtype=act_dtype)
    tok, s_new = _gdn(qkv, ba, gdn_s, gdn_conv, p["conv_w_a"], p["alog_pad"],
                      p["dtb_pad"], p["o_norm_g"], chunk=chunk, bt=gdn_bt, lt=gdn_lt)
    x = _attn_out(tok, xq, gate, x, mem_k, mem_v, p["w_out"][0], p["final_norm_g"],
                  layer=0, bt=bt, lt=lt, final=False)
    tok, gate, xq, sconv_new = _inproj(
        x, p["norm_g"][1:2], p["w_in_b"], sc_conv, p["conv_w_b"], mixer="sconv", bt=bt, lt=lt,
        act_dtype=act_dtype)
    y = _attn_out(tok, xq, gate, x, mem_k, mem_v, p["w_out"][1], p["final_norm_g"],
                  layer=1, bt=bt, lt=lt, final=True)
    return y, s_new[None], gconv_new[None], sconv_new[None]


def kernel(x_prompt, x_sample, mem_prompt, state_gdn, state_gdn_conv, state_sconv, cache_mem_k, cache_mem_v, norm_g, w_in_a, conv_w_a, a_log, dt_bias, o_norm_g, w_in_b, conv_w_b, mem_norm_g, w_mem_kv, w_out, final_norm_g):
    bp = x_prompt.shape[0]

    wa = jnp.transpose(w_in_a[0])
    c_b = QKV_WIDTH
    c_g = QKV_WIDTH + 2 * GDN_HEADS
    wa = jnp.concatenate(
        [wa[:c_b], wa[c_g:], wa[c_b:c_g],
         jnp.zeros((BA_PAD - 2 * GDN_HEADS, D_MODEL), wa.dtype)], axis=0).astype(BF16)
    pad_lo = jnp.zeros((GDN_HEADS,), F32)
    pad_hi = jnp.zeros((LANES - 2 * GDN_HEADS,), F32)
    params = {
        "norm_g": norm_g,
        "w_in_a": wa,
        "conv_w_a": conv_w_a[0],
        "alog_pad": jnp.concatenate([pad_lo, a_log[0], pad_hi])[None],
        "dtb_pad": jnp.concatenate([pad_lo, dt_bias[0], pad_hi])[None],
        "o_norm_g": o_norm_g,
        "w_in_b": w_in_b[0].astype(BF16),
        "conv_w_b": conv_w_b[0],
        "w_out": w_out.astype(BF16),
        "final_norm_g": final_norm_g[None],
    }

    def to_cache(t):
        t = t.reshape(t.shape[0], t.shape[1], X_HEADS, X_HEAD_DIM, t.shape[3])
        return jnp.transpose(t, (0, 1, 4, 2, 3))

    def from_cache(t):
        t = jnp.transpose(t, (0, 1, 3, 4, 2))
        return t.reshape(t.shape[0], t.shape[1], X_WIDTH, t.shape[4])

    mem_kt, mem_vt = _memkv(mem_prompt, mem_norm_g[None], w_mem_kv.astype(BF16))
    mem_k_p = to_cache(mem_kt)
    mem_v_p = to_cache(mem_vt)

    s0_p = jnp.zeros((bp,) + state_gdn.shape[2:], F32)
    gc0_p = jnp.zeros((bp,) + state_gdn_conv.shape[2:], F32)
    sc0_p = jnp.zeros((bp,) + state_sconv.shape[2:], F32)
    y_p, s_p, gc_p, sc_p = _trunk(x_prompt, mem_kt, mem_vt, s0_p, gc0_p, sc0_p,
                                  params, bt=1, lt=512, chunk=GDN_CHUNK, gdn_bt=1, gdn_lt=512,
                                  act_dtype=BF16)
    dec_seq = x_sample.shape[1]
    y_s, s_s, gc_s, sc_s = _trunk(x_sample, from_cache(cache_mem_k), from_cache(cache_mem_v),
                                  state_gdn[0], state_gdn_conv[0], state_sconv[0], params,
                                  bt=32, lt=dec_seq, chunk=dec_seq, gdn_bt=8, gdn_lt=dec_seq,
                                  act_dtype=F32)
    return (y_p, y_s, s_p, gc_p, sc_p, mem_k_p, mem_v_p, s_s, gc_s, sc_s)
```

```python
import functools

import jax
import jax.numpy as jnp
from jax import lax
from jax.experimental import pallas as pl
from jax.experimental.pallas import tpu as pltpu

F32 = jnp.float32
BF16 = jnp.bfloat16

D_MODEL = 1024
N_MEM = 256
X_WIDTH = 256
X_HEADS = 4
X_HEAD_DIM = 64
GDN_HEADS = 6
GDN_DK = 128
GDN_DV = 128
GDN_WIDTH = GDN_HEADS * GDN_DV
QK_WIDTH = GDN_HEADS * GDN_DK
QKV_WIDTH = 2 * QK_WIDTH + GDN_WIDTH
GDN_CHUNK = 64
SC_WIDTH = 768
BRANCH_WIDTH = 1024
EPS = 1e-6

LANES = 128
SUBLANES = 8
COL_CHUNK = 256
BA_PAD = LANES
VMEM_LIMIT = 56 * 1024 * 1024

HIGHEST = lax.Precision.HIGHEST


def _cparams(*sem):
    return pltpu.CompilerParams(dimension_semantics=sem, vmem_limit_bytes=VMEM_LIMIT)


def _rms_rows(x, g):
    r = lax.rsqrt(jnp.mean(x * x, axis=-1, keepdims=True) + EPS)
    return x * r * g


def _silu(x):
    h = 0.5 * x
    return h + h * jnp.tanh(h)


def _dot(a, b):
    return jnp.dot(a.astype(BF16), b.astype(BF16), preferred_element_type=F32)


def _dot_nt(a, b):
    return lax.dot_general(a.astype(BF16), b.astype(BF16), (((1,), (1,)), ((), ())),
                           preferred_element_type=F32)


def _dot_tn(a, b):
    return lax.dot_general(a.astype(BF16), b.astype(BF16), (((0,), (0,)), ((), ())),
                           preferred_element_type=F32)


def _dot_f32(a, b):
    return jnp.dot(a, b, preferred_element_type=F32, precision=HIGHEST)


def _col_chunks(n):
    return [slice(c, min(c + COL_CHUNK, n)) for c in range(0, n, COL_CHUNK)]


def _round_robin(gens):
    gens = list(gens)
    while gens:
        alive = []
        for g in gens:
            try:
                next(g)
                alive.append(g)
            except StopIteration:
                pass
        gens = alive


def _delayed(gen, rounds):
    for _ in range(rounds):
        yield
    yield from gen


def _memkv_kernel(m_ref, g_ref, w_ref, kt_ref, vt_ref):
    h = _rms_rows(m_ref[...], g_ref[...])
    kv = _dot(h, w_ref[...])
    kt_ref[...] = kv[:, :X_WIDTH].T
    vt_ref[...] = kv[:, X_WIDTH:].T


def _memkv(mem, g, w_bf16):
    bn, n_mem, _ = mem.shape
    depth = w_bf16.shape[0]
    out = jax.ShapeDtypeStruct((depth, bn, X_WIDTH, n_mem), F32)
    out_spec = pl.BlockSpec((None, None, X_WIDTH, n_mem), lambda l, b: (l, b, 0, 0))
    return pl.pallas_call(
        _memkv_kernel,
        grid=(depth, bn),
        in_specs=[
            pl.BlockSpec((None, n_mem, D_MODEL), lambda l, b: (b, 0, 0)),
            pl.BlockSpec((1, D_MODEL), lambda l, b: (0, 0)),
            pl.BlockSpec((None, D_MODEL, 2 * X_WIDTH), lambda l, b: (l, 0, 0)),
        ],
        out_specs=[out_spec, out_spec],
        out_shape=[out, out],
        compiler_params=_cparams("arbitrary", "arbitrary"),
        name="memkv",
    )(mem, g, w_bf16)


def _init_conv_carry(carry, hist_ref):
    @pl.when(pl.program_id(1) == 0)
    def _():
        carry[:, SUBLANES - hist_ref.shape[1]:, :] = hist_ref[...]


def _causal_conv_cols(x, cols, carry, hist_out, cw_ref, seqs=slice(None)):
    width = cw_ref.shape[0]
    bt, lt, n = x.shape
    g = lt // SUBLANES
    xe = jnp.concatenate([carry[seqs, :, cols], x], axis=1).reshape(bt, g + 1, SUBLANES, n)
    sub = lax.broadcasted_iota(jnp.int32, (1, 1, SUBLANES, n), 2)
    y = x.reshape(bt, g, SUBLANES, n) * cw_ref[width - 1:width, cols]
    for j in range(width - 1):
        s = width - 1 - j
        r = pltpu.roll(xe, s, axis=2)
        y = y + jnp.where(sub < s, r[:, :g], r[:, 1:]) * cw_ref[j:j + 1, cols]
    carry[seqs, :, cols] = x[:, lt - SUBLANES:, :]
    if hist_out is not None:
        hist_out[seqs, :, cols] = x[:, lt - (width - 1):, :]
    return y.reshape(bt, lt, n)


def _inproj_kernel(x_ref, g_ref, *rest, mixer, bt, lt):
    tm = bt * lt
    h = _rms_rows(x_ref[...].reshape(tm, D_MODEL), g_ref[...]).astype(BF16)
    if mixer == "gdn":
        n_out = (len(rest) - 1) // 2
        w_refs, (mix_out, *plain_outs, hist_out) = rest[:n_out], rest[n_out:]

        def proj(i, cols):
            return _dot_nt(h, w_refs[i][cols, :]).reshape(bt, lt, -1)
    else:
        w_ref, hist_ref, cw_ref, mix_out, *plain_outs, hist_out, carry = rest
        _init_conv_carry(carry, hist_ref)
        bases = [0, 3 * SC_WIDTH]
        for out in plain_outs[:-1]:
            bases.append(bases[-1] + out.shape[-1])

        def proj(i, cols, offset=0):
            cols = slice(bases[i] + offset + cols.start, bases[i] + offset + cols.stop)
            return jnp.dot(h, w_ref[:, cols], preferred_element_type=F32).reshape(bt, lt, -1)

    def mixer_chunk(cols):
        if mixer == "gdn":
            y = proj(0, cols)
            hist_out[:, :, cols] = y[:, lt - hist_out.shape[1]:, :]
        else:
            gate_b = proj(0, cols)
            pre = proj(0, cols, SC_WIDTH) * proj(0, cols, 2 * SC_WIDTH)
            yield
            y = gate_b * _causal_conv_cols(pre, cols, carry, hist_out, cw_ref)
        mix_out[:, :, cols] = y.astype(mix_out.dtype)
        yield

    def plain_chunk(i, out, cols):
        out[:, :, cols] = proj(i, cols).astype(out.dtype)
        yield

    gens = [mixer_chunk(cols) for cols in _col_chunks(mix_out.shape[-1])]
    for i, out in enumerate(plain_outs):
        gens += [plain_chunk(i + 1, out, cols) for cols in _col_chunks(out.shape[-1])]
    _round_robin([_delayed(g, i) for i, g in enumerate(gens)])


def _inproj(x, g, ws, hist, cw, *, mixer, bt, lt, act_dtype):
    bn, seq, _ = x.shape
    width, cc = cw.shape

    def act(n, dtype=act_dtype):
        return (pl.BlockSpec((bt, lt, n), lambda b, l: (b, l, 0)),
                jax.ShapeDtypeStruct((bn, seq, n), dtype))

    hist_spec = pl.BlockSpec((bt, width - 1, cc), lambda b, l: (b, 0, 0))
    args = [x, g, *ws]
    in_specs = [
        pl.BlockSpec((bt, lt, D_MODEL), lambda b, l: (b, l, 0)),
        pl.BlockSpec((1, D_MODEL), lambda b, l: (0, 0)),
    ] + [pl.BlockSpec(w.shape, lambda b, l: (0, 0)) for w in ws]
    outs = [act(cc), act(BRANCH_WIDTH), act(X_WIDTH)]
    scratch = []
    if mixer == "gdn":
        outs.append(act(BA_PAD, F32))
    else:
        args += [hist, cw]
        in_specs += [hist_spec, pl.BlockSpec((width, cc), lambda b, l: (0, 0))]
        scratch.append(pltpu.VMEM((bt, SUBLANES, cc), F32))
    outs.append((hist_spec, jax.ShapeDtypeStruct((bn, width - 1, cc), F32)))
    return pl.pallas_call(
        functools.partial(_inproj_kernel, mixer=mixer, bt=bt, lt=lt),
        grid=(bn // bt, seq // lt),
        in_specs=in_specs,
        out_specs=[o[0] for o in outs],
        out_shape=[o[1] for o in outs],
        scratch_shapes=scratch,
        compiler_params=_cparams("arbitrary", "arbitrary"),
        name="inproj_" + mixer,
    )(*args)


UNITS_PER_WAVE = 4


def _gdn_kernel(qkv_ref, ba_ref, s0_ref, hist_ref, cw_ref, alog_ref, dtb_ref, og_ref,
                o_ref, s_ref, carry, *, chunk, group):
    bt, lt, _ = qkv_ref.shape
    nchunks = lt // chunk

    @pl.when(pl.program_id(1) == 0)
    def _():
        s_ref[...] = s0_ref[...]

    _init_conv_carry(carry, hist_ref)

    heads = range(GDN_HEADS)
    groups = [tuple(range(g, g + group)) for g in range(0, GDN_HEADS, group)]
    gw = group * chunk
    row = lax.broadcasted_iota(jnp.int32, (chunk, gw), 0)
    col = lax.broadcasted_iota(jnp.int32, (chunk, gw), 1) % chunk
    tril = row >= col
    strict = row > col
    lane_blk = lax.broadcasted_iota(jnp.int32, (1, gw), 1) // chunk
    ones_tril = (lax.broadcasted_iota(jnp.int32, (chunk, chunk), 0)
                 >= lax.broadcasted_iota(jnp.int32, (chunk, chunk), 1)).astype(F32)
    neg_a = -jnp.exp(alog_ref[...])
    dtb = dtb_ref[...]
    og = og_ref[...]

    def rows(c):
        return slice(c * chunk, (c + 1) * chunk)

    def lanes(base, h, n):
        return slice(base + h * n, base + (h + 1) * n)

    def cat(xs, axis):
        return xs[0] if len(xs) == 1 else jnp.concatenate(xs, axis=axis)

    def pick(xs):
        out = xs[0]
        for j in range(1, len(xs)):
            out = jnp.where(lane_blk == j, xs[j], out)
        return out

    def blockdiag(xs):
        z = jnp.zeros_like(xs[0])
        return cat([cat([x if i == j else z for i in range(len(xs))], 1)
                    for j, x in enumerate(xs)], 0)

    def blockdiag_of_lane_blocks(x):
        if group == 1:
            return x
        return cat([jnp.where(lane_blk == j, x, 0.0) for j in range(group)], 0)

    def conv_silu(u, cols):
        b, c = u
        x = qkv_ref[b, rows(c), cols].astype(F32)
        y = _causal_conv_cols(x[None], cols, carry, None, cw_ref, seqs=slice(b, b + 1))
        return _silu(y[0])

    def phase1(units, res):
        uh = [(u, h) for u in units for h in heads]
        ug = [(u, g) for u in units for g in range(len(groups))]
        gcum, gcum_t, beta_all = {}, {}, {}
        for u in units:
            ba = ba_ref[u[0], rows(u[1]), :]
            beta_all[u] = 1.0 / (1.0 + jnp.exp(-ba))
            z = ba + dtb
            softplus = jnp.maximum(z, 0.0) + jnp.log1p(jnp.exp(-jnp.abs(z)))
            gcum[u] = _dot_f32(ones_tril, neg_a * softplus)
        q, k, v = {}, {}, {}
        for u in units:
            for h in heads:
                q[u, h] = conv_silu(u, lanes(0, h, GDN_DK))
                k[u, h] = conv_silu(u, lanes(QK_WIDTH, h, GDN_DK))
                v[u, h] = conv_silu(u, lanes(2 * QK_WIDTH, h, GDN_DV))
            yield
        for key in uh:
            x = q[key]
            q[key] = x * (lax.rsqrt(jnp.sum(x * x, axis=-1, keepdims=True) + EPS)
                          * (GDN_DK ** -0.5))
            x = k[key]
            k[key] = x * lax.rsqrt(jnp.sum(x * x, axis=-1, keepdims=True) + EPS)
        yield
        for u in units:
            gcum_t[u] = cat([gcum[u]] * group, 0).T
        gc = {(u, h): jnp.broadcast_to(gcum[u][:, GDN_HEADS + h:GDN_HEADS + h + 1],
                                       (chunk, GDN_DK)) for u, h in uh}
        gl = {(u, h): gcum[u][chunk - 1:chunk, GDN_HEADS + h:GDN_HEADS + h + 1]
              for u, h in uh}
        beta = {(u, h): jnp.broadcast_to(beta_all[u][:, h:h + 1], (chunk, GDN_DK))
                for u, h in uh}
        eg = {key: jnp.exp(gc[key]) for key in uh}
        kb = {key: k[key] * beta[key] for key in uh}
        kkqk = {}
        for u, g in ug:
            grp = groups[g]
            kkqk[u, g] = _dot_nt(
                cat([cat([kb[u, h] for h in grp], 1), cat([q[u, h] for h in grp], 1)], 0),
                blockdiag([k[u, h] for h in grp]))
        yield
        n, p, aqk = {}, {}, {}
        for u, g in ug:
            grp = groups[g]
            g_col = pick([gc[u, h][:, :gw] for h in grp])
            g_row = pick([gcum_t[u][GDN_HEADS + h:GDN_HEADS + h + 1, :] for h in grp])
            decay = jnp.exp(jnp.where(tril, g_col - g_row, -jnp.inf))
            a = jnp.where(strict, kkqk[u, g][:chunk] * decay, 0.0)
            aqk[u, g] = (kkqk[u, g][chunk:] * decay).astype(BF16)
            n[u, g] = -a
            p[u, g] = _dot(a, blockdiag_of_lane_blocks(a))
        yield
        span = 2
        while 2 * span < chunk:
            for key in ug:
                both = _dot(cat([n[key], p[key]], 0), blockdiag_of_lane_blocks(p[key]))
                n[key] = n[key] + p[key] + both[:chunk]
                p[key] = both[chunk:]
            span *= 2
            yield
        for key in ug:
            n[key] = n[key] + p[key] + _dot(n[key], blockdiag_of_lane_blocks(p[key]))
        yield
        wu = {}
        for u, h in uh:
            rhs = jnp.concatenate([kb[u, h] * eg[u, h], v[u, h] * beta[u, h]], axis=-1)
            wu[u, h] = rhs + _dot(n[u, h // group][:, lanes(0, h % group, chunk)], rhs)
        yield
        for u in units:
            res[u] = dict(
                w_qd=[jnp.concatenate([wu[u, h][:, :GDN_DK], q[u, h] * eg[u, h]],
                                      axis=0).astype(BF16) for h in heads],
                u=[wu[u, h][:, GDN_DK:] for h in heads],
                kd=[(k[u, h] * jnp.exp(gl[u, h] - gc[u, h])).astype(BF16) for h in heads],
                aqk=[aqk[u, g] for g in range(len(groups))],
                egl=[jnp.exp(gl[u, h]) for h in heads])
        yield

    def phase2(units, res):
        for c in sorted({c for _, c in units}):
            us = [u for u in units if u[1] == c]
            uh = [(u, h) for u in us for h in heads]
            s = {(u, h): s_ref[u[0], h] for u, h in uh}
            ws_qs = {(u, h): _dot(res[u]["w_qd"][h], s[u, h]) for u, h in uh}
            yield
            v_new = {(u, h): res[u]["u"][h] - ws_qs[u, h][:chunk] for u, h in uh}
            o_grp = {(u, g): _dot(res[u]["aqk"][g], blockdiag([v_new[u, h] for h in grp]))
                     for u in us for g, grp in enumerate(groups)}
            for u, h in uh:
                s_ref[u[0], h] = (s[u, h] * res[u]["egl"][h]
                                  + _dot_tn(res[u]["kd"][h], v_new[u, h]))
            yield
            for u, h in uh:
                oh = ws_qs[u, h][chunk:] + o_grp[u, h // group][:, lanes(0, h % group, GDN_DV)]
                oh = oh * lax.rsqrt(jnp.mean(oh * oh, axis=-1, keepdims=True) + EPS) * og
                o_ref[u[0], rows(c), lanes(0, h, GDN_DV)] = oh.astype(o_ref.dtype)
            yield

    units = [(b, c) for b in range(bt) for c in range(nchunks)]
    waves = [units[i:i + UNITS_PER_WAVE] for i in range(0, len(units), UNITS_PER_WAVE)]
    res = {}
    prev = None
    for wave in waves:
        gens = [phase1(wave, res)]
        if prev is not None:
            gens.append(phase2(prev, res))
        _round_robin(gens)
        prev = wave
    _round_robin([phase2(prev, res)])


def _gdn(qkv, ba, s0, hist, cw, alog_pad, dtb_pad, og, *, chunk, bt, lt):
    group = 2 if 2 * chunk == LANES else 1
    bn, seq, _ = qkv.shape
    width, cc = cw.shape
    s_spec = pl.BlockSpec((bt, GDN_HEADS, GDN_DK, GDN_DV), lambda b, l: (b, 0, 0, 0))
    hist_spec = pl.BlockSpec((bt, width - 1, cc), lambda b, l: (b, 0, 0))
    vec = pl.BlockSpec((1, LANES), lambda b, l: (0, 0))
    return pl.pallas_call(
        functools.partial(_gdn_kernel, chunk=chunk, group=group),
        grid=(bn // bt, seq // lt),
        in_specs=[
            pl.BlockSpec((bt, lt, QKV_WIDTH), lambda b, l: (b, l, 0)),
            pl.BlockSpec((bt, lt, BA_PAD), lambda b, l: (b, l, 0)),
            s_spec, hist_spec,
            pl.BlockSpec((width, cc), lambda b, l: (0, 0)),
            vec, vec, vec,
        ],
        out_specs=[
            pl.BlockSpec((bt, lt, GDN_WIDTH), lambda b, l: (b, l, 0)),
            s_spec,
        ],
        out_shape=[
            jax.ShapeDtypeStruct((bn, seq, GDN_WIDTH), qkv.dtype),
            jax.ShapeDtypeStruct(s0.shape, F32),
        ],
        scratch_shapes=[pltpu.VMEM((bt, SUBLANES, cc), F32)],
        compiler_params=_cparams("arbitrary", "arbitrary"),
        name="gdn",
    )(qkv, ba, s0, hist, cw, alog_pad, dtb_pad, og)


ATTN_ROWS = 128
ATTN_SEQS = 4


def _attn_out_kernel(tok_ref, xq_ref, gate_ref, x_ref, kt_ref, vt_ref, w_ref, fg_ref,
                     y_ref, xo_scr, *, bt, lt, final):
    lane_head = lax.broadcasted_iota(jnp.int32, (1, X_WIDTH), 1) // X_HEAD_DIM
    scale = X_HEAD_DIM ** -0.5
    tm = bt * lt

    def attend(b, rs):
        r = rs.stop - rs.start
        q = xq_ref[b, rs, :]
        q = q * jnp.asarray(scale, q.dtype)
        qx = jnp.concatenate([jnp.where(lane_head == h, q, jnp.zeros_like(q))
                              for h in range(X_HEADS)], axis=0)
        s = _dot(qx, kt_ref[b])
        yield
        e = jnp.exp(s - jnp.max(s, axis=-1, keepdims=True))
        p = e / jnp.sum(e, axis=-1, keepdims=True)
        o4 = _dot_nt(p, vt_ref[b])
        yield
        xo = jnp.where(lane_head == 0, o4[0:r], 0.0)
        for h in range(1, X_HEADS):
            xo = xo + jnp.where(lane_head == h, o4[h * r:(h + 1) * r], 0.0)
        xo_scr[b, rs, :] = xo
        yield

    acc = {}

    def tok_proj():
        sg = _silu(gate_ref[:, :, :GDN_WIDTH].astype(F32).reshape(tm, GDN_WIDTH))
        br = (tok_ref[...].astype(F32).reshape(tm, GDN_WIDTH) * sg).astype(BF16)
        for cols in _col_chunks(D_MODEL):
            acc[cols.start] = (x_ref[:, :, cols].reshape(tm, -1)
                               + _dot(br, w_ref[:GDN_WIDTH, cols]))
            yield

    if bt == 1:
        blocks = [slice(r, r + ATTN_ROWS) for r in range(0, lt, ATTN_ROWS)]
        _round_robin([_delayed(attend(0, rs), i) for i, rs in enumerate(blocks)] + [tok_proj()])
    else:
        def seq_group(i, carry):
            _round_robin([attend(i * ATTN_SEQS + j, slice(0, lt)) for j in range(ATTN_SEQS)])
            return carry

        lax.fori_loop(0, bt // ATTN_SEQS, seq_group, 0)
        _round_robin([tok_proj()])

    sg_x = _silu(gate_ref[:, :, GDN_WIDTH:].astype(F32).reshape(tm, X_WIDTH))
    br_x = (xo_scr[...].reshape(tm, X_WIDTH) * sg_x).astype(BF16)
    y = [acc[cols.start] + _dot(br_x, w_ref[GDN_WIDTH:, cols]) for cols in _col_chunks(D_MODEL)]
    if final:
        ms = sum(jnp.sum(c * c, axis=-1, keepdims=True) for c in y) * (1.0 / D_MODEL)
        r = lax.rsqrt(ms + EPS)
        y = [c * r * fg_ref[:, cols] for c, cols in zip(y, _col_chunks(D_MODEL))]
    for c, cols in zip(y, _col_chunks(D_MODEL)):
        y_ref[:, :, cols] = c.reshape(bt, lt, -1)


def _attn_out(tok, xq, gate, x, mk, mv, w_bf16, fg, *, layer, bt, lt, final):
    bn, seq, _ = x.shape

    def act(n):
        return pl.BlockSpec((bt, lt, n), lambda b, l: (b, l, 0))

    mem = pl.BlockSpec((None, bt, X_WIDTH, N_MEM), lambda b, l: (layer, b, 0, 0))
    return pl.pallas_call(
        functools.partial(_attn_out_kernel, bt=bt, lt=lt, final=final),
        grid=(bn // bt, seq // lt),
        in_specs=[
            act(GDN_WIDTH), act(X_WIDTH), act(BRANCH_WIDTH), act(D_MODEL), mem, mem,
            pl.BlockSpec((BRANCH_WIDTH, D_MODEL), lambda b, l: (0, 0)),
            pl.BlockSpec((1, D_MODEL), lambda b, l: (0, 0)),
        ],
        out_specs=act(D_MODEL),
        out_shape=jax.ShapeDtypeStruct(x.shape, F32),
        scratch_shapes=[pltpu.VMEM((bt, lt, X_WIDTH), F32)],
        compiler_params=_cparams("arbitrary", "arbitrary"),
        name="attn_out",
    )(tok, xq, gate, x, mk, mv, w_bf16, fg)


def _trunk(x, mem_k, mem_v, gdn_s, gdn_conv, sc_conv, p, *, bt, lt, chunk, gdn_bt, gdn_lt,
           act_dtype):
    qkv, gate, xq, ba, gconv_new = _inproj(
        x, p["norm_g"][0:1], p["w_in_a"], None, p["conv_w_a"], mixer="gdn", bt=bt, lt=lt,
        act_dtype=act_dtype)
    tok, s_new = _gdn(qkv, ba, gdn_s, gdn_conv, p["conv_w_a"], p["alog_pad"],
                      p["dtb_pad"], p["o_norm_g"], chunk=chunk, bt=gdn_bt, lt=gdn_lt)
    x = _attn_out(tok, xq, gate, x, mem_k, mem_v, p["w_out"][0], p["final_norm_g"],
                  layer=0, bt=bt, lt=lt, final=False)
    tok, gate, xq, sconv_new = _inproj(
        x, p["norm_g"][1:2], p["w_in_b"], sc_conv, p["conv_w_b"], mixer="sconv", bt=bt, lt=lt,
        act_dtype=act_dtype)
    y = _attn_out(tok, xq, gate, x, mem_k, mem_v, p["w_out"][1], p["final_norm_g"],
                  layer=1, bt=bt, lt=lt, final=True)
    return y, s_new[None], gconv_new[None], sconv_new[None]


def kernel(x_prompt, x_sample, mem_prompt, state_gdn, state_gdn_conv, state_sconv, cache_mem_k, cache_mem_v, norm_g, w_in_a, conv_w_a, a_log, dt_bias, o_norm_g, w_in_b, conv_w_b, mem_norm_g, w_mem_kv, w_out, final_norm_g):
    bp = x_prompt.shape[0]

    wa = jnp.transpose(w_in_a[0])
    c_b = QKV_WIDTH
    c_g = c_b + 2 * GDN_HEADS
    c_x = c_g + BRANCH_WIDTH
    wa = [wa[:c_b], wa[c_g:c_x], wa[c_x:],
          jnp.concatenate([wa[c_b:c_g], jnp.zeros((BA_PAD - 2 * GDN_HEADS, D_MODEL), wa.dtype)])]
    wa = [w.astype(BF16) for w in wa]
    pad_lo = jnp.zeros((GDN_HEADS,), F32)
    pad_hi = jnp.zeros((LANES - 2 * GDN_HEADS,), F32)
    params = {
        "norm_g": norm_g,
        "w_in_a": wa,
        "conv_w_a": conv_w_a[0],
        "alog_pad": jnp.concatenate([pad_lo, a_log[0], pad_hi])[None],
        "dtb_pad": jnp.concatenate([pad_lo, dt_bias[0], pad_hi])[None],
        "o_norm_g": o_norm_g,
        "w_in_b": [w_in_b[0].astype(BF16)],
        "conv_w_b": conv_w_b[0],
        "w_out": w_out.astype(BF16),
        "final_norm_g": final_norm_g[None],
    }

    def to_cache(t):
        t = t.reshape(t.shape[0], t.shape[1], X_HEADS, X_HEAD_DIM, t.shape[3])
        return jnp.transpose(t, (0, 1, 4, 2, 3))

    def from_cache(t):
        t = jnp.transpose(t, (0, 1, 3, 4, 2))
        return t.reshape(t.shape[0], t.shape[1], X_WIDTH, t.shape[4])

    mem_kt, mem_vt = _memkv(mem_prompt, mem_norm_g[None], w_mem_kv.astype(BF16))
    mem_k_p = to_cache(mem_kt)
    mem_v_p = to_cache(mem_vt)

    s0_p = jnp.zeros((bp,) + state_gdn.shape[2:], F32)
    gc0_p = jnp.zeros((bp,) + state_gdn_conv.shape[2:], F32)
    sc0_p = jnp.zeros((bp,) + state_sconv.shape[2:], F32)
    y_p, s_p, gc_p, sc_p = _trunk(x_prompt, mem_kt, mem_vt, s0_p, gc0_p, sc0_p,
                                  params, bt=1, lt=512, chunk=GDN_CHUNK, gdn_bt=1, gdn_lt=512,
                                  act_dtype=BF16)
    dec_seq = x_sample.shape[1]
    y_s, s_s, gc_s, sc_s = _trunk(x_sample, from_cache(cache_mem_k), from_cache(cache_mem_v),
                                  state_gdn[0], state_gdn_conv[0], state_sconv[0], params,
                                  bt=32, lt=dec_seq, chunk=dec_seq, gdn_bt=8, gdn_lt=dec_seq,
                                  act_dtype=F32)
    return (y_p, y_s, s_p, gc_p, sc_p, mem_k_p, mem_v_p, s_s, gc_s, sc_s)
```

```python
import functools

import jax
import jax.numpy as jnp
from jax import lax
from jax.experimental import pallas as pl
from jax.experimental.pallas import tpu as pltpu

F32 = jnp.float32
BF16 = jnp.bfloat16

D_MODEL = 1024
N_MEM = 256
X_WIDTH = 256
X_HEADS = 4
X_HEAD_DIM = 64
GDN_HEADS = 6
GDN_DK = 128
GDN_DV = 128
GDN_WIDTH = GDN_HEADS * GDN_DV
QK_WIDTH = GDN_HEADS * GDN_DK
QKV_WIDTH = 2 * QK_WIDTH + GDN_WIDTH
GDN_CHUNK = 64
SC_WIDTH = 768
BRANCH_WIDTH = 1024
EPS = 1e-6

LANES = 128
SUBLANES = 8
COL_CHUNK = 256
BA_PAD = LANES
VMEM_LIMIT = 56 * 1024 * 1024

HIGHEST = lax.Precision.HIGHEST


def _cparams(*sem):
    return pltpu.CompilerParams(dimension_semantics=sem, vmem_limit_bytes=VMEM_LIMIT)


def _rms_rows(x, g):
    r = lax.rsqrt(jnp.mean(x * x, axis=-1, keepdims=True) + EPS)
    return x * r * g


def _silu(x):
    h = 0.5 * x
    return h + h * jnp.tanh(h)


def _dot(a, b):
    return jnp.dot(a.astype(BF16), b.astype(BF16), preferred_element_type=F32)


def _dot_nt(a, b):
    return lax.dot_general(a.astype(BF16), b.astype(BF16), (((1,), (1,)), ((), ())),
                           preferred_element_type=F32)


def _dot_tn(a, b):
    return lax.dot_general(a.astype(BF16), b.astype(BF16), (((0,), (0,)), ((), ())),
                           preferred_element_type=F32)


def _dot_f32(a, b):
    return jnp.dot(a, b, preferred_element_type=F32, precision=HIGHEST)


def _col_chunks(n):
    return [slice(c, min(c + COL_CHUNK, n)) for c in range(0, n, COL_CHUNK)]


def _round_robin(gens):
    gens = list(gens)
    while gens:
        alive = []
        for g in gens:
            try:
                next(g)
                alive.append(g)
            except StopIteration:
                pass
        gens = alive


def _delayed(gen, rounds):
    for _ in range(rounds):
        yield
    yield from gen


def _memkv_kernel(m_ref, g_ref, w_ref, kt_ref, vt_ref):
    h = _rms_rows(m_ref[...], g_ref[...])
    kv = _dot(h, w_ref[...])
    kt_ref[...] = kv[:, :X_WIDTH].T
    vt_ref[...] = kv[:, X_WIDTH:].T


def _memkv(mem, g, w_bf16):
    bn, n_mem, _ = mem.shape
    depth = w_bf16.shape[0]
    out = jax.ShapeDtypeStruct((depth, bn, X_WIDTH, n_mem), F32)
    out_spec = pl.BlockSpec((None, None, X_WIDTH, n_mem), lambda l, b: (l, b, 0, 0))
    return pl.pallas_call(
        _memkv_kernel,
        grid=(depth, bn),
        in_specs=[
            pl.BlockSpec((None, n_mem, D_MODEL), lambda l, b: (b, 0, 0)),
            pl.BlockSpec((1, D_MODEL), lambda l, b: (0, 0)),
            pl.BlockSpec((None, D_MODEL, 2 * X_WIDTH), lambda l, b: (l, 0, 0)),
        ],
        out_specs=[out_spec, out_spec],
        out_shape=[out, out],
        compiler_params=_cparams("arbitrary", "arbitrary"),
        name="memkv",
    )(mem, g, w_bf16)


def _init_conv_carry(carry, hist_ref):
    @pl.when(pl.program_id(1) == 0)
    def _():
        carry[:, SUBLANES - hist_ref.shape[1]:, :] = hist_ref[...]


def _causal_conv_cols(x, cols, carry, hist_out, cw_ref, seqs=slice(None)):
    width = cw_ref.shape[0]
    bt, lt, n = x.shape
    g = lt // SUBLANES
    xe = jnp.concatenate([carry[seqs, :, cols], x], axis=1).reshape(bt, g + 1, SUBLANES, n)
    sub = lax.broadcasted_iota(jnp.int32, (1, 1, SUBLANES, n), 2)
    y = x.reshape(bt, g, SUBLANES, n) * cw_ref[width - 1:width, cols]
    for j in range(width - 1):
        s = width - 1 - j
        r = pltpu.roll(xe, s, axis=2)
        y = y + jnp.where(sub < s, r[:, :g], r[:, 1:]) * cw_ref[j:j + 1, cols]
    carry[seqs, :, cols] = x[:, lt - SUBLANES:, :]
    if hist_out is not None:
        hist_out[seqs, :, cols] = x[:, lt - (width - 1):, :]
    return y.reshape(bt, lt, n)


def _inproj_kernel(x_ref, g_ref, *rest, mixer, bt, lt):
    tm = bt * lt
    h = _rms_rows(x_ref[...].reshape(tm, D_MODEL), g_ref[...]).astype(BF16)
    if mixer == "gdn":
        n_out = (len(rest) - 1) // 2
        w_refs, (mix_out, *plain_outs, hist_out) = rest[:n_out], rest[n_out:]

        def proj(i, cols):
            return _dot_nt(h, w_refs[i][cols, :]).reshape(bt, lt, -1)
    else:
        w_ref, hist_ref, cw_ref, mix_out, *plain_outs, hist_out, carry = rest
        _init_conv_carry(carry, hist_ref)
        bases = [0, 3 * SC_WIDTH]
        for out in plain_outs[:-1]:
            bases.append(bases[-1] + out.shape[-1])

        def proj(i, cols, offset=0):
            cols = slice(bases[i] + offset + cols.start, bases[i] + offset + cols.stop)
            return jnp.dot(h, w_ref[:, cols], preferred_element_type=F32).reshape(bt, lt, -1)

    def mixer_chunk(cols):
        if mixer == "gdn":
            y = proj(0, cols)
            hist_out[:, :, cols] = y[:, lt - hist_out.shape[1]:, :]
        else:
            gate_b = proj(0, cols)
            pre = proj(0, cols, SC_WIDTH) * proj(0, cols, 2 * SC_WIDTH)
            yield
            y = gate_b * _causal_conv_cols(pre, cols, carry, hist_out, cw_ref)
        mix_out[:, :, cols] = y.astype(mix_out.dtype)
        yield

    def plain_chunk(i, out, cols):
        out[:, :, cols] = proj(i, cols).astype(out.dtype)
        yield

    gens = [mixer_chunk(cols) for cols in _col_chunks(mix_out.shape[-1])]
    for i, out in enumerate(plain_outs):
        gens += [plain_chunk(i + 1, out, cols) for cols in _col_chunks(out.shape[-1])]
    _round_robin([_delayed(g, i) for i, g in enumerate(gens)])


def _inproj(x, g, ws, hist, cw, *, mixer, bt, lt, act_dtype):
    bn, seq, _ = x.shape
    width, cc = cw.shape

    def act(n, dtype=act_dtype):
        return (pl.BlockSpec((bt, lt, n), lambda b, l: (b, l, 0)),
                jax.ShapeDtypeStruct((bn, seq, n), dtype))

    hist_spec = pl.BlockSpec((bt, width - 1, cc), lambda b, l: (b, 0, 0))
    args = [x, g, *ws]
    in_specs = [
        pl.BlockSpec((bt, lt, D_MODEL), lambda b, l: (b, l, 0)),
        pl.BlockSpec((1, D_MODEL), lambda b, l: (0, 0)),
    ] + [pl.BlockSpec(w.shape, lambda b, l: (0, 0)) for w in ws]
    outs = [act(cc), act(BRANCH_WIDTH), act(X_WIDTH)]
    scratch = []
    if mixer == "gdn":
        outs.append(act(BA_PAD, F32))
    else:
        args += [hist, cw]
        in_specs += [hist_spec, pl.BlockSpec((width, cc), lambda b, l: (0, 0))]
        scratch.append(pltpu.VMEM((bt, SUBLANES, cc), F32))
    outs.append((hist_spec, jax.ShapeDtypeStruct((bn, width - 1, cc), F32)))
    return pl.pallas_call(
        functools.partial(_inproj_kernel, mixer=mixer, bt=bt, lt=lt),
        grid=(bn // bt, seq // lt),
        in_specs=in_specs,
        out_specs=[o[0] for o in outs],
        out_shape=[o[1] for o in outs],
        scratch_shapes=scratch,
        compiler_params=_cparams("arbitrary", "arbitrary"),
        name="inproj_" + mixer,
    )(*args)


UNITS_PER_WAVE = 4


def _gdn_kernel(qkv_ref, ba_ref, s0_ref, hist_ref, cw_ref, alog_ref, dtb_ref, og_ref,
                o_ref, s_ref, carry, *, chunk, group):
    bt, lt, _ = qkv_ref.shape
    nchunks = lt // chunk

    @pl.when(pl.program_id(1) == 0)
    def _():
        s_ref[...] = s0_ref[...]

    _init_conv_carry(carry, hist_ref)

    heads = range(GDN_HEADS)
    groups = [tuple(range(g, g + group)) for g in range(0, GDN_HEADS, group)]
    gw = group * chunk
    row = lax.broadcasted_iota(jnp.int32, (chunk, gw), 0)
    col = lax.broadcasted_iota(jnp.int32, (chunk, gw), 1) % chunk
    tril = row >= col
    strict = row > col
    lane_blk = lax.broadcasted_iota(jnp.int32, (1, gw), 1) // chunk
    ones_tril = (lax.broadcasted_iota(jnp.int32, (chunk, chunk), 0)
                 >= lax.broadcasted_iota(jnp.int32, (chunk, chunk), 1)).astype(F32)
    neg_a = -jnp.exp(alog_ref[...])
    dtb = dtb_ref[...]
    og = og_ref[...]

    def rows(c):
        return slice(c * chunk, (c + 1) * chunk)

    def lanes(base, h, n):
        return slice(base + h * n, base + (h + 1) * n)

    def cat(xs, axis):
        return xs[0] if len(xs) == 1 else jnp.concatenate(xs, axis=axis)

    def pick(xs):
        out = xs[0]
        for j in range(1, len(xs)):
            out = jnp.where(lane_blk == j, xs[j], out)
        return out

    def blockdiag(xs):
        z = jnp.zeros_like(xs[0])
        return cat([cat([x if i == j else z for i in range(len(xs))], 1)
                    for j, x in enumerate(xs)], 0)

    def blockdiag_of_lane_blocks(x):
        if group == 1:
            return x
        return cat([jnp.where(lane_blk == j, x, 0.0) for j in range(group)], 0)

    def conv_silu(u, cols):
        b, c = u
        x = qkv_ref[b, rows(c), cols].astype(F32)
        y = _causal_conv_cols(x[None], cols, carry, None, cw_ref, seqs=slice(b, b + 1))
        return _silu(y[0])

    def phase1(units, res):
        uh = [(u, h) for u in units for h in heads]
        ug = [(u, g) for u in units for g in range(len(groups))]
        gcum, gcum_t, beta_all = {}, {}, {}
        for u in units:
            ba = ba_ref[u[0], rows(u[1]), :]
            beta_all[u] = 1.0 / (1.0 + jnp.exp(-ba))
            z = ba + dtb
            softplus = jnp.maximum(z, 0.0) + jnp.log1p(jnp.exp(-jnp.abs(z)))
            gcum[u] = _dot_f32(ones_tril, neg_a * softplus)
        q, k, v = {}, {}, {}
        for u in units:
            for h in heads:
                q[u, h] = conv_silu(u, lanes(0, h, GDN_DK))
                k[u, h] = conv_silu(u, lanes(QK_WIDTH, h, GDN_DK))
                v[u, h] = conv_silu(u, lanes(2 * QK_WIDTH, h, GDN_DV))
            yield
        for key in uh:
            x = q[key]
            q[key] = x * (lax.rsqrt(jnp.sum(x * x, axis=-1, keepdims=True) + EPS)
                          * (GDN_DK ** -0.5))
            x = k[key]
            k[key] = x * lax.rsqrt(jnp.sum(x * x, axis=-1, keepdims=True) + EPS)
        yield
        for u in units:
            gcum_t[u] = cat([gcum[u]] * group, 0).T
        gc = {(u, h): jnp.broadcast_to(gcum[u][:, GDN_HEADS + h:GDN_HEADS + h + 1],
                                       (chunk, GDN_DK)) for u, h in uh}
        gl = {(u, h): gcum[u][chunk - 1:chunk, GDN_HEADS + h:GDN_HEADS + h + 1]
              for u, h in uh}
        beta = {(u, h): jnp.broadcast_to(beta_all[u][:, h:h + 1], (chunk, GDN_DK))
                for u, h in uh}
        eg = {key: jnp.exp(gc[key]) for key in uh}
        kb = {key: k[key] * beta[key] for key in uh}
        kkqk = {}
        for u, g in ug:
            grp = groups[g]
            kkqk[u, g] = _dot_nt(
                cat([cat([kb[u, h] for h in grp], 1), cat([q[u, h] for h in grp], 1)], 0),
                blockdiag([k[u, h] for h in grp]))
        yield
        n, p, aqk = {}, {}, {}
        for u, g in ug:
            grp = groups[g]
            g_col = pick([gc[u, h][:, :gw] for h in grp])
            g_row = pick([gcum_t[u][GDN_HEADS + h:GDN_HEADS + h + 1, :] for h in grp])
            decay = jnp.exp(jnp.where(tril, g_col - g_row, -jnp.inf))
            a = jnp.where(strict, kkqk[u, g][:chunk] * decay, 0.0)
            aqk[u, g] = (kkqk[u, g][chunk:] * decay).astype(BF16)
            n[u, g] = -a
            p[u, g] = _dot(a, blockdiag_of_lane_blocks(a))
        yield
        span = 2
        while 2 * span < chunk:
            for key in ug:
                both = _dot(cat([n[key], p[key]], 0), blockdiag_of_lane_blocks(p[key]))
                n[key] = n[key] + p[key] + both[:chunk]
                p[key] = both[chunk:]
            span *= 2
            yield
        for key in ug:
            n[key] = n[key] + p[key] + _dot(n[key], blockdiag_of_lane_blocks(p[key]))
        yield
        wu = {}
        for u, h in uh:
            rhs = jnp.concatenate([kb[u, h] * eg[u, h], v[u, h] * beta[u, h]], axis=-1)
            wu[u, h] = rhs + _dot(n[u, h // group][:, lanes(0, h % group, chunk)], rhs)
        yield
        for u in units:
            res[u] = dict(
                w_qd=[jnp.concatenate([wu[u, h][:, :GDN_DK], q[u, h] * eg[u, h]],
                                      axis=0).astype(BF16) for h in heads],
                u=[wu[u, h][:, GDN_DK:] for h in heads],
                kd=[(k[u, h] * jnp.exp(gl[u, h] - gc[u, h])).astype(BF16) for h in heads],
                aqk=[aqk[u, g] for g in range(len(groups))],
                egl=[jnp.exp(gl[u, h]) for h in heads])
        yield

    def phase2(units, res):
        for c in sorted({c for _, c in units}):
            us = [u for u in units if u[1] == c]
            uh = [(u, h) for u in us for h in heads]
            s = {(u, h): s_ref[u[0], h] for u, h in uh}
            ws_qs = {(u, h): _dot(res[u]["w_qd"][h], s[u, h]) for u, h in uh}
            yield
            v_new = {(u, h): res[u]["u"][h] - ws_qs[u, h][:chunk] for u, h in uh}
            o_grp = {(u, g): _dot(res[u]["aqk"][g], blockdiag([v_new[u, h] for h in grp]))
                     for u in us for g, grp in enumerate(groups)}
            for u, h in uh:
                s_ref[u[0], h] = (s[u, h] * res[u]["egl"][h]
                                  + _dot_tn(res[u]["kd"][h], v_new[u, h]))
            yield
            for u, h in uh:
                oh = ws_qs[u, h][chunk:] + o_grp[u, h // group][:, lanes(0, h % group, GDN_DV)]
                oh = oh * lax.rsqrt(jnp.mean(oh * oh, axis=-1, keepdims=True) + EPS) * og
                o_ref[u[0], rows(c), lanes(0, h, GDN_DV)] = oh.astype(o_ref.dtype)
            yield

    units = [(b, c) for b in range(bt) for c in range(nchunks)]
    waves = [units[i:i + UNITS_PER_WAVE] for i in range(0, len(units), UNITS_PER_WAVE)]
    res = {}
    prev = None
    for wave in waves:
        gens = [phase1(wave, res)]
        if prev is not None:
            gens.append(phase2(prev, res))
        _round_robin(gens)
        prev = wave
    _round_robin([phase2(prev, res)])


def _gdn(qkv, ba, s0, hist, cw, alog_pad, dtb_pad, og, *, chunk, bt, lt):
    group = 2 if 2 * chunk == LANES else 1
    bn, seq, _ = qkv.shape
    width, cc = cw.shape
    s_spec = pl.BlockSpec((bt, GDN_HEADS, GDN_DK, GDN_DV), lambda b, l: (b, 0, 0, 0))
    hist_spec = pl.BlockSpec((bt, width - 1, cc), lambda b, l: (b, 0, 0))
    vec = pl.BlockSpec((1, LANES), lambda b, l: (0, 0))
    return pl.pallas_call(
        functools.partial(_gdn_kernel, chunk=chunk, group=group),
        grid=(bn // bt, seq // lt),
        in_specs=[
            pl.BlockSpec((bt, lt, QKV_WIDTH), lambda b, l: (b, l, 0)),
            pl.BlockSpec((bt, lt, BA_PAD), lambda b, l: (b, l, 0)),
            s_spec, hist_spec,
            pl.BlockSpec((width, cc), lambda b, l: (0, 0)),
            vec, vec, vec,
        ],
        out_specs=[
            pl.BlockSpec((bt, lt, GDN_WIDTH), lambda b, l: (b, l, 0)),
            s_spec,
        ],
        out_shape=[
            jax.ShapeDtypeStruct((bn, seq, GDN_WIDTH), qkv.dtype),
            jax.ShapeDtypeStruct(s0.shape, F32),
        ],
        scratch_shapes=[pltpu.VMEM((bt, SUBLANES, cc), F32)],
        compiler_params=_cparams("arbitrary", "arbitrary"),
        name="gdn",
    )(qkv, ba, s0, hist, cw, alog_pad, dtb_pad, og)


ATTN_ROWS = 256
ATTN_SEQS = 8


def _attn_out_kernel(tok_ref, xq_ref, gate_ref, x_ref, kt_ref, vt_ref, w_ref, fg_ref,
                     y_ref, xo_scr, *, bt, lt, final):
    lane_head = lax.broadcasted_iota(jnp.int32, (1, X_WIDTH), 1) // X_HEAD_DIM
    scale = X_HEAD_DIM ** -0.5
    tm = bt * lt

    def attend(b, rs):
        r = rs.stop - rs.start
        q = xq_ref[b, rs, :]
        q = q * jnp.asarray(scale, q.dtype)
        qx = jnp.concatenate([jnp.where(lane_head == h, q, jnp.zeros_like(q))
                              for h in range(X_HEADS)], axis=0)
        s = _dot(qx, kt_ref[b])
        yield
        e = jnp.exp(s - jnp.max(s, axis=-1, keepdims=True))
        p = e / jnp.sum(e, axis=-1, keepdims=True)
        o4 = _dot_nt(p, vt_ref[b])
        yield
        xo = jnp.where(lane_head == 0, o4[0:r], 0.0)
        for h in range(1, X_HEADS):
            xo = xo + jnp.where(lane_head == h, o4[h * r:(h + 1) * r], 0.0)
        xo_scr[b, rs, :] = xo
        yield

    acc = {}

    def tok_proj():
        sg = _silu(gate_ref[:, :, :GDN_WIDTH].astype(F32).reshape(tm, GDN_WIDTH))
        br = (tok_ref[...].astype(F32).reshape(tm, GDN_WIDTH) * sg).astype(BF16)
        for cols in _col_chunks(D_MODEL):
            acc[cols.start] = (x_ref[:, :, cols].reshape(tm, -1)
                               + _dot(br, w_ref[:GDN_WIDTH, cols]))
            yield

    if bt == 1:
        blocks = [slice(r, r + ATTN_ROWS) for r in range(0, lt, ATTN_ROWS)]
        _round_robin([attend(0, rs) for rs in blocks] + [tok_proj()])
    else:
        def seq_group(i, carry):
            _round_robin([attend(i * ATTN_SEQS + j, slice(0, lt)) for j in range(ATTN_SEQS)])
            return carry

        lax.fori_loop(0, bt // ATTN_SEQS, seq_group, 0)
        _round_robin([tok_proj()])

    sg_x = _silu(gate_ref[:, :, GDN_WIDTH:].astype(F32).reshape(tm, X_WIDTH))
    br_x = (xo_scr[...].reshape(tm, X_WIDTH) * sg_x).astype(BF16)
    y = [acc[cols.start] + _dot(br_x, w_ref[GDN_WIDTH:, cols]) for cols in _col_chunks(D_MODEL)]
    if final:
        ms = sum(jnp.sum(c * c, axis=-1, keepdims=True) for c in y) * (1.0 / D_MODEL)
        r = lax.rsqrt(ms + EPS)
        y = [c * r * fg_ref[:, cols] for c, cols in zip(y, _col_chunks(D_MODEL))]
    for c, cols in zip(y, _col_chunks(D_MODEL)):
        y_ref[:, :, cols] = c.reshape(bt, lt, -1)


def _attn_out(tok, xq, gate, x, mk, mv, w_bf16, fg, *, layer, bt, lt, final):
    bn, seq, _ = x.shape

    def act(n):
        return pl.BlockSpec((bt, lt, n), lambda b, l: (b, l, 0))

    mem = pl.BlockSpec((None, bt, X_WIDTH, N_MEM), lambda b, l: (layer, b, 0, 0))
    return pl.pallas_call(
        functools.partial(_attn_out_kernel, bt=bt, lt=lt, final=final),
        grid=(bn // bt, seq // lt),
        in_specs=[
            act(GDN_WIDTH), act(X_WIDTH), act(BRANCH_WIDTH), act(D_MODEL), mem, mem,
            pl.BlockSpec((BRANCH_WIDTH, D_MODEL), lambda b, l: (0, 0)),
            pl.BlockSpec((1, D_MODEL), lambda b, l: (0, 0)),
        ],
        out_specs=act(D_MODEL),
        out_shape=jax.ShapeDtypeStruct(x.shape, F32),
        scratch_shapes=[pltpu.VMEM((bt, lt, X_WIDTH), F32)],
        compiler_params=_cparams("arbitrary", "arbitrary"),
        name="attn_out",
    )(tok, xq, gate, x, mk, mv, w_bf16, fg)


def _trunk(x, mem_k, mem_v, gdn_s, gdn_conv, sc_conv, p, *, bt, lt, chunk, gdn_bt, gdn_lt,
           act_dtype):
    qkv, gate, xq, ba, gconv_new = _inproj(
        x, p["norm_g"][0:1], p["w_in_a"], None, p["conv_w_a"], mixer="gdn", bt=bt, lt=lt,
        act_dtype=act_dtype)
    tok, s_new = _gdn(qkv, ba, gdn_s, gdn_conv, p["conv_w_a"], p["alog_pad"],
                      p["dtb_pad"], p["o_norm_g"], chunk=chunk, bt=gdn_bt, lt=gdn_lt)
    x = _attn_out(tok, xq, gate, x, mem_k, mem_v, p["w_out"][0], p["final_norm_g"],
                  layer=0, bt=bt, lt=lt, final=False)
    tok, gate, xq, sconv_new = _inproj(
        x, p["norm_g"][1:2], p["w_in_b"], sc_conv, p["conv_w_b"], mixer="sconv", bt=bt, lt=lt,
        act_dtype=act_dtype)
    y = _attn_out(tok, xq, gate, x, mem_k, mem_v, p["w_out"][1], p["final_norm_g"],
                  layer=1, bt=bt, lt=lt, final=True)
    return y, s_new[None], gconv_new[None], sconv_new[None]


def kernel(x_prompt, x_sample, mem_prompt, state_gdn, state_gdn_conv, state_sconv, cache_mem_k, cache_mem_v, norm_g, w_in_a, conv_w_a, a_log, dt_bias, o_norm_g, w_in_b, conv_w_b, mem_norm_g, w_mem_kv, w_out, final_norm_g):
    bp = x_prompt.shape[0]

    wa = jnp.transpose(w_in_a[0])
    c_b = QKV_WIDTH
    c_g = c_b + 2 * GDN_HEADS
    c_x = c_g + BRANCH_WIDTH
    wa = [wa[:c_b], wa[c_g:c_x], wa[c_x:],
          jnp.concatenate([wa[c_b:c_g], jnp.zeros((BA_PAD - 2 * GDN_HEADS, D_MODEL), wa.dtype)])]
    wa = [w.astype(BF16) for w in wa]
    pad_lo = jnp.zeros((GDN_HEADS,), F32)
    pad_hi = jnp.zeros((LANES - 2 * GDN_HEADS,), F32)
    params = {
        "norm_g": norm_g,
        "w_in_a": wa,
        "conv_w_a": conv_w_a[0],
        "alog_pad": jnp.concatenate([pad_lo, a_log[0], pad_hi])[None],
        "dtb_pad": jnp.concatenate([pad_lo, dt_bias[0], pad_hi])[None],
        "o_norm_g": o_norm_g,
        "w_in_b": [w_in_b[0].astype(BF16)],
        "conv_w_b": conv_w_b[0],
        "w_out": w_out.astype(BF16),
        "final_norm_g": final_norm_g[None],
    }

    def to_cache(t):
        t = t.reshape(t.shape[0], t.shape[1], X_HEADS, X_HEAD_DIM, t.shape[3])
        return jnp.transpose(t, (0, 1, 4, 2, 3))

    def from_cache(t):
        t = jnp.transpose(t, (0, 1, 3, 4, 2))
        return t.reshape(t.shape[0], t.shape[1], X_WIDTH, t.shape[4])

    mem_kt, mem_vt = _memkv(mem_prompt, mem_norm_g[None], w_mem_kv.astype(BF16))
    mem_k_p = to_cache(mem_kt)
    mem_v_p = to_cache(mem_vt)

    s0_p = jnp.zeros((bp,) + state_gdn.shape[2:], F32)
    gc0_p = jnp.zeros((bp,) + state_gdn_conv.shape[2:], F32)
    sc0_p = jnp.zeros((bp,) + state_sconv.shape[2:], F32)
    y_p, s_p, gc_p, sc_p = _trunk(x_prompt, mem_kt, mem_vt, s0_p, gc0_p, sc0_p,
                                  params, bt=1, lt=512, chunk=GDN_CHUNK, gdn_bt=1, gdn_lt=512,
                                  act_dtype=BF16)
    dec_seq = x_sample.shape[1]
    y_s, s_s, gc_s, sc_s = _trunk(x_sample, from_cache(cache_mem_k), from_cache(cache_mem_v),
                                  state_gdn[0], state_gdn_conv[0], state_sconv[0], params,
                                  bt=32, lt=dec_seq, chunk=dec_seq, gdn_bt=8, gdn_lt=dec_seq,
                                  act_dtype=F32)
    return (y_p, y_s, s_p, gc_p, sc_p, mem_k_p, mem_v_p, s_s, gc_s, sc_s)
```

```python
import functools

import jax
import jax.numpy as jnp
from jax import lax
from jax.experimental import pallas as pl
from jax.experimental.pallas import tpu as pltpu

F32 = jnp.float32
BF16 = jnp.bfloat16

D_MODEL = 1024
N_MEM = 256
X_WIDTH = 256
X_HEADS = 4
X_HEAD_DIM = 64
GDN_HEADS = 6
GDN_DK = 128
GDN_DV = 128
GDN_WIDTH = GDN_HEADS * GDN_DV
QK_WIDTH = GDN_HEADS * GDN_DK
QKV_WIDTH = 2 * QK_WIDTH + GDN_WIDTH
GDN_CHUNK = 64
SC_WIDTH = 768
BRANCH_WIDTH = 1024
EPS = 1e-6

LANES = 128
SUBLANES = 8
COL_CHUNK = 256
BA_PAD = LANES
VMEM_LIMIT = 56 * 1024 * 1024

HIGHEST = lax.Precision.HIGHEST


def _cparams(*sem):
    return pltpu.CompilerParams(dimension_semantics=sem, vmem_limit_bytes=VMEM_LIMIT)


def _rms_rows(x, g):
    r = lax.rsqrt(jnp.mean(x * x, axis=-1, keepdims=True) + EPS)
    return x * r * g


def _silu(x):
    h = 0.5 * x
    return h + h * jnp.tanh(h)


def _dot(a, b):
    return jnp.dot(a.astype(BF16), b.astype(BF16), preferred_element_type=F32)


def _dot_nt(a, b):
    return lax.dot_general(a.astype(BF16), b.astype(BF16), (((1,), (1,)), ((), ())),
                           preferred_element_type=F32)


def _dot_tn(a, b):
    return lax.dot_general(a.astype(BF16), b.astype(BF16), (((0,), (0,)), ((), ())),
                           preferred_element_type=F32)


def _dot_f32(a, b):
    return jnp.dot(a, b, preferred_element_type=F32, precision=HIGHEST)


def _col_chunks(n):
    return [slice(c, min(c + COL_CHUNK, n)) for c in range(0, n, COL_CHUNK)]


def _interleaved(gens):
    gens = list(gens)
    while gens:
        alive = []
        for g in gens:
            try:
                next(g)
                alive.append(g)
            except StopIteration:
                pass
        gens = alive
        yield


def _round_robin(gens):
    for _ in _interleaved(gens):
        pass


def _delayed(gen, rounds):
    for _ in range(rounds):
        yield
    yield from gen


def _memkv_kernel(m_ref, g_ref, w_ref, kt_ref, vt_ref):
    h = _rms_rows(m_ref[...], g_ref[...])
    kv = _dot(h, w_ref[...])
    kt_ref[...] = kv[:, :X_WIDTH].T
    vt_ref[...] = kv[:, X_WIDTH:].T


def _memkv(mem, g, w_bf16):
    bn, n_mem, _ = mem.shape
    depth = w_bf16.shape[0]
    out = jax.ShapeDtypeStruct((depth, bn, X_WIDTH, n_mem), F32)
    out_spec = pl.BlockSpec((None, None, X_WIDTH, n_mem), lambda l, b: (l, b, 0, 0))
    return pl.pallas_call(
        _memkv_kernel,
        grid=(depth, bn),
        in_specs=[
            pl.BlockSpec((None, n_mem, D_MODEL), lambda l, b: (b, 0, 0)),
            pl.BlockSpec((1, D_MODEL), lambda l, b: (0, 0)),
            pl.BlockSpec((None, D_MODEL, 2 * X_WIDTH), lambda l, b: (l, 0, 0)),
        ],
        out_specs=[out_spec, out_spec],
        out_shape=[out, out],
        compiler_params=_cparams("arbitrary", "arbitrary"),
        name="memkv",
    )(mem, g, w_bf16)


def _init_conv_carry(carry, hist_ref):
    @pl.when(pl.program_id(1) == 0)
    def _():
        carry[:, SUBLANES - hist_ref.shape[1]:, :] = hist_ref[...]


def _causal_conv_cols(x, cols, carry, hist_out, cw_ref, seqs=slice(None)):
    width = cw_ref.shape[0]
    bt, lt, n = x.shape
    g = lt // SUBLANES
    xe = jnp.concatenate([carry[seqs, :, cols], x], axis=1).reshape(bt, g + 1, SUBLANES, n)
    sub = lax.broadcasted_iota(jnp.int32, (1, 1, SUBLANES, n), 2)
    y = x.reshape(bt, g, SUBLANES, n) * cw_ref[width - 1:width, cols]
    for j in range(width - 1):
        s = width - 1 - j
        r = pltpu.roll(xe, s, axis=2)
        y = y + jnp.where(sub < s, r[:, :g], r[:, 1:]) * cw_ref[j:j + 1, cols]
    carry[seqs, :, cols] = x[:, lt - SUBLANES:, :]
    if hist_out is not None:
        hist_out[seqs, :, cols] = x[:, lt - (width - 1):, :]
    return y.reshape(bt, lt, n)


def _inproj_chunks(x_ref, g_ref, rest, *, mixer, bt, lt):
    tm = bt * lt
    h = _rms_rows(x_ref[...].reshape(tm, D_MODEL), g_ref[...]).astype(BF16)
    if mixer == "gdn":
        n_out = (len(rest) - 1) // 2
        w_refs, (mix_out, *plain_outs, hist_out) = rest[:n_out], rest[n_out:]

        def proj(i, cols):
            return _dot_nt(h, w_refs[i][cols, :]).reshape(bt, lt, -1)
    else:
        w_ref, hist_ref, cw_ref, mix_out, *plain_outs, hist_out, carry = rest
        _init_conv_carry(carry, hist_ref)
        bases = [0, 3 * SC_WIDTH]
        for out in plain_outs[:-1]:
            bases.append(bases[-1] + out.shape[-1])

        def proj(i, cols, offset=0):
            cols = slice(bases[i] + offset + cols.start, bases[i] + offset + cols.stop)
            return jnp.dot(h, w_ref[:, cols], preferred_element_type=F32).reshape(bt, lt, -1)

    def mixer_chunk(cols):
        if mixer == "gdn":
            y = proj(0, cols)
            hist_out[:, :, cols] = y[:, lt - hist_out.shape[1]:, :]
        else:
            gate_b = proj(0, cols)
            pre = proj(0, cols, SC_WIDTH) * proj(0, cols, 2 * SC_WIDTH)
            yield
            y = gate_b * _causal_conv_cols(pre, cols, carry, hist_out, cw_ref)
        mix_out[:, :, cols] = y.astype(mix_out.dtype)
        yield

    def plain_chunk(i, out, cols):
        out[:, :, cols] = proj(i, cols).astype(out.dtype)
        yield

    gens = [mixer_chunk(cols) for cols in _col_chunks(mix_out.shape[-1])]
    for i, out in enumerate(plain_outs):
        gens += [plain_chunk(i + 1, out, cols) for cols in _col_chunks(out.shape[-1])]
    return gens


def _inproj_kernel(x_ref, g_ref, *rest, mixer, bt, lt):
    gens = _inproj_chunks(x_ref, g_ref, rest, mixer=mixer, bt=bt, lt=lt)
    _round_robin([_delayed(g, i) for i, g in enumerate(gens)])


def _inproj(x, g, ws, hist, cw, *, mixer, bt, lt, act_dtype):
    bn, seq, _ = x.shape
    width, cc = cw.shape

    def act(n, dtype=act_dtype):
        return (pl.BlockSpec((bt, lt, n), lambda b, l: (b, l, 0)),
                jax.ShapeDtypeStruct((bn, seq, n), dtype))

    hist_spec = pl.BlockSpec((bt, width - 1, cc), lambda b, l: (b, 0, 0))
    args = [x, g, *ws]
    in_specs = [
        pl.BlockSpec((bt, lt, D_MODEL), lambda b, l: (b, l, 0)),
        pl.BlockSpec((1, D_MODEL), lambda b, l: (0, 0)),
    ] + [pl.BlockSpec(w.shape, lambda b, l: (0, 0)) for w in ws]
    outs = [act(cc), act(BRANCH_WIDTH), act(X_WIDTH)]
    scratch = []
    if mixer == "gdn":
        outs.append(act(BA_PAD, F32))
    else:
        args += [hist, cw]
        in_specs += [hist_spec, pl.BlockSpec((width, cc), lambda b, l: (0, 0))]
        scratch.append(pltpu.VMEM((bt, SUBLANES, cc), F32))
    outs.append((hist_spec, jax.ShapeDtypeStruct((bn, width - 1, cc), F32)))
    return pl.pallas_call(
        functools.partial(_inproj_kernel, mixer=mixer, bt=bt, lt=lt),
        grid=(bn // bt, seq // lt),
        in_specs=in_specs,
        out_specs=[o[0] for o in outs],
        out_shape=[o[1] for o in outs],
        scratch_shapes=scratch,
        compiler_params=_cparams("arbitrary", "arbitrary"),
        name="inproj_" + mixer,
    )(*args)


UNITS_PER_WAVE = 4


def _gdn_tile(qkv_ref, ba_ref, cw_ref, alog_ref, dtb_ref, og_ref, o_ref, s_ref, carry,
              *, chunk, group):
    bt, lt, _ = qkv_ref.shape
    nchunks = lt // chunk
    heads = range(GDN_HEADS)
    groups = [tuple(range(g, g + group)) for g in range(0, GDN_HEADS, group)]
    gw = group * chunk
    row = lax.broadcasted_iota(jnp.int32, (chunk, gw), 0)
    col = lax.broadcasted_iota(jnp.int32, (chunk, gw), 1) % chunk
    tril = row >= col
    strict = row > col
    lane_blk = lax.broadcasted_iota(jnp.int32, (1, gw), 1) // chunk
    ones_tril = (lax.broadcasted_iota(jnp.int32, (chunk, chunk), 0)
                 >= lax.broadcasted_iota(jnp.int32, (chunk, chunk), 1)).astype(F32)
    neg_a = -jnp.exp(alog_ref[...])
    dtb = dtb_ref[...]
    og = og_ref[...]

    def rows(c):
        return slice(c * chunk, (c + 1) * chunk)

    def lanes(base, h, n):
        return slice(base + h * n, base + (h + 1) * n)

    def cat(xs, axis):
        return xs[0] if len(xs) == 1 else jnp.concatenate(xs, axis=axis)

    def pick(xs):
        out = xs[0]
        for j in range(1, len(xs)):
            out = jnp.where(lane_blk == j, xs[j], out)
        return out

    def blockdiag(xs):
        z = jnp.zeros_like(xs[0])
        return cat([cat([x if i == j else z for i in range(len(xs))], 1)
                    for j, x in enumerate(xs)], 0)

    def blockdiag_of_lane_blocks(x):
        if group == 1:
            return x
        return cat([jnp.where(lane_blk == j, x, 0.0) for j in range(group)], 0)

    def conv_silu(u, cols):
        b, c = u
        x = qkv_ref[b, rows(c), cols].astype(F32)
        y = _causal_conv_cols(x[None], cols, carry, None, cw_ref, seqs=slice(b, b + 1))
        return _silu(y[0])

    def phase1(units, res):
        uh = [(u, h) for u in units for h in heads]
        ug = [(u, g) for u in units for g in range(len(groups))]
        gcum, gcum_t, beta_all = {}, {}, {}
        for u in units:
            ba = ba_ref[u[0], rows(u[1]), :]
            beta_all[u] = 1.0 / (1.0 + jnp.exp(-ba))
            z = ba + dtb
            softplus = jnp.maximum(z, 0.0) + jnp.log1p(jnp.exp(-jnp.abs(z)))
            gcum[u] = _dot_f32(ones_tril, neg_a * softplus)
        q, k, v = {}, {}, {}
        for u in units:
            for h in heads:
                q[u, h] = conv_silu(u, lanes(0, h, GDN_DK))
                k[u, h] = conv_silu(u, lanes(QK_WIDTH, h, GDN_DK))
                v[u, h] = conv_silu(u, lanes(2 * QK_WIDTH, h, GDN_DV))
            yield
        for key in uh:
            x = q[key]
            q[key] = x * (lax.rsqrt(jnp.sum(x * x, axis=-1, keepdims=True) + EPS)
                          * (GDN_DK ** -0.5))
            x = k[key]
            k[key] = x * lax.rsqrt(jnp.sum(x * x, axis=-1, keepdims=True) + EPS)
        yield
        for u in units:
            gcum_t[u] = cat([gcum[u]] * group, 0).T
        gc = {(u, h): jnp.broadcast_to(gcum[u][:, GDN_HEADS + h:GDN_HEADS + h + 1],
                                       (chunk, GDN_DK)) for u, h in uh}
        gl = {(u, h): gcum[u][chunk - 1:chunk, GDN_HEADS + h:GDN_HEADS + h + 1]
              for u, h in uh}
        beta = {(u, h): jnp.broadcast_to(beta_all[u][:, h:h + 1], (chunk, GDN_DK))
                for u, h in uh}
        eg = {key: jnp.exp(gc[key]) for key in uh}
        kb = {key: k[key] * beta[key] for key in uh}
        kkqk = {}
        for u, g in ug:
            grp = groups[g]
            kkqk[u, g] = _dot_nt(
                cat([cat([kb[u, h] for h in grp], 1), cat([q[u, h] for h in grp], 1)], 0),
                blockdiag([k[u, h] for h in grp]))
        yield
        n, p, aqk = {}, {}, {}
        for u, g in ug:
            grp = groups[g]
            g_col = pick([gc[u, h][:, :gw] for h in grp])
            g_row = pick([gcum_t[u][GDN_HEADS + h:GDN_HEADS + h + 1, :] for h in grp])
            decay = jnp.exp(jnp.where(tril, g_col - g_row, -jnp.inf))
            a = jnp.where(strict, kkqk[u, g][:chunk] * decay, 0.0)
            aqk[u, g] = (kkqk[u, g][chunk:] * decay).astype(BF16)
            n[u, g] = -a
            p[u, g] = _dot(a, blockdiag_of_lane_blocks(a))
        yield
        span = 2
        while 2 * span < chunk:
            for key in ug:
                both = _dot(cat([n[key], p[key]], 0), blockdiag_of_lane_blocks(p[key]))
                n[key] = n[key] + p[key] + both[:chunk]
                p[key] = both[chunk:]
            span *= 2
            yield
        for key in ug:
            n[key] = n[key] + p[key] + _dot(n[key], blockdiag_of_lane_blocks(p[key]))
        yield
        wu = {}
        for u, h in uh:
            rhs = jnp.concatenate([kb[u, h] * eg[u, h], v[u, h] * beta[u, h]], axis=-1)
            wu[u, h] = rhs + _dot(n[u, h // group][:, lanes(0, h % group, chunk)], rhs)
        yield
        for u in units:
            res[u] = dict(
                w_qd=[jnp.concatenate([wu[u, h][:, :GDN_DK], q[u, h] * eg[u, h]],
                                      axis=0).astype(BF16) for h in heads],
                u=[wu[u, h][:, GDN_DK:] for h in heads],
                kd=[(k[u, h] * jnp.exp(gl[u, h] - gc[u, h])).astype(BF16) for h in heads],
                aqk=[aqk[u, g] for g in range(len(groups))],
                egl=[jnp.exp(gl[u, h]) for h in heads])
        yield

    def phase2(units, res):
        for c in sorted({c for _, c in units}):
            us = [u for u in units if u[1] == c]
            uh = [(u, h) for u in us for h in heads]
            s = {(u, h): s_ref[u[0], h] for u, h in uh}
            ws_qs = {(u, h): _dot(res[u]["w_qd"][h], s[u, h]) for u, h in uh}
            yield
            v_new = {(u, h): res[u]["u"][h] - ws_qs[u, h][:chunk] for u, h in uh}
            o_grp = {(u, g): _dot(res[u]["aqk"][g], blockdiag([v_new[u, h] for h in grp]))
                     for u in us for g, grp in enumerate(groups)}
            for u, h in uh:
                s_ref[u[0], h] = (s[u, h] * res[u]["egl"][h]
                                  + _dot_tn(res[u]["kd"][h], v_new[u, h]))
            yield
            for u, h in uh:
                oh = ws_qs[u, h][chunk:] + o_grp[u, h // group][:, lanes(0, h % group, GDN_DV)]
                oh = oh * lax.rsqrt(jnp.mean(oh * oh, axis=-1, keepdims=True) + EPS) * og
                o_ref[u[0], rows(c), lanes(0, h, GDN_DV)] = oh.astype(o_ref.dtype)
            yield

    units = [(b, c) for b in range(bt) for c in range(nchunks)]
    waves = [units[i:i + UNITS_PER_WAVE] for i in range(0, len(units), UNITS_PER_WAVE)]
    res = {}
    prev = None
    for wave in waves:
        gens = [phase1(wave, res)]
        if prev is not None:
            gens.append(phase2(prev, res))
        yield from _interleaved(gens)
        prev = wave
    yield from phase2(prev, res)


def _gdn_kernel(qkv_ref, ba_ref, s0_ref, hist_ref, cw_ref, alog_ref, dtb_ref, og_ref,
                o_ref, s_ref, carry, *, chunk, group):
    @pl.when(pl.program_id(1) == 0)
    def _():
        s_ref[...] = s0_ref[...]

    _init_conv_carry(carry, hist_ref)
    _round_robin([_gdn_tile(qkv_ref, ba_ref, cw_ref, alog_ref, dtb_ref, og_ref, o_ref, s_ref,
                            carry, chunk=chunk, group=group)])


def _gdn(qkv, ba, s0, hist, cw, alog_pad, dtb_pad, og, *, chunk, bt, lt):
    group = 2 if 2 * chunk == LANES else 1
    bn, seq, _ = qkv.shape
    width, cc = cw.shape
    s_spec = pl.BlockSpec((bt, GDN_HEADS, GDN_DK, GDN_DV), lambda b, l: (b, 0, 0, 0))
    hist_spec = pl.BlockSpec((bt, width - 1, cc), lambda b, l: (b, 0, 0))
    vec = pl.BlockSpec((1, LANES), lambda b, l: (0, 0))
    return pl.pallas_call(
        functools.partial(_gdn_kernel, chunk=chunk, group=group),
        grid=(bn // bt, seq // lt),
        in_specs=[
            pl.BlockSpec((bt, lt, QKV_WIDTH), lambda b, l: (b, l, 0)),
            pl.BlockSpec((bt, lt, BA_PAD), lambda b, l: (b, l, 0)),
            s_spec, hist_spec,
            pl.BlockSpec((width, cc), lambda b, l: (0, 0)),
            vec, vec, vec,
        ],
        out_specs=[
            pl.BlockSpec((bt, lt, GDN_WIDTH), lambda b, l: (b, l, 0)),
            s_spec,
        ],
        out_shape=[
            jax.ShapeDtypeStruct((bn, seq, GDN_WIDTH), qkv.dtype),
            jax.ShapeDtypeStruct(s0.shape, F32),
        ],
        scratch_shapes=[pltpu.VMEM((bt, SUBLANES, cc), F32)],
        compiler_params=_cparams("arbitrary", "arbitrary"),
        name="gdn",
    )(qkv, ba, s0, hist, cw, alog_pad, dtb_pad, og)


PROJ_CHUNK_SPACING = (5, 2)


def _gdn_layer_kernel(x_ref, g_ref, w_qkv, w_gate, w_xq, w_ba, s0_ref, hist_ref, cw_ref,
                      alog_ref, dtb_ref, og_ref, gate_out, xq_out, hist_out, o_ref, s_ref,
                      qkv_scr, ba_scr, carry, *, lt, tiles_per_seq, chunk, group):
    i = pl.program_id(0)
    slot = i % 2
    prev = jnp.maximum(i - 1, 0)

    @pl.when(i == 0)
    def _():
        qkv_scr[1] = jnp.zeros(qkv_scr.shape[1:], F32)
        ba_scr[1] = jnp.zeros(ba_scr.shape[1:], F32)

    @pl.when(prev % tiles_per_seq == 0)
    def _():
        s_ref[...] = s0_ref[...]
        carry[:, SUBLANES - hist_ref.shape[1]:, :] = hist_ref[...]

    proj = _inproj_chunks(
        x_ref, g_ref, (w_qkv, w_gate, w_xq, w_ba, qkv_scr.at[slot], gate_out, xq_out,
                       ba_scr.at[slot], hist_out), mixer="gdn", bt=1, lt=lt)
    rule = _gdn_tile(qkv_scr.at[1 - slot], ba_scr.at[1 - slot], cw_ref, alog_ref, dtb_ref,
                     og_ref, o_ref, s_ref, carry, chunk=chunk, group=group)
    num, den = PROJ_CHUNK_SPACING
    _round_robin([rule] + [_delayed(p, k * num // den) for k, p in enumerate(proj)])


def _gdn_layer(x, g, ws, s0, hist, cw, alog_pad, dtb_pad, og, *, chunk, lt, act_dtype):
    group = 2 if 2 * chunk == LANES else 1
    bn, seq, _ = x.shape
    nl = seq // lt
    tiles = bn * nl
    width, cc = cw.shape

    def cur(i):
        return jnp.minimum(i, tiles - 1)

    def prev(i):
        return jnp.maximum(i - 1, 0)

    def tile(n, which):
        return pl.BlockSpec((1, lt, n), lambda i: (which(i), 0, 0))

    def per_seq(shape, which):
        return pl.BlockSpec((1,) + shape, lambda i: (which(i) // nl,) + (0,) * len(shape))

    def whole(a):
        return pl.BlockSpec(a.shape, lambda i: (0,) * a.ndim)

    state = (GDN_HEADS, GDN_DK, GDN_DV)
    gate, xq, hist_new, tok, s_new = pl.pallas_call(
        functools.partial(_gdn_layer_kernel, lt=lt, tiles_per_seq=nl, chunk=chunk, group=group),
        grid=(tiles + 1,),
        in_specs=[tile(D_MODEL, cur), whole(g)] + [whole(w) for w in ws] + [
            per_seq(state, prev), per_seq((width - 1, cc), prev), whole(cw),
            whole(alog_pad), whole(dtb_pad), whole(og)],
        out_specs=[
            tile(BRANCH_WIDTH, cur), tile(X_WIDTH, cur), per_seq((width - 1, cc), cur),
            tile(GDN_WIDTH, prev), per_seq(state, prev)],
        out_shape=[
            jax.ShapeDtypeStruct((tiles, lt, BRANCH_WIDTH), act_dtype),
            jax.ShapeDtypeStruct((tiles, lt, X_WIDTH), act_dtype),
            jax.ShapeDtypeStruct(hist.shape, F32),
            jax.ShapeDtypeStruct((tiles, lt, GDN_WIDTH), act_dtype),
            jax.ShapeDtypeStruct(s0.shape, F32)],
        scratch_shapes=[
            pltpu.VMEM((2, 1, lt, cc), F32),
            pltpu.VMEM((2, 1, lt, BA_PAD), F32),
            pltpu.VMEM((1, SUBLANES, cc), F32)],
        compiler_params=_cparams("arbitrary"),
        name="gdn_layer",
    )(x.reshape(tiles, lt, D_MODEL), g, *ws, s0, hist, cw, alog_pad, dtb_pad, og)
    return (gate.reshape(bn, seq, -1), xq.reshape(bn, seq, -1), tok.reshape(bn, seq, -1),
            s_new, hist_new)


ATTN_ROWS = 256
ATTN_SEQS = 8


def _attn_out_kernel(tok_ref, xq_ref, gate_ref, x_ref, kt_ref, vt_ref, w_ref, fg_ref,
                     y_ref, xo_scr, *, bt, lt, final):
    lane_head = lax.broadcasted_iota(jnp.int32, (1, X_WIDTH), 1) // X_HEAD_DIM
    scale = X_HEAD_DIM ** -0.5
    tm = bt * lt

    def attend(b, rs):
        r = rs.stop - rs.start
        q = xq_ref[b, rs, :]
        q = q * jnp.asarray(scale, q.dtype)
        qx = jnp.concatenate([jnp.where(lane_head == h, q, jnp.zeros_like(q))
                              for h in range(X_HEADS)], axis=0)
        s = _dot(qx, kt_ref[b])
        yield
        e = jnp.exp(s - jnp.max(s, axis=-1, keepdims=True))
        p = e / jnp.sum(e, axis=-1, keepdims=True)
        o4 = _dot_nt(p, vt_ref[b])
        yield
        xo = jnp.where(lane_head == 0, o4[0:r], 0.0)
        for h in range(1, X_HEADS):
            xo = xo + jnp.where(lane_head == h, o4[h * r:(h + 1) * r], 0.0)
        xo_scr[b, rs, :] = xo
        yield

    acc = {}

    def tok_proj():
        sg = _silu(gate_ref[:, :, :GDN_WIDTH].astype(F32).reshape(tm, GDN_WIDTH))
        br = (tok_ref[...].astype(F32).reshape(tm, GDN_WIDTH) * sg).astype(BF16)
        for cols in _col_chunks(D_MODEL):
            acc[cols.start] = (x_ref[:, :, cols].reshape(tm, -1)
                               + _dot(br, w_ref[:GDN_WIDTH, cols]))
            yield

    if bt == 1:
        blocks = [slice(r, r + ATTN_ROWS) for r in range(0, lt, ATTN_ROWS)]
        _round_robin([attend(0, rs) for rs in blocks] + [tok_proj()])
    else:
        def seq_group(i, carry):
            _round_robin([attend(i * ATTN_SEQS + j, slice(0, lt)) for j in range(ATTN_SEQS)])
            return carry

        lax.fori_loop(0, bt // ATTN_SEQS, seq_group, 0)
        _round_robin([tok_proj()])

    sg_x = _silu(gate_ref[:, :, GDN_WIDTH:].astype(F32).reshape(tm, X_WIDTH))
    br_x = (xo_scr[...].reshape(tm, X_WIDTH) * sg_x).astype(BF16)
    y = [acc[cols.start] + _dot(br_x, w_ref[GDN_WIDTH:, cols]) for cols in _col_chunks(D_MODEL)]
    if final:
        ms = sum(jnp.sum(c * c, axis=-1, keepdims=True) for c in y) * (1.0 / D_MODEL)
        r = lax.rsqrt(ms + EPS)
        y = [c * r * fg_ref[:, cols] for c, cols in zip(y, _col_chunks(D_MODEL))]
    for c, cols in zip(y, _col_chunks(D_MODEL)):
        y_ref[:, :, cols] = c.reshape(bt, lt, -1)


def _attn_out(tok, xq, gate, x, mk, mv, w_bf16, fg, *, layer, bt, lt, final):
    bn, seq, _ = x.shape

    def act(n):
        return pl.BlockSpec((bt, lt, n), lambda b, l: (b, l, 0))

    mem = pl.BlockSpec((None, bt, X_WIDTH, N_MEM), lambda b, l: (layer, b, 0, 0))
    return pl.pallas_call(
        functools.partial(_attn_out_kernel, bt=bt, lt=lt, final=final),
        grid=(bn // bt, seq // lt),
        in_specs=[
            act(GDN_WIDTH), act(X_WIDTH), act(BRANCH_WIDTH), act(D_MODEL), mem, mem,
            pl.BlockSpec((BRANCH_WIDTH, D_MODEL), lambda b, l: (0, 0)),
            pl.BlockSpec((1, D_MODEL), lambda b, l: (0, 0)),
        ],
        out_specs=act(D_MODEL),
        out_shape=jax.ShapeDtypeStruct(x.shape, F32),
        scratch_shapes=[pltpu.VMEM((bt, lt, X_WIDTH), F32)],
        compiler_params=_cparams("arbitrary", "arbitrary"),
        name="attn_out",
    )(tok, xq, gate, x, mk, mv, w_bf16, fg)


def _trunk(x, mem_k, mem_v, gdn_s, gdn_conv, sc_conv, p, *, bt, lt, chunk, gdn_bt, gdn_lt,
           act_dtype):
    if gdn_bt == 1 and bt == 1 and gdn_lt == lt:
        gate, xq, tok, s_new, gconv_new = _gdn_layer(
            x, p["norm_g"][0:1], p["w_in_a"], gdn_s, gdn_conv, p["conv_w_a"], p["alog_pad"],
            p["dtb_pad"], p["o_norm_g"], chunk=chunk, lt=lt, act_dtype=act_dtype)
    else:
        qkv, gate, xq, ba, gconv_new = _inproj(
            x, p["norm_g"][0:1], p["w_in_a"], None, p["conv_w_a"], mixer="gdn", bt=bt, lt=lt,
            act_dtype=act_dtype)
        tok, s_new = _gdn(qkv, ba, gdn_s, gdn_conv, p["conv_w_a"], p["alog_pad"],
                          p["dtb_pad"], p["o_norm_g"], chunk=chunk, bt=gdn_bt, lt=gdn_lt)
    x = _attn_out(tok, xq, gate, x, mem_k, mem_v, p["w_out"][0], p["final_norm_g"],
                  layer=0, bt=bt, lt=lt, final=False)
    tok, gate, xq, sconv_new = _inproj(
        x, p["norm_g"][1:2], p["w_in_b"], sc_conv, p["conv_w_b"], mixer="sconv", bt=bt, lt=lt,
        act_dtype=act_dtype)
    y = _attn_out(tok, xq, gate, x, mem_k, mem_v, p["w_out"][1], p["final_norm_g"],
                  layer=1, bt=bt, lt=lt, final=True)
    return y, s_new[None], gconv_new[None], sconv_new[None]


def kernel(x_prompt, x_sample, mem_prompt, state_gdn, state_gdn_conv, state_sconv, cache_mem_k, cache_mem_v, norm_g, w_in_a, conv_w_a, a_log, dt_bias, o_norm_g, w_in_b, conv_w_b, mem_norm_g, w_mem_kv, w_out, final_norm_g):
    bp = x_prompt.shape[0]

    wa = jnp.transpose(w_in_a[0])
    c_b = QKV_WIDTH
    c_g = c_b + 2 * GDN_HEADS
    c_x = c_g + BRANCH_WIDTH
    wa = [wa[:c_b], wa[c_g:c_x], wa[c_x:],
          jnp.concatenate([wa[c_b:c_g], jnp.zeros((BA_PAD - 2 * GDN_HEADS, D_MODEL), wa.dtype)])]
    wa = [w.astype(BF16) for w in wa]
    pad_lo = jnp.zeros((GDN_HEADS,), F32)
    pad_hi = jnp.zeros((LANES - 2 * GDN_HEADS,), F32)
    params = {
        "norm_g": norm_g,
        "w_in_a": wa,
        "conv_w_a": conv_w_a[0],
        "alog_pad": jnp.concatenate([pad_lo, a_log[0], pad_hi])[None],
        "dtb_pad": jnp.concatenate([pad_lo, dt_bias[0], pad_hi])[None],
        "o_norm_g": o_norm_g,
        "w_in_b": [w_in_b[0].astype(BF16)],
        "conv_w_b": conv_w_b[0],
        "w_out": w_out.astype(BF16),
        "final_norm_g": final_norm_g[None],
    }

    def to_cache(t):
        t = t.reshape(t.shape[0], t.shape[1], X_HEADS, X_HEAD_DIM, t.shape[3])
        return jnp.transpose(t, (0, 1, 4, 2, 3))

    def from_cache(t):
        t = jnp.transpose(t, (0, 1, 3, 4, 2))
        return t.reshape(t.shape[0], t.shape[1], X_WIDTH, t.shape[4])

    mem_kt, mem_vt = _memkv(mem_prompt, mem_norm_g[None], w_mem_kv.astype(BF16))
    mem_k_p = to_cache(mem_kt)
    mem_v_p = to_cache(mem_vt)

    s0_p = jnp.zeros((bp,) + state_gdn.shape[2:], F32)
    gc0_p = jnp.zeros((bp,) + state_gdn_conv.shape[2:], F32)
    sc0_p = jnp.zeros((bp,) + state_sconv.shape[2:], F32)
    y_p, s_p, gc_p, sc_p = _trunk(x_prompt, mem_kt, mem_vt, s0_p, gc0_p, sc0_p,
                                  params, bt=1, lt=512, chunk=GDN_CHUNK, gdn_bt=1, gdn_lt=512,
                                  act_dtype=BF16)
    dec_seq = x_sample.shape[1]
    y_s, s_s, gc_s, sc_s = _trunk(x_sample, from_cache(cache_mem_k), from_cache(cache_mem_v),
                                  state_gdn[0], state_gdn_conv[0], state_sconv[0], params,
                                  bt=32, lt=dec_seq, chunk=dec_seq, gdn_bt=8, gdn_lt=dec_seq,
                                  act_dtype=F32)
    return (y_p, y_s, s_p, gc_p, sc_p, mem_k_p, mem_v_p, s_s, gc_s, sc_s)
```

```python
import functools

import jax
import jax.numpy as jnp
from jax import lax
from jax.experimental import pallas as pl
from jax.experimental.pallas import tpu as pltpu

F32 = jnp.float32
BF16 = jnp.bfloat16

D_MODEL = 1024
N_MEM = 256
X_WIDTH = 256
X_HEADS = 4
X_HEAD_DIM = 64
GDN_HEADS = 6
GDN_DK = 128
GDN_DV = 128
GDN_WIDTH = GDN_HEADS * GDN_DV
QK_WIDTH = GDN_HEADS * GDN_DK
QKV_WIDTH = 2 * QK_WIDTH + GDN_WIDTH
GDN_CHUNK = 64
SC_WIDTH = 768
BRANCH_WIDTH = 1024
EPS = 1e-6

LANES = 128
SUBLANES = 8
COL_CHUNK = 256
BA_PAD = LANES
VMEM_LIMIT = 56 * 1024 * 1024

HIGHEST = lax.Precision.HIGHEST


def _cparams(*sem):
    return pltpu.CompilerParams(dimension_semantics=sem, vmem_limit_bytes=VMEM_LIMIT)


def _rms_rows(x, g):
    r = lax.rsqrt(jnp.mean(x * x, axis=-1, keepdims=True) + EPS)
    return x * r * g


def _silu(x):
    h = 0.5 * x
    return h + h * jnp.tanh(h)


def _dot(a, b):
    return jnp.dot(a.astype(BF16), b.astype(BF16), preferred_element_type=F32)


def _dot_nt(a, b):
    return lax.dot_general(a.astype(BF16), b.astype(BF16), (((1,), (1,)), ((), ())),
                           preferred_element_type=F32)


def _dot_tn(a, b):
    return lax.dot_general(a.astype(BF16), b.astype(BF16), (((0,), (0,)), ((), ())),
                           preferred_element_type=F32)


def _dot_f32(a, b):
    return jnp.dot(a, b, preferred_element_type=F32, precision=HIGHEST)


def _col_chunks(n):
    return [slice(c, min(c + COL_CHUNK, n)) for c in range(0, n, COL_CHUNK)]


def _interleaved(gens):
    gens = list(gens)
    while gens:
        alive = []
        for g in gens:
            try:
                next(g)
                alive.append(g)
            except StopIteration:
                pass
        gens = alive
        yield


def _round_robin(gens):
    for _ in _interleaved(gens):
        pass


def _delayed(gen, rounds):
    for _ in range(rounds):
        yield
    yield from gen


def _memkv_kernel(m_ref, g_ref, w_ref, kt_ref, vt_ref):
    h = _rms_rows(m_ref[...], g_ref[...]).astype(BF16)
    for layer in range(w_ref.shape[0]):
        kv = jnp.dot(h, w_ref[layer], preferred_element_type=F32)
        kt_ref[layer] = kv[:, :X_WIDTH].T
        vt_ref[layer] = kv[:, X_WIDTH:].T


def _memkv(mem, g, w_bf16):
    bn, n_mem, _ = mem.shape
    depth = w_bf16.shape[0]
    out = jax.ShapeDtypeStruct((depth, bn, X_WIDTH, n_mem), F32)
    out_spec = pl.BlockSpec((depth, None, X_WIDTH, n_mem), lambda b: (0, b, 0, 0))
    return pl.pallas_call(
        _memkv_kernel,
        grid=(bn,),
        in_specs=[
            pl.BlockSpec((None, n_mem, D_MODEL), lambda b: (b, 0, 0)),
            pl.BlockSpec((1, D_MODEL), lambda b: (0, 0)),
            pl.BlockSpec(w_bf16.shape, lambda b: (0, 0, 0)),
        ],
        out_specs=[out_spec, out_spec],
        out_shape=[out, out],
        compiler_params=_cparams("arbitrary"),
        name="memkv",
    )(mem, g, w_bf16)


def _init_conv_carry(carry, hist_ref):
    @pl.when(pl.program_id(1) == 0)
    def _():
        carry[:, SUBLANES - hist_ref.shape[1]:, :] = hist_ref[...]


def _causal_conv_cols(x, cols, carry, hist_out, cw_ref, seqs=slice(None)):
    width = cw_ref.shape[0]
    bt, lt, n = x.shape
    g = lt // SUBLANES
    xe = jnp.concatenate([carry[seqs, :, cols], x], axis=1).reshape(bt, g + 1, SUBLANES, n)
    sub = lax.broadcasted_iota(jnp.int32, (1, 1, SUBLANES, n), 2)
    y = x.reshape(bt, g, SUBLANES, n) * cw_ref[width - 1:width, cols]
    for j in range(width - 1):
        s = width - 1 - j
        r = pltpu.roll(xe, s, axis=2)
        y = y + jnp.where(sub < s, r[:, :g], r[:, 1:]) * cw_ref[j:j + 1, cols]
    carry[seqs, :, cols] = x[:, lt - SUBLANES:, :]
    if hist_out is not None:
        hist_out[seqs, :, cols] = x[:, lt - (width - 1):, :]
    return y.reshape(bt, lt, n)


def _inproj_chunks(x_ref, g_ref, rest, *, mixer, bt, lt):
    tm = bt * lt
    h = _rms_rows(x_ref[...].reshape(tm, D_MODEL), g_ref[...]).astype(BF16)
    if mixer == "gdn":
        n_out = (len(rest) - 1) // 2
        w_refs, (mix_out, *plain_outs, hist_out) = rest[:n_out], rest[n_out:]

        def proj(i, cols):
            return _dot_nt(h, w_refs[i][cols, :]).reshape(bt, lt, -1)
    else:
        w_ref, hist_ref, cw_ref, mix_out, *plain_outs, hist_out, carry = rest
        _init_conv_carry(carry, hist_ref)
        bases = [0, 3 * SC_WIDTH]
        for out in plain_outs[:-1]:
            bases.append(bases[-1] + out.shape[-1])

        def proj(i, cols, offset=0):
            cols = slice(bases[i] + offset + cols.start, bases[i] + offset + cols.stop)
            return jnp.dot(h, w_ref[:, cols], preferred_element_type=F32).reshape(bt, lt, -1)

    def mixer_chunk(cols):
        if mixer == "gdn":
            y = proj(0, cols)
            hist_out[:, :, cols] = y[:, lt - hist_out.shape[1]:, :]
        else:
            gate_b = proj(0, cols)
            pre = proj(0, cols, SC_WIDTH) * proj(0, cols, 2 * SC_WIDTH)
            yield
            y = gate_b * _causal_conv_cols(pre, cols, carry, hist_out, cw_ref)
        mix_out[:, :, cols] = y.astype(mix_out.dtype)
        yield

    def plain_chunk(i, out, cols):
        out[:, :, cols] = proj(i, cols).astype(out.dtype)
        yield

    gens = [mixer_chunk(cols) for cols in _col_chunks(mix_out.shape[-1])]
    for i, out in enumerate(plain_outs):
        gens += [plain_chunk(i + 1, out, cols) for cols in _col_chunks(out.shape[-1])]
    return gens


def _inproj_kernel(x_ref, g_ref, *rest, mixer, bt, lt):
    gens = _inproj_chunks(x_ref, g_ref, rest, mixer=mixer, bt=bt, lt=lt)
    _round_robin([_delayed(g, i) for i, g in enumerate(gens)])


def _inproj(x, g, ws, hist, cw, *, mixer, bt, lt, act_dtype):
    bn, seq, _ = x.shape
    width, cc = cw.shape

    def act(n, dtype=act_dtype):
        return (pl.BlockSpec((bt, lt, n), lambda b, l: (b, l, 0)),
                jax.ShapeDtypeStruct((bn, seq, n), dtype))

    hist_spec = pl.BlockSpec((bt, width - 1, cc), lambda b, l: (b, 0, 0))
    args = [x, g, *ws]
    in_specs = [
        pl.BlockSpec((bt, lt, D_MODEL), lambda b, l: (b, l, 0)),
        pl.BlockSpec((1, D_MODEL), lambda b, l: (0, 0)),
    ] + [pl.BlockSpec(w.shape, lambda b, l: (0, 0)) for w in ws]
    outs = [act(cc), act(BRANCH_WIDTH), act(X_WIDTH)]
    scratch = []
    if mixer == "gdn":
        outs.append(act(BA_PAD, F32))
    else:
        args += [hist, cw]
        in_specs += [hist_spec, pl.BlockSpec((width, cc), lambda b, l: (0, 0))]
        scratch.append(pltpu.VMEM((bt, SUBLANES, cc), F32))
    outs.append((hist_spec, jax.ShapeDtypeStruct((bn, width - 1, cc), F32)))
    return pl.pallas_call(
        functools.partial(_inproj_kernel, mixer=mixer, bt=bt, lt=lt),
        grid=(bn // bt, seq // lt),
        in_specs=in_specs,
        out_specs=[o[0] for o in outs],
        out_shape=[o[1] for o in outs],
        scratch_shapes=scratch,
        compiler_params=_cparams("arbitrary", "arbitrary"),
        name="inproj_" + mixer,
    )(*args)


UNITS_PER_WAVE = 4


def _gdn_tile(qkv_ref, ba_ref, cw_ref, alog_ref, dtb_ref, og_ref, o_ref, s_ref, carry,
              *, chunk, group):
    bt, lt, _ = qkv_ref.shape
    nchunks = lt // chunk
    heads = range(GDN_HEADS)
    groups = [tuple(range(g, g + group)) for g in range(0, GDN_HEADS, group)]
    gw = group * chunk
    row = lax.broadcasted_iota(jnp.int32, (chunk, gw), 0)
    col = lax.broadcasted_iota(jnp.int32, (chunk, gw), 1) % chunk
    tril = row >= col
    strict = row > col
    lane_blk = lax.broadcasted_iota(jnp.int32, (1, gw), 1) // chunk
    ones_tril = (lax.broadcasted_iota(jnp.int32, (chunk, chunk), 0)
                 >= lax.broadcasted_iota(jnp.int32, (chunk, chunk), 1)).astype(F32)
    neg_a = -jnp.exp(alog_ref[...])
    dtb = dtb_ref[...]
    og = og_ref[...]

    def rows(c):
        return slice(c * chunk, (c + 1) * chunk)

    def lanes(base, h, n):
        return slice(base + h * n, base + (h + 1) * n)

    def cat(xs, axis):
        return xs[0] if len(xs) == 1 else jnp.concatenate(xs, axis=axis)

    def pick(xs):
        out = xs[0]
        for j in range(1, len(xs)):
            out = jnp.where(lane_blk == j, xs[j], out)
        return out

    def blockdiag(xs):
        z = jnp.zeros_like(xs[0])
        return cat([cat([x if i == j else z for i in range(len(xs))], 1)
                    for j, x in enumerate(xs)], 0)

    def blockdiag_of_lane_blocks(x):
        if group == 1:
            return x
        return cat([jnp.where(lane_blk == j, x, 0.0) for j in range(group)], 0)

    def conv_silu(u, cols):
        b, c = u
        x = qkv_ref[b, rows(c), cols].astype(F32)
        y = _causal_conv_cols(x[None], cols, carry, None, cw_ref, seqs=slice(b, b + 1))
        return _silu(y[0])

    def phase1(units, res):
        uh = [(u, h) for u in units for h in heads]
        ug = [(u, g) for u in units for g in range(len(groups))]
        gcum, gcum_t, beta_all = {}, {}, {}
        for u in units:
            ba = ba_ref[u[0], rows(u[1]), :]
            beta_all[u] = 1.0 / (1.0 + jnp.exp(-ba))
            z = ba + dtb
            softplus = jnp.maximum(z, 0.0) + jnp.log1p(jnp.exp(-jnp.abs(z)))
            gcum[u] = _dot_f32(ones_tril, neg_a * softplus)
        q, k, v = {}, {}, {}
        for u in units:
            for h in heads:
                q[u, h] = conv_silu(u, lanes(0, h, GDN_DK))
                k[u, h] = conv_silu(u, lanes(QK_WIDTH, h, GDN_DK))
                v[u, h] = conv_silu(u, lanes(2 * QK_WIDTH, h, GDN_DV))
            yield
        for key in uh:
            x = q[key]
            q[key] = x * (lax.rsqrt(jnp.sum(x * x, axis=-1, keepdims=True) + EPS)
                          * (GDN_DK ** -0.5))
            x = k[key]
            k[key] = x * lax.rsqrt(jnp.sum(x * x, axis=-1, keepdims=True) + EPS)
        yield
        for u in units:
            gcum_t[u] = cat([gcum[u]] * group, 0).T
        gc = {(u, h): jnp.broadcast_to(gcum[u][:, GDN_HEADS + h:GDN_HEADS + h + 1],
                                       (chunk, GDN_DK)) for u, h in uh}
        gl = {(u, h): gcum[u][chunk - 1:chunk, GDN_HEADS + h:GDN_HEADS + h + 1]
              for u, h in uh}
        beta = {(u, h): jnp.broadcast_to(beta_all[u][:, h:h + 1], (chunk, GDN_DK))
                for u, h in uh}
        eg = {key: jnp.exp(gc[key]) for key in uh}
        kb = {key: k[key] * beta[key] for key in uh}
        kkqk = {}
        for u, g in ug:
            grp = groups[g]
            kkqk[u, g] = _dot_nt(
                cat([cat([kb[u, h] for h in grp], 1), cat([q[u, h] for h in grp], 1)], 0),
                blockdiag([k[u, h] for h in grp]))
        yield
        n, p, aqk = {}, {}, {}
        for u, g in ug:
            grp = groups[g]
            g_col = pick([gc[u, h][:, :gw] for h in grp])
            g_row = pick([gcum_t[u][GDN_HEADS + h:GDN_HEADS + h + 1, :] for h in grp])
            decay = jnp.exp(jnp.where(tril, g_col - g_row, -jnp.inf))
            a = jnp.where(strict, kkqk[u, g][:chunk] * decay, 0.0)
            aqk[u, g] = (kkqk[u, g][chunk:] * decay).astype(BF16)
            n[u, g] = -a
            p[u, g] = _dot(a, blockdiag_of_lane_blocks(a))
        yield
        span = 2
        while 2 * span < chunk:
            for key in ug:
                both = _dot(cat([n[key], p[key]], 0), blockdiag_of_lane_blocks(p[key]))
                n[key] = n[key] + p[key] + both[:chunk]
                p[key] = both[chunk:]
            span *= 2
            yield
        for key in ug:
            n[key] = n[key] + p[key] + _dot(n[key], blockdiag_of_lane_blocks(p[key]))
        yield
        wu = {}
        for u, h in uh:
            rhs = jnp.concatenate([kb[u, h] * eg[u, h], v[u, h] * beta[u, h]], axis=-1)
            wu[u, h] = rhs + _dot(n[u, h // group][:, lanes(0, h % group, chunk)], rhs)
        yield
        for u in units:
            res[u] = dict(
                w_qd=[jnp.concatenate([wu[u, h][:, :GDN_DK], q[u, h] * eg[u, h]],
                                      axis=0).astype(BF16) for h in heads],
                u=[wu[u, h][:, GDN_DK:] for h in heads],
                kd=[(k[u, h] * jnp.exp(gl[u, h] - gc[u, h])).astype(BF16) for h in heads],
                aqk=[aqk[u, g] for g in range(len(groups))],
                egl=[jnp.exp(gl[u, h]) for h in heads])
        yield

    def phase2(units, res):
        for c in sorted({c for _, c in units}):
            us = [u for u in units if u[1] == c]
            uh = [(u, h) for u in us for h in heads]
            s = {(u, h): s_ref[u[0], h] for u, h in uh}
            ws_qs = {(u, h): _dot(res[u]["w_qd"][h], s[u, h]) for u, h in uh}
            yield
            v_new = {(u, h): res[u]["u"][h] - ws_qs[u, h][:chunk] for u, h in uh}
            o_grp = {(u, g): _dot(res[u]["aqk"][g], blockdiag([v_new[u, h] for h in grp]))
                     for u in us for g, grp in enumerate(groups)}
            for u, h in uh:
                s_ref[u[0], h] = (s[u, h] * res[u]["egl"][h]
                                  + _dot_tn(res[u]["kd"][h], v_new[u, h]))
            yield
            for u, h in uh:
                oh = ws_qs[u, h][chunk:] + o_grp[u, h // group][:, lanes(0, h % group, GDN_DV)]
                oh = oh * lax.rsqrt(jnp.mean(oh * oh, axis=-1, keepdims=True) + EPS) * og
                o_ref[u[0], rows(c), lanes(0, h, GDN_DV)] = oh.astype(o_ref.dtype)
            yield

    units = [(b, c) for b in range(bt) for c in range(nchunks)]
    waves = [units[i:i + UNITS_PER_WAVE] for i in range(0, len(units), UNITS_PER_WAVE)]
    res = {}
    prev = None
    for wave in waves:
        gens = [phase1(wave, res)]
        if prev is not None:
            gens.append(phase2(prev, res))
        yield from _interleaved(gens)
        prev = wave
    yield from phase2(prev, res)


def _gdn_kernel(qkv_ref, ba_ref, s0_ref, hist_ref, cw_ref, alog_ref, dtb_ref, og_ref,
                o_ref, s_ref, carry, *, chunk, group):
    @pl.when(pl.program_id(1) == 0)
    def _():
        s_ref[...] = s0_ref[...]

    _init_conv_carry(carry, hist_ref)
    _round_robin([_gdn_tile(qkv_ref, ba_ref, cw_ref, alog_ref, dtb_ref, og_ref, o_ref, s_ref,
                            carry, chunk=chunk, group=group)])


def _gdn(qkv, ba, s0, hist, cw, alog_pad, dtb_pad, og, *, chunk, bt, lt):
    group = 2 if 2 * chunk == LANES else 1
    bn, seq, _ = qkv.shape
    width, cc = cw.shape
    s_spec = pl.BlockSpec((bt, GDN_HEADS, GDN_DK, GDN_DV), lambda b, l: (b, 0, 0, 0))
    hist_spec = pl.BlockSpec((bt, width - 1, cc), lambda b, l: (b, 0, 0))
    vec = pl.BlockSpec((1, LANES), lambda b, l: (0, 0))
    return pl.pallas_call(
        functools.partial(_gdn_kernel, chunk=chunk, group=group),
        grid=(bn // bt, seq // lt),
        in_specs=[
            pl.BlockSpec((bt, lt, QKV_WIDTH), lambda b, l: (b, l, 0)),
            pl.BlockSpec((bt, lt, BA_PAD), lambda b, l: (b, l, 0)),
            s_spec, hist_spec,
            pl.BlockSpec((width, cc), lambda b, l: (0, 0)),
            vec, vec, vec,
        ],
        out_specs=[
            pl.BlockSpec((bt, lt, GDN_WIDTH), lambda b, l: (b, l, 0)),
            s_spec,
        ],
        out_shape=[
            jax.ShapeDtypeStruct((bn, seq, GDN_WIDTH), qkv.dtype),
            jax.ShapeDtypeStruct(s0.shape, F32),
        ],
        scratch_shapes=[pltpu.VMEM((bt, SUBLANES, cc), F32)],
        compiler_params=_cparams("arbitrary", "arbitrary"),
        name="gdn",
    )(qkv, ba, s0, hist, cw, alog_pad, dtb_pad, og)


PROJ_CHUNK_SPACING = (5, 2)


def _gdn_layer_kernel(x_ref, g_ref, w_qkv, w_gate, w_xq, w_ba, s0_ref, hist_ref, cw_ref,
                      alog_ref, dtb_ref, og_ref, gate_out, xq_out, hist_out, o_ref, s_ref,
                      qkv_scr, ba_scr, carry, *, lt, tiles_per_seq, chunk, group):
    i = pl.program_id(0)
    slot = i % 2
    prev = jnp.maximum(i - 1, 0)

    @pl.when(i == 0)
    def _():
        qkv_scr[1] = jnp.zeros(qkv_scr.shape[1:], F32)
        ba_scr[1] = jnp.zeros(ba_scr.shape[1:], F32)

    @pl.when(prev % tiles_per_seq == 0)
    def _():
        s_ref[...] = s0_ref[...]
        carry[:, SUBLANES - hist_ref.shape[1]:, :] = hist_ref[...]

    proj = _inproj_chunks(
        x_ref, g_ref, (w_qkv, w_gate, w_xq, w_ba, qkv_scr.at[slot], gate_out, xq_out,
                       ba_scr.at[slot], hist_out), mixer="gdn", bt=1, lt=lt)
    rule = _gdn_tile(qkv_scr.at[1 - slot], ba_scr.at[1 - slot], cw_ref, alog_ref, dtb_ref,
                     og_ref, o_ref, s_ref, carry, chunk=chunk, group=group)
    num, den = PROJ_CHUNK_SPACING
    _round_robin([rule] + [_delayed(p, k * num // den) for k, p in enumerate(proj)])


def _gdn_layer(x, g, ws, s0, hist, cw, alog_pad, dtb_pad, og, *, chunk, lt, act_dtype):
    group = 2 if 2 * chunk == LANES else 1
    bn, seq, _ = x.shape
    nl = seq // lt
    tiles = bn * nl
    width, cc = cw.shape

    def cur(i):
        return jnp.minimum(i, tiles - 1)

    def prev(i):
        return jnp.maximum(i - 1, 0)

    def tile(n, which):
        return pl.BlockSpec((1, lt, n), lambda i: (which(i), 0, 0))

    def per_seq(shape, which):
        return pl.BlockSpec((1,) + shape, lambda i: (which(i) // nl,) + (0,) * len(shape))

    def whole(a):
        return pl.BlockSpec(a.shape, lambda i: (0,) * a.ndim)

    state = (GDN_HEADS, GDN_DK, GDN_DV)
    gate, xq, hist_new, tok, s_new = pl.pallas_call(
        functools.partial(_gdn_layer_kernel, lt=lt, tiles_per_seq=nl, chunk=chunk, group=group),
        grid=(tiles + 1,),
        in_specs=[tile(D_MODEL, cur), whole(g)] + [whole(w) for w in ws] + [
            per_seq(state, prev), per_seq((width - 1, cc), prev), whole(cw),
            whole(alog_pad), whole(dtb_pad), whole(og)],
        out_specs=[
            tile(BRANCH_WIDTH, cur), tile(X_WIDTH, cur), per_seq((width - 1, cc), cur),
            tile(GDN_WIDTH, prev), per_seq(state, prev)],
        out_shape=[
            jax.ShapeDtypeStruct((tiles, lt, BRANCH_WIDTH), act_dtype),
            jax.ShapeDtypeStruct((tiles, lt, X_WIDTH), act_dtype),
            jax.ShapeDtypeStruct(hist.shape, F32),
            jax.ShapeDtypeStruct((tiles, lt, GDN_WIDTH), act_dtype),
            jax.ShapeDtypeStruct(s0.shape, F32)],
        scratch_shapes=[
            pltpu.VMEM((2, 1, lt, cc), F32),
            pltpu.VMEM((2, 1, lt, BA_PAD), F32),
            pltpu.VMEM((1, SUBLANES, cc), F32)],
        compiler_params=_cparams("arbitrary"),
        name="gdn_layer",
    )(x.reshape(tiles, lt, D_MODEL), g, *ws, s0, hist, cw, alog_pad, dtb_pad, og)
    return (gate.reshape(bn, seq, -1), xq.reshape(bn, seq, -1), tok.reshape(bn, seq, -1),
            s_new, hist_new)


ATTN_ROWS = 256
ATTN_SEQS = 8


def _attn_out_kernel(tok_ref, xq_ref, gate_ref, x_ref, kt_ref, vt_ref, w_ref, fg_ref,
                     y_ref, xo_scr, *, bt, lt, final):
    lane_head = lax.broadcasted_iota(jnp.int32, (1, X_WIDTH), 1) // X_HEAD_DIM
    scale = X_HEAD_DIM ** -0.5
    tm = bt * lt

    def attend(b, rs):
        r = rs.stop - rs.start
        q = xq_ref[b, rs, :]
        q = q * jnp.asarray(scale, q.dtype)
        qx = jnp.concatenate([jnp.where(lane_head == h, q, jnp.zeros_like(q))
                              for h in range(X_HEADS)], axis=0)
        s = _dot(qx, kt_ref[b])
        yield
        e = jnp.exp(s - jnp.max(s, axis=-1, keepdims=True))
        p = e / jnp.sum(e, axis=-1, keepdims=True)
        o4 = _dot_nt(p, vt_ref[b])
        yield
        xo = jnp.where(lane_head == 0, o4[0:r], 0.0)
        for h in range(1, X_HEADS):
            xo = xo + jnp.where(lane_head == h, o4[h * r:(h + 1) * r], 0.0)
        xo_scr[b, rs, :] = xo
        yield

    acc = {}

    def tok_proj():
        sg = _silu(gate_ref[:, :, :GDN_WIDTH].astype(F32).reshape(tm, GDN_WIDTH))
        br = (tok_ref[...].astype(F32).reshape(tm, GDN_WIDTH) * sg).astype(BF16)
        for cols in _col_chunks(D_MODEL):
            acc[cols.start] = (x_ref[:, :, cols].reshape(tm, -1)
                               + _dot(br, w_ref[:GDN_WIDTH, cols]))
            yield

    if bt == 1:
        blocks = [slice(r, r + ATTN_ROWS) for r in range(0, lt, ATTN_ROWS)]
        _round_robin([attend(0, rs) for rs in blocks] + [tok_proj()])
    else:
        def seq_group(i, carry):
            _round_robin([attend(i * ATTN_SEQS + j, slice(0, lt)) for j in range(ATTN_SEQS)])
            return carry

        lax.fori_loop(0, bt // ATTN_SEQS, seq_group, 0)
        _round_robin([tok_proj()])

    sg_x = _silu(gate_ref[:, :, GDN_WIDTH:].astype(F32).reshape(tm, X_WIDTH))
    br_x = (xo_scr[...].reshape(tm, X_WIDTH) * sg_x).astype(BF16)
    y = [acc[cols.start] + _dot(br_x, w_ref[GDN_WIDTH:, cols]) for cols in _col_chunks(D_MODEL)]
    if final:
        ms = sum(jnp.sum(c * c, axis=-1, keepdims=True) for c in y) * (1.0 / D_MODEL)
        r = lax.rsqrt(ms + EPS)
        y = [c * r * fg_ref[:, cols] for c, cols in zip(y, _col_chunks(D_MODEL))]
    for c, cols in zip(y, _col_chunks(D_MODEL)):
        y_ref[:, :, cols] = c.reshape(bt, lt, -1)


def _attn_out(tok, xq, gate, x, mk, mv, w_bf16, fg, *, layer, bt, lt, final):
    bn, seq, _ = x.shape

    def act(n):
        return pl.BlockSpec((bt, lt, n), lambda b, l: (b, l, 0))

    mem = pl.BlockSpec((None, bt, X_WIDTH, N_MEM), lambda b, l: (layer, b, 0, 0))
    return pl.pallas_call(
        functools.partial(_attn_out_kernel, bt=bt, lt=lt, final=final),
        grid=(bn // bt, seq // lt),
        in_specs=[
            act(GDN_WIDTH), act(X_WIDTH), act(BRANCH_WIDTH), act(D_MODEL), mem, mem,
            pl.BlockSpec((BRANCH_WIDTH, D_MODEL), lambda b, l: (0, 0)),
            pl.BlockSpec((1, D_MODEL), lambda b, l: (0, 0)),
        ],
        out_specs=act(D_MODEL),
        out_shape=jax.ShapeDtypeStruct(x.shape, F32),
        scratch_shapes=[pltpu.VMEM((bt, lt, X_WIDTH), F32)],
        compiler_params=_cparams("arbitrary", "arbitrary"),
        name="attn_out",
    )(tok, xq, gate, x, mk, mv, w_bf16, fg)


def _trunk(x, mem_k, mem_v, gdn_s, gdn_conv, sc_conv, p, *, bt, lt, chunk, gdn_bt, gdn_lt,
           act_dtype):
    if gdn_bt == 1 and bt == 1:
        gate, xq, tok, s_new, gconv_new = _gdn_layer(
            x, p["norm_g"][0:1], p["w_in_a"], gdn_s, gdn_conv, p["conv_w_a"], p["alog_pad"],
            p["dtb_pad"], p["o_norm_g"], chunk=chunk, lt=gdn_lt, act_dtype=act_dtype)
    else:
        qkv, gate, xq, ba, gconv_new = _inproj(
            x, p["norm_g"][0:1], p["w_in_a"], None, p["conv_w_a"], mixer="gdn", bt=bt, lt=lt,
            act_dtype=act_dtype)
        tok, s_new = _gdn(qkv, ba, gdn_s, gdn_conv, p["conv_w_a"], p["alog_pad"],
                          p["dtb_pad"], p["o_norm_g"], chunk=chunk, bt=gdn_bt, lt=gdn_lt)
    x = _attn_out(tok, xq, gate, x, mem_k, mem_v, p["w_out"][0], p["final_norm_g"],
                  layer=0, bt=bt, lt=lt, final=False)
    tok, gate, xq, sconv_new = _inproj(
        x, p["norm_g"][1:2], p["w_in_b"], sc_conv, p["conv_w_b"], mixer="sconv", bt=bt, lt=lt,
        act_dtype=act_dtype)
    y = _attn_out(tok, xq, gate, x, mem_k, mem_v, p["w_out"][1], p["final_norm_g"],
                  layer=1, bt=bt, lt=lt, final=True)
    return y, s_new[None], gconv_new[None], sconv_new[None]


def kernel(x_prompt, x_sample, mem_prompt, state_gdn, state_gdn_conv, state_sconv, cache_mem_k, cache_mem_v, norm_g, w_in_a, conv_w_a, a_log, dt_bias, o_norm_g, w_in_b, conv_w_b, mem_norm_g, w_mem_kv, w_out, final_norm_g):
    bp = x_prompt.shape[0]

    wa = jnp.transpose(w_in_a[0])
    c_b = QKV_WIDTH
    c_g = c_b + 2 * GDN_HEADS
    c_x = c_g + BRANCH_WIDTH
    wa = [wa[:c_b], wa[c_g:c_x], wa[c_x:],
          jnp.concatenate([wa[c_b:c_g], jnp.zeros((BA_PAD - 2 * GDN_HEADS, D_MODEL), wa.dtype)])]
    wa = [w.astype(BF16) for w in wa]
    pad_lo = jnp.zeros((GDN_HEADS,), F32)
    pad_hi = jnp.zeros((LANES - 2 * GDN_HEADS,), F32)
    params = {
        "norm_g": norm_g,
        "w_in_a": wa,
        "conv_w_a": conv_w_a[0],
        "alog_pad": jnp.concatenate([pad_lo, a_log[0], pad_hi])[None],
        "dtb_pad": jnp.concatenate([pad_lo, dt_bias[0], pad_hi])[None],
        "o_norm_g": o_norm_g,
        "w_in_b": [w_in_b[0].astype(BF16)],
        "conv_w_b": conv_w_b[0],
        "w_out": w_out.astype(BF16),
        "final_norm_g": final_norm_g[None],
    }

    def to_cache(t):
        t = t.reshape(t.shape[0], t.shape[1], X_HEADS, X_HEAD_DIM, t.shape[3])
        return jnp.transpose(t, (0, 1, 4, 2, 3))

    def from_cache(t):
        t = jnp.transpose(t, (0, 1, 3, 4, 2))
        return t.reshape(t.shape[0], t.shape[1], X_WIDTH, t.shape[4])

    mem_kt, mem_vt = _memkv(mem_prompt, mem_norm_g[None], w_mem_kv.astype(BF16))
    mem_k_p = to_cache(mem_kt)
    mem_v_p = to_cache(mem_vt)

    s0_p = jnp.zeros((bp,) + state_gdn.shape[2:], F32)
    gc0_p = jnp.zeros((bp,) + state_gdn_conv.shape[2:], F32)
    sc0_p = jnp.zeros((bp,) + state_sconv.shape[2:], F32)
    y_p, s_p, gc_p, sc_p = _trunk(x_prompt, mem_kt, mem_vt, s0_p, gc0_p, sc0_p,
                                  params, bt=1, lt=512, chunk=GDN_CHUNK, gdn_bt=1, gdn_lt=512,
                                  act_dtype=BF16)
    dec_seq = x_sample.shape[1]
    y_s, s_s, gc_s, sc_s = _trunk(x_sample, from_cache(cache_mem_k), from_cache(cache_mem_v),
                                  state_gdn[0], state_gdn_conv[0], state_sconv[0], params,
                                  bt=32, lt=dec_seq, chunk=dec_seq, gdn_bt=16, gdn_lt=dec_seq,
                                  act_dtype=F32)
    return (y_p, y_s, s_p, gc_p, sc_p, mem_k_p, mem_v_p, s_s, gc_s, sc_s)
```

```python
import functools

import jax
import jax.numpy as jnp
from jax import lax
from jax.experimental import pallas as pl
from jax.experimental.pallas import tpu as pltpu

F32 = jnp.float32
BF16 = jnp.bfloat16

D_MODEL = 1024
N_MEM = 256
X_WIDTH = 256
X_HEADS = 4
X_HEAD_DIM = 64
GDN_HEADS = 6
GDN_DK = 128
GDN_DV = 128
GDN_WIDTH = GDN_HEADS * GDN_DV
QK_WIDTH = GDN_HEADS * GDN_DK
QKV_WIDTH = 2 * QK_WIDTH + GDN_WIDTH
GDN_CHUNK = 64
SC_WIDTH = 768
BRANCH_WIDTH = 1024
EPS = 1e-6

LANES = 128
SUBLANES = 8
COL_CHUNK = 256
BA_PAD = LANES
VMEM_LIMIT = 56 * 1024 * 1024

HIGHEST = lax.Precision.HIGHEST


def _cparams(*sem):
    return pltpu.CompilerParams(dimension_semantics=sem, vmem_limit_bytes=VMEM_LIMIT)


def _rms_rows(x, g):
    r = lax.rsqrt(jnp.mean(x * x, axis=-1, keepdims=True) + EPS)
    return x * r * g


def _silu(x):
    h = 0.5 * x
    return h + h * jnp.tanh(h)


def _dot(a, b):
    return jnp.dot(a.astype(BF16), b.astype(BF16), preferred_element_type=F32)


def _dot_nt(a, b):
    return lax.dot_general(a.astype(BF16), b.astype(BF16), (((1,), (1,)), ((), ())),
                           preferred_element_type=F32)


def _dot_tn(a, b):
    return lax.dot_general(a.astype(BF16), b.astype(BF16), (((0,), (0,)), ((), ())),
                           preferred_element_type=F32)


def _dot_f32(a, b):
    return jnp.dot(a, b, preferred_element_type=F32, precision=HIGHEST)


def _col_chunks(n):
    return [slice(c, min(c + COL_CHUNK, n)) for c in range(0, n, COL_CHUNK)]


def _interleaved(gens):
    gens = list(gens)
    while gens:
        alive = []
        for g in gens:
            try:
                next(g)
                alive.append(g)
            except StopIteration:
                pass
        gens = alive
        yield


def _round_robin(gens):
    for _ in _interleaved(gens):
        pass


def _delayed(gen, rounds):
    for _ in range(rounds):
        yield
    yield from gen


def _memkv_kernel(m_ref, g_ref, w_ref, kt_ref, vt_ref):
    h = _rms_rows(m_ref[...], g_ref[...]).astype(BF16)
    for layer in range(w_ref.shape[0]):
        kv = jnp.dot(h, w_ref[layer], preferred_element_type=F32)
        kt_ref[layer] = kv[:, :X_WIDTH].T
        vt_ref[layer] = kv[:, X_WIDTH:].T


def _memkv(mem, g, w_bf16):
    bn, n_mem, _ = mem.shape
    depth = w_bf16.shape[0]
    out = jax.ShapeDtypeStruct((depth, bn, X_WIDTH, n_mem), F32)
    out_spec = pl.BlockSpec((depth, None, X_WIDTH, n_mem), lambda b: (0, b, 0, 0))
    return pl.pallas_call(
        _memkv_kernel,
        grid=(bn,),
        in_specs=[
            pl.BlockSpec((None, n_mem, D_MODEL), lambda b: (b, 0, 0)),
            pl.BlockSpec((1, D_MODEL), lambda b: (0, 0)),
            pl.BlockSpec(w_bf16.shape, lambda b: (0, 0, 0)),
        ],
        out_specs=[out_spec, out_spec],
        out_shape=[out, out],
        compiler_params=_cparams("arbitrary"),
        name="memkv",
    )(mem, g, w_bf16)


def _init_conv_carry(carry, hist_ref):
    @pl.when(pl.program_id(1) == 0)
    def _():
        carry[:, SUBLANES - hist_ref.shape[1]:, :] = hist_ref[...]


def _causal_conv_cols(x, cols, carry, hist_out, cw_ref, seqs=slice(None)):
    width = cw_ref.shape[0]
    bt, lt, n = x.shape
    g = lt // SUBLANES
    xe = jnp.concatenate([carry[seqs, :, cols], x], axis=1).reshape(bt, g + 1, SUBLANES, n)
    sub = lax.broadcasted_iota(jnp.int32, (1, 1, SUBLANES, n), 2)
    y = x.reshape(bt, g, SUBLANES, n) * cw_ref[width - 1:width, cols]
    for j in range(width - 1):
        s = width - 1 - j
        r = pltpu.roll(xe, s, axis=2)
        y = y + jnp.where(sub < s, r[:, :g], r[:, 1:]) * cw_ref[j:j + 1, cols]
    carry[seqs, :, cols] = x[:, lt - SUBLANES:, :]
    if hist_out is not None:
        hist_out[seqs, :, cols] = x[:, lt - (width - 1):, :]
    return y.reshape(bt, lt, n)


def _inproj_chunks(x_ref, g_ref, rest, *, mixer, bt, lt):
    tm = bt * lt
    h = _rms_rows(x_ref[...].reshape(tm, D_MODEL), g_ref[...]).astype(BF16)
    if mixer == "gdn":
        n_out = (len(rest) - 1) // 2
        w_refs, (mix_out, *plain_outs, hist_out) = rest[:n_out], rest[n_out:]

        def proj(i, cols):
            return _dot_nt(h, w_refs[i][cols, :]).reshape(bt, lt, -1)
    else:
        w_ref, hist_ref, cw_ref, mix_out, *plain_outs, hist_out, carry = rest
        _init_conv_carry(carry, hist_ref)
        bases = [0, 3 * SC_WIDTH]
        for out in plain_outs[:-1]:
            bases.append(bases[-1] + out.shape[-1])

        def proj(i, cols, offset=0):
            cols = slice(bases[i] + offset + cols.start, bases[i] + offset + cols.stop)
            return jnp.dot(h, w_ref[:, cols], preferred_element_type=F32).reshape(bt, lt, -1)

    def mixer_chunk(cols):
        if mixer == "gdn":
            y = proj(0, cols)
            hist_out[:, :, cols] = y[:, lt - hist_out.shape[1]:, :]
        else:
            gate_b = proj(0, cols)
            pre = proj(0, cols, SC_WIDTH) * proj(0, cols, 2 * SC_WIDTH)
            yield
            y = gate_b * _causal_conv_cols(pre, cols, carry, hist_out, cw_ref)
        mix_out[:, :, cols] = y.astype(mix_out.dtype)
        yield

    def plain_chunk(i, out, cols):
        out[:, :, cols] = proj(i, cols).astype(out.dtype)
        yield

    gens = [mixer_chunk(cols) for cols in _col_chunks(mix_out.shape[-1])]
    for i, out in enumerate(plain_outs):
        gens += [plain_chunk(i + 1, out, cols) for cols in _col_chunks(out.shape[-1])]
    return gens


def _inproj_kernel(x_ref, g_ref, *rest, mixer, bt, lt):
    gens = _inproj_chunks(x_ref, g_ref, rest, mixer=mixer, bt=bt, lt=lt)
    _round_robin([_delayed(g, i) for i, g in enumerate(gens)])


def _inproj(x, g, ws, hist, cw, *, mixer, bt, lt, act_dtype):
    bn, seq, _ = x.shape
    width, cc = cw.shape

    def act(n, dtype=act_dtype):
        return (pl.BlockSpec((bt, lt, n), lambda b, l: (b, l, 0)),
                jax.ShapeDtypeStruct((bn, seq, n), dtype))

    hist_spec = pl.BlockSpec((bt, width - 1, cc), lambda b, l: (b, 0, 0))
    args = [x, g, *ws]
    in_specs = [
        pl.BlockSpec((bt, lt, D_MODEL), lambda b, l: (b, l, 0)),
        pl.BlockSpec((1, D_MODEL), lambda b, l: (0, 0)),
    ] + [pl.BlockSpec(w.shape, lambda b, l: (0, 0)) for w in ws]
    outs = [act(cc), act(BRANCH_WIDTH), act(X_WIDTH)]
    scratch = []
    if mixer == "gdn":
        outs.append(act(BA_PAD, F32))
    else:
        args += [hist, cw]
        in_specs += [hist_spec, pl.BlockSpec((width, cc), lambda b, l: (0, 0))]
        scratch.append(pltpu.VMEM((bt, SUBLANES, cc), F32))
    outs.append((hist_spec, jax.ShapeDtypeStruct((bn, width - 1, cc), F32)))
    return pl.pallas_call(
        functools.partial(_inproj_kernel, mixer=mixer, bt=bt, lt=lt),
        grid=(bn // bt, seq // lt),
        in_specs=in_specs,
        out_specs=[o[0] for o in outs],
        out_shape=[o[1] for o in outs],
        scratch_shapes=scratch,
        compiler_params=_cparams("arbitrary", "arbitrary"),
        name="inproj_" + mixer,
    )(*args)


CHUNKS_PER_WAVE = 4
SEQS_PER_WAVE = 8


def _gdn_tile(qkv_ref, ba_ref, cw_ref, alog_ref, dtb_ref, og_ref, o_ref, s_ref, carry,
              *, chunk, group):
    bt, lt, _ = qkv_ref.shape
    nchunks = lt // chunk
    heads = range(GDN_HEADS)
    groups = [tuple(range(g, g + group)) for g in range(0, GDN_HEADS, group)]
    gw = group * chunk
    row = lax.broadcasted_iota(jnp.int32, (chunk, gw), 0)
    col = lax.broadcasted_iota(jnp.int32, (chunk, gw), 1) % chunk
    tril = row >= col
    strict = row > col
    lane_blk = lax.broadcasted_iota(jnp.int32, (1, gw), 1) // chunk
    ones_tril = (lax.broadcasted_iota(jnp.int32, (chunk, chunk), 0)
                 >= lax.broadcasted_iota(jnp.int32, (chunk, chunk), 1)).astype(F32)
    neg_a = -jnp.exp(alog_ref[...])
    dtb = dtb_ref[...]
    og = og_ref[...]

    def rows(c):
        return slice(c * chunk, (c + 1) * chunk)

    def lanes(base, h, n):
        return slice(base + h * n, base + (h + 1) * n)

    def cat(xs, axis):
        return xs[0] if len(xs) == 1 else jnp.concatenate(xs, axis=axis)

    def pick(xs):
        out = xs[0]
        for j in range(1, len(xs)):
            out = jnp.where(lane_blk == j, xs[j], out)
        return out

    def blockdiag(xs):
        z = jnp.zeros_like(xs[0])
        return cat([cat([x if i == j else z for i in range(len(xs))], 1)
                    for j, x in enumerate(xs)], 0)

    def blockdiag_of_lane_blocks(x):
        if group == 1:
            return x
        return cat([jnp.where(lane_blk == j, x, 0.0) for j in range(group)], 0)

    def conv_silu(u, cols):
        b, c = u
        x = qkv_ref[b, rows(c), cols].astype(F32)
        y = _causal_conv_cols(x[None], cols, carry, None, cw_ref, seqs=slice(b, b + 1))
        return _silu(y[0])

    def phase1(units, res):
        uh = [(u, h) for u in units for h in heads]
        ug = [(u, g) for u in units for g in range(len(groups))]
        gcum, gcum_t, beta_all = {}, {}, {}
        for u in units:
            ba = ba_ref[u[0], rows(u[1]), :]
            beta_all[u] = 1.0 / (1.0 + jnp.exp(-ba))
            z = ba + dtb
            softplus = jnp.maximum(z, 0.0) + jnp.log1p(jnp.exp(-jnp.abs(z)))
            gcum[u] = _dot_f32(ones_tril, neg_a * softplus)
        q, k, v = {}, {}, {}
        for u in units:
            for h in heads:
                q[u, h] = conv_silu(u, lanes(0, h, GDN_DK))
                k[u, h] = conv_silu(u, lanes(QK_WIDTH, h, GDN_DK))
                v[u, h] = conv_silu(u, lanes(2 * QK_WIDTH, h, GDN_DV))
            yield
        for key in uh:
            x = q[key]
            q[key] = x * (lax.rsqrt(jnp.sum(x * x, axis=-1, keepdims=True) + EPS)
                          * (GDN_DK ** -0.5))
            x = k[key]
            k[key] = x * lax.rsqrt(jnp.sum(x * x, axis=-1, keepdims=True) + EPS)
        yield
        for u in units:
            gcum_t[u] = cat([gcum[u]] * group, 0).T
        gc = {(u, h): jnp.broadcast_to(gcum[u][:, GDN_HEADS + h:GDN_HEADS + h + 1],
                                       (chunk, GDN_DK)) for u, h in uh}
        gl = {(u, h): gcum[u][chunk - 1:chunk, GDN_HEADS + h:GDN_HEADS + h + 1]
              for u, h in uh}
        beta = {(u, h): jnp.broadcast_to(beta_all[u][:, h:h + 1], (chunk, GDN_DK))
                for u, h in uh}
        eg = {key: jnp.exp(gc[key]) for key in uh}
        kb = {key: k[key] * beta[key] for key in uh}
        kkqk = {}
        for u, g in ug:
            grp = groups[g]
            kkqk[u, g] = _dot_nt(
                cat([cat([kb[u, h] for h in grp], 1), cat([q[u, h] for h in grp], 1)], 0),
                blockdiag([k[u, h] for h in grp]))
        yield
        n, p, aqk = {}, {}, {}
        for u, g in ug:
            grp = groups[g]
            g_col = pick([gc[u, h][:, :gw] for h in grp])
            g_row = pick([gcum_t[u][GDN_HEADS + h:GDN_HEADS + h + 1, :] for h in grp])
            decay = jnp.exp(jnp.where(tril, g_col - g_row, -jnp.inf))
            a = jnp.where(strict, kkqk[u, g][:chunk] * decay, 0.0)
            aqk[u, g] = (kkqk[u, g][chunk:] * decay).astype(BF16)
            n[u, g] = -a
            p[u, g] = _dot(a, blockdiag_of_lane_blocks(a))
        yield
        span = 2
        while 2 * span < chunk:
            for key in ug:
                both = _dot(cat([n[key], p[key]], 0), blockdiag_of_lane_blocks(p[key]))
                n[key] = n[key] + p[key] + both[:chunk]
                p[key] = both[chunk:]
            span *= 2
            yield
        for key in ug:
            n[key] = n[key] + p[key] + _dot(n[key], blockdiag_of_lane_blocks(p[key]))
        yield
        wu = {}
        for u, h in uh:
            rhs = jnp.concatenate([kb[u, h] * eg[u, h], v[u, h] * beta[u, h]], axis=-1)
            wu[u, h] = rhs + _dot(n[u, h // group][:, lanes(0, h % group, chunk)], rhs)
        yield
        for u in units:
            res[u] = dict(
                w_qd=[jnp.concatenate([wu[u, h][:, :GDN_DK], q[u, h] * eg[u, h]],
                                      axis=0).astype(BF16) for h in heads],
                u=[wu[u, h][:, GDN_DK:] for h in heads],
                kd=[(k[u, h] * jnp.exp(gl[u, h] - gc[u, h])).astype(BF16) for h in heads],
                aqk=[aqk[u, g] for g in range(len(groups))],
                egl=[jnp.exp(gl[u, h]) for h in heads])
        yield

    def phase2(units, res):
        for c in sorted({c for _, c in units}):
            us = [u for u in units if u[1] == c]
            uh = [(u, h) for u in us for h in heads]
            s = {(u, h): s_ref[u[0], h] for u, h in uh}
            ws_qs = {(u, h): _dot(res[u]["w_qd"][h], s[u, h]) for u, h in uh}
            yield
            v_new = {(u, h): res[u]["u"][h] - ws_qs[u, h][:chunk] for u, h in uh}
            o_grp = {(u, g): _dot(res[u]["aqk"][g], blockdiag([v_new[u, h] for h in grp]))
                     for u in us for g, grp in enumerate(groups)}
            for u, h in uh:
                s_ref[u[0], h] = (s[u, h] * res[u]["egl"][h]
                                  + _dot_tn(res[u]["kd"][h], v_new[u, h]))
            yield
            for u, h in uh:
                oh = ws_qs[u, h][chunk:] + o_grp[u, h // group][:, lanes(0, h % group, GDN_DV)]
                oh = oh * lax.rsqrt(jnp.mean(oh * oh, axis=-1, keepdims=True) + EPS) * og
                o_ref[u[0], rows(c), lanes(0, h, GDN_DV)] = oh.astype(o_ref.dtype)
            yield

    units = [(b, c) for b in range(bt) for c in range(nchunks)]
    per_wave = CHUNKS_PER_WAVE if nchunks > 1 else SEQS_PER_WAVE
    waves = [units[i:i + per_wave] for i in range(0, len(units), per_wave)]
    res = {}
    prev = None
    for wave in waves:
        gens = [phase1(wave, res)]
        if prev is not None:
            gens.append(phase2(prev, res))
        yield from _interleaved(gens)
        prev = wave
    yield from phase2(prev, res)


def _gdn_kernel(qkv_ref, ba_ref, s0_ref, hist_ref, cw_ref, alog_ref, dtb_ref, og_ref,
                o_ref, s_ref, carry, *, chunk, group):
    @pl.when(pl.program_id(1) == 0)
    def _():
        s_ref[...] = s0_ref[...]

    _init_conv_carry(carry, hist_ref)
    _round_robin([_gdn_tile(qkv_ref, ba_ref, cw_ref, alog_ref, dtb_ref, og_ref, o_ref, s_ref,
                            carry, chunk=chunk, group=group)])


def _gdn(qkv, ba, s0, hist, cw, alog_pad, dtb_pad, og, *, chunk, bt, lt):
    group = 2 if 2 * chunk == LANES else 1
    bn, seq, _ = qkv.shape
    width, cc = cw.shape
    s_spec = pl.BlockSpec((bt, GDN_HEADS, GDN_DK, GDN_DV), lambda b, l: (b, 0, 0, 0))
    hist_spec = pl.BlockSpec((bt, width - 1, cc), lambda b, l: (b, 0, 0))
    vec = pl.BlockSpec((1, LANES), lambda b, l: (0, 0))
    return pl.pallas_call(
        functools.partial(_gdn_kernel, chunk=chunk, group=group),
        grid=(bn // bt, seq // lt),
        in_specs=[
            pl.BlockSpec((bt, lt, QKV_WIDTH), lambda b, l: (b, l, 0)),
            pl.BlockSpec((bt, lt, BA_PAD), lambda b, l: (b, l, 0)),
            s_spec, hist_spec,
            pl.BlockSpec((width, cc), lambda b, l: (0, 0)),
            vec, vec, vec,
        ],
        out_specs=[
            pl.BlockSpec((bt, lt, GDN_WIDTH), lambda b, l: (b, l, 0)),
            s_spec,
        ],
        out_shape=[
            jax.ShapeDtypeStruct((bn, seq, GDN_WIDTH), qkv.dtype),
            jax.ShapeDtypeStruct(s0.shape, F32),
        ],
        scratch_shapes=[pltpu.VMEM((bt, SUBLANES, cc), F32)],
        compiler_params=_cparams("arbitrary", "arbitrary"),
        name="gdn",
    )(qkv, ba, s0, hist, cw, alog_pad, dtb_pad, og)


PROJ_CHUNK_SPACING = (5, 2)


def _gdn_layer_kernel(x_ref, g_ref, w_qkv, w_gate, w_xq, w_ba, s0_ref, hist_ref, cw_ref,
                      alog_ref, dtb_ref, og_ref, gate_out, xq_out, hist_out, o_ref, s_ref,
                      qkv_scr, ba_scr, carry, *, lt, tiles_per_seq, chunk, group):
    i = pl.program_id(0)
    slot = i % 2
    prev = jnp.maximum(i - 1, 0)

    @pl.when(i == 0)
    def _():
        qkv_scr[1] = jnp.zeros(qkv_scr.shape[1:], F32)
        ba_scr[1] = jnp.zeros(ba_scr.shape[1:], F32)

    @pl.when(prev % tiles_per_seq == 0)
    def _():
        s_ref[...] = s0_ref[...]
        carry[:, SUBLANES - hist_ref.shape[1]:, :] = hist_ref[...]

    proj = _inproj_chunks(
        x_ref, g_ref, (w_qkv, w_gate, w_xq, w_ba, qkv_scr.at[slot], gate_out, xq_out,
                       ba_scr.at[slot], hist_out), mixer="gdn", bt=1, lt=lt)
    rule = _gdn_tile(qkv_scr.at[1 - slot], ba_scr.at[1 - slot], cw_ref, alog_ref, dtb_ref,
                     og_ref, o_ref, s_ref, carry, chunk=chunk, group=group)
    num, den = PROJ_CHUNK_SPACING
    _round_robin([rule] + [_delayed(p, k * num // den) for k, p in enumerate(proj)])


def _gdn_layer(x, g, ws, s0, hist, cw, alog_pad, dtb_pad, og, *, chunk, lt, act_dtype):
    group = 2 if 2 * chunk == LANES else 1
    bn, seq, _ = x.shape
    nl = seq // lt
    tiles = bn * nl
    width, cc = cw.shape

    def cur(i):
        return jnp.minimum(i, tiles - 1)

    def prev(i):
        return jnp.maximum(i - 1, 0)

    def tile(n, which):
        return pl.BlockSpec((1, lt, n), lambda i: (which(i), 0, 0))

    def per_seq(shape, which):
        return pl.BlockSpec((1,) + shape, lambda i: (which(i) // nl,) + (0,) * len(shape))

    def whole(a):
        return pl.BlockSpec(a.shape, lambda i: (0,) * a.ndim)

    state = (GDN_HEADS, GDN_DK, GDN_DV)
    gate, xq, hist_new, tok, s_new = pl.pallas_call(
        functools.partial(_gdn_layer_kernel, lt=lt, tiles_per_seq=nl, chunk=chunk, group=group),
        grid=(tiles + 1,),
        in_specs=[tile(D_MODEL, cur), whole(g)] + [whole(w) for w in ws] + [
            per_seq(state, prev), per_seq((width - 1, cc), prev), whole(cw),
            whole(alog_pad), whole(dtb_pad), whole(og)],
        out_specs=[
            tile(BRANCH_WIDTH, cur), tile(X_WIDTH, cur), per_seq((width - 1, cc), cur),
            tile(GDN_WIDTH, prev), per_seq(state, prev)],
        out_shape=[
            jax.ShapeDtypeStruct((tiles, lt, BRANCH_WIDTH), act_dtype),
            jax.ShapeDtypeStruct((tiles, lt, X_WIDTH), act_dtype),
            jax.ShapeDtypeStruct(hist.shape, F32),
            jax.ShapeDtypeStruct((tiles, lt, GDN_WIDTH), act_dtype),
            jax.ShapeDtypeStruct(s0.shape, F32)],
        scratch_shapes=[
            pltpu.VMEM((2, 1, lt, cc), F32),
            pltpu.VMEM((2, 1, lt, BA_PAD), F32),
            pltpu.VMEM((1, SUBLANES, cc), F32)],
        compiler_params=_cparams("arbitrary"),
        name="gdn_layer",
    )(x.reshape(tiles, lt, D_MODEL), g, *ws, s0, hist, cw, alog_pad, dtb_pad, og)
    return (gate.reshape(bn, seq, -1), xq.reshape(bn, seq, -1), tok.reshape(bn, seq, -1),
            s_new, hist_new)


ATTN_ROWS = 256
ATTN_SEQS = 8


def _attn_out_kernel(tok_ref, xq_ref, gate_ref, x_ref, kt_ref, vt_ref, w_ref, fg_ref,
                     y_ref, xo_scr, *, bt, lt, final):
    lane_head = lax.broadcasted_iota(jnp.int32, (1, X_WIDTH), 1) // X_HEAD_DIM
    scale = X_HEAD_DIM ** -0.5
    tm = bt * lt

    def attend(b, rs):
        r = rs.stop - rs.start
        q = xq_ref[b, rs, :]
        q = q * jnp.asarray(scale, q.dtype)
        qx = jnp.concatenate([jnp.where(lane_head == h, q, jnp.zeros_like(q))
                              for h in range(X_HEADS)], axis=0)
        s = _dot(qx, kt_ref[b])
        yield
        e = jnp.exp(s - jnp.max(s, axis=-1, keepdims=True))
        p = e / jnp.sum(e, axis=-1, keepdims=True)
        o4 = _dot_nt(p, vt_ref[b])
        yield
        xo = jnp.where(lane_head == 0, o4[0:r], 0.0)
        for h in range(1, X_HEADS):
            xo = xo + jnp.where(lane_head == h, o4[h * r:(h + 1) * r], 0.0)
        xo_scr[b, rs, :] = xo
        yield

    acc = {}

    def tok_proj():
        sg = _silu(gate_ref[:, :, :GDN_WIDTH].astype(F32).reshape(tm, GDN_WIDTH))
        br = (tok_ref[...].astype(F32).reshape(tm, GDN_WIDTH) * sg).astype(BF16)
        for cols in _col_chunks(D_MODEL):
            acc[cols.start] = (x_ref[:, :, cols].reshape(tm, -1)
                               + _dot(br, w_ref[:GDN_WIDTH, cols]))
            yield

    if bt == 1:
        blocks = [slice(r, r + ATTN_ROWS) for r in range(0, lt, ATTN_ROWS)]
        _round_robin([attend(0, rs) for rs in blocks] + [tok_proj()])
    else:
        def seq_group(i, carry):
            _round_robin([attend(i * ATTN_SEQS + j, slice(0, lt)) for j in range(ATTN_SEQS)])
            return carry

        lax.fori_loop(0, bt // ATTN_SEQS, seq_group, 0)
        _round_robin([tok_proj()])

    sg_x = _silu(gate_ref[:, :, GDN_WIDTH:].astype(F32).reshape(tm, X_WIDTH))
    br_x = (xo_scr[...].reshape(tm, X_WIDTH) * sg_x).astype(BF16)
    y = [acc[cols.start] + _dot(br_x, w_ref[GDN_WIDTH:, cols]) for cols in _col_chunks(D_MODEL)]
    if final:
        ms = sum(jnp.sum(c * c, axis=-1, keepdims=True) for c in y) * (1.0 / D_MODEL)
        r = lax.rsqrt(ms + EPS)
        y = [c * r * fg_ref[:, cols] for c, cols in zip(y, _col_chunks(D_MODEL))]
    for c, cols in zip(y, _col_chunks(D_MODEL)):
        y_ref[:, :, cols] = c.reshape(bt, lt, -1)


def _attn_out(tok, xq, gate, x, mk, mv, w_bf16, fg, *, layer, bt, lt, final):
    bn, seq, _ = x.shape

    def act(n):
        return pl.BlockSpec((bt, lt, n), lambda b, l: (b, l, 0))

    mem = pl.BlockSpec((None, bt, X_WIDTH, N_MEM), lambda b, l: (layer, b, 0, 0))
    return pl.pallas_call(
        functools.partial(_attn_out_kernel, bt=bt, lt=lt, final=final),
        grid=(bn // bt, seq // lt),
        in_specs=[
            act(GDN_WIDTH), act(X_WIDTH), act(BRANCH_WIDTH), act(D_MODEL), mem, mem,
            pl.BlockSpec((BRANCH_WIDTH, D_MODEL), lambda b, l: (0, 0)),
            pl.BlockSpec((1, D_MODEL), lambda b, l: (0, 0)),
        ],
        out_specs=act(D_MODEL),
        out_shape=jax.ShapeDtypeStruct(x.shape, F32),
        scratch_shapes=[pltpu.VMEM((bt, lt, X_WIDTH), F32)],
        compiler_params=_cparams("arbitrary", "arbitrary"),
        name="attn_out",
    )(tok, xq, gate, x, mk, mv, w_bf16, fg)


def _trunk(x, mem_k, mem_v, gdn_s, gdn_conv, sc_conv, p, *, bt, lt, chunk, gdn_bt, gdn_lt,
           act_dtype):
    if gdn_bt == 1 and bt == 1:
        gate, xq, tok, s_new, gconv_new = _gdn_layer(
            x, p["norm_g"][0:1], p["w_in_a"], gdn_s, gdn_conv, p["conv_w_a"], p["alog_pad"],
            p["dtb_pad"], p["o_norm_g"], chunk=chunk, lt=gdn_lt, act_dtype=act_dtype)
    else:
        qkv, gate, xq, ba, gconv_new = _inproj(
            x, p["norm_g"][0:1], p["w_in_a"], None, p["conv_w_a"], mixer="gdn", bt=bt, lt=lt,
            act_dtype=act_dtype)
        tok, s_new = _gdn(qkv, ba, gdn_s, gdn_conv, p["conv_w_a"], p["alog_pad"],
                          p["dtb_pad"], p["o_norm_g"], chunk=chunk, bt=gdn_bt, lt=gdn_lt)
    x = _attn_out(tok, xq, gate, x, mem_k, mem_v, p["w_out"][0], p["final_norm_g"],
                  layer=0, bt=bt, lt=lt, final=False)
    tok, gate, xq, sconv_new = _inproj(
        x, p["norm_g"][1:2], p["w_in_b"], sc_conv, p["conv_w_b"], mixer="sconv", bt=bt, lt=lt,
        act_dtype=act_dtype)
    y = _attn_out(tok, xq, gate, x, mem_k, mem_v, p["w_out"][1], p["final_norm_g"],
                  layer=1, bt=bt, lt=lt, final=True)
    return y, s_new[None], gconv_new[None], sconv_new[None]


def kernel(x_prompt, x_sample, mem_prompt, state_gdn, state_gdn_conv, state_sconv, cache_mem_k, cache_mem_v, norm_g, w_in_a, conv_w_a, a_log, dt_bias, o_norm_g, w_in_b, conv_w_b, mem_norm_g, w_mem_kv, w_out, final_norm_g):
    bp = x_prompt.shape[0]

    wa = jnp.transpose(w_in_a[0])
    c_b = QKV_WIDTH
    c_g = c_b + 2 * GDN_HEADS
    c_x = c_g + BRANCH_WIDTH
    wa = [wa[:c_b], wa[c_g:c_x], wa[c_x:],
          jnp.concatenate([wa[c_b:c_g], jnp.zeros((BA_PAD - 2 * GDN_HEADS, D_MODEL), wa.dtype)])]
    wa = [w.astype(BF16) for w in wa]
    pad_lo = jnp.zeros((GDN_HEADS,), F32)
    pad_hi = jnp.zeros((LANES - 2 * GDN_HEADS,), F32)
    params = {
        "norm_g": norm_g,
        "w_in_a": wa,
        "conv_w_a": conv_w_a[0],
        "alog_pad": jnp.concatenate([pad_lo, a_log[0], pad_hi])[None],
        "dtb_pad": jnp.concatenate([pad_lo, dt_bias[0], pad_hi])[None],
        "o_norm_g": o_norm_g,
        "w_in_b": [w_in_b[0].astype(BF16)],
        "conv_w_b": conv_w_b[0],
        "w_out": w_out.astype(BF16),
        "final_norm_g": final_norm_g[None],
    }

    def to_cache(t):
        t = t.reshape(t.shape[0], t.shape[1], X_HEADS, X_HEAD_DIM, t.shape[3])
        return jnp.transpose(t, (0, 1, 4, 2, 3))

    def from_cache(t):
        t = jnp.transpose(t, (0, 1, 3, 4, 2))
        return t.reshape(t.shape[0], t.shape[1], X_WIDTH, t.shape[4])

    mem_kt, mem_vt = _memkv(mem_prompt, mem_norm_g[None], w_mem_kv.astype(BF16))
    mem_k_p = to_cache(mem_kt)
    mem_v_p = to_cache(mem_vt)

    s0_p = jnp.zeros((bp,) + state_gdn.shape[2:], F32)
    gc0_p = jnp.zeros((bp,) + state_gdn_conv.shape[2:], F32)
    sc0_p = jnp.zeros((bp,) + state_sconv.shape[2:], F32)
    y_p, s_p, gc_p, sc_p = _trunk(x_prompt, mem_kt, mem_vt, s0_p, gc0_p, sc0_p,
                                  params, bt=1, lt=512, chunk=GDN_CHUNK, gdn_bt=1, gdn_lt=512,
                                  act_dtype=BF16)
    dec_seq = x_sample.shape[1]
    y_s, s_s, gc_s, sc_s = _trunk(x_sample, from_cache(cache_mem_k), from_cache(cache_mem_v),
                                  state_gdn[0], state_gdn_conv[0], state_sconv[0], params,
                                  bt=32, lt=dec_seq, chunk=dec_seq, gdn_bt=16, gdn_lt=dec_seq,
                                  act_dtype=F32)
    return (y_p, y_s, s_p, gc_p, sc_p, mem_k_p, mem_v_p, s_s, gc_s, sc_s)
```

```python
import functools

import jax
import jax.numpy as jnp
from jax import lax
from jax.experimental import pallas as pl
from jax.experimental.pallas import tpu as pltpu

F32 = jnp.float32
BF16 = jnp.bfloat16

D_MODEL = 1024
N_MEM = 256
X_WIDTH = 256
X_HEADS = 4
X_HEAD_DIM = 64
GDN_HEADS = 6
GDN_DK = 128
GDN_DV = 128
GDN_WIDTH = GDN_HEADS * GDN_DV
QK_WIDTH = GDN_HEADS * GDN_DK
QKV_WIDTH = 2 * QK_WIDTH + GDN_WIDTH
GDN_CHUNK = 64
SC_WIDTH = 768
BRANCH_WIDTH = 1024
EPS = 1e-6

LANES = 128
SUBLANES = 8
COL_CHUNK = 256
BA_PAD = LANES
VMEM_LIMIT = 56 * 1024 * 1024

HIGHEST = lax.Precision.HIGHEST


def _cparams(*sem):
    return pltpu.CompilerParams(dimension_semantics=sem, vmem_limit_bytes=VMEM_LIMIT)


def _rms_rows(x, g):
    r = lax.rsqrt(jnp.mean(x * x, axis=-1, keepdims=True) + EPS)
    return x * r * g


def _silu(x):
    h = 0.5 * x
    return h + h * jnp.tanh(h)


def _dot(a, b):
    return jnp.dot(a.astype(BF16), b.astype(BF16), preferred_element_type=F32)


def _dot_nt(a, b):
    return lax.dot_general(a.astype(BF16), b.astype(BF16), (((1,), (1,)), ((), ())),
                           preferred_element_type=F32)


def _dot_tn(a, b):
    return lax.dot_general(a.astype(BF16), b.astype(BF16), (((0,), (0,)), ((), ())),
                           preferred_element_type=F32)


def _dot_f32(a, b):
    return jnp.dot(a, b, preferred_element_type=F32, precision=HIGHEST)


def _col_chunks(n):
    return [slice(c, min(c + COL_CHUNK, n)) for c in range(0, n, COL_CHUNK)]


def _interleaved(gens):
    gens = list(gens)
    while gens:
        alive = []
        for g in gens:
            try:
                next(g)
                alive.append(g)
            except StopIteration:
                pass
        gens = alive
        yield


def _round_robin(gens):
    for _ in _interleaved(gens):
        pass


def _delayed(gen, rounds):
    for _ in range(rounds):
        yield
    yield from gen


def _memkv_kernel(m_ref, g_ref, w_ref, kt_ref, vt_ref):
    h = _rms_rows(m_ref[...], g_ref[...]).astype(BF16)
    for layer in range(w_ref.shape[0]):
        kv = jnp.dot(h, w_ref[layer], preferred_element_type=F32)
        kt_ref[layer] = kv[:, :X_WIDTH].T
        vt_ref[layer] = kv[:, X_WIDTH:].T


def _memkv(mem, g, w_bf16):
    bn, n_mem, _ = mem.shape
    depth = w_bf16.shape[0]
    out = jax.ShapeDtypeStruct((depth, bn, X_WIDTH, n_mem), F32)
    out_spec = pl.BlockSpec((depth, None, X_WIDTH, n_mem), lambda b: (0, b, 0, 0))
    return pl.pallas_call(
        _memkv_kernel,
        grid=(bn,),
        in_specs=[
            pl.BlockSpec((None, n_mem, D_MODEL), lambda b: (b, 0, 0)),
            pl.BlockSpec((1, D_MODEL), lambda b: (0, 0)),
            pl.BlockSpec(w_bf16.shape, lambda b: (0, 0, 0)),
        ],
        out_specs=[out_spec, out_spec],
        out_shape=[out, out],
        compiler_params=_cparams("arbitrary"),
        name="memkv",
    )(mem, g, w_bf16)


def _init_conv_carry(carry, hist_ref):
    @pl.when(pl.program_id(1) == 0)
    def _():
        carry[:, SUBLANES - hist_ref.shape[1]:, :] = hist_ref[...]


def _causal_conv_cols(x, cols, carry, hist_out, cw_ref, seqs=slice(None)):
    width = cw_ref.shape[0]
    bt, lt, n = x.shape
    g = lt // SUBLANES
    xe = jnp.concatenate([carry[seqs, :, cols], x], axis=1).reshape(bt, g + 1, SUBLANES, n)
    sub = lax.broadcasted_iota(jnp.int32, (1, 1, SUBLANES, n), 2)
    y = x.reshape(bt, g, SUBLANES, n) * cw_ref[width - 1:width, cols]
    for j in range(width - 1):
        s = width - 1 - j
        r = pltpu.roll(xe, s, axis=2)
        y = y + jnp.where(sub < s, r[:, :g], r[:, 1:]) * cw_ref[j:j + 1, cols]
    carry[seqs, :, cols] = x[:, lt - SUBLANES:, :]
    if hist_out is not None:
        hist_out[seqs, :, cols] = x[:, lt - (width - 1):, :]
    return y.reshape(bt, lt, n)


def _inproj_chunks(x_ref, g_ref, rest, *, mixer, bt, lt):
    tm = bt * lt
    h = _rms_rows(x_ref[...].reshape(tm, D_MODEL), g_ref[...]).astype(BF16)
    if mixer == "gdn":
        n_out = (len(rest) - 1) // 2
        w_refs, (mix_out, *plain_outs, hist_out) = rest[:n_out], rest[n_out:]

        def proj(i, cols):
            return _dot_nt(h, w_refs[i][cols, :]).reshape(bt, lt, -1)
    else:
        w_ref, hist_ref, cw_ref, mix_out, *plain_outs, hist_out, carry = rest
        _init_conv_carry(carry, hist_ref)
        bases = [0, 3 * SC_WIDTH]
        for out in plain_outs[:-1]:
            bases.append(bases[-1] + out.shape[-1])

        def proj(i, cols, offset=0):
            cols = slice(bases[i] + offset + cols.start, bases[i] + offset + cols.stop)
            return jnp.dot(h, w_ref[:, cols], preferred_element_type=F32).reshape(bt, lt, -1)

    def mixer_chunk(cols):
        if mixer == "gdn":
            y = proj(0, cols)
            hist_out[:, :, cols] = y[:, lt - hist_out.shape[1]:, :]
        else:
            gate_b = proj(0, cols)
            pre = proj(0, cols, SC_WIDTH) * proj(0, cols, 2 * SC_WIDTH)
            yield
            y = gate_b * _causal_conv_cols(pre, cols, carry, hist_out, cw_ref)
        mix_out[:, :, cols] = y.astype(mix_out.dtype)
        yield

    def plain_chunk(i, out, cols):
        out[:, :, cols] = proj(i, cols).astype(out.dtype)
        yield

    gens = [mixer_chunk(cols) for cols in _col_chunks(mix_out.shape[-1])]
    for i, out in enumerate(plain_outs):
        gens += [plain_chunk(i + 1, out, cols) for cols in _col_chunks(out.shape[-1])]
    return gens


def _inproj_kernel(x_ref, g_ref, *rest, mixer, bt, lt):
    gens = _inproj_chunks(x_ref, g_ref, rest, mixer=mixer, bt=bt, lt=lt)
    _round_robin([_delayed(g, i) for i, g in enumerate(gens)])


def _inproj(x, g, ws, hist, cw, *, mixer, bt, lt, act_dtype):
    bn, seq, _ = x.shape
    width, cc = cw.shape

    def act(n, dtype=act_dtype):
        return (pl.BlockSpec((bt, lt, n), lambda b, l: (b, l, 0)),
                jax.ShapeDtypeStruct((bn, seq, n), dtype))

    hist_spec = pl.BlockSpec((bt, width - 1, cc), lambda b, l: (b, 0, 0))
    args = [x, g, *ws]
    in_specs = [
        pl.BlockSpec((bt, lt, D_MODEL), lambda b, l: (b, l, 0)),
        pl.BlockSpec((1, D_MODEL), lambda b, l: (0, 0)),
    ] + [pl.BlockSpec(w.shape, lambda b, l: (0, 0)) for w in ws]
    outs = [act(cc), act(BRANCH_WIDTH), act(X_WIDTH)]
    scratch = []
    if mixer == "gdn":
        outs.append(act(BA_PAD, F32))
    else:
        args += [hist, cw]
        in_specs += [hist_spec, pl.BlockSpec((width, cc), lambda b, l: (0, 0))]
        scratch.append(pltpu.VMEM((bt, SUBLANES, cc), F32))
    outs.append((hist_spec, jax.ShapeDtypeStruct((bn, width - 1, cc), F32)))
    return pl.pallas_call(
        functools.partial(_inproj_kernel, mixer=mixer, bt=bt, lt=lt),
        grid=(bn // bt, seq // lt),
        in_specs=in_specs,
        out_specs=[o[0] for o in outs],
        out_shape=[o[1] for o in outs],
        scratch_shapes=scratch,
        compiler_params=_cparams("arbitrary", "arbitrary"),
        name="inproj_" + mixer,
    )(*args)


CHUNKS_PER_WAVE = 4
SEQS_PER_WAVE = 8


def _gdn_tile(qkv_ref, ba_ref, cw_ref, alog_ref, dtb_ref, og_ref, o_ref, s_ref, carry,
              *, chunk, group):
    bt, lt, _ = qkv_ref.shape
    nchunks = lt // chunk
    heads = range(GDN_HEADS)
    groups = [tuple(range(g, g + group)) for g in range(0, GDN_HEADS, group)]
    gw = group * chunk
    row = lax.broadcasted_iota(jnp.int32, (chunk, gw), 0)
    col = lax.broadcasted_iota(jnp.int32, (chunk, gw), 1) % chunk
    tril = row >= col
    strict = row > col
    lane_blk = lax.broadcasted_iota(jnp.int32, (1, gw), 1) // chunk
    ones_tril = (lax.broadcasted_iota(jnp.int32, (chunk, chunk), 0)
                 >= lax.broadcasted_iota(jnp.int32, (chunk, chunk), 1)).astype(F32)
    neg_a = -jnp.exp(alog_ref[...])
    dtb = dtb_ref[...]
    og = og_ref[...]

    def rows(c):
        return slice(c * chunk, (c + 1) * chunk)

    def lanes(base, h, n):
        return slice(base + h * n, base + (h + 1) * n)

    def cat(xs, axis):
        return xs[0] if len(xs) == 1 else jnp.concatenate(xs, axis=axis)

    def pick(xs):
        out = xs[0]
        for j in range(1, len(xs)):
            out = jnp.where(lane_blk == j, xs[j], out)
        return out

    def blockdiag(xs):
        z = jnp.zeros_like(xs[0])
        return cat([cat([x if i == j else z for i in range(len(xs))], 1)
                    for j, x in enumerate(xs)], 0)

    def blockdiag_of_lane_blocks(x):
        if group == 1:
            return x
        return cat([jnp.where(lane_blk == j, x, 0.0) for j in range(group)], 0)

    def conv_silu(u, cols):
        b, c = u
        x = qkv_ref[b, rows(c), cols].astype(F32)
        y = _causal_conv_cols(x[None], cols, carry, None, cw_ref, seqs=slice(b, b + 1))
        return _silu(y[0])

    def phase1(units, res):
        uh = [(u, h) for u in units for h in heads]
        ug = [(u, g) for u in units for g in range(len(groups))]
        gcum, gcum_t, beta_all = {}, {}, {}
        for u in units:
            ba = ba_ref[u[0], rows(u[1]), :]
            beta_all[u] = 1.0 / (1.0 + jnp.exp(-ba))
            z = ba + dtb
            softplus = jnp.maximum(z, 0.0) + jnp.log1p(jnp.exp(-jnp.abs(z)))
            gcum[u] = _dot_f32(ones_tril, neg_a * softplus)
        q, k, v = {}, {}, {}
        for u in units:
            for h in heads:
                q[u, h] = conv_silu(u, lanes(0, h, GDN_DK))
                k[u, h] = conv_silu(u, lanes(QK_WIDTH, h, GDN_DK))
                v[u, h] = conv_silu(u, lanes(2 * QK_WIDTH, h, GDN_DV))
            yield
        for key in uh:
            x = q[key]
            q[key] = x * (lax.rsqrt(jnp.sum(x * x, axis=-1, keepdims=True) + EPS)
                          * (GDN_DK ** -0.5))
            x = k[key]
            k[key] = x * lax.rsqrt(jnp.sum(x * x, axis=-1, keepdims=True) + EPS)
        yield
        for u in units:
            gcum_t[u] = cat([gcum[u]] * group, 0).T
        gc = {(u, h): jnp.broadcast_to(gcum[u][:, GDN_HEADS + h:GDN_HEADS + h + 1],
                                       (chunk, GDN_DK)) for u, h in uh}
        gl = {(u, h): gcum[u][chunk - 1:chunk, GDN_HEADS + h:GDN_HEADS + h + 1]
              for u, h in uh}
        beta = {(u, h): jnp.broadcast_to(beta_all[u][:, h:h + 1], (chunk, GDN_DK))
                for u, h in uh}
        eg = {key: jnp.exp(gc[key]) for key in uh}
        kb = {key: k[key] * beta[key] for key in uh}
        kkqk = {}
        for u, g in ug:
            grp = groups[g]
            kkqk[u, g] = _dot_nt(
                cat([cat([kb[u, h] for h in grp], 1), cat([q[u, h] for h in grp], 1)], 0),
                blockdiag([k[u, h] for h in grp]))
        yield
        n, p, aqk = {}, {}, {}
        for u, g in ug:
            grp = groups[g]
            g_col = pick([gc[u, h][:, :gw] for h in grp])
            g_row = pick([gcum_t[u][GDN_HEADS + h:GDN_HEADS + h + 1, :] for h in grp])
            decay = jnp.exp(jnp.where(tril, g_col - g_row, -jnp.inf))
            a = jnp.where(strict, kkqk[u, g][:chunk] * decay, 0.0)
            aqk[u, g] = (kkqk[u, g][chunk:] * decay).astype(BF16)
            n[u, g] = -a
            p[u, g] = _dot(a, blockdiag_of_lane_blocks(a))
        yield
        span = 2
        while 2 * span < chunk:
            for key in ug:
                both = _dot(cat([n[key], p[key]], 0), blockdiag_of_lane_blocks(p[key]))
                n[key] = n[key] + p[key] + both[:chunk]
                p[key] = both[chunk:]
            span *= 2
            yield
        for key in ug:
            n[key] = n[key] + p[key] + _dot(n[key], blockdiag_of_lane_blocks(p[key]))
        yield
        wu = {}
        for u, h in uh:
            rhs = jnp.concatenate([kb[u, h] * eg[u, h], v[u, h] * beta[u, h]], axis=-1)
            wu[u, h] = rhs + _dot(n[u, h // group][:, lanes(0, h % group, chunk)], rhs)
        yield
        for u in units:
            res[u] = dict(
                w_qd=[jnp.concatenate([wu[u, h][:, :GDN_DK], q[u, h] * eg[u, h]],
                                      axis=0).astype(BF16) for h in heads],
                u=[wu[u, h][:, GDN_DK:] for h in heads],
                kd=[(k[u, h] * jnp.exp(gl[u, h] - gc[u, h])).astype(BF16) for h in heads],
                aqk=[aqk[u, g] for g in range(len(groups))],
                egl=[jnp.exp(gl[u, h]) for h in heads])
        yield

    def phase2(units, res):
        for c in sorted({c for _, c in units}):
            us = [u for u in units if u[1] == c]
            uh = [(u, h) for u in us for h in heads]
            s = {(u, h): s_ref[u[0], h] for u, h in uh}
            ws_qs = {(u, h): _dot(res[u]["w_qd"][h], s[u, h]) for u, h in uh}
            yield
            v_new = {(u, h): res[u]["u"][h] - ws_qs[u, h][:chunk] for u, h in uh}
            o_grp = {(u, g): _dot(res[u]["aqk"][g], blockdiag([v_new[u, h] for h in grp]))
                     for u in us for g, grp in enumerate(groups)}
            for u, h in uh:
                s_ref[u[0], h] = (s[u, h] * res[u]["egl"][h]
                                  + _dot_tn(res[u]["kd"][h], v_new[u, h]))
            yield
            for u, h in uh:
                oh = ws_qs[u, h][chunk:] + o_grp[u, h // group][:, lanes(0, h % group, GDN_DV)]
                oh = oh * lax.rsqrt(jnp.mean(oh * oh, axis=-1, keepdims=True) + EPS) * og
                o_ref[u[0], rows(c), lanes(0, h, GDN_DV)] = oh.astype(o_ref.dtype)
            yield

    units = [(b, c) for b in range(bt) for c in range(nchunks)]
    per_wave = CHUNKS_PER_WAVE if nchunks > 1 else SEQS_PER_WAVE
    waves = [units[i:i + per_wave] for i in range(0, len(units), per_wave)]
    res = {}
    prev = None
    for wave in waves:
        gens = [phase1(wave, res)]
        if prev is not None:
            gens.append(phase2(prev, res))
        yield from _interleaved(gens)
        prev = wave
    yield from phase2(prev, res)


def _gdn_kernel(qkv_ref, ba_ref, s0_ref, hist_ref, cw_ref, alog_ref, dtb_ref, og_ref,
                o_ref, s_ref, carry, *, chunk, group):
    @pl.when(pl.program_id(1) == 0)
    def _():
        s_ref[...] = s0_ref[...]

    _init_conv_carry(carry, hist_ref)
    _round_robin([_gdn_tile(qkv_ref, ba_ref, cw_ref, alog_ref, dtb_ref, og_ref, o_ref, s_ref,
                            carry, chunk=chunk, group=group)])


def _gdn(qkv, ba, s0, hist, cw, alog_pad, dtb_pad, og, *, chunk, bt, lt):
    group = 2 if 2 * chunk == LANES else 1
    bn, seq, _ = qkv.shape
    width, cc = cw.shape
    s_spec = pl.BlockSpec((bt, GDN_HEADS, GDN_DK, GDN_DV), lambda b, l: (b, 0, 0, 0))
    hist_spec = pl.BlockSpec((bt, width - 1, cc), lambda b, l: (b, 0, 0))
    vec = pl.BlockSpec((1, LANES), lambda b, l: (0, 0))
    return pl.pallas_call(
        functools.partial(_gdn_kernel, chunk=chunk, group=group),
        grid=(bn // bt, seq // lt),
        in_specs=[
            pl.BlockSpec((bt, lt, QKV_WIDTH), lambda b, l: (b, l, 0)),
            pl.BlockSpec((bt, lt, BA_PAD), lambda b, l: (b, l, 0)),
            s_spec, hist_spec,
            pl.BlockSpec((width, cc), lambda b, l: (0, 0)),
            vec, vec, vec,
        ],
        out_specs=[
            pl.BlockSpec((bt, lt, GDN_WIDTH), lambda b, l: (b, l, 0)),
            s_spec,
        ],
        out_shape=[
            jax.ShapeDtypeStruct((bn, seq, GDN_WIDTH), qkv.dtype),
            jax.ShapeDtypeStruct(s0.shape, F32),
        ],
        scratch_shapes=[pltpu.VMEM((bt, SUBLANES, cc), F32)],
        compiler_params=_cparams("arbitrary", "arbitrary"),
        name="gdn",
    )(qkv, ba, s0, hist, cw, alog_pad, dtb_pad, og)


PROJ_CHUNK_SPACING = (5, 2)


def _gdn_layer_kernel(x_ref, g_ref, w_qkv, w_gate, w_xq, w_ba, s0_ref, hist_ref, cw_ref,
                      alog_ref, dtb_ref, og_ref, gate_out, xq_out, hist_out, o_ref, s_ref,
                      qkv_scr, ba_scr, carry, *, lt, tiles_per_seq, chunk, group):
    i = pl.program_id(0)
    slot = i % 2
    prev = jnp.maximum(i - 1, 0)

    @pl.when(i == 0)
    def _():
        qkv_scr[1] = jnp.zeros(qkv_scr.shape[1:], F32)
        ba_scr[1] = jnp.zeros(ba_scr.shape[1:], F32)

    @pl.when(prev % tiles_per_seq == 0)
    def _():
        s_ref[...] = s0_ref[...]
        carry[:, SUBLANES - hist_ref.shape[1]:, :] = hist_ref[...]

    proj = _inproj_chunks(
        x_ref, g_ref, (w_qkv, w_gate, w_xq, w_ba, qkv_scr.at[slot], gate_out, xq_out,
                       ba_scr.at[slot], hist_out), mixer="gdn", bt=1, lt=lt)
    rule = _gdn_tile(qkv_scr.at[1 - slot], ba_scr.at[1 - slot], cw_ref, alog_ref, dtb_ref,
                     og_ref, o_ref, s_ref, carry, chunk=chunk, group=group)
    num, den = PROJ_CHUNK_SPACING
    _round_robin([rule] + [_delayed(p, k * num // den) for k, p in enumerate(proj)])


def _gdn_layer(x, g, ws, s0, hist, cw, alog_pad, dtb_pad, og, *, chunk, lt, act_dtype):
    group = 2 if 2 * chunk == LANES else 1
    bn, seq, _ = x.shape
    nl = seq // lt
    tiles = bn * nl
    width, cc = cw.shape

    def cur(i):
        return jnp.minimum(i, tiles - 1)

    def prev(i):
        return jnp.maximum(i - 1, 0)

    def tile(n, which):
        return pl.BlockSpec((1, lt, n), lambda i: (which(i), 0, 0))

    def per_seq(shape, which):
        return pl.BlockSpec((1,) + shape, lambda i: (which(i) // nl,) + (0,) * len(shape))

    def whole(a):
        return pl.BlockSpec(a.shape, lambda i: (0,) * a.ndim)

    state = (GDN_HEADS, GDN_DK, GDN_DV)
    gate, xq, hist_new, tok, s_new = pl.pallas_call(
        functools.partial(_gdn_layer_kernel, lt=lt, tiles_per_seq=nl, chunk=chunk, group=group),
        grid=(tiles + 1,),
        in_specs=[tile(D_MODEL, cur), whole(g)] + [whole(w) for w in ws] + [
            per_seq(state, prev), per_seq((width - 1, cc), prev), whole(cw),
            whole(alog_pad), whole(dtb_pad), whole(og)],
        out_specs=[
            tile(BRANCH_WIDTH, cur), tile(X_WIDTH, cur), per_seq((width - 1, cc), cur),
            tile(GDN_WIDTH, prev), per_seq(state, prev)],
        out_shape=[
            jax.ShapeDtypeStruct((tiles, lt, BRANCH_WIDTH), act_dtype),
            jax.ShapeDtypeStruct((tiles, lt, X_WIDTH), act_dtype),
            jax.ShapeDtypeStruct(hist.shape, F32),
            jax.ShapeDtypeStruct((tiles, lt, GDN_WIDTH), act_dtype),
            jax.ShapeDtypeStruct(s0.shape, F32)],
        scratch_shapes=[
            pltpu.VMEM((2, 1, lt, cc), F32),
            pltpu.VMEM((2, 1, lt, BA_PAD), F32),
            pltpu.VMEM((1, SUBLANES, cc), F32)],
        compiler_params=_cparams("arbitrary"),
        name="gdn_layer",
    )(x.reshape(tiles, lt, D_MODEL), g, *ws, s0, hist, cw, alog_pad, dtb_pad, og)
    return (gate.reshape(bn, seq, -1), xq.reshape(bn, seq, -1), tok.reshape(bn, seq, -1),
            s_new, hist_new)


ATTN_ROWS = 256
ATTN_SEQS = 8


def _attn_out_kernel(tok_ref, xq_ref, gate_ref, x_ref, kt_ref, vt_ref, w_ref, fg_ref,
                     y_ref, xo_scr, *, bt, lt, final):
    lane_head = lax.broadcasted_iota(jnp.int32, (1, X_WIDTH), 1) // X_HEAD_DIM
    scale = X_HEAD_DIM ** -0.5
    tm = bt * lt

    def attend(b, rs):
        r = rs.stop - rs.start
        q = xq_ref[b, rs, :]
        q = q * jnp.asarray(scale, q.dtype)
        qx = jnp.concatenate([jnp.where(lane_head == h, q, jnp.zeros_like(q))
                              for h in range(X_HEADS)], axis=0)
        s = _dot(qx, kt_ref[b])
        yield
        e = jnp.exp(s - jnp.max(s, axis=-1, keepdims=True))
        p = e / jnp.sum(e, axis=-1, keepdims=True)
        o4 = _dot_nt(p, vt_ref[b])
        yield
        xo = jnp.where(lane_head == 0, o4[0:r], 0.0)
        for h in range(1, X_HEADS):
            xo = xo + jnp.where(lane_head == h, o4[h * r:(h + 1) * r], 0.0)
        xo_scr[b, rs, :] = xo
        yield

    acc = {}

    def tok_proj():
        sg = _silu(gate_ref[:, :, :GDN_WIDTH].astype(F32).reshape(tm, GDN_WIDTH))
        br = (tok_ref[...].astype(F32).reshape(tm, GDN_WIDTH) * sg).astype(BF16)
        for cols in _col_chunks(D_MODEL):
            acc[cols.start] = (x_ref[:, :, cols].reshape(tm, -1)
                               + _dot(br, w_ref[:GDN_WIDTH, cols]))
            yield

    if bt == 1:
        blocks = [slice(r, r + ATTN_ROWS) for r in range(0, lt, ATTN_ROWS)]
        _round_robin([attend(0, rs) for rs in blocks] + [tok_proj()])
    else:
        def seq_group(i, carry):
            _round_robin([attend(i * ATTN_SEQS + j, slice(0, lt)) for j in range(ATTN_SEQS)])
            return carry

        lax.fori_loop(0, bt // ATTN_SEQS, seq_group, 0)
        _round_robin([tok_proj()])

    sg_x = _silu(gate_ref[:, :, GDN_WIDTH:].astype(F32).reshape(tm, X_WIDTH))
    br_x = (xo_scr[...].reshape(tm, X_WIDTH) * sg_x).astype(BF16)
    y = [acc[cols.start] + _dot(br_x, w_ref[GDN_WIDTH:, cols]) for cols in _col_chunks(D_MODEL)]
    if final:
        ms = sum(jnp.sum(c * c, axis=-1, keepdims=True) for c in y) * (1.0 / D_MODEL)
        r = lax.rsqrt(ms + EPS)
        y = [c * r * fg_ref[:, cols] for c, cols in zip(y, _col_chunks(D_MODEL))]
    for c, cols in zip(y, _col_chunks(D_MODEL)):
        y_ref[:, :, cols] = c.reshape(bt, lt, -1)


def _attn_out(tok, xq, gate, x, mk, mv, w_bf16, fg, *, layer, bt, lt, final):
    bn, seq, _ = x.shape

    def act(n):
        return pl.BlockSpec((bt, lt, n), lambda b, l: (b, l, 0))

    mem = pl.BlockSpec((None, bt, X_WIDTH, N_MEM), lambda b, l: (layer, b, 0, 0))
    return pl.pallas_call(
        functools.partial(_attn_out_kernel, bt=bt, lt=lt, final=final),
        grid=(bn // bt, seq // lt),
        in_specs=[
            act(GDN_WIDTH), act(X_WIDTH), act(BRANCH_WIDTH), act(D_MODEL), mem, mem,
            pl.BlockSpec((BRANCH_WIDTH, D_MODEL), lambda b, l: (0, 0)),
            pl.BlockSpec((1, D_MODEL), lambda b, l: (0, 0)),
        ],
        out_specs=act(D_MODEL),
        out_shape=jax.ShapeDtypeStruct(x.shape, F32),
        scratch_shapes=[pltpu.VMEM((bt, lt, X_WIDTH), F32)],
        compiler_params=_cparams("arbitrary", "arbitrary"),
        name="attn_out",
    )(tok, xq, gate, x, mk, mv, w_bf16, fg)


def _trunk(x, mem_k, mem_v, gdn_s, gdn_conv, sc_conv, p, *, bt, lt, chunk, gdn_bt, gdn_lt,
           attn_bt, attn_lt, act_dtype):
    if gdn_bt == 1 and bt == 1:
        gate, xq, tok, s_new, gconv_new = _gdn_layer(
            x, p["norm_g"][0:1], p["w_in_a"], gdn_s, gdn_conv, p["conv_w_a"], p["alog_pad"],
            p["dtb_pad"], p["o_norm_g"], chunk=chunk, lt=gdn_lt, act_dtype=act_dtype)
    else:
        qkv, gate, xq, ba, gconv_new = _inproj(
            x, p["norm_g"][0:1], p["w_in_a"], None, p["conv_w_a"], mixer="gdn", bt=bt, lt=lt,
            act_dtype=act_dtype)
        tok, s_new = _gdn(qkv, ba, gdn_s, gdn_conv, p["conv_w_a"], p["alog_pad"],
                          p["dtb_pad"], p["o_norm_g"], chunk=chunk, bt=gdn_bt, lt=gdn_lt)
    x = _attn_out(tok, xq, gate, x, mem_k, mem_v, p["w_out"][0], p["final_norm_g"],
                  layer=0, bt=attn_bt, lt=attn_lt, final=False)
    tok, gate, xq, sconv_new = _inproj(
        x, p["norm_g"][1:2], p["w_in_b"], sc_conv, p["conv_w_b"], mixer="sconv", bt=bt, lt=lt,
        act_dtype=act_dtype)
    y = _attn_out(tok, xq, gate, x, mem_k, mem_v, p["w_out"][1], p["final_norm_g"],
                  layer=1, bt=attn_bt, lt=attn_lt, final=True)
    return y, s_new[None], gconv_new[None], sconv_new[None]


def kernel(x_prompt, x_sample, mem_prompt, state_gdn, state_gdn_conv, state_sconv, cache_mem_k, cache_mem_v, norm_g, w_in_a, conv_w_a, a_log, dt_bias, o_norm_g, w_in_b, conv_w_b, mem_norm_g, w_mem_kv, w_out, final_norm_g):
    bp = x_prompt.shape[0]

    wa = jnp.transpose(w_in_a[0])
    c_b = QKV_WIDTH
    c_g = c_b + 2 * GDN_HEADS
    c_x = c_g + BRANCH_WIDTH
    wa = [wa[:c_b], wa[c_g:c_x], wa[c_x:],
          jnp.concatenate([wa[c_b:c_g], jnp.zeros((BA_PAD - 2 * GDN_HEADS, D_MODEL), wa.dtype)])]
    wa = [w.astype(BF16) for w in wa]
    pad_lo = jnp.zeros((GDN_HEADS,), F32)
    pad_hi = jnp.zeros((LANES - 2 * GDN_HEADS,), F32)
    params = {
        "norm_g": norm_g,
        "w_in_a": wa,
        "conv_w_a": conv_w_a[0],
        "alog_pad": jnp.concatenate([pad_lo, a_log[0], pad_hi])[None],
        "dtb_pad": jnp.concatenate([pad_lo, dt_bias[0], pad_hi])[None],
        "o_norm_g": o_norm_g,
        "w_in_b": [w_in_b[0].astype(BF16)],
        "conv_w_b": conv_w_b[0],
        "w_out": w_out.astype(BF16),
        "final_norm_g": final_norm_g[None],
    }

    def to_cache(t):
        t = t.reshape(t.shape[0], t.shape[1], X_HEADS, X_HEAD_DIM, t.shape[3])
        return jnp.transpose(t, (0, 1, 4, 2, 3))

    def from_cache(t):
        t = jnp.transpose(t, (0, 1, 3, 4, 2))
        return t.reshape(t.shape[0], t.shape[1], X_WIDTH, t.shape[4])

    mem_kt, mem_vt = _memkv(mem_prompt, mem_norm_g[None], w_mem_kv.astype(BF16))
    mem_k_p = to_cache(mem_kt)
    mem_v_p = to_cache(mem_vt)

    s0_p = jnp.zeros((bp,) + state_gdn.shape[2:], F32)
    gc0_p = jnp.zeros((bp,) + state_gdn_conv.shape[2:], F32)
    sc0_p = jnp.zeros((bp,) + state_sconv.shape[2:], F32)
    y_p, s_p, gc_p, sc_p = _trunk(x_prompt, mem_kt, mem_vt, s0_p, gc0_p, sc0_p,
                                  params, bt=1, lt=1024, chunk=GDN_CHUNK, gdn_bt=1, gdn_lt=512,
                                  attn_bt=1, attn_lt=512, act_dtype=BF16)
    dec_seq = x_sample.shape[1]
    y_s, s_s, gc_s, sc_s = _trunk(x_sample, from_cache(cache_mem_k), from_cache(cache_mem_v),
                                  state_gdn[0], state_gdn_conv[0], state_sconv[0], params,
                                  bt=32, lt=dec_seq, chunk=dec_seq, gdn_bt=16, gdn_lt=dec_seq,
                                  attn_bt=16, attn_lt=dec_seq, act_dtype=F32)
    return (y_p, y_s, s_p, gc_p, sc_p, mem_k_p, mem_v_p, s_s, gc_s, sc_s)
```

```python
import functools

import jax
import jax.numpy as jnp
from jax import lax
from jax.experimental import pallas as pl
from jax.experimental.pallas import tpu as pltpu

F32 = jnp.float32
BF16 = jnp.bfloat16

D_MODEL = 1024
N_MEM = 256
X_WIDTH = 256
X_HEADS = 4
X_HEAD_DIM = 64
GDN_HEADS = 6
GDN_DK = 128
GDN_DV = 128
GDN_WIDTH = GDN_HEADS * GDN_DV
QK_WIDTH = GDN_HEADS * GDN_DK
QKV_WIDTH = 2 * QK_WIDTH + GDN_WIDTH
GDN_CHUNK = 64
SC_WIDTH = 768
BRANCH_WIDTH = 1024
EPS = 1e-6

LANES = 128
SUBLANES = 8
COL_CHUNK = 256
BA_PAD = LANES
VMEM_LIMIT = 56 * 1024 * 1024

HIGHEST = lax.Precision.HIGHEST


def _cparams(*sem):
    return pltpu.CompilerParams(dimension_semantics=sem, vmem_limit_bytes=VMEM_LIMIT)


def _rms_rows(x, g):
    r = lax.rsqrt(jnp.mean(x * x, axis=-1, keepdims=True) + EPS)
    return x * r * g


def _silu(x):
    h = 0.5 * x
    return h + h * jnp.tanh(h)


def _dot(a, b):
    return jnp.dot(a.astype(BF16), b.astype(BF16), preferred_element_type=F32)


def _dot_nt(a, b):
    return lax.dot_general(a.astype(BF16), b.astype(BF16), (((1,), (1,)), ((), ())),
                           preferred_element_type=F32)


def _dot_tn(a, b):
    return lax.dot_general(a.astype(BF16), b.astype(BF16), (((0,), (0,)), ((), ())),
                           preferred_element_type=F32)


def _dot_f32(a, b):
    return jnp.dot(a, b, preferred_element_type=F32, precision=HIGHEST)


def _col_chunks(n):
    return [slice(c, min(c + COL_CHUNK, n)) for c in range(0, n, COL_CHUNK)]


def _interleaved(gens):
    gens = list(gens)
    while gens:
        alive = []
        for g in gens:
            try:
                next(g)
                alive.append(g)
            except StopIteration:
                pass
        gens = alive
        yield


def _round_robin(gens):
    for _ in _interleaved(gens):
        pass


def _delayed(gen, rounds):
    for _ in range(rounds):
        yield
    yield from gen


def _memkv_kernel(m_ref, g_ref, w_ref, kt_ref, vt_ref):
    h = _rms_rows(m_ref[...], g_ref[...]).astype(BF16)
    for layer in range(w_ref.shape[0]):
        kv = jnp.dot(h, w_ref[layer], preferred_element_type=F32)
        kt_ref[layer] = kv[:, :X_WIDTH].T
        vt_ref[layer] = kv[:, X_WIDTH:].T


def _memkv(mem, g, w_bf16):
    bn, n_mem, _ = mem.shape
    depth = w_bf16.shape[0]
    out = jax.ShapeDtypeStruct((depth, bn, X_WIDTH, n_mem), F32)
    out_spec = pl.BlockSpec((depth, None, X_WIDTH, n_mem), lambda b: (0, b, 0, 0))
    return pl.pallas_call(
        _memkv_kernel,
        grid=(bn,),
        in_specs=[
            pl.BlockSpec((None, n_mem, D_MODEL), lambda b: (b, 0, 0)),
            pl.BlockSpec((1, D_MODEL), lambda b: (0, 0)),
            pl.BlockSpec(w_bf16.shape, lambda b: (0, 0, 0)),
        ],
        out_specs=[out_spec, out_spec],
        out_shape=[out, out],
        compiler_params=_cparams("arbitrary"),
        name="memkv",
    )(mem, g, w_bf16)


def _init_conv_carry(carry, hist_ref):
    @pl.when(pl.program_id(1) == 0)
    def _():
        carry[:, SUBLANES - hist_ref.shape[1]:, :] = hist_ref[...]


def _causal_conv_cols(x, cols, carry, hist_out, cw_ref, seqs=slice(None)):
    width = cw_ref.shape[0]
    bt, lt, n = x.shape
    g = lt // SUBLANES
    xe = jnp.concatenate([carry[seqs, :, cols], x], axis=1).reshape(bt, g + 1, SUBLANES, n)
    sub = lax.broadcasted_iota(jnp.int32, (1, 1, SUBLANES, n), 2)
    y = x.reshape(bt, g, SUBLANES, n) * cw_ref[width - 1:width, cols]
    for j in range(width - 1):
        s = width - 1 - j
        r = pltpu.roll(xe, s, axis=2)
        y = y + jnp.where(sub < s, r[:, :g], r[:, 1:]) * cw_ref[j:j + 1, cols]
    carry[seqs, :, cols] = x[:, lt - SUBLANES:, :]
    if hist_out is not None:
        hist_out[seqs, :, cols] = x[:, lt - (width - 1):, :]
    return y.reshape(bt, lt, n)


def _inproj_chunks(x_ref, g_ref, rest, *, mixer, bt, lt):
    tm = bt * lt
    h = _rms_rows(x_ref[...].reshape(tm, D_MODEL), g_ref[...]).astype(BF16)
    if mixer == "gdn":
        n_out = (len(rest) - 1) // 2
        w_refs, (mix_out, *plain_outs, hist_out) = rest[:n_out], rest[n_out:]

        def proj(i, cols):
            return _dot_nt(h, w_refs[i][cols, :]).reshape(bt, lt, -1)
    else:
        w_ref, hist_ref, cw_ref, mix_out, *plain_outs, hist_out, carry = rest
        _init_conv_carry(carry, hist_ref)
        bases = [0, 3 * SC_WIDTH]
        for out in plain_outs[:-1]:
            bases.append(bases[-1] + out.shape[-1])

        def proj(i, cols, offset=0):
            cols = slice(bases[i] + offset + cols.start, bases[i] + offset + cols.stop)
            return jnp.dot(h, w_ref[:, cols], preferred_element_type=F32).reshape(bt, lt, -1)

    def mixer_chunk(cols):
        if mixer == "gdn":
            y = proj(0, cols)
            hist_out[:, :, cols] = y[:, lt - hist_out.shape[1]:, :]
        else:
            gate_b = proj(0, cols)
            pre = proj(0, cols, SC_WIDTH) * proj(0, cols, 2 * SC_WIDTH)
            yield
            y = gate_b * _causal_conv_cols(pre, cols, carry, hist_out, cw_ref)
        mix_out[:, :, cols] = y.astype(mix_out.dtype)
        yield

    def plain_chunk(i, out, cols):
        out[:, :, cols] = proj(i, cols).astype(out.dtype)
        yield

    gens = [mixer_chunk(cols) for cols in _col_chunks(mix_out.shape[-1])]
    for i, out in enumerate(plain_outs):
        gens += [plain_chunk(i + 1, out, cols) for cols in _col_chunks(out.shape[-1])]
    return gens


def _inproj_kernel(x_ref, g_ref, *rest, mixer, bt, lt):
    gens = _inproj_chunks(x_ref, g_ref, rest, mixer=mixer, bt=bt, lt=lt)
    _round_robin([_delayed(g, i) for i, g in enumerate(gens)])


def _inproj(x, g, ws, hist, cw, *, mixer, bt, lt, act_dtype):
    bn, seq, _ = x.shape
    width, cc = cw.shape

    def act(n, dtype=act_dtype):
        return (pl.BlockSpec((bt, lt, n), lambda b, l: (b, l, 0)),
                jax.ShapeDtypeStruct((bn, seq, n), dtype))

    hist_spec = pl.BlockSpec((bt, width - 1, cc), lambda b, l: (b, 0, 0))
    args = [x, g, *ws]
    in_specs = [
        pl.BlockSpec((bt, lt, D_MODEL), lambda b, l: (b, l, 0)),
        pl.BlockSpec((1, D_MODEL), lambda b, l: (0, 0)),
    ] + [pl.BlockSpec(w.shape, lambda b, l: (0, 0)) for w in ws]
    outs = [act(cc), act(BRANCH_WIDTH), act(X_WIDTH)]
    scratch = []
    if mixer == "gdn":
        outs.append(act(BA_PAD, F32))
    else:
        args += [hist, cw]
        in_specs += [hist_spec, pl.BlockSpec((width, cc), lambda b, l: (0, 0))]
        scratch.append(pltpu.VMEM((bt, SUBLANES, cc), F32))
    outs.append((hist_spec, jax.ShapeDtypeStruct((bn, width - 1, cc), F32)))
    return pl.pallas_call(
        functools.partial(_inproj_kernel, mixer=mixer, bt=bt, lt=lt),
        grid=(bn // bt, seq // lt),
        in_specs=in_specs,
        out_specs=[o[0] for o in outs],
        out_shape=[o[1] for o in outs],
        scratch_shapes=scratch,
        compiler_params=_cparams("arbitrary", "arbitrary"),
        name="inproj_" + mixer,
    )(*args)


CHUNKS_PER_WAVE = 4
SEQS_PER_WAVE = 8


def _gdn_tile(qkv_ref, ba_ref, cw_ref, alog_ref, dtb_ref, og_ref, o_ref, s_ref, carry,
              *, chunk, group):
    bt, lt, _ = qkv_ref.shape
    nchunks = lt // chunk
    heads = range(GDN_HEADS)
    groups = [tuple(range(g, g + group)) for g in range(0, GDN_HEADS, group)]
    gw = group * chunk
    row = lax.broadcasted_iota(jnp.int32, (chunk, gw), 0)
    col = lax.broadcasted_iota(jnp.int32, (chunk, gw), 1) % chunk
    tril = row >= col
    strict = row > col
    lane_blk = lax.broadcasted_iota(jnp.int32, (1, gw), 1) // chunk
    ones_tril = (lax.broadcasted_iota(jnp.int32, (chunk, chunk), 0)
                 >= lax.broadcasted_iota(jnp.int32, (chunk, chunk), 1)).astype(F32)
    neg_a = -jnp.exp(alog_ref[...])
    dtb = dtb_ref[...]
    og = og_ref[...]

    def rows(c):
        return slice(c * chunk, (c + 1) * chunk)

    def lanes(base, h, n):
        return slice(base + h * n, base + (h + 1) * n)

    def cat(xs, axis):
        return xs[0] if len(xs) == 1 else jnp.concatenate(xs, axis=axis)

    def pick(xs):
        out = xs[0]
        for j in range(1, len(xs)):
            out = jnp.where(lane_blk == j, xs[j], out)
        return out

    def blockdiag(xs):
        z = jnp.zeros_like(xs[0])
        return cat([cat([x if i == j else z for i in range(len(xs))], 1)
                    for j, x in enumerate(xs)], 0)

    def blockdiag_of_lane_blocks(x):
        if group == 1:
            return x
        return cat([jnp.where(lane_blk == j, x, 0.0) for j in range(group)], 0)

    def conv_silu(u, cols):
        b, c = u
        x = qkv_ref[b, rows(c), cols].astype(F32)
        y = _causal_conv_cols(x[None], cols, carry, None, cw_ref, seqs=slice(b, b + 1))
        return _silu(y[0])

    def phase1(units, res):
        uh = [(u, h) for u in units for h in heads]
        ug = [(u, g) for u in units for g in range(len(groups))]
        gcum, gcum_t, beta_all = {}, {}, {}
        for u in units:
            ba = ba_ref[u[0], rows(u[1]), :]
            beta_all[u] = 1.0 / (1.0 + jnp.exp(-ba))
            z = ba + dtb
            softplus = jnp.maximum(z, 0.0) + jnp.log1p(jnp.exp(-jnp.abs(z)))
            gcum[u] = _dot_f32(ones_tril, neg_a * softplus)
        q, k, v = {}, {}, {}
        for u in units:
            for h in heads:
                q[u, h] = conv_silu(u, lanes(0, h, GDN_DK))
                k[u, h] = conv_silu(u, lanes(QK_WIDTH, h, GDN_DK))
                v[u, h] = conv_silu(u, lanes(2 * QK_WIDTH, h, GDN_DV))
            yield
        for key in uh:
            x = q[key]
            q[key] = x * (lax.rsqrt(jnp.sum(x * x, axis=-1, keepdims=True) + EPS)
                          * (GDN_DK ** -0.5))
            x = k[key]
            k[key] = x * lax.rsqrt(jnp.sum(x * x, axis=-1, keepdims=True) + EPS)
        yield
        for u in units:
            gcum_t[u] = cat([gcum[u]] * group, 0).T
        gc = {(u, h): jnp.broadcast_to(gcum[u][:, GDN_HEADS + h:GDN_HEADS + h + 1],
                                       (chunk, GDN_DK)) for u, h in uh}
        gl = {(u, h): gcum[u][chunk - 1:chunk, GDN_HEADS + h:GDN_HEADS + h + 1]
              for u, h in uh}
        beta = {(u, h): jnp.broadcast_to(beta_all[u][:, h:h + 1], (chunk, GDN_DK))
                for u, h in uh}
        eg = {key: jnp.exp(gc[key]) for key in uh}
        kb = {key: k[key] * beta[key] for key in uh}
        kkqk = {}
        for u, g in ug:
            grp = groups[g]
            kkqk[u, g] = _dot_nt(
                cat([cat([kb[u, h] for h in grp], 1), cat([q[u, h] for h in grp], 1)], 0),
                blockdiag([k[u, h] for h in grp]))
        yield
        n, p, aqk = {}, {}, {}
        for u, g in ug:
            grp = groups[g]
            g_col = pick([gc[u, h][:, :gw] for h in grp])
            g_row = pick([gcum_t[u][GDN_HEADS + h:GDN_HEADS + h + 1, :] for h in grp])
            decay = jnp.exp(jnp.where(tril, g_col - g_row, -jnp.inf))
            a = jnp.where(strict, kkqk[u, g][:chunk] * decay, 0.0)
            aqk[u, g] = (kkqk[u, g][chunk:] * decay).astype(BF16)
            n[u, g] = -a
            p[u, g] = _dot(a, blockdiag_of_lane_blocks(a))
        yield
        span = 2
        while 2 * span < chunk:
            for key in ug:
                both = _dot(cat([n[key], p[key]], 0), blockdiag_of_lane_blocks(p[key]))
                n[key] = n[key] + p[key] + both[:chunk]
                p[key] = both[chunk:]
            span *= 2
            yield
        for key in ug:
            n[key] = n[key] + p[key] + _dot(n[key], blockdiag_of_lane_blocks(p[key]))
        yield
        wu = {}
        for u, h in uh:
            rhs = jnp.concatenate([kb[u, h] * eg[u, h], v[u, h] * beta[u, h]], axis=-1)
            wu[u, h] = rhs + _dot(n[u, h // group][:, lanes(0, h % group, chunk)], rhs)
        yield
        for u in units:
            res[u] = dict(
                w_qd=[jnp.concatenate([wu[u, h][:, :GDN_DK], q[u, h] * eg[u, h]],
                                      axis=0).astype(BF16) for h in heads],
                u=[wu[u, h][:, GDN_DK:] for h in heads],
                kd=[(k[u, h] * jnp.exp(gl[u, h] - gc[u, h])).astype(BF16) for h in heads],
                aqk=[aqk[u, g] for g in range(len(groups))],
                egl=[jnp.exp(gl[u, h]) for h in heads])
        yield

    def phase2(units, res):
        for c in sorted({c for _, c in units}):
            us = [u for u in units if u[1] == c]
            uh = [(u, h) for u in us for h in heads]
            s = {(u, h): s_ref[u[0], h] for u, h in uh}
            ws_qs = {(u, h): _dot(res[u]["w_qd"][h], s[u, h]) for u, h in uh}
            yield
            v_new = {(u, h): res[u]["u"][h] - ws_qs[u, h][:chunk] for u, h in uh}
            o_grp = {(u, g): _dot(res[u]["aqk"][g], blockdiag([v_new[u, h] for h in grp]))
                     for u in us for g, grp in enumerate(groups)}
            for u, h in uh:
                s_ref[u[0], h] = (s[u, h] * res[u]["egl"][h]
                                  + _dot_tn(res[u]["kd"][h], v_new[u, h]))
            yield
            for u, h in uh:
                oh = ws_qs[u, h][chunk:] + o_grp[u, h // group][:, lanes(0, h % group, GDN_DV)]
                oh = oh * lax.rsqrt(jnp.mean(oh * oh, axis=-1, keepdims=True) + EPS) * og
                o_ref[u[0], rows(c), lanes(0, h, GDN_DV)] = oh.astype(o_ref.dtype)
            yield

    units = [(b, c) for b in range(bt) for c in range(nchunks)]
    per_wave = CHUNKS_PER_WAVE if nchunks > 1 else SEQS_PER_WAVE
    waves = [units[i:i + per_wave] for i in range(0, len(units), per_wave)]
    res = {}
    prev = None
    for wave in waves:
        gens = [phase1(wave, res)]
        if prev is not None:
            gens.append(phase2(prev, res))
        yield from _interleaved(gens)
        prev = wave
    yield from phase2(prev, res)


def _gdn_kernel(qkv_ref, ba_ref, s0_ref, hist_ref, cw_ref, alog_ref, dtb_ref, og_ref,
                o_ref, s_ref, carry, *, chunk, group):
    @pl.when(pl.program_id(1) == 0)
    def _():
        s_ref[...] = s0_ref[...]

    _init_conv_carry(carry, hist_ref)
    _round_robin([_gdn_tile(qkv_ref, ba_ref, cw_ref, alog_ref, dtb_ref, og_ref, o_ref, s_ref,
                            carry, chunk=chunk, group=group)])


def _gdn(qkv, ba, s0, hist, cw, alog_pad, dtb_pad, og, *, chunk, bt, lt):
    group = 2 if 2 * chunk == LANES else 1
    bn, seq, _ = qkv.shape
    width, cc = cw.shape
    s_spec = pl.BlockSpec((bt, GDN_HEADS, GDN_DK, GDN_DV), lambda b, l: (b, 0, 0, 0))
    hist_spec = pl.BlockSpec((bt, width - 1, cc), lambda b, l: (b, 0, 0))
    vec = pl.BlockSpec((1, LANES), lambda b, l: (0, 0))
    return pl.pallas_call(
        functools.partial(_gdn_kernel, chunk=chunk, group=group),
        grid=(bn // bt, seq // lt),
        in_specs=[
            pl.BlockSpec((bt, lt, QKV_WIDTH), lambda b, l: (b, l, 0)),
            pl.BlockSpec((bt, lt, BA_PAD), lambda b, l: (b, l, 0)),
            s_spec, hist_spec,
            pl.BlockSpec((width, cc), lambda b, l: (0, 0)),
            vec, vec, vec,
        ],
        out_specs=[
            pl.BlockSpec((bt, lt, GDN_WIDTH), lambda b, l: (b, l, 0)),
            s_spec,
        ],
        out_shape=[
            jax.ShapeDtypeStruct((bn, seq, GDN_WIDTH), qkv.dtype),
            jax.ShapeDtypeStruct(s0.shape, F32),
        ],
        scratch_shapes=[pltpu.VMEM((bt, SUBLANES, cc), F32)],
        compiler_params=_cparams("arbitrary", "arbitrary"),
        name="gdn",
    )(qkv, ba, s0, hist, cw, alog_pad, dtb_pad, og)


PROJ_ROUNDS = ((0, 5), (14, 5), (28, 1), (30, 1), (32, 1), (34, 1), (36, 1))


def _gdn_layer_kernel(x_ref, g_ref, w_qkv, w_gate, w_xq, w_ba, s0_ref, hist_ref, cw_ref,
                      alog_ref, dtb_ref, og_ref, gate_out, xq_out, hist_out, o_ref, s_ref,
                      qkv_scr, ba_scr, carry, *, lt, tiles_per_seq, chunk, group):
    i = pl.program_id(0)
    slot = i % 2
    prev = jnp.maximum(i - 1, 0)

    @pl.when(i == 0)
    def _():
        qkv_scr[1] = jnp.zeros(qkv_scr.shape[1:], F32)
        ba_scr[1] = jnp.zeros(ba_scr.shape[1:], F32)

    @pl.when(prev % tiles_per_seq == 0)
    def _():
        s_ref[...] = s0_ref[...]
        carry[:, SUBLANES - hist_ref.shape[1]:, :] = hist_ref[...]

    proj = _inproj_chunks(
        x_ref, g_ref, (w_qkv, w_gate, w_xq, w_ba, qkv_scr.at[slot], gate_out, xq_out,
                       ba_scr.at[slot], hist_out), mixer="gdn", bt=1, lt=lt)
    rule = _gdn_tile(qkv_scr.at[1 - slot], ba_scr.at[1 - slot], cw_ref, alog_ref, dtb_ref,
                     og_ref, o_ref, s_ref, carry, chunk=chunk, group=group)
    starts = [r for w, n in PROJ_ROUNDS for r in range(w, w + n)]
    _round_robin([rule] + [_delayed(p, starts[k]) for k, p in enumerate(proj)])


def _gdn_layer(x, g, ws, s0, hist, cw, alog_pad, dtb_pad, og, *, chunk, lt, act_dtype):
    group = 2 if 2 * chunk == LANES else 1
    bn, seq, _ = x.shape
    nl = seq // lt
    tiles = bn * nl
    width, cc = cw.shape

    def cur(i):
        return jnp.minimum(i, tiles - 1)

    def prev(i):
        return jnp.maximum(i - 1, 0)

    def tile(n, which):
        return pl.BlockSpec((1, lt, n), lambda i: (which(i), 0, 0))

    def per_seq(shape, which):
        return pl.BlockSpec((1,) + shape, lambda i: (which(i) // nl,) + (0,) * len(shape))

    def whole(a):
        return pl.BlockSpec(a.shape, lambda i: (0,) * a.ndim)

    state = (GDN_HEADS, GDN_DK, GDN_DV)
    gate, xq, hist_new, tok, s_new = pl.pallas_call(
        functools.partial(_gdn_layer_kernel, lt=lt, tiles_per_seq=nl, chunk=chunk, group=group),
        grid=(tiles + 1,),
        in_specs=[tile(D_MODEL, cur), whole(g)] + [whole(w) for w in ws] + [
            per_seq(state, prev), per_seq((width - 1, cc), prev), whole(cw),
            whole(alog_pad), whole(dtb_pad), whole(og)],
        out_specs=[
            tile(BRANCH_WIDTH, cur), tile(X_WIDTH, cur), per_seq((width - 1, cc), cur),
            tile(GDN_WIDTH, prev), per_seq(state, prev)],
        out_shape=[
            jax.ShapeDtypeStruct((tiles, lt, BRANCH_WIDTH), act_dtype),
            jax.ShapeDtypeStruct((tiles, lt, X_WIDTH), act_dtype),
            jax.ShapeDtypeStruct(hist.shape, F32),
            jax.ShapeDtypeStruct((tiles, lt, GDN_WIDTH), act_dtype),
            jax.ShapeDtypeStruct(s0.shape, F32)],
        scratch_shapes=[
            pltpu.VMEM((2, 1, lt, cc), F32),
            pltpu.VMEM((2, 1, lt, BA_PAD), F32),
            pltpu.VMEM((1, SUBLANES, cc), F32)],
        compiler_params=_cparams("arbitrary"),
        name="gdn_layer",
    )(x.reshape(tiles, lt, D_MODEL), g, *ws, s0, hist, cw, alog_pad, dtb_pad, og)
    return (gate.reshape(bn, seq, -1), xq.reshape(bn, seq, -1), tok.reshape(bn, seq, -1),
            s_new, hist_new)


ATTN_ROWS = 256
ATTN_SEQS = 8


def _attn_out_kernel(tok_ref, xq_ref, gate_ref, x_ref, kt_ref, vt_ref, w_ref, fg_ref,
                     y_ref, xo_scr, *, bt, lt, final):
    lane_head = lax.broadcasted_iota(jnp.int32, (1, X_WIDTH), 1) // X_HEAD_DIM
    scale = X_HEAD_DIM ** -0.5
    tm = bt * lt

    def attend(b, rs):
        r = rs.stop - rs.start
        q = xq_ref[b, rs, :]
        q = q * jnp.asarray(scale, q.dtype)
        qx = jnp.concatenate([jnp.where(lane_head == h, q, jnp.zeros_like(q))
                              for h in range(X_HEADS)], axis=0)
        s = _dot(qx, kt_ref[b])
        yield
        e = jnp.exp(s - jnp.max(s, axis=-1, keepdims=True))
        p = e / jnp.sum(e, axis=-1, keepdims=True)
        o4 = _dot_nt(p, vt_ref[b])
        yield
        xo = jnp.where(lane_head == 0, o4[0:r], 0.0)
        for h in range(1, X_HEADS):
            xo = xo + jnp.where(lane_head == h, o4[h * r:(h + 1) * r], 0.0)
        xo_scr[b, rs, :] = xo
        yield

    acc = {}

    def tok_proj():
        sg = _silu(gate_ref[:, :, :GDN_WIDTH].astype(F32).reshape(tm, GDN_WIDTH))
        br = (tok_ref[...].astype(F32).reshape(tm, GDN_WIDTH) * sg).astype(BF16)
        for cols in _col_chunks(D_MODEL):
            acc[cols.start] = (x_ref[:, :, cols].reshape(tm, -1)
                               + _dot(br, w_ref[:GDN_WIDTH, cols]))
            yield

    if bt == 1:
        blocks = [slice(r, r + ATTN_ROWS) for r in range(0, lt, ATTN_ROWS)]
        _round_robin([attend(0, rs) for rs in blocks] + [tok_proj()])
    else:
        def seq_group(i, carry):
            _round_robin([attend(i * ATTN_SEQS + j, slice(0, lt)) for j in range(ATTN_SEQS)])
            return carry

        lax.fori_loop(0, bt // ATTN_SEQS, seq_group, 0)
        _round_robin([tok_proj()])

    sg_x = _silu(gate_ref[:, :, GDN_WIDTH:].astype(F32).reshape(tm, X_WIDTH))
    br_x = (xo_scr[...].reshape(tm, X_WIDTH) * sg_x).astype(BF16)
    y = [acc[cols.start] + _dot(br_x, w_ref[GDN_WIDTH:, cols]) for cols in _col_chunks(D_MODEL)]
    if final:
        ms = sum(jnp.sum(c * c, axis=-1, keepdims=True) for c in y) * (1.0 / D_MODEL)
        r = lax.rsqrt(ms + EPS)
        y = [c * r * fg_ref[:, cols] for c, cols in zip(y, _col_chunks(D_MODEL))]
    for c, cols in zip(y, _col_chunks(D_MODEL)):
        y_ref[:, :, cols] = c.reshape(bt, lt, -1)


def _attn_out(tok, xq, gate, x, mk, mv, w_bf16, fg, *, layer, bt, lt, final):
    bn, seq, _ = x.shape

    def act(n):
        return pl.BlockSpec((bt, lt, n), lambda b, l: (b, l, 0))

    mem = pl.BlockSpec((None, bt, X_WIDTH, N_MEM), lambda b, l: (layer, b, 0, 0))
    return pl.pallas_call(
        functools.partial(_attn_out_kernel, bt=bt, lt=lt, final=final),
        grid=(bn // bt, seq // lt),
        in_specs=[
            act(GDN_WIDTH), act(X_WIDTH), act(BRANCH_WIDTH), act(D_MODEL), mem, mem,
            pl.BlockSpec((BRANCH_WIDTH, D_MODEL), lambda b, l: (0, 0)),
            pl.BlockSpec((1, D_MODEL), lambda b, l: (0, 0)),
        ],
        out_specs=act(D_MODEL),
        out_shape=jax.ShapeDtypeStruct(x.shape, F32),
        scratch_shapes=[pltpu.VMEM((bt, lt, X_WIDTH), F32)],
        compiler_params=_cparams("arbitrary", "arbitrary"),
        name="attn_out",
    )(tok, xq, gate, x, mk, mv, w_bf16, fg)


def _trunk(x, mem_k, mem_v, gdn_s, gdn_conv, sc_conv, p, *, bt, lt, chunk, gdn_bt, gdn_lt,
           attn_bt, attn_lt, act_dtype):
    if gdn_bt == 1 and bt == 1:
        gate, xq, tok, s_new, gconv_new = _gdn_layer(
            x, p["norm_g"][0:1], p["w_in_a"], gdn_s, gdn_conv, p["conv_w_a"], p["alog_pad"],
            p["dtb_pad"], p["o_norm_g"], chunk=chunk, lt=gdn_lt, act_dtype=act_dtype)
    else:
        qkv, gate, xq, ba, gconv_new = _inproj(
            x, p["norm_g"][0:1], p["w_in_a"], None, p["conv_w_a"], mixer="gdn", bt=bt, lt=lt,
            act_dtype=act_dtype)
        tok, s_new = _gdn(qkv, ba, gdn_s, gdn_conv, p["conv_w_a"], p["alog_pad"],
                          p["dtb_pad"], p["o_norm_g"], chunk=chunk, bt=gdn_bt, lt=gdn_lt)
    x = _attn_out(tok, xq, gate, x, mem_k, mem_v, p["w_out"][0], p["final_norm_g"],
                  layer=0, bt=attn_bt, lt=attn_lt, final=False)
    tok, gate, xq, sconv_new = _inproj(
        x, p["norm_g"][1:2], p["w_in_b"], sc_conv, p["conv_w_b"], mixer="sconv", bt=bt, lt=lt,
        act_dtype=act_dtype)
    y = _attn_out(tok, xq, gate, x, mem_k, mem_v, p["w_out"][1], p["final_norm_g"],
                  layer=1, bt=attn_bt, lt=attn_lt, final=True)
    return y, s_new[None], gconv_new[None], sconv_new[None]


def kernel(x_prompt, x_sample, mem_prompt, state_gdn, state_gdn_conv, state_sconv, cache_mem_k, cache_mem_v, norm_g, w_in_a, conv_w_a, a_log, dt_bias, o_norm_g, w_in_b, conv_w_b, mem_norm_g, w_mem_kv, w_out, final_norm_g):
    bp = x_prompt.shape[0]

    wa = jnp.transpose(w_in_a[0])
    c_b = QKV_WIDTH
    c_g = c_b + 2 * GDN_HEADS
    c_x = c_g + BRANCH_WIDTH
    wa = [wa[:c_b], wa[c_g:c_x], wa[c_x:],
          jnp.concatenate([wa[c_b:c_g], jnp.zeros((BA_PAD - 2 * GDN_HEADS, D_MODEL), wa.dtype)])]
    wa = [w.astype(BF16) for w in wa]
    pad_lo = jnp.zeros((GDN_HEADS,), F32)
    pad_hi = jnp.zeros((LANES - 2 * GDN_HEADS,), F32)
    params = {
        "norm_g": norm_g,
        "w_in_a": wa,
        "conv_w_a": conv_w_a[0],
        "alog_pad": jnp.concatenate([pad_lo, a_log[0], pad_hi])[None],
        "dtb_pad": jnp.concatenate([pad_lo, dt_bias[0], pad_hi])[None],
        "o_norm_g": o_norm_g,
        "w_in_b": [w_in_b[0].astype(BF16)],
        "conv_w_b": conv_w_b[0],
        "w_out": w_out.astype(BF16),
        "final_norm_g": final_norm_g[None],
    }

    def to_cache(t):
        t = t.reshape(t.shape[0], t.shape[1], X_HEADS, X_HEAD_DIM, t.shape[3])
        return jnp.transpose(t, (0, 1, 4, 2, 3))

    def from_cache(t):
        t = jnp.transpose(t, (0, 1, 3, 4, 2))
        return t.reshape(t.shape[0], t.shape[1], X_WIDTH, t.shape[4])

    mem_kt, mem_vt = _memkv(mem_prompt, mem_norm_g[None], w_mem_kv.astype(BF16))
    mem_k_p = to_cache(mem_kt)
    mem_v_p = to_cache(mem_vt)

    s0_p = jnp.zeros((bp,) + state_gdn.shape[2:], F32)
    gc0_p = jnp.zeros((bp,) + state_gdn_conv.shape[2:], F32)
    sc0_p = jnp.zeros((bp,) + state_sconv.shape[2:], F32)
    y_p, s_p, gc_p, sc_p = _trunk(x_prompt, mem_kt, mem_vt, s0_p, gc0_p, sc0_p,
                                  params, bt=1, lt=1024, chunk=GDN_CHUNK, gdn_bt=1, gdn_lt=512,
                                  attn_bt=1, attn_lt=512, act_dtype=BF16)
    dec_seq = x_sample.shape[1]
    y_s, s_s, gc_s, sc_s = _trunk(x_sample, from_cache(cache_mem_k), from_cache(cache_mem_v),
                                  state_gdn[0], state_gdn_conv[0], state_sconv[0], params,
                                  bt=32, lt=dec_seq, chunk=dec_seq, gdn_bt=16, gdn_lt=dec_seq,
                                  attn_bt=16, attn_lt=dec_seq, act_dtype=F32)
    return (y_p, y_s, s_p, gc_p, sc_p, mem_k_p, mem_v_p, s_s, gc_s, sc_s)
```

```python
import functools

import jax
import jax.numpy as jnp
from jax import lax
from jax.experimental import pallas as pl
from jax.experimental.pallas import tpu as pltpu

F32 = jnp.float32
BF16 = jnp.bfloat16

D_MODEL = 1024
N_MEM = 256
X_WIDTH = 256
X_HEADS = 4
X_HEAD_DIM = 64
GDN_HEADS = 6
GDN_DK = 128
GDN_DV = 128
GDN_WIDTH = GDN_HEADS * GDN_DV
QK_WIDTH = GDN_HEADS * GDN_DK
QKV_WIDTH = 2 * QK_WIDTH + GDN_WIDTH
GDN_CHUNK = 64
SC_WIDTH = 768
BRANCH_WIDTH = 1024
EPS = 1e-6

LANES = 128
SUBLANES = 8
COL_CHUNK = 256
BA_PAD = LANES
VMEM_LIMIT = 56 * 1024 * 1024

HIGHEST = lax.Precision.HIGHEST


def _cparams(*sem):
    return pltpu.CompilerParams(dimension_semantics=sem, vmem_limit_bytes=VMEM_LIMIT)


def _rms_rows(x, g):
    r = lax.rsqrt(jnp.mean(x * x, axis=-1, keepdims=True) + EPS)
    return x * r * g


def _silu(x):
    h = 0.5 * x
    return h + h * jnp.tanh(h)


def _dot(a, b):
    return jnp.dot(a.astype(BF16), b.astype(BF16), preferred_element_type=F32)


def _dot_nt(a, b):
    return lax.dot_general(a.astype(BF16), b.astype(BF16), (((1,), (1,)), ((), ())),
                           preferred_element_type=F32)


def _dot_tn(a, b):
    return lax.dot_general(a.astype(BF16), b.astype(BF16), (((0,), (0,)), ((), ())),
                           preferred_element_type=F32)


def _dot_f32(a, b):
    return jnp.dot(a, b, preferred_element_type=F32, precision=HIGHEST)


def _col_chunks(n):
    return [slice(c, min(c + COL_CHUNK, n)) for c in range(0, n, COL_CHUNK)]


def _interleaved(gens):
    gens = list(gens)
    while gens:
        alive = []
        for g in gens:
            try:
                next(g)
                alive.append(g)
            except StopIteration:
                pass
        gens = alive
        yield


def _round_robin(gens):
    for _ in _interleaved(gens):
        pass


def _delayed(gen, rounds):
    for _ in range(rounds):
        yield
    yield from gen


def _memkv_kernel(m_ref, g_ref, w_ref, kt_ref, vt_ref):
    h = _rms_rows(m_ref[...], g_ref[...]).astype(BF16)
    for layer in range(w_ref.shape[0]):
        kv = jnp.dot(h, w_ref[layer], preferred_element_type=F32)
        kt_ref[layer] = kv[:, :X_WIDTH].T
        vt_ref[layer] = kv[:, X_WIDTH:].T


def _memkv(mem, g, w_bf16):
    bn, n_mem, _ = mem.shape
    depth = w_bf16.shape[0]
    out = jax.ShapeDtypeStruct((depth, bn, X_WIDTH, n_mem), F32)
    out_spec = pl.BlockSpec((depth, None, X_WIDTH, n_mem), lambda b: (0, b, 0, 0))
    return pl.pallas_call(
        _memkv_kernel,
        grid=(bn,),
        in_specs=[
            pl.BlockSpec((None, n_mem, D_MODEL), lambda b: (b, 0, 0)),
            pl.BlockSpec((1, D_MODEL), lambda b: (0, 0)),
            pl.BlockSpec(w_bf16.shape, lambda b: (0, 0, 0)),
        ],
        out_specs=[out_spec, out_spec],
        out_shape=[out, out],
        compiler_params=_cparams("arbitrary"),
        name="memkv",
    )(mem, g, w_bf16)


def _init_conv_carry(carry, hist_ref):
    @pl.when(pl.program_id(1) == 0)
    def _():
        carry[:, SUBLANES - hist_ref.shape[1]:, :] = hist_ref[...]


def _causal_conv_cols(x, cols, carry, hist_out, cw_ref, seqs=slice(None)):
    width = cw_ref.shape[0]
    bt, lt, n = x.shape
    g = lt // SUBLANES
    xe = jnp.concatenate([carry[seqs, :, cols], x], axis=1).reshape(bt, g + 1, SUBLANES, n)
    sub = lax.broadcasted_iota(jnp.int32, (1, 1, SUBLANES, n), 2)
    y = x.reshape(bt, g, SUBLANES, n) * cw_ref[width - 1:width, cols]
    for j in range(width - 1):
        s = width - 1 - j
        r = pltpu.roll(xe, s, axis=2)
        y = y + jnp.where(sub < s, r[:, :g], r[:, 1:]) * cw_ref[j:j + 1, cols]
    carry[seqs, :, cols] = x[:, lt - SUBLANES:, :]
    if hist_out is not None:
        hist_out[seqs, :, cols] = x[:, lt - (width - 1):, :]
    return y.reshape(bt, lt, n)


def _inproj_chunks(x_ref, g_ref, rest, *, mixer, bt, lt):
    tm = bt * lt
    if g_ref is None:
        h = x_ref[...].reshape(tm, D_MODEL)
    else:
        h = _rms_rows(x_ref[...].reshape(tm, D_MODEL), g_ref[...]).astype(BF16)
    if mixer == "gdn":
        n_out = (len(rest) - 1) // 2
        w_refs, (mix_out, *plain_outs, hist_out) = rest[:n_out], rest[n_out:]

        def proj(i, cols):
            return _dot_nt(h, w_refs[i][cols, :]).reshape(bt, lt, -1)
    else:
        w_ref, _, cw_ref, mix_out, *plain_outs, hist_out, carry = rest
        bases = [0, 3 * SC_WIDTH]
        for out in plain_outs[:-1]:
            bases.append(bases[-1] + out.shape[-1])

        def proj(i, cols, offset=0):
            cols = slice(bases[i] + offset + cols.start, bases[i] + offset + cols.stop)
            return jnp.dot(h, w_ref[:, cols], preferred_element_type=F32).reshape(bt, lt, -1)

    def mixer_chunk(cols):
        if mixer == "gdn":
            y = proj(0, cols)
            hist_out[:, :, cols] = y[:, lt - hist_out.shape[1]:, :]
        else:
            gate_b = proj(0, cols)
            pre = proj(0, cols, SC_WIDTH) * proj(0, cols, 2 * SC_WIDTH)
            yield
            y = gate_b * _causal_conv_cols(pre, cols, carry, hist_out, cw_ref)
        mix_out[:, :, cols] = y.astype(mix_out.dtype)
        yield

    def plain_chunk(i, out, cols):
        out[:, :, cols] = proj(i, cols).astype(out.dtype)
        yield

    gens = [mixer_chunk(cols) for cols in _col_chunks(mix_out.shape[-1])]
    for i, out in enumerate(plain_outs):
        gens += [plain_chunk(i + 1, out, cols) for cols in _col_chunks(out.shape[-1])]
    return gens


def _inproj_kernel(x_ref, g_ref, *rest, mixer, bt, lt):
    if mixer == "sconv":
        _init_conv_carry(rest[-1], rest[1])
    gens = _inproj_chunks(x_ref, g_ref, rest, mixer=mixer, bt=bt, lt=lt)
    _round_robin([_delayed(g, i) for i, g in enumerate(gens)])


def _inproj(x, g, ws, hist, cw, *, mixer, bt, lt, act_dtype):
    bn, seq, _ = x.shape
    width, cc = cw.shape

    def act(n, dtype=act_dtype):
        return (pl.BlockSpec((bt, lt, n), lambda b, l: (b, l, 0)),
                jax.ShapeDtypeStruct((bn, seq, n), dtype))

    hist_spec = pl.BlockSpec((bt, width - 1, cc), lambda b, l: (b, 0, 0))
    args = [x, g, *ws]
    in_specs = [
        pl.BlockSpec((bt, lt, D_MODEL), lambda b, l: (b, l, 0)),
        pl.BlockSpec((1, D_MODEL), lambda b, l: (0, 0)),
    ] + [pl.BlockSpec(w.shape, lambda b, l: (0, 0)) for w in ws]
    outs = [act(cc), act(BRANCH_WIDTH), act(X_WIDTH)]
    scratch = []
    if mixer == "gdn":
        outs.append(act(BA_PAD, F32))
    else:
        args += [hist, cw]
        in_specs += [hist_spec, pl.BlockSpec((width, cc), lambda b, l: (0, 0))]
        scratch.append(pltpu.VMEM((bt, SUBLANES, cc), F32))
    outs.append((hist_spec, jax.ShapeDtypeStruct((bn, width - 1, cc), F32)))
    return pl.pallas_call(
        functools.partial(_inproj_kernel, mixer=mixer, bt=bt, lt=lt),
        grid=(bn // bt, seq // lt),
        in_specs=in_specs,
        out_specs=[o[0] for o in outs],
        out_shape=[o[1] for o in outs],
        scratch_shapes=scratch,
        compiler_params=_cparams("arbitrary", "arbitrary"),
        name="inproj_" + mixer,
    )(*args)


CHUNKS_PER_WAVE = 4
SEQS_PER_WAVE = 8


def _gdn_tile(qkv_ref, ba_ref, cw_ref, alog_ref, dtb_ref, og_ref, o_ref, s_ref, carry,
              *, chunk, group):
    bt, lt, _ = qkv_ref.shape
    nchunks = lt // chunk
    heads = range(GDN_HEADS)
    groups = [tuple(range(g, g + group)) for g in range(0, GDN_HEADS, group)]
    gw = group * chunk
    row = lax.broadcasted_iota(jnp.int32, (chunk, gw), 0)
    col = lax.broadcasted_iota(jnp.int32, (chunk, gw), 1) % chunk
    tril = row >= col
    strict = row > col
    lane_blk = lax.broadcasted_iota(jnp.int32, (1, gw), 1) // chunk
    ones_tril = (lax.broadcasted_iota(jnp.int32, (chunk, chunk), 0)
                 >= lax.broadcasted_iota(jnp.int32, (chunk, chunk), 1)).astype(F32)
    neg_a = -jnp.exp(alog_ref[...])
    dtb = dtb_ref[...]
    og = og_ref[...]

    def rows(c):
        return slice(c * chunk, (c + 1) * chunk)

    def lanes(base, h, n):
        return slice(base + h * n, base + (h + 1) * n)

    def cat(xs, axis):
        return xs[0] if len(xs) == 1 else jnp.concatenate(xs, axis=axis)

    def pick(xs):
        out = xs[0]
        for j in range(1, len(xs)):
            out = jnp.where(lane_blk == j, xs[j], out)
        return out

    def blockdiag(xs):
        z = jnp.zeros_like(xs[0])
        return cat([cat([x if i == j else z for i in range(len(xs))], 1)
                    for j, x in enumerate(xs)], 0)

    def blockdiag_of_lane_blocks(x):
        if group == 1:
            return x
        return cat([jnp.where(lane_blk == j, x, 0.0) for j in range(group)], 0)

    def conv_silu(u, cols):
        b, c = u
        x = qkv_ref[b, rows(c), cols].astype(F32)
        y = _causal_conv_cols(x[None], cols, carry, None, cw_ref, seqs=slice(b, b + 1))
        return _silu(y[0])

    def phase1(units, res):
        uh = [(u, h) for u in units for h in heads]
        ug = [(u, g) for u in units for g in range(len(groups))]
        gcum, gcum_t, beta_all = {}, {}, {}
        for u in units:
            ba = ba_ref[u[0], rows(u[1]), :]
            beta_all[u] = 1.0 / (1.0 + jnp.exp(-ba))
            z = ba + dtb
            softplus = jnp.maximum(z, 0.0) + jnp.log1p(jnp.exp(-jnp.abs(z)))
            gcum[u] = _dot_f32(ones_tril, neg_a * softplus)
        q, k, v = {}, {}, {}
        for u in units:
            for h in heads:
                q[u, h] = conv_silu(u, lanes(0, h, GDN_DK))
                k[u, h] = conv_silu(u, lanes(QK_WIDTH, h, GDN_DK))
                v[u, h] = conv_silu(u, lanes(2 * QK_WIDTH, h, GDN_DV))
            yield
        for key in uh:
            x = q[key]
            q[key] = x * (lax.rsqrt(jnp.sum(x * x, axis=-1, keepdims=True) + EPS)
                          * (GDN_DK ** -0.5))
            x = k[key]
            k[key] = x * lax.rsqrt(jnp.sum(x * x, axis=-1, keepdims=True) + EPS)
        yield
        for u in units:
            gcum_t[u] = cat([gcum[u]] * group, 0).T
        gc = {(u, h): jnp.broadcast_to(gcum[u][:, GDN_HEADS + h:GDN_HEADS + h + 1],
                                       (chunk, GDN_DK)) for u, h in uh}
        gl = {(u, h): gcum[u][chunk - 1:chunk, GDN_HEADS + h:GDN_HEADS + h + 1]
              for u, h in uh}
        beta = {(u, h): jnp.broadcast_to(beta_all[u][:, h:h + 1], (chunk, GDN_DK))
                for u, h in uh}
        eg = {key: jnp.exp(gc[key]) for key in uh}
        kb = {key: k[key] * beta[key] for key in uh}
        kkqk = {}
        for u, g in ug:
            grp = groups[g]
            kkqk[u, g] = _dot_nt(
                cat([cat([kb[u, h] for h in grp], 1), cat([q[u, h] for h in grp], 1)], 0),
                blockdiag([k[u, h] for h in grp]))
        yield
        n, p, aqk = {}, {}, {}
        for u, g in ug:
            grp = groups[g]
            g_col = pick([gc[u, h][:, :gw] for h in grp])
            g_row = pick([gcum_t[u][GDN_HEADS + h:GDN_HEADS + h + 1, :] for h in grp])
            decay = jnp.exp(jnp.where(tril, g_col - g_row, -jnp.inf))
            a = jnp.where(strict, kkqk[u, g][:chunk] * decay, 0.0)
            aqk[u, g] = (kkqk[u, g][chunk:] * decay).astype(BF16)
            n[u, g] = -a
            p[u, g] = _dot(a, blockdiag_of_lane_blocks(a))
        yield
        span = 2
        while 2 * span < chunk:
            for key in ug:
                both = _dot(cat([n[key], p[key]], 0), blockdiag_of_lane_blocks(p[key]))
                n[key] = n[key] + p[key] + both[:chunk]
                p[key] = both[chunk:]
            span *= 2
            yield
        for key in ug:
            n[key] = n[key] + p[key] + _dot(n[key], blockdiag_of_lane_blocks(p[key]))
        yield
        wu = {}
        for u, h in uh:
            rhs = jnp.concatenate([kb[u, h] * eg[u, h], v[u, h] * beta[u, h]], axis=-1)
            wu[u, h] = rhs + _dot(n[u, h // group][:, lanes(0, h % group, chunk)], rhs)
        yield
        for u in units:
            res[u] = dict(
                w_qd=[jnp.concatenate([wu[u, h][:, :GDN_DK], q[u, h] * eg[u, h]],
                                      axis=0).astype(BF16) for h in heads],
                u=[wu[u, h][:, GDN_DK:] for h in heads],
                kd=[(k[u, h] * jnp.exp(gl[u, h] - gc[u, h])).astype(BF16) for h in heads],
                aqk=[aqk[u, g] for g in range(len(groups))],
                egl=[jnp.exp(gl[u, h]) for h in heads])
        yield

    def phase2(units, res):
        for c in sorted({c for _, c in units}):
            us = [u for u in units if u[1] == c]
            uh = [(u, h) for u in us for h in heads]
            s = {(u, h): s_ref[u[0], h] for u, h in uh}
            ws_qs = {(u, h): _dot(res[u]["w_qd"][h], s[u, h]) for u, h in uh}
            yield
            v_new = {(u, h): res[u]["u"][h] - ws_qs[u, h][:chunk] for u, h in uh}
            o_grp = {(u, g): _dot(res[u]["aqk"][g], blockdiag([v_new[u, h] for h in grp]))
                     for u in us for g, grp in enumerate(groups)}
            for u, h in uh:
                s_ref[u[0], h] = (s[u, h] * res[u]["egl"][h]
                                  + _dot_tn(res[u]["kd"][h], v_new[u, h]))
            yield
            for u, h in uh:
                oh = ws_qs[u, h][chunk:] + o_grp[u, h // group][:, lanes(0, h % group, GDN_DV)]
                oh = oh * lax.rsqrt(jnp.mean(oh * oh, axis=-1, keepdims=True) + EPS) * og
                o_ref[u[0], rows(c), lanes(0, h, GDN_DV)] = oh.astype(o_ref.dtype)
            yield

    units = [(b, c) for b in range(bt) for c in range(nchunks)]
    per_wave = CHUNKS_PER_WAVE if nchunks > 1 else SEQS_PER_WAVE
    waves = [units[i:i + per_wave] for i in range(0, len(units), per_wave)]
    res = {}
    prev = None
    for wave in waves:
        gens = [phase1(wave, res)]
        if prev is not None:
            gens.append(phase2(prev, res))
        yield from _interleaved(gens)
        prev = wave
    yield from phase2(prev, res)


def _gdn_kernel(qkv_ref, ba_ref, s0_ref, hist_ref, cw_ref, alog_ref, dtb_ref, og_ref,
                o_ref, s_ref, carry, *, chunk, group):
    @pl.when(pl.program_id(1) == 0)
    def _():
        s_ref[...] = s0_ref[...]

    _init_conv_carry(carry, hist_ref)
    _round_robin([_gdn_tile(qkv_ref, ba_ref, cw_ref, alog_ref, dtb_ref, og_ref, o_ref, s_ref,
                            carry, chunk=chunk, group=group)])


def _gdn(qkv, ba, s0, hist, cw, alog_pad, dtb_pad, og, *, chunk, bt, lt):
    group = 2 if 2 * chunk == LANES else 1
    bn, seq, _ = qkv.shape
    width, cc = cw.shape
    s_spec = pl.BlockSpec((bt, GDN_HEADS, GDN_DK, GDN_DV), lambda b, l: (b, 0, 0, 0))
    hist_spec = pl.BlockSpec((bt, width - 1, cc), lambda b, l: (b, 0, 0))
    vec = pl.BlockSpec((1, LANES), lambda b, l: (0, 0))
    return pl.pallas_call(
        functools.partial(_gdn_kernel, chunk=chunk, group=group),
        grid=(bn // bt, seq // lt),
        in_specs=[
            pl.BlockSpec((bt, lt, QKV_WIDTH), lambda b, l: (b, l, 0)),
            pl.BlockSpec((bt, lt, BA_PAD), lambda b, l: (b, l, 0)),
            s_spec, hist_spec,
            pl.BlockSpec((width, cc), lambda b, l: (0, 0)),
            vec, vec, vec,
        ],
        out_specs=[
            pl.BlockSpec((bt, lt, GDN_WIDTH), lambda b, l: (b, l, 0)),
            s_spec,
        ],
        out_shape=[
            jax.ShapeDtypeStruct((bn, seq, GDN_WIDTH), qkv.dtype),
            jax.ShapeDtypeStruct(s0.shape, F32),
        ],
        scratch_shapes=[pltpu.VMEM((bt, SUBLANES, cc), F32)],
        compiler_params=_cparams("arbitrary", "arbitrary"),
        name="gdn",
    )(qkv, ba, s0, hist, cw, alog_pad, dtb_pad, og)


PROJ_ROUNDS = ((0, 5), (14, 5), (28, 1), (30, 1), (32, 1), (34, 1), (36, 1))


def _gdn_layer_kernel(x_ref, g_ref, w_qkv, w_gate, w_xq, w_ba, s0_ref, hist_ref, cw_ref,
                      alog_ref, dtb_ref, og_ref, gate_out, xq_out, hist_out, o_ref, s_ref,
                      qkv_scr, ba_scr, carry, *, lt, tiles_per_seq, chunk, group):
    i = pl.program_id(0)
    slot = i % 2
    prev = jnp.maximum(i - 1, 0)

    @pl.when(i == 0)
    def _():
        qkv_scr[1] = jnp.zeros(qkv_scr.shape[1:], F32)
        ba_scr[1] = jnp.zeros(ba_scr.shape[1:], F32)

    @pl.when(prev % tiles_per_seq == 0)
    def _():
        s_ref[...] = s0_ref[...]
        carry[:, SUBLANES - hist_ref.shape[1]:, :] = hist_ref[...]

    proj = _inproj_chunks(
        x_ref, g_ref, (w_qkv, w_gate, w_xq, w_ba, qkv_scr.at[slot], gate_out, xq_out,
                       ba_scr.at[slot], hist_out), mixer="gdn", bt=1, lt=lt)
    rule = _gdn_tile(qkv_scr.at[1 - slot], ba_scr.at[1 - slot], cw_ref, alog_ref, dtb_ref,
                     og_ref, o_ref, s_ref, carry, chunk=chunk, group=group)
    starts = [r for w, n in PROJ_ROUNDS for r in range(w, w + n)]
    _round_robin([rule] + [_delayed(p, starts[k]) for k, p in enumerate(proj)])


def _gdn_layer(x, g, ws, s0, hist, cw, alog_pad, dtb_pad, og, *, chunk, lt, act_dtype):
    group = 2 if 2 * chunk == LANES else 1
    bn, seq, _ = x.shape
    nl = seq // lt
    tiles = bn * nl
    width, cc = cw.shape

    def cur(i):
        return jnp.minimum(i, tiles - 1)

    def prev(i):
        return jnp.maximum(i - 1, 0)

    def tile(n, which):
        return pl.BlockSpec((1, lt, n), lambda i: (which(i), 0, 0))

    def per_seq(shape, which):
        return pl.BlockSpec((1,) + shape, lambda i: (which(i) // nl,) + (0,) * len(shape))

    def whole(a):
        return pl.BlockSpec(a.shape, lambda i: (0,) * a.ndim)

    state = (GDN_HEADS, GDN_DK, GDN_DV)
    gate, xq, hist_new, tok, s_new = pl.pallas_call(
        functools.partial(_gdn_layer_kernel, lt=lt, tiles_per_seq=nl, chunk=chunk, group=group),
        grid=(tiles + 1,),
        in_specs=[tile(D_MODEL, cur), whole(g)] + [whole(w) for w in ws] + [
            per_seq(state, prev), per_seq((width - 1, cc), prev), whole(cw),
            whole(alog_pad), whole(dtb_pad), whole(og)],
        out_specs=[
            tile(BRANCH_WIDTH, cur), tile(X_WIDTH, cur), per_seq((width - 1, cc), cur),
            tile(GDN_WIDTH, prev), per_seq(state, prev)],
        out_shape=[
            jax.ShapeDtypeStruct((tiles, lt, BRANCH_WIDTH), act_dtype),
            jax.ShapeDtypeStruct((tiles, lt, X_WIDTH), act_dtype),
            jax.ShapeDtypeStruct(hist.shape, F32),
            jax.ShapeDtypeStruct((tiles, lt, GDN_WIDTH), act_dtype),
            jax.ShapeDtypeStruct(s0.shape, F32)],
        scratch_shapes=[
            pltpu.VMEM((2, 1, lt, cc), F32),
            pltpu.VMEM((2, 1, lt, BA_PAD), F32),
            pltpu.VMEM((1, SUBLANES, cc), F32)],
        compiler_params=_cparams("arbitrary"),
        name="gdn_layer",
    )(x.reshape(tiles, lt, D_MODEL), g, *ws, s0, hist, cw, alog_pad, dtb_pad, og)
    return (gate.reshape(bn, seq, -1), xq.reshape(bn, seq, -1), tok.reshape(bn, seq, -1),
            s_new, hist_new)


ATTN_ROWS = 256
ATTN_SEQS = 8


def _attn_out_tile(tok_ref, xq_ref, gate_ref, x_ref, kt_ref, vt_ref, w_ref, fg_ref,
                   y_ref, xo_scr, *, bt, lt, final, next_norm=None):
    lane_head = lax.broadcasted_iota(jnp.int32, (1, X_WIDTH), 1) // X_HEAD_DIM
    scale = X_HEAD_DIM ** -0.5
    tm = bt * lt

    def attend(b, rs):
        r = rs.stop - rs.start
        q = xq_ref[b, rs, :]
        q = q * jnp.asarray(scale, q.dtype)
        qx = jnp.concatenate([jnp.where(lane_head == h, q, jnp.zeros_like(q))
                              for h in range(X_HEADS)], axis=0)
        s = _dot(qx, kt_ref[b])
        yield
        e = jnp.exp(s - jnp.max(s, axis=-1, keepdims=True))
        p = e / jnp.sum(e, axis=-1, keepdims=True)
        o4 = _dot_nt(p, vt_ref[b])
        yield
        xo = jnp.where(lane_head == 0, o4[0:r], 0.0)
        for h in range(1, X_HEADS):
            xo = xo + jnp.where(lane_head == h, o4[h * r:(h + 1) * r], 0.0)
        xo_scr[b, rs, :] = xo
        yield

    acc = {}

    def tok_proj():
        sg = _silu(gate_ref[:, :, :GDN_WIDTH].astype(F32).reshape(tm, GDN_WIDTH))
        br = (tok_ref[...].astype(F32).reshape(tm, GDN_WIDTH) * sg).astype(BF16)
        for cols in _col_chunks(D_MODEL):
            acc[cols.start] = (x_ref[:, :, cols].reshape(tm, -1)
                               + _dot(br, w_ref[:GDN_WIDTH, cols]))
            yield

    if bt == 1:
        blocks = [slice(r, r + ATTN_ROWS) for r in range(0, lt, ATTN_ROWS)]
        yield from _interleaved([attend(0, rs) for rs in blocks] + [tok_proj()])
    else:
        def seq_group(i, carry):
            _round_robin([attend(i * ATTN_SEQS + j, slice(0, lt)) for j in range(ATTN_SEQS)])
            return carry

        lax.fori_loop(0, bt // ATTN_SEQS, seq_group, 0)
        yield from tok_proj()

    sg_x = _silu(gate_ref[:, :, GDN_WIDTH:].astype(F32).reshape(tm, X_WIDTH))
    br_x = (xo_scr[...].reshape(tm, X_WIDTH) * sg_x).astype(BF16)
    y = [acc[cols.start] + _dot(br_x, w_ref[GDN_WIDTH:, cols]) for cols in _col_chunks(D_MODEL)]
    if final or next_norm is not None:
        ms = sum(jnp.sum(c * c, axis=-1, keepdims=True) for c in y) * (1.0 / D_MODEL)
        r = lax.rsqrt(ms + EPS)
    if next_norm is not None:
        h_ref, g_ref = next_norm
        for c, cols in zip(y, _col_chunks(D_MODEL)):
            h_ref[:, :, cols] = (c * r * g_ref[:, cols]).astype(h_ref.dtype).reshape(bt, lt, -1)
    if final:
        y = [c * r * fg_ref[:, cols] for c, cols in zip(y, _col_chunks(D_MODEL))]
    for c, cols in zip(y, _col_chunks(D_MODEL)):
        y_ref[:, :, cols] = c.reshape(bt, lt, -1)
    yield


def _attn_out_kernel(tok_ref, xq_ref, gate_ref, x_ref, kt_ref, vt_ref, w_ref, fg_ref,
                     y_ref, xo_scr, *, bt, lt, final):
    _round_robin([_attn_out_tile(tok_ref, xq_ref, gate_ref, x_ref, kt_ref, vt_ref, w_ref,
                                 fg_ref, y_ref, xo_scr, bt=bt, lt=lt, final=final)])


def _attn_out(tok, xq, gate, x, mk, mv, w_bf16, fg, *, layer, bt, lt, final):
    bn, seq, _ = x.shape

    def act(n):
        return pl.BlockSpec((bt, lt, n), lambda b, l: (b, l, 0))

    mem = pl.BlockSpec((None, bt, X_WIDTH, N_MEM), lambda b, l: (layer, b, 0, 0))
    return pl.pallas_call(
        functools.partial(_attn_out_kernel, bt=bt, lt=lt, final=final),
        grid=(bn // bt, seq // lt),
        in_specs=[
            act(GDN_WIDTH), act(X_WIDTH), act(BRANCH_WIDTH), act(D_MODEL), mem, mem,
            pl.BlockSpec((BRANCH_WIDTH, D_MODEL), lambda b, l: (0, 0)),
            pl.BlockSpec((1, D_MODEL), lambda b, l: (0, 0)),
        ],
        out_specs=act(D_MODEL),
        out_shape=jax.ShapeDtypeStruct(x.shape, F32),
        scratch_shapes=[pltpu.VMEM((bt, lt, X_WIDTH), F32)],
        compiler_params=_cparams("arbitrary", "arbitrary"),
        name="attn_out",
    )(tok, xq, gate, x, mk, mv, w_bf16, fg)


SCONV_PROJ_ROUNDS = (0, 0, 1, 1, 2, 3, 4, 4)


def _attn_then_sconv_kernel(tok_ref, xq_ref, gate_ref, x_ref, kt_ref, vt_ref, w_out_ref,
                            fg_ref, g_ref, w_in_ref, hist_ref, cw_ref,
                            y_ref, tok_out, gate_out, xq_out, hist_out,
                            xo_scr, h_scr, carry, *, lt, tiles_per_seq):
    i = pl.program_id(0)
    slot = i % 2
    prev = jnp.maximum(i - 1, 0)

    @pl.when(i == 0)
    def _():
        h_scr[1] = jnp.zeros(h_scr.shape[1:], BF16)

    @pl.when(prev % tiles_per_seq == 0)
    def _():
        carry[:, SUBLANES - hist_ref.shape[1]:, :] = hist_ref[...]

    attn = _attn_out_tile(tok_ref, xq_ref, gate_ref, x_ref, kt_ref, vt_ref, w_out_ref, fg_ref,
                          y_ref, xo_scr, bt=1, lt=lt, final=False,
                          next_norm=(h_scr.at[slot], g_ref))
    proj = _inproj_chunks(h_scr.at[1 - slot], None, (w_in_ref, hist_ref, cw_ref, tok_out,
                                                     gate_out, xq_out, hist_out, carry),
                          mixer="sconv", bt=1, lt=lt)
    _round_robin([attn] + [_delayed(p, SCONV_PROJ_ROUNDS[k]) for k, p in enumerate(proj)])


def _attn_then_sconv(tok, xq, gate, x, mk, mv, w_out_bf16, fg, g, w_in, hist, cw,
                     *, layer, lt, act_dtype):
    bn, seq, _ = x.shape
    nl = seq // lt
    tiles = bn * nl
    width, cc = cw.shape

    def cur(i):
        return jnp.minimum(i, tiles - 1)

    def prev(i):
        return jnp.maximum(i - 1, 0)

    def tile(n, which):
        return pl.BlockSpec((1, lt, n), lambda i: (which(i), 0, 0))

    def whole(a):
        return pl.BlockSpec(a.shape, lambda i: (0,) * a.ndim)

    def tiled(a):
        return a.reshape(tiles, lt, a.shape[-1])

    mem = pl.BlockSpec((None, 1, X_WIDTH, N_MEM), lambda i: (layer, cur(i) // nl, 0, 0))
    hist_spec = pl.BlockSpec((1, width - 1, cc), lambda i: (prev(i) // nl, 0, 0))
    y, tok1, gate1, xq1, hist_new = pl.pallas_call(
        functools.partial(_attn_then_sconv_kernel, lt=lt, tiles_per_seq=nl),
        grid=(tiles + 1,),
        in_specs=[
            tile(GDN_WIDTH, cur), tile(X_WIDTH, cur), tile(BRANCH_WIDTH, cur), tile(D_MODEL, cur),
            mem, mem, whole(w_out_bf16), whole(fg), whole(g), whole(w_in), hist_spec, whole(cw)],
        out_specs=[
            tile(D_MODEL, cur), tile(cc, prev), tile(BRANCH_WIDTH, prev), tile(X_WIDTH, prev),
            hist_spec],
        out_shape=[
            jax.ShapeDtypeStruct((tiles, lt, D_MODEL), F32),
            jax.ShapeDtypeStruct((tiles, lt, cc), act_dtype),
            jax.ShapeDtypeStruct((tiles, lt, BRANCH_WIDTH), act_dtype),
            jax.ShapeDtypeStruct((tiles, lt, X_WIDTH), act_dtype),
            jax.ShapeDtypeStruct(hist.shape, F32)],
        scratch_shapes=[
            pltpu.VMEM((1, lt, X_WIDTH), F32),
            pltpu.VMEM((2, 1, lt, D_MODEL), BF16),
            pltpu.VMEM((1, SUBLANES, cc), F32)],
        compiler_params=_cparams("arbitrary"),
        name="attn_then_sconv",
    )(tiled(tok), tiled(xq), tiled(gate), tiled(x), mk, mv, w_out_bf16, fg, g, w_in, hist, cw)
    return (y.reshape(x.shape), tok1.reshape(bn, seq, -1), gate1.reshape(bn, seq, -1),
            xq1.reshape(bn, seq, -1), hist_new)


def _trunk(x, mem_k, mem_v, gdn_s, gdn_conv, sc_conv, p, *, bt, lt, chunk, gdn_bt, gdn_lt,
           attn_bt, attn_lt, act_dtype):
    if gdn_bt == 1 and bt == 1:
        gate, xq, tok, s_new, gconv_new = _gdn_layer(
            x, p["norm_g"][0:1], p["w_in_a"], gdn_s, gdn_conv, p["conv_w_a"], p["alog_pad"],
            p["dtb_pad"], p["o_norm_g"], chunk=chunk, lt=gdn_lt, act_dtype=act_dtype)
    else:
        qkv, gate, xq, ba, gconv_new = _inproj(
            x, p["norm_g"][0:1], p["w_in_a"], None, p["conv_w_a"], mixer="gdn", bt=bt, lt=lt,
            act_dtype=act_dtype)
        tok, s_new = _gdn(qkv, ba, gdn_s, gdn_conv, p["conv_w_a"], p["alog_pad"],
                          p["dtb_pad"], p["o_norm_g"], chunk=chunk, bt=gdn_bt, lt=gdn_lt)
    if attn_bt == 1 and bt == 1:
        x, tok, gate, xq, sconv_new = _attn_then_sconv(
            tok, xq, gate, x, mem_k, mem_v, p["w_out"][0], p["final_norm_g"], p["norm_g"][1:2],
            p["w_in_b"][0], sc_conv, p["conv_w_b"], layer=0, lt=attn_lt, act_dtype=act_dtype)
    else:
        x = _attn_out(tok, xq, gate, x, mem_k, mem_v, p["w_out"][0], p["final_norm_g"],
                      layer=0, bt=attn_bt, lt=attn_lt, final=False)
        tok, gate, xq, sconv_new = _inproj(
            x, p["norm_g"][1:2], p["w_in_b"], sc_conv, p["conv_w_b"], mixer="sconv", bt=bt,
            lt=lt, act_dtype=act_dtype)
    y = _attn_out(tok, xq, gate, x, mem_k, mem_v, p["w_out"][1], p["final_norm_g"],
                  layer=1, bt=attn_bt, lt=attn_lt, final=True)
    return y, s_new[None], gconv_new[None], sconv_new[None]


def kernel(x_prompt, x_sample, mem_prompt, state_gdn, state_gdn_conv, state_sconv, cache_mem_k, cache_mem_v, norm_g, w_in_a, conv_w_a, a_log, dt_bias, o_norm_g, w_in_b, conv_w_b, mem_norm_g, w_mem_kv, w_out, final_norm_g):
    bp = x_prompt.shape[0]

    wa = jnp.transpose(w_in_a[0])
    c_b = QKV_WIDTH
    c_g = c_b + 2 * GDN_HEADS
    c_x = c_g + BRANCH_WIDTH
    wa = [wa[:c_b], wa[c_g:c_x], wa[c_x:],
          jnp.concatenate([wa[c_b:c_g], jnp.zeros((BA_PAD - 2 * GDN_HEADS, D_MODEL), wa.dtype)])]
    wa = [w.astype(BF16) for w in wa]
    pad_lo = jnp.zeros((GDN_HEADS,), F32)
    pad_hi = jnp.zeros((LANES - 2 * GDN_HEADS,), F32)
    params = {
        "norm_g": norm_g,
        "w_in_a": wa,
        "conv_w_a": conv_w_a[0],
        "alog_pad": jnp.concatenate([pad_lo, a_log[0], pad_hi])[None],
        "dtb_pad": jnp.concatenate([pad_lo, dt_bias[0], pad_hi])[None],
        "o_norm_g": o_norm_g,
        "w_in_b": [w_in_b[0].astype(BF16)],
        "conv_w_b": conv_w_b[0],
        "w_out": w_out.astype(BF16),
        "final_norm_g": final_norm_g[None],
    }

    def to_cache(t):
        t = t.reshape(t.shape[0], t.shape[1], X_HEADS, X_HEAD_DIM, t.shape[3])
        return jnp.transpose(t, (0, 1, 4, 2, 3))

    def from_cache(t):
        t = jnp.transpose(t, (0, 1, 3, 4, 2))
        return t.reshape(t.shape[0], t.shape[1], X_WIDTH, t.shape[4])

    mem_kt, mem_vt = _memkv(mem_prompt, mem_norm_g[None], w_mem_kv.astype(BF16))
    mem_k_p = to_cache(mem_kt)
    mem_v_p = to_cache(mem_vt)

    s0_p = jnp.zeros((bp,) + state_gdn.shape[2:], F32)
    gc0_p = jnp.zeros((bp,) + state_gdn_conv.shape[2:], F32)
    sc0_p = jnp.zeros((bp,) + state_sconv.shape[2:], F32)
    y_p, s_p, gc_p, sc_p = _trunk(x_prompt, mem_kt, mem_vt, s0_p, gc0_p, sc0_p,
                                  params, bt=1, lt=1024, chunk=GDN_CHUNK, gdn_bt=1, gdn_lt=512,
                                  attn_bt=1, attn_lt=512, act_dtype=BF16)
    dec_seq = x_sample.shape[1]
    y_s, s_s, gc_s, sc_s = _trunk(x_sample, from_cache(cache_mem_k), from_cache(cache_mem_v),
                                  state_gdn[0], state_gdn_conv[0], state_sconv[0], params,
                                  bt=32, lt=dec_seq, chunk=dec_seq, gdn_bt=16, gdn_lt=dec_seq,
                                  attn_bt=16, attn_lt=dec_seq, act_dtype=F32)
    return (y_p, y_s, s_p, gc_p, sc_p, mem_k_p, mem_v_p, s_s, gc_s, sc_s)
```

```python
import functools

import jax
import jax.numpy as jnp
from jax import lax
from jax.experimental import pallas as pl
from jax.experimental.pallas import tpu as pltpu

F32 = jnp.float32
BF16 = jnp.bfloat16

D_MODEL = 1024
N_MEM = 256
X_WIDTH = 256
X_HEADS = 4
X_HEAD_DIM = 64
GDN_HEADS = 6
GDN_DK = 128
GDN_DV = 128
GDN_WIDTH = GDN_HEADS * GDN_DV
QK_WIDTH = GDN_HEADS * GDN_DK
QKV_WIDTH = 2 * QK_WIDTH + GDN_WIDTH
GDN_CHUNK = 64
SC_WIDTH = 768
BRANCH_WIDTH = 1024
EPS = 1e-6

LANES = 128
SUBLANES = 8
COL_CHUNK = 256
BA_PAD = LANES
VMEM_LIMIT = 56 * 1024 * 1024

HIGHEST = lax.Precision.HIGHEST


def _cparams(*sem):
    return pltpu.CompilerParams(dimension_semantics=sem, vmem_limit_bytes=VMEM_LIMIT)


def _rms_rows(x, g):
    r = lax.rsqrt(jnp.mean(x * x, axis=-1, keepdims=True) + EPS)
    return x * r * g


def _silu(x):
    h = 0.5 * x
    return h + h * jnp.tanh(h)


def _dot(a, b):
    return jnp.dot(a.astype(BF16), b.astype(BF16), preferred_element_type=F32)


def _dot_nt(a, b):
    return lax.dot_general(a.astype(BF16), b.astype(BF16), (((1,), (1,)), ((), ())),
                           preferred_element_type=F32)


def _dot_tn(a, b):
    return lax.dot_general(a.astype(BF16), b.astype(BF16), (((0,), (0,)), ((), ())),
                           preferred_element_type=F32)


def _dot_f32(a, b):
    return jnp.dot(a, b, preferred_element_type=F32, precision=HIGHEST)


def _col_chunks(n):
    return [slice(c, min(c + COL_CHUNK, n)) for c in range(0, n, COL_CHUNK)]


def _interleaved(gens):
    gens = list(gens)
    while gens:
        alive = []
        for g in gens:
            try:
                next(g)
                alive.append(g)
            except StopIteration:
                pass
        gens = alive
        yield


def _round_robin(gens):
    for _ in _interleaved(gens):
        pass


def _delayed(gen, rounds):
    for _ in range(rounds):
        yield
    yield from gen


def _memkv_kernel(m_ref, g_ref, w_ref, kt_ref, vt_ref):
    h = _rms_rows(m_ref[...], g_ref[...]).astype(BF16)
    for layer in range(w_ref.shape[0]):
        kv = jnp.dot(h, w_ref[layer], preferred_element_type=F32)
        kt_ref[layer] = kv[:, :X_WIDTH].T
        vt_ref[layer] = kv[:, X_WIDTH:].T


def _memkv(mem, g, w_bf16):
    bn, n_mem, _ = mem.shape
    depth = w_bf16.shape[0]
    out = jax.ShapeDtypeStruct((depth, bn, X_WIDTH, n_mem), F32)
    out_spec = pl.BlockSpec((depth, None, X_WIDTH, n_mem), lambda b: (0, b, 0, 0))
    return pl.pallas_call(
        _memkv_kernel,
        grid=(bn,),
        in_specs=[
            pl.BlockSpec((None, n_mem, D_MODEL), lambda b: (b, 0, 0)),
            pl.BlockSpec((1, D_MODEL), lambda b: (0, 0)),
            pl.BlockSpec(w_bf16.shape, lambda b: (0, 0, 0)),
        ],
        out_specs=[out_spec, out_spec],
        out_shape=[out, out],
        compiler_params=_cparams("arbitrary"),
        name="memkv",
    )(mem, g, w_bf16)


def _init_conv_carry(carry, hist_ref):
    @pl.when(pl.program_id(1) == 0)
    def _():
        carry[:, SUBLANES - hist_ref.shape[1]:, :] = hist_ref[...]


def _causal_conv_cols(x, cols, carry, hist_out, cw_ref, seqs=slice(None)):
    width = cw_ref.shape[0]
    bt, lt, n = x.shape
    g = lt // SUBLANES
    xe = jnp.concatenate([carry[seqs, :, cols], x], axis=1).reshape(bt, g + 1, SUBLANES, n)
    sub = lax.broadcasted_iota(jnp.int32, (1, 1, SUBLANES, n), 2)
    y = x.reshape(bt, g, SUBLANES, n) * cw_ref[width - 1:width, cols]
    for j in range(width - 1):
        s = width - 1 - j
        r = pltpu.roll(xe, s, axis=2)
        y = y + jnp.where(sub < s, r[:, :g], r[:, 1:]) * cw_ref[j:j + 1, cols]
    carry[seqs, :, cols] = x[:, lt - SUBLANES:, :]
    if hist_out is not None:
        hist_out[seqs, :, cols] = x[:, lt - (width - 1):, :]
    return y.reshape(bt, lt, n)


def _inproj_chunks(x_ref, g_ref, rest, *, mixer, bt, lt):
    tm = bt * lt
    if g_ref is None:
        h = x_ref[...].reshape(tm, D_MODEL)
    else:
        h = _rms_rows(x_ref[...].reshape(tm, D_MODEL), g_ref[...]).astype(BF16)
    if mixer == "gdn":
        n_out = (len(rest) - 1) // 2
        w_refs, (mix_out, *plain_outs, hist_out) = rest[:n_out], rest[n_out:]

        def proj(i, cols):
            return _dot_nt(h, w_refs[i][cols, :]).reshape(bt, lt, -1)
    else:
        w_ref, _, cw_ref, mix_out, *plain_outs, hist_out, carry = rest
        bases = [0, 3 * SC_WIDTH]
        for out in plain_outs[:-1]:
            bases.append(bases[-1] + out.shape[-1])

        def proj(i, cols, offset=0):
            cols = slice(bases[i] + offset + cols.start, bases[i] + offset + cols.stop)
            return jnp.dot(h, w_ref[:, cols], preferred_element_type=F32).reshape(bt, lt, -1)

    def mixer_chunk(cols):
        if mixer == "gdn":
            y = proj(0, cols)
            hist_out[:, :, cols] = y[:, lt - hist_out.shape[1]:, :]
        else:
            gate_b = proj(0, cols)
            pre = proj(0, cols, SC_WIDTH) * proj(0, cols, 2 * SC_WIDTH)
            yield
            y = gate_b * _causal_conv_cols(pre, cols, carry, hist_out, cw_ref)
        mix_out[:, :, cols] = y.astype(mix_out.dtype)
        yield

    def plain_chunk(i, out, cols):
        out[:, :, cols] = proj(i, cols).astype(out.dtype)
        yield

    gens = [mixer_chunk(cols) for cols in _col_chunks(mix_out.shape[-1])]
    for i, out in enumerate(plain_outs):
        gens += [plain_chunk(i + 1, out, cols) for cols in _col_chunks(out.shape[-1])]
    return gens


def _inproj_kernel(x_ref, g_ref, *rest, mixer, bt, lt):
    if mixer == "sconv":
        _init_conv_carry(rest[-1], rest[1])
    gens = _inproj_chunks(x_ref, g_ref, rest, mixer=mixer, bt=bt, lt=lt)
    _round_robin([_delayed(g, i) for i, g in enumerate(gens)])


def _inproj(x, g, ws, hist, cw, *, mixer, bt, lt, act_dtype):
    bn, seq, _ = x.shape
    width, cc = cw.shape

    def act(n, dtype=act_dtype):
        return (pl.BlockSpec((bt, lt, n), lambda b, l: (b, l, 0)),
                jax.ShapeDtypeStruct((bn, seq, n), dtype))

    hist_spec = pl.BlockSpec((bt, width - 1, cc), lambda b, l: (b, 0, 0))
    args = [x, g, *ws]
    in_specs = [
        pl.BlockSpec((bt, lt, D_MODEL), lambda b, l: (b, l, 0)),
        pl.BlockSpec((1, D_MODEL), lambda b, l: (0, 0)),
    ] + [pl.BlockSpec(w.shape, lambda b, l: (0, 0)) for w in ws]
    outs = [act(cc), act(BRANCH_WIDTH), act(X_WIDTH)]
    scratch = []
    if mixer == "gdn":
        outs.append(act(BA_PAD, F32))
    else:
        args += [hist, cw]
        in_specs += [hist_spec, pl.BlockSpec((width, cc), lambda b, l: (0, 0))]
        scratch.append(pltpu.VMEM((bt, SUBLANES, cc), F32))
    outs.append((hist_spec, jax.ShapeDtypeStruct((bn, width - 1, cc), F32)))
    return pl.pallas_call(
        functools.partial(_inproj_kernel, mixer=mixer, bt=bt, lt=lt),
        grid=(bn // bt, seq // lt),
        in_specs=in_specs,
        out_specs=[o[0] for o in outs],
        out_shape=[o[1] for o in outs],
        scratch_shapes=scratch,
        compiler_params=_cparams("arbitrary", "arbitrary"),
        name="inproj_" + mixer,
    )(*args)


CHUNKS_PER_WAVE = 4
SEQS_PER_WAVE = 8


def _gdn_tile(qkv_ref, ba_ref, cw_ref, alog_ref, dtb_ref, og_ref, o_ref, s_ref, carry,
              *, chunk, group):
    bt, lt, _ = qkv_ref.shape
    nchunks = lt // chunk
    heads = range(GDN_HEADS)
    groups = [tuple(range(g, g + group)) for g in range(0, GDN_HEADS, group)]
    gw = group * chunk
    row = lax.broadcasted_iota(jnp.int32, (chunk, gw), 0)
    col = lax.broadcasted_iota(jnp.int32, (chunk, gw), 1) % chunk
    tril = row >= col
    strict = row > col
    lane_blk = lax.broadcasted_iota(jnp.int32, (1, gw), 1) // chunk
    ones_tril = (lax.broadcasted_iota(jnp.int32, (chunk, chunk), 0)
                 >= lax.broadcasted_iota(jnp.int32, (chunk, chunk), 1)).astype(F32)
    neg_a = -jnp.exp(alog_ref[...])
    dtb = dtb_ref[...]
    og = og_ref[...]

    def rows(c):
        return slice(c * chunk, (c + 1) * chunk)

    def lanes(base, h, n):
        return slice(base + h * n, base + (h + 1) * n)

    def cat(xs, axis):
        return xs[0] if len(xs) == 1 else jnp.concatenate(xs, axis=axis)

    def pick(xs):
        out = xs[0]
        for j in range(1, len(xs)):
            out = jnp.where(lane_blk == j, xs[j], out)
        return out

    def blockdiag(xs):
        z = jnp.zeros_like(xs[0])
        return cat([cat([x if i == j else z for i in range(len(xs))], 1)
                    for j, x in enumerate(xs)], 0)

    def blockdiag_of_lane_blocks(x):
        if group == 1:
            return x
        return cat([jnp.where(lane_blk == j, x, 0.0) for j in range(group)], 0)

    def conv_silu(u, cols):
        b, c = u
        x = qkv_ref[b, rows(c), cols].astype(F32)
        y = _causal_conv_cols(x[None], cols, carry, None, cw_ref, seqs=slice(b, b + 1))
        return _silu(y[0])

    def phase1(units, res):
        uh = [(u, h) for u in units for h in heads]
        ug = [(u, g) for u in units for g in range(len(groups))]
        gcum, gcum_t, beta_all = {}, {}, {}
        for u in units:
            ba = ba_ref[u[0], rows(u[1]), :]
            beta_all[u] = 1.0 / (1.0 + jnp.exp(-ba))
            z = ba + dtb
            softplus = jnp.maximum(z, 0.0) + jnp.log1p(jnp.exp(-jnp.abs(z)))
            gcum[u] = _dot_f32(ones_tril, neg_a * softplus)
        q, k, v = {}, {}, {}
        for u in units:
            for h in heads:
                q[u, h] = conv_silu(u, lanes(0, h, GDN_DK))
                k[u, h] = conv_silu(u, lanes(QK_WIDTH, h, GDN_DK))
                v[u, h] = conv_silu(u, lanes(2 * QK_WIDTH, h, GDN_DV))
            yield
        for key in uh:
            x = q[key]
            q[key] = x * (lax.rsqrt(jnp.sum(x * x, axis=-1, keepdims=True) + EPS)
                          * (GDN_DK ** -0.5))
            x = k[key]
            k[key] = x * lax.rsqrt(jnp.sum(x * x, axis=-1, keepdims=True) + EPS)
        yield
        for u in units:
            gcum_t[u] = cat([gcum[u]] * group, 0).T
        gc = {(u, h): jnp.broadcast_to(gcum[u][:, GDN_HEADS + h:GDN_HEADS + h + 1],
                                       (chunk, GDN_DK)) for u, h in uh}
        gl = {(u, h): gcum[u][chunk - 1:chunk, GDN_HEADS + h:GDN_HEADS + h + 1]
              for u, h in uh}
        beta = {(u, h): jnp.broadcast_to(beta_all[u][:, h:h + 1], (chunk, GDN_DK))
                for u, h in uh}
        eg = {key: jnp.exp(gc[key]) for key in uh}
        kb = {key: k[key] * beta[key] for key in uh}
        kkqk = {}
        for u, g in ug:
            grp = groups[g]
            kkqk[u, g] = _dot_nt(
                cat([cat([kb[u, h] for h in grp], 1), cat([q[u, h] for h in grp], 1)], 0),
                blockdiag([k[u, h] for h in grp]))
        yield
        n, p, aqk = {}, {}, {}
        for u, g in ug:
            grp = groups[g]
            g_col = pick([gc[u, h][:, :gw] for h in grp])
            g_row = pick([gcum_t[u][GDN_HEADS + h:GDN_HEADS + h + 1, :] for h in grp])
            decay = jnp.exp(jnp.where(tril, g_col - g_row, -jnp.inf))
            a = jnp.where(strict, kkqk[u, g][:chunk] * decay, 0.0)
            aqk[u, g] = (kkqk[u, g][chunk:] * decay).astype(BF16)
            n[u, g] = -a
            p[u, g] = _dot(a, blockdiag_of_lane_blocks(a))
        yield
        span = 2
        while 2 * span < chunk:
            for key in ug:
                both = _dot(cat([n[key], p[key]], 0), blockdiag_of_lane_blocks(p[key]))
                n[key] = n[key] + p[key] + both[:chunk]
                p[key] = both[chunk:]
            span *= 2
            yield
        for key in ug:
            n[key] = n[key] + p[key] + _dot(n[key], blockdiag_of_lane_blocks(p[key]))
        yield
        wu = {}
        for u, h in uh:
            rhs = jnp.concatenate([kb[u, h] * eg[u, h], v[u, h] * beta[u, h]], axis=-1)
            wu[u, h] = rhs + _dot(n[u, h // group][:, lanes(0, h % group, chunk)], rhs)
        yield
        for u in units:
            res[u] = dict(
                w_qd=[jnp.concatenate([wu[u, h][:, :GDN_DK], q[u, h] * eg[u, h]],
                                      axis=0).astype(BF16) for h in heads],
                u=[wu[u, h][:, GDN_DK:] for h in heads],
                kd=[(k[u, h] * jnp.exp(gl[u, h] - gc[u, h])).astype(BF16) for h in heads],
                aqk=[aqk[u, g] for g in range(len(groups))],
                egl=[jnp.exp(gl[u, h]) for h in heads])
        yield

    def phase2(units, res):
        for c in sorted({c for _, c in units}):
            us = [u for u in units if u[1] == c]
            uh = [(u, h) for u in us for h in heads]
            s = {(u, h): s_ref[u[0], h] for u, h in uh}
            ws_qs = {(u, h): _dot(res[u]["w_qd"][h], s[u, h]) for u, h in uh}
            yield
            v_new = {(u, h): res[u]["u"][h] - ws_qs[u, h][:chunk] for u, h in uh}
            o_grp = {(u, g): _dot(res[u]["aqk"][g], blockdiag([v_new[u, h] for h in grp]))
                     for u in us for g, grp in enumerate(groups)}
            for u, h in uh:
                s_ref[u[0], h] = (s[u, h] * res[u]["egl"][h]
                                  + _dot_tn(res[u]["kd"][h], v_new[u, h]))
            yield
            for u, h in uh:
                oh = ws_qs[u, h][chunk:] + o_grp[u, h // group][:, lanes(0, h % group, GDN_DV)]
                oh = oh * lax.rsqrt(jnp.mean(oh * oh, axis=-1, keepdims=True) + EPS) * og
                o_ref[u[0], rows(c), lanes(0, h, GDN_DV)] = oh.astype(o_ref.dtype)
            yield

    units = [(b, c) for b in range(bt) for c in range(nchunks)]
    per_wave = CHUNKS_PER_WAVE if nchunks > 1 else SEQS_PER_WAVE
    waves = [units[i:i + per_wave] for i in range(0, len(units), per_wave)]
    res = {}
    prev = None
    for wave in waves:
        gens = [phase1(wave, res)]
        if prev is not None:
            gens.append(phase2(prev, res))
        yield from _interleaved(gens)
        prev = wave
    yield from phase2(prev, res)


def _gdn_kernel(qkv_ref, ba_ref, s0_ref, hist_ref, cw_ref, alog_ref, dtb_ref, og_ref,
                o_ref, s_ref, carry, *, chunk, group):
    @pl.when(pl.program_id(1) == 0)
    def _():
        s_ref[...] = s0_ref[...]

    _init_conv_carry(carry, hist_ref)
    _round_robin([_gdn_tile(qkv_ref, ba_ref, cw_ref, alog_ref, dtb_ref, og_ref, o_ref, s_ref,
                            carry, chunk=chunk, group=group)])


def _gdn(qkv, ba, s0, hist, cw, alog_pad, dtb_pad, og, *, chunk, bt, lt):
    group = 2 if 2 * chunk == LANES else 1
    bn, seq, _ = qkv.shape
    width, cc = cw.shape
    s_spec = pl.BlockSpec((bt, GDN_HEADS, GDN_DK, GDN_DV), lambda b, l: (b, 0, 0, 0))
    hist_spec = pl.BlockSpec((bt, width - 1, cc), lambda b, l: (b, 0, 0))
    vec = pl.BlockSpec((1, LANES), lambda b, l: (0, 0))
    return pl.pallas_call(
        functools.partial(_gdn_kernel, chunk=chunk, group=group),
        grid=(bn // bt, seq // lt),
        in_specs=[
            pl.BlockSpec((bt, lt, QKV_WIDTH), lambda b, l: (b, l, 0)),
            pl.BlockSpec((bt, lt, BA_PAD), lambda b, l: (b, l, 0)),
            s_spec, hist_spec,
            pl.BlockSpec((width, cc), lambda b, l: (0, 0)),
            vec, vec, vec,
        ],
        out_specs=[
            pl.BlockSpec((bt, lt, GDN_WIDTH), lambda b, l: (b, l, 0)),
            s_spec,
        ],
        out_shape=[
            jax.ShapeDtypeStruct((bn, seq, GDN_WIDTH), qkv.dtype),
            jax.ShapeDtypeStruct(s0.shape, F32),
        ],
        scratch_shapes=[pltpu.VMEM((bt, SUBLANES, cc), F32)],
        compiler_params=_cparams("arbitrary", "arbitrary"),
        name="gdn",
    )(qkv, ba, s0, hist, cw, alog_pad, dtb_pad, og)


PROJ_ROUNDS = ((0, 5), (14, 5), (28, 1), (30, 1), (32, 1), (34, 1), (36, 1))


def _gdn_layer_kernel(x_ref, g_ref, w_qkv, w_gate, w_xq, w_ba, s0_ref, hist_ref, cw_ref,
                      alog_ref, dtb_ref, og_ref, gate_out, xq_out, hist_out, o_ref, s_ref,
                      qkv_scr, ba_scr, carry, *, lt, tiles_per_seq, chunk, group):
    i = pl.program_id(0)
    slot = i % 2
    prev = jnp.maximum(i - 1, 0)

    @pl.when(i == 0)
    def _():
        qkv_scr[1] = jnp.zeros(qkv_scr.shape[1:], F32)
        ba_scr[1] = jnp.zeros(ba_scr.shape[1:], F32)

    @pl.when(prev % tiles_per_seq == 0)
    def _():
        s_ref[...] = s0_ref[...]
        carry[:, SUBLANES - hist_ref.shape[1]:, :] = hist_ref[...]

    proj = _inproj_chunks(
        x_ref, g_ref, (w_qkv, w_gate, w_xq, w_ba, qkv_scr.at[slot], gate_out, xq_out,
                       ba_scr.at[slot], hist_out), mixer="gdn", bt=1, lt=lt)
    rule = _gdn_tile(qkv_scr.at[1 - slot], ba_scr.at[1 - slot], cw_ref, alog_ref, dtb_ref,
                     og_ref, o_ref, s_ref, carry, chunk=chunk, group=group)
    starts = [r for w, n in PROJ_ROUNDS for r in range(w, w + n)]
    _round_robin([rule] + [_delayed(p, starts[k]) for k, p in enumerate(proj)])


def _gdn_layer(x, g, ws, s0, hist, cw, alog_pad, dtb_pad, og, *, chunk, lt, act_dtype):
    group = 2 if 2 * chunk == LANES else 1
    bn, seq, _ = x.shape
    nl = seq // lt
    tiles = bn * nl
    width, cc = cw.shape

    def cur(i):
        return jnp.minimum(i, tiles - 1)

    def prev(i):
        return jnp.maximum(i - 1, 0)

    def tile(n, which):
        return pl.BlockSpec((1, lt, n), lambda i: (which(i), 0, 0))

    def per_seq(shape, which):
        return pl.BlockSpec((1,) + shape, lambda i: (which(i) // nl,) + (0,) * len(shape))

    def whole(a):
        return pl.BlockSpec(a.shape, lambda i: (0,) * a.ndim)

    state = (GDN_HEADS, GDN_DK, GDN_DV)
    gate, xq, hist_new, tok, s_new = pl.pallas_call(
        functools.partial(_gdn_layer_kernel, lt=lt, tiles_per_seq=nl, chunk=chunk, group=group),
        grid=(tiles + 1,),
        in_specs=[tile(D_MODEL, cur), whole(g)] + [whole(w) for w in ws] + [
            per_seq(state, prev), per_seq((width - 1, cc), prev), whole(cw),
            whole(alog_pad), whole(dtb_pad), whole(og)],
        out_specs=[
            tile(BRANCH_WIDTH, cur), tile(X_WIDTH, cur), per_seq((width - 1, cc), cur),
            tile(GDN_WIDTH, prev), per_seq(state, prev)],
        out_shape=[
            jax.ShapeDtypeStruct((tiles, lt, BRANCH_WIDTH), act_dtype),
            jax.ShapeDtypeStruct((tiles, lt, X_WIDTH), act_dtype),
            jax.ShapeDtypeStruct(hist.shape, F32),
            jax.ShapeDtypeStruct((tiles, lt, GDN_WIDTH), act_dtype),
            jax.ShapeDtypeStruct(s0.shape, F32)],
        scratch_shapes=[
            pltpu.VMEM((2, 1, lt, cc), F32),
            pltpu.VMEM((2, 1, lt, BA_PAD), F32),
            pltpu.VMEM((1, SUBLANES, cc), F32)],
        compiler_params=_cparams("arbitrary"),
        name="gdn_layer",
    )(x.reshape(tiles, lt, D_MODEL), g, *ws, s0, hist, cw, alog_pad, dtb_pad, og)
    return (gate.reshape(bn, seq, -1), xq.reshape(bn, seq, -1), tok.reshape(bn, seq, -1),
            s_new, hist_new)


ATTN_ROWS = 256
ATTN_SEQS = 8


def _attn_out_tile(tok_ref, xq_ref, gate_ref, x_ref, kt_ref, vt_ref, w_ref, fg_ref,
                   y_ref, xo_scr, *, bt, lt, final, next_norm=None):
    lane_head = lax.broadcasted_iota(jnp.int32, (1, X_WIDTH), 1) // X_HEAD_DIM
    scale = X_HEAD_DIM ** -0.5
    tm = bt * lt

    def attend(b, rs):
        r = rs.stop - rs.start
        q = xq_ref[b, rs, :]
        q = q * jnp.asarray(scale, q.dtype)
        qx = jnp.concatenate([jnp.where(lane_head == h, q, jnp.zeros_like(q))
                              for h in range(X_HEADS)], axis=0)
        s = _dot(qx, kt_ref[b])
        yield
        e = jnp.exp(s - jnp.max(s, axis=-1, keepdims=True))
        p = e / jnp.sum(e, axis=-1, keepdims=True)
        o4 = _dot_nt(p, vt_ref[b])
        yield
        xo = jnp.where(lane_head == 0, o4[0:r], 0.0)
        for h in range(1, X_HEADS):
            xo = xo + jnp.where(lane_head == h, o4[h * r:(h + 1) * r], 0.0)
        xo_scr[b, rs, :] = xo
        yield

    acc = {}

    def tok_proj():
        sg = _silu(gate_ref[:, :, :GDN_WIDTH].astype(F32).reshape(tm, GDN_WIDTH))
        br = (tok_ref[...].astype(F32).reshape(tm, GDN_WIDTH) * sg).astype(BF16)
        for cols in _col_chunks(D_MODEL):
            acc[cols.start] = (x_ref[:, :, cols].reshape(tm, -1)
                               + _dot(br, w_ref[:GDN_WIDTH, cols]))
            yield

    if bt == 1:
        blocks = [slice(r, r + ATTN_ROWS) for r in range(0, lt, ATTN_ROWS)]
        yield from _interleaved([attend(0, rs) for rs in blocks] + [tok_proj()])
    else:
        def seq_group(i, carry):
            _round_robin([attend(i * ATTN_SEQS + j, slice(0, lt)) for j in range(ATTN_SEQS)])
            return carry

        lax.fori_loop(0, bt // ATTN_SEQS, seq_group, 0)
        yield from tok_proj()

    sg_x = _silu(gate_ref[:, :, GDN_WIDTH:].astype(F32).reshape(tm, X_WIDTH))
    br_x = (xo_scr[...].reshape(tm, X_WIDTH) * sg_x).astype(BF16)
    y = [acc[cols.start] + _dot(br_x, w_ref[GDN_WIDTH:, cols]) for cols in _col_chunks(D_MODEL)]
    if final or next_norm is not None:
        ms = sum(jnp.sum(c * c, axis=-1, keepdims=True) for c in y) * (1.0 / D_MODEL)
        r = lax.rsqrt(ms + EPS)
    if next_norm is not None:
        h_ref, g_ref = next_norm
        for c, cols in zip(y, _col_chunks(D_MODEL)):
            h_ref[:, :, cols] = (c * r * g_ref[:, cols]).astype(h_ref.dtype).reshape(bt, lt, -1)
    if final:
        y = [c * r * fg_ref[:, cols] for c, cols in zip(y, _col_chunks(D_MODEL))]
    for c, cols in zip(y, _col_chunks(D_MODEL)):
        y_ref[:, :, cols] = c.reshape(bt, lt, -1)
    yield


def _attn_out_kernel(tok_ref, xq_ref, gate_ref, x_ref, kt_ref, vt_ref, w_ref, fg_ref,
                     y_ref, xo_scr, *, bt, lt, final):
    _round_robin([_attn_out_tile(tok_ref, xq_ref, gate_ref, x_ref, kt_ref, vt_ref, w_ref,
                                 fg_ref, y_ref, xo_scr, bt=bt, lt=lt, final=final)])


def _attn_out(tok, xq, gate, x, mk, mv, w_bf16, fg, *, layer, bt, lt, final):
    bn, seq, _ = x.shape

    def act(n):
        return pl.BlockSpec((bt, lt, n), lambda b, l: (b, l, 0))

    mem = pl.BlockSpec((None, bt, X_WIDTH, N_MEM), lambda b, l: (layer, b, 0, 0))
    return pl.pallas_call(
        functools.partial(_attn_out_kernel, bt=bt, lt=lt, final=final),
        grid=(bn // bt, seq // lt),
        in_specs=[
            act(GDN_WIDTH), act(X_WIDTH), act(BRANCH_WIDTH), act(D_MODEL), mem, mem,
            pl.BlockSpec((BRANCH_WIDTH, D_MODEL), lambda b, l: (0, 0)),
            pl.BlockSpec((1, D_MODEL), lambda b, l: (0, 0)),
        ],
        out_specs=act(D_MODEL),
        out_shape=jax.ShapeDtypeStruct(x.shape, F32),
        scratch_shapes=[pltpu.VMEM((bt, lt, X_WIDTH), F32)],
        compiler_params=_cparams("arbitrary", "arbitrary"),
        name="attn_out",
    )(tok, xq, gate, x, mk, mv, w_bf16, fg)


SCONV_PROJ_ROUNDS = (0, 0, 1, 1, 2, 3, 4, 4)


def _tail_kernel(tok_ref, xq_ref, gate_ref, x_ref, kt0_ref, vt0_ref, kt1_ref, vt1_ref,
                 w_out0_ref, w_out1_ref, fg_ref, g_ref, w_in_ref, hist_ref, cw_ref,
                 y_ref, hist_out,
                 xo0_scr, xo1_scr, h_scr, x1_scr, tok1_scr, gate1_scr, xq1_scr, carry,
                 *, lt, tiles_per_seq):
    i = pl.program_id(0)
    a2, b2, c2 = i % 2, (i + 1) % 2, i % 2
    a3, c3 = i % 3, (i + 1) % 3
    prev = jnp.maximum(i - 1, 0)

    @pl.when(i == 0)
    def _():
        h_scr[1] = jnp.zeros(h_scr.shape[1:], BF16)
        x1_scr[1] = jnp.zeros(x1_scr.shape[1:], F32)
        x1_scr[2] = jnp.zeros(x1_scr.shape[1:], F32)
        tok1_scr[0] = jnp.zeros(tok1_scr.shape[1:], BF16)
        gate1_scr[0] = jnp.zeros(gate1_scr.shape[1:], BF16)
        xq1_scr[0] = jnp.zeros(xq1_scr.shape[1:], BF16)

    @pl.when(prev % tiles_per_seq == 0)
    def _():
        carry[:, SUBLANES - hist_ref.shape[1]:, :] = hist_ref[...]

    attn0 = _attn_out_tile(tok_ref, xq_ref, gate_ref, x_ref, kt0_ref, vt0_ref, w_out0_ref,
                           fg_ref, x1_scr.at[a3], xo0_scr, bt=1, lt=lt, final=False,
                           next_norm=(h_scr.at[a2], g_ref))
    proj = _inproj_chunks(h_scr.at[1 - a2], None,
                          (w_in_ref, hist_ref, cw_ref, tok1_scr.at[b2], gate1_scr.at[b2],
                           xq1_scr.at[b2], hist_out, carry), mixer="sconv", bt=1, lt=lt)
    attn1 = _attn_out_tile(tok1_scr.at[c2], xq1_scr.at[c2], gate1_scr.at[c2], x1_scr.at[c3],
                           kt1_ref, vt1_ref, w_out1_ref, fg_ref, y_ref, xo1_scr,
                           bt=1, lt=lt, final=True)
    _round_robin([attn0, _delayed(attn1, 4)]
                 + [_delayed(p, SCONV_PROJ_ROUNDS[k]) for k, p in enumerate(proj)])


def _tail(tok, xq, gate, x, mk, mv, w_out_bf16, fg, g, w_in, hist, cw, *, lt):
    bn, seq, _ = x.shape
    nl = seq // lt
    tiles = bn * nl
    width, cc = cw.shape

    def clamp(i, lag):
        return jnp.clip(i - lag, 0, tiles - 1)

    def tile(n, lag):
        return pl.BlockSpec((1, lt, n), lambda i: (clamp(i, lag), 0, 0))

    def whole(a):
        return pl.BlockSpec(a.shape, lambda i: (0,) * a.ndim, pipeline_mode=pl.Buffered(1))

    def mem(layer, lag):
        return pl.BlockSpec((None, 1, X_WIDTH, N_MEM),
                            lambda i: (layer, clamp(i, lag) // nl, 0, 0))

    def tiled(a):
        return a.reshape(tiles, lt, a.shape[-1])

    def slots(n, width_, dtype):
        return pltpu.VMEM((n, 1, lt, width_), dtype)

    hist_spec = pl.BlockSpec((1, width - 1, cc), lambda i: (clamp(i, 1) // nl, 0, 0))
    y, hist_new = pl.pallas_call(
        functools.partial(_tail_kernel, lt=lt, tiles_per_seq=nl),
        grid=(tiles + 2,),
        in_specs=[
            tile(GDN_WIDTH, 0), tile(X_WIDTH, 0), tile(BRANCH_WIDTH, 0), tile(D_MODEL, 0),
            mem(0, 0), mem(0, 0), mem(1, 2), mem(1, 2),
            whole(w_out_bf16[0]), whole(w_out_bf16[1]), whole(fg), whole(g), whole(w_in),
            hist_spec, whole(cw)],
        out_specs=[tile(D_MODEL, 2), hist_spec],
        out_shape=[
            jax.ShapeDtypeStruct((tiles, lt, D_MODEL), F32),
            jax.ShapeDtypeStruct(hist.shape, F32)],
        scratch_shapes=[
            pltpu.VMEM((1, lt, X_WIDTH), F32), pltpu.VMEM((1, lt, X_WIDTH), F32),
            slots(2, D_MODEL, BF16), slots(3, D_MODEL, F32),
            slots(2, cc, BF16), slots(2, BRANCH_WIDTH, BF16), slots(2, X_WIDTH, BF16),
            pltpu.VMEM((1, SUBLANES, cc), F32)],
        compiler_params=_cparams("arbitrary"),
        name="tail",
    )(tiled(tok), tiled(xq), tiled(gate), tiled(x), mk, mv, mk, mv,
      w_out_bf16[0], w_out_bf16[1], fg, g, w_in, hist, cw)
    return y.reshape(x.shape), hist_new


def _trunk(x, mem_k, mem_v, gdn_s, gdn_conv, sc_conv, p, *, bt, lt, chunk, gdn_bt, gdn_lt,
           attn_bt, attn_lt, act_dtype):
    if gdn_bt == 1 and bt == 1:
        gate, xq, tok, s_new, gconv_new = _gdn_layer(
            x, p["norm_g"][0:1], p["w_in_a"], gdn_s, gdn_conv, p["conv_w_a"], p["alog_pad"],
            p["dtb_pad"], p["o_norm_g"], chunk=chunk, lt=gdn_lt, act_dtype=act_dtype)
    else:
        qkv, gate, xq, ba, gconv_new = _inproj(
            x, p["norm_g"][0:1], p["w_in_a"], None, p["conv_w_a"], mixer="gdn", bt=bt, lt=lt,
            act_dtype=act_dtype)
        tok, s_new = _gdn(qkv, ba, gdn_s, gdn_conv, p["conv_w_a"], p["alog_pad"],
                          p["dtb_pad"], p["o_norm_g"], chunk=chunk, bt=gdn_bt, lt=gdn_lt)
    if attn_bt == 1 and bt == 1:
        y, sconv_new = _tail(tok, xq, gate, x, mem_k, mem_v, p["w_out"], p["final_norm_g"],
                             p["norm_g"][1:2], p["w_in_b"][0], sc_conv, p["conv_w_b"],
                             lt=attn_lt)
    else:
        x = _attn_out(tok, xq, gate, x, mem_k, mem_v, p["w_out"][0], p["final_norm_g"],
                      layer=0, bt=attn_bt, lt=attn_lt, final=False)
        tok, gate, xq, sconv_new = _inproj(
            x, p["norm_g"][1:2], p["w_in_b"], sc_conv, p["conv_w_b"], mixer="sconv", bt=bt,
            lt=lt, act_dtype=act_dtype)
        y = _attn_out(tok, xq, gate, x, mem_k, mem_v, p["w_out"][1], p["final_norm_g"],
                      layer=1, bt=attn_bt, lt=attn_lt, final=True)
    return y, s_new[None], gconv_new[None], sconv_new[None]


def kernel(x_prompt, x_sample, mem_prompt, state_gdn, state_gdn_conv, state_sconv, cache_mem_k, cache_mem_v, norm_g, w_in_a, conv_w_a, a_log, dt_bias, o_norm_g, w_in_b, conv_w_b, mem_norm_g, w_mem_kv, w_out, final_norm_g):
    bp = x_prompt.shape[0]

    wa = jnp.transpose(w_in_a[0])
    c_b = QKV_WIDTH
    c_g = c_b + 2 * GDN_HEADS
    c_x = c_g + BRANCH_WIDTH
    wa = [wa[:c_b], wa[c_g:c_x], wa[c_x:],
          jnp.concatenate([wa[c_b:c_g], jnp.zeros((BA_PAD - 2 * GDN_HEADS, D_MODEL), wa.dtype)])]
    wa = [w.astype(BF16) for w in wa]
    pad_lo = jnp.zeros((GDN_HEADS,), F32)
    pad_hi = jnp.zeros((LANES - 2 * GDN_HEADS,), F32)
    params = {
        "norm_g": norm_g,
        "w_in_a": wa,
        "conv_w_a": conv_w_a[0],
        "alog_pad": jnp.concatenate([pad_lo, a_log[0], pad_hi])[None],
        "dtb_pad": jnp.concatenate([pad_lo, dt_bias[0], pad_hi])[None],
        "o_norm_g": o_norm_g,
        "w_in_b": [w_in_b[0].astype(BF16)],
        "conv_w_b": conv_w_b[0],
        "w_out": w_out.astype(BF16),
        "final_norm_g": final_norm_g[None],
    }

    def to_cache(t):
        t = t.reshape(t.shape[0], t.shape[1], X_HEADS, X_HEAD_DIM, t.shape[3])
        return jnp.transpose(t, (0, 1, 4, 2, 3))

    def from_cache(t):
        t = jnp.transpose(t, (0, 1, 3, 4, 2))
        return t.reshape(t.shape[0], t.shape[1], X_WIDTH, t.shape[4])

    mem_kt, mem_vt = _memkv(mem_prompt, mem_norm_g[None], w_mem_kv.astype(BF16))
    mem_k_p = to_cache(mem_kt)
    mem_v_p = to_cache(mem_vt)

    s0_p = jnp.zeros((bp,) + state_gdn.shape[2:], F32)
    gc0_p = jnp.zeros((bp,) + state_gdn_conv.shape[2:], F32)
    sc0_p = jnp.zeros((bp,) + state_sconv.shape[2:], F32)
    y_p, s_p, gc_p, sc_p = _trunk(x_prompt, mem_kt, mem_vt, s0_p, gc0_p, sc0_p,
                                  params, bt=1, lt=1024, chunk=GDN_CHUNK, gdn_bt=1, gdn_lt=512,
                                  attn_bt=1, attn_lt=512, act_dtype=BF16)
    dec_seq = x_sample.shape[1]
    y_s, s_s, gc_s, sc_s = _trunk(x_sample, from_cache(cache_mem_k), from_cache(cache_mem_v),
                                  state_gdn[0], state_gdn_conv[0], state_sconv[0], params,
                                  bt=32, lt=dec_seq, chunk=dec_seq, gdn_bt=16, gdn_lt=dec_seq,
                                  attn_bt=16, attn_lt=dec_seq, act_dtype=F32)
    return (y_p, y_s, s_p, gc_p, sc_p, mem_k_p, mem_v_p, s_s, gc_s, sc_s)
```

```python
import functools

import jax
import jax.numpy as jnp
from jax import lax
from jax.experimental import pallas as pl
from jax.experimental.pallas import tpu as pltpu

F32 = jnp.float32
BF16 = jnp.bfloat16

D_MODEL = 1024
N_MEM = 256
X_WIDTH = 256
X_HEADS = 4
X_HEAD_DIM = 64
GDN_HEADS = 6
GDN_DK = 128
GDN_DV = 128
GDN_WIDTH = GDN_HEADS * GDN_DV
QK_WIDTH = GDN_HEADS * GDN_DK
QKV_WIDTH = 2 * QK_WIDTH + GDN_WIDTH
GDN_CHUNK = 64
SC_WIDTH = 768
BRANCH_WIDTH = 1024
EPS = 1e-6

LANES = 128
SUBLANES = 8
COL_CHUNK = 256
BA_PAD = LANES
VMEM_LIMIT = 56 * 1024 * 1024

HIGHEST = lax.Precision.HIGHEST


def _cparams(*sem):
    return pltpu.CompilerParams(dimension_semantics=sem, vmem_limit_bytes=VMEM_LIMIT)


def _rms_rows(x, g):
    r = lax.rsqrt(jnp.mean(x * x, axis=-1, keepdims=True) + EPS)
    return x * r * g


def _silu(x):
    h = 0.5 * x
    return h + h * jnp.tanh(h)


def _dot(a, b):
    return jnp.dot(a.astype(BF16), b.astype(BF16), preferred_element_type=F32)


def _dot_nt(a, b):
    return lax.dot_general(a.astype(BF16), b.astype(BF16), (((1,), (1,)), ((), ())),
                           preferred_element_type=F32)


def _dot_tn(a, b):
    return lax.dot_general(a.astype(BF16), b.astype(BF16), (((0,), (0,)), ((), ())),
                           preferred_element_type=F32)


def _dot_f32(a, b):
    return jnp.dot(a, b, preferred_element_type=F32, precision=HIGHEST)


def _col_chunks(n):
    return [slice(c, min(c + COL_CHUNK, n)) for c in range(0, n, COL_CHUNK)]


def _interleaved(gens):
    gens = list(gens)
    while gens:
        alive = []
        for g in gens:
            try:
                next(g)
                alive.append(g)
            except StopIteration:
                pass
        gens = alive
        yield


def _round_robin(gens):
    for _ in _interleaved(gens):
        pass


def _delayed(gen, rounds):
    for _ in range(rounds):
        yield
    yield from gen


def _memkv_kernel(m_ref, g_ref, w_ref, kt_ref, vt_ref):
    h = _rms_rows(m_ref[...], g_ref[...]).astype(BF16)
    for layer in range(w_ref.shape[0]):
        kv = jnp.dot(h, w_ref[layer], preferred_element_type=F32)
        kt_ref[layer] = kv[:, :X_WIDTH].T
        vt_ref[layer] = kv[:, X_WIDTH:].T


def _memkv(mem, g, w_bf16):
    bn, n_mem, _ = mem.shape
    depth = w_bf16.shape[0]
    out = jax.ShapeDtypeStruct((depth, bn, X_WIDTH, n_mem), F32)
    out_spec = pl.BlockSpec((depth, None, X_WIDTH, n_mem), lambda b: (0, b, 0, 0))
    return pl.pallas_call(
        _memkv_kernel,
        grid=(bn,),
        in_specs=[
            pl.BlockSpec((None, n_mem, D_MODEL), lambda b: (b, 0, 0)),
            pl.BlockSpec((1, D_MODEL), lambda b: (0, 0)),
            pl.BlockSpec(w_bf16.shape, lambda b: (0, 0, 0)),
        ],
        out_specs=[out_spec, out_spec],
        out_shape=[out, out],
        compiler_params=_cparams("arbitrary"),
        name="memkv",
    )(mem, g, w_bf16)


def _init_conv_carry(carry, hist_ref):
    @pl.when(pl.program_id(1) == 0)
    def _():
        carry[:, SUBLANES - hist_ref.shape[1]:, :] = hist_ref[...]


def _causal_conv_cols(x, cols, carry, hist_out, cw_ref, seqs=slice(None)):
    width = cw_ref.shape[0]
    bt, lt, n = x.shape
    g = lt // SUBLANES
    xe = jnp.concatenate([carry[seqs, :, cols], x], axis=1).reshape(bt, g + 1, SUBLANES, n)
    sub = lax.broadcasted_iota(jnp.int32, (1, 1, SUBLANES, n), 2)
    y = x.reshape(bt, g, SUBLANES, n) * cw_ref[width - 1:width, cols]
    for j in range(width - 1):
        s = width - 1 - j
        r = pltpu.roll(xe, s, axis=2)
        y = y + jnp.where(sub < s, r[:, :g], r[:, 1:]) * cw_ref[j:j + 1, cols]
    carry[seqs, :, cols] = x[:, lt - SUBLANES:, :]
    if hist_out is not None:
        hist_out[seqs, :, cols] = x[:, lt - (width - 1):, :]
    return y.reshape(bt, lt, n)


def _inproj_chunks(x_ref, g_ref, rest, *, mixer, bt, lt):
    tm = bt * lt
    if g_ref is None:
        h = x_ref[...].reshape(tm, D_MODEL)
    else:
        h = _rms_rows(x_ref[...].reshape(tm, D_MODEL), g_ref[...]).astype(BF16)
    if mixer == "gdn":
        n_out = (len(rest) - 1) // 2
        w_refs, (mix_out, *plain_outs, hist_out) = rest[:n_out], rest[n_out:]

        def proj(i, cols):
            return _dot_nt(h, w_refs[i][cols, :]).reshape(bt, lt, -1)
    else:
        w_ref, _, cw_ref, mix_out, *plain_outs, hist_out, carry = rest
        bases = [0, 3 * SC_WIDTH]
        for out in plain_outs[:-1]:
            bases.append(bases[-1] + out.shape[-1])

        def proj(i, cols, offset=0):
            cols = slice(bases[i] + offset + cols.start, bases[i] + offset + cols.stop)
            return jnp.dot(h, w_ref[:, cols], preferred_element_type=F32).reshape(bt, lt, -1)

    def mixer_chunk(cols):
        if mixer == "gdn":
            y = proj(0, cols)
            hist_out[:, :, cols] = y[:, lt - hist_out.shape[1]:, :]
        else:
            gate_b = proj(0, cols)
            pre = proj(0, cols, SC_WIDTH) * proj(0, cols, 2 * SC_WIDTH)
            yield
            y = gate_b * _causal_conv_cols(pre, cols, carry, hist_out, cw_ref)
        mix_out[:, :, cols] = y.astype(mix_out.dtype)
        yield

    def plain_chunk(i, out, cols):
        out[:, :, cols] = proj(i, cols).astype(out.dtype)
        yield

    gens = [mixer_chunk(cols) for cols in _col_chunks(mix_out.shape[-1])]
    for i, out in enumerate(plain_outs):
        gens += [plain_chunk(i + 1, out, cols) for cols in _col_chunks(out.shape[-1])]
    return gens


def _inproj_kernel(x_ref, g_ref, *rest, mixer, bt, lt):
    if mixer == "sconv":
        _init_conv_carry(rest[-1], rest[1])
    gens = _inproj_chunks(x_ref, g_ref, rest, mixer=mixer, bt=bt, lt=lt)
    _round_robin([_delayed(g, i) for i, g in enumerate(gens)])


def _inproj(x, g, ws, hist, cw, *, mixer, bt, lt, act_dtype):
    bn, seq, _ = x.shape
    width, cc = cw.shape

    def act(n, dtype=act_dtype):
        return (pl.BlockSpec((bt, lt, n), lambda b, l: (b, l, 0)),
                jax.ShapeDtypeStruct((bn, seq, n), dtype))

    hist_spec = pl.BlockSpec((bt, width - 1, cc), lambda b, l: (b, 0, 0))
    args = [x, g, *ws]
    in_specs = [
        pl.BlockSpec((bt, lt, D_MODEL), lambda b, l: (b, l, 0)),
        pl.BlockSpec((1, D_MODEL), lambda b, l: (0, 0)),
    ] + [pl.BlockSpec(w.shape, lambda b, l: (0, 0)) for w in ws]
    outs = [act(cc), act(BRANCH_WIDTH), act(X_WIDTH)]
    scratch = []
    if mixer == "gdn":
        outs.append(act(BA_PAD, F32))
    else:
        args += [hist, cw]
        in_specs += [hist_spec, pl.BlockSpec((width, cc), lambda b, l: (0, 0))]
        scratch.append(pltpu.VMEM((bt, SUBLANES, cc), F32))
    outs.append((hist_spec, jax.ShapeDtypeStruct((bn, width - 1, cc), F32)))
    return pl.pallas_call(
        functools.partial(_inproj_kernel, mixer=mixer, bt=bt, lt=lt),
        grid=(bn // bt, seq // lt),
        in_specs=in_specs,
        out_specs=[o[0] for o in outs],
        out_shape=[o[1] for o in outs],
        scratch_shapes=scratch,
        compiler_params=_cparams("arbitrary", "arbitrary"),
        name="inproj_" + mixer,
    )(*args)


CHUNKS_PER_WAVE = 4
SEQS_PER_WAVE = 8


def _gdn_tile(qkv_ref, ba_ref, cw_ref, alog_ref, dtb_ref, og_ref, o_ref, s_ref, carry,
              *, chunk, group):
    bt, lt, _ = qkv_ref.shape
    nchunks = lt // chunk
    heads = range(GDN_HEADS)
    groups = [tuple(range(g, g + group)) for g in range(0, GDN_HEADS, group)]
    gw = group * chunk
    row = lax.broadcasted_iota(jnp.int32, (chunk, gw), 0)
    col = lax.broadcasted_iota(jnp.int32, (chunk, gw), 1) % chunk
    tril = row >= col
    strict = row > col
    lane_blk = lax.broadcasted_iota(jnp.int32, (1, gw), 1) // chunk
    ones_tril = (lax.broadcasted_iota(jnp.int32, (chunk, chunk), 0)
                 >= lax.broadcasted_iota(jnp.int32, (chunk, chunk), 1)).astype(F32)
    neg_a = -jnp.exp(alog_ref[...])
    dtb = dtb_ref[...]
    og = og_ref[...]

    def rows(c):
        return slice(c * chunk, (c + 1) * chunk)

    def lanes(base, h, n):
        return slice(base + h * n, base + (h + 1) * n)

    def cat(xs, axis):
        return xs[0] if len(xs) == 1 else jnp.concatenate(xs, axis=axis)

    def pick(xs):
        out = xs[0]
        for j in range(1, len(xs)):
            out = jnp.where(lane_blk == j, xs[j], out)
        return out

    def blockdiag(xs):
        z = jnp.zeros_like(xs[0])
        return cat([cat([x if i == j else z for i in range(len(xs))], 1)
                    for j, x in enumerate(xs)], 0)

    def blockdiag_of_lane_blocks(x):
        if group == 1:
            return x
        return cat([jnp.where(lane_blk == j, x, 0.0) for j in range(group)], 0)

    def conv_silu(u, cols):
        b, c = u
        x = qkv_ref[b, rows(c), cols].astype(F32)
        y = _causal_conv_cols(x[None], cols, carry, None, cw_ref, seqs=slice(b, b + 1))
        return _silu(y[0])

    def phase1(units, res):
        uh = [(u, h) for u in units for h in heads]
        ug = [(u, g) for u in units for g in range(len(groups))]
        gcum, gcum_t, beta_all = {}, {}, {}
        for u in units:
            ba = ba_ref[u[0], rows(u[1]), :]
            beta_all[u] = 1.0 / (1.0 + jnp.exp(-ba))
            z = ba + dtb
            softplus = jnp.maximum(z, 0.0) + jnp.log1p(jnp.exp(-jnp.abs(z)))
            gcum[u] = _dot_f32(ones_tril, neg_a * softplus)
        q, k, v = {}, {}, {}
        for u in units:
            for h in heads:
                q[u, h] = conv_silu(u, lanes(0, h, GDN_DK))
                k[u, h] = conv_silu(u, lanes(QK_WIDTH, h, GDN_DK))
                v[u, h] = conv_silu(u, lanes(2 * QK_WIDTH, h, GDN_DV))
            yield
        for key in uh:
            x = q[key]
            q[key] = x * (lax.rsqrt(jnp.sum(x * x, axis=-1, keepdims=True) + EPS)
                          * (GDN_DK ** -0.5))
            x = k[key]
            k[key] = x * lax.rsqrt(jnp.sum(x * x, axis=-1, keepdims=True) + EPS)
        yield
        for u in units:
            gcum_t[u] = cat([gcum[u]] * group, 0).T
        gc = {(u, h): jnp.broadcast_to(gcum[u][:, GDN_HEADS + h:GDN_HEADS + h + 1],
                                       (chunk, GDN_DK)) for u, h in uh}
        gl = {(u, h): gcum[u][chunk - 1:chunk, GDN_HEADS + h:GDN_HEADS + h + 1]
              for u, h in uh}
        beta = {(u, h): jnp.broadcast_to(beta_all[u][:, h:h + 1], (chunk, GDN_DK))
                for u, h in uh}
        eg = {key: jnp.exp(gc[key]) for key in uh}
        kb = {key: k[key] * beta[key] for key in uh}
        kkqk = {}
        for u, g in ug:
            grp = groups[g]
            kkqk[u, g] = _dot_nt(
                cat([cat([kb[u, h] for h in grp], 1), cat([q[u, h] for h in grp], 1)], 0),
                blockdiag([k[u, h] for h in grp]))
        yield
        n, p, aqk = {}, {}, {}
        for u, g in ug:
            grp = groups[g]
            g_col = pick([gc[u, h][:, :gw] for h in grp])
            g_row = pick([gcum_t[u][GDN_HEADS + h:GDN_HEADS + h + 1, :] for h in grp])
            decay = jnp.exp(jnp.where(tril, g_col - g_row, -jnp.inf))
            a = jnp.where(strict, kkqk[u, g][:chunk] * decay, 0.0)
            aqk[u, g] = (kkqk[u, g][chunk:] * decay).astype(BF16)
            n[u, g] = -a
            p[u, g] = _dot(a, blockdiag_of_lane_blocks(a))
        yield
        span = 2
        while 2 * span < chunk:
            for key in ug:
                both = _dot(cat([n[key], p[key]], 0), blockdiag_of_lane_blocks(p[key]))
                n[key] = n[key] + p[key] + both[:chunk]
                p[key] = both[chunk:]
            span *= 2
            yield
        for key in ug:
            n[key] = n[key] + p[key] + _dot(n[key], blockdiag_of_lane_blocks(p[key]))
        yield
        wu = {}
        for u, h in uh:
            rhs = jnp.concatenate([kb[u, h] * eg[u, h], v[u, h] * beta[u, h]], axis=-1)
            wu[u, h] = rhs + _dot(n[u, h // group][:, lanes(0, h % group, chunk)], rhs)
        yield
        for u in units:
            res[u] = dict(
                w_qd=[jnp.concatenate([wu[u, h][:, :GDN_DK], q[u, h] * eg[u, h]],
                                      axis=0).astype(BF16) for h in heads],
                u=[wu[u, h][:, GDN_DK:] for h in heads],
                kd=[(k[u, h] * jnp.exp(gl[u, h] - gc[u, h])).astype(BF16) for h in heads],
                aqk=[aqk[u, g] for g in range(len(groups))],
                egl=[jnp.exp(gl[u, h]) for h in heads])
        yield

    def phase2(units, res):
        for c in sorted({c for _, c in units}):
            us = [u for u in units if u[1] == c]
            uh = [(u, h) for u in us for h in heads]
            s = {(u, h): s_ref[u[0], h] for u, h in uh}
            ws_qs = {(u, h): _dot(res[u]["w_qd"][h], s[u, h]) for u, h in uh}
            yield
            v_new = {(u, h): res[u]["u"][h] - ws_qs[u, h][:chunk] for u, h in uh}
            o_grp = {(u, g): _dot(res[u]["aqk"][g], blockdiag([v_new[u, h] for h in grp]))
                     for u in us for g, grp in enumerate(groups)}
            for u, h in uh:
                s_ref[u[0], h] = (s[u, h] * res[u]["egl"][h]
                                  + _dot_tn(res[u]["kd"][h], v_new[u, h]))
            yield
            for u, h in uh:
                oh = ws_qs[u, h][chunk:] + o_grp[u, h // group][:, lanes(0, h % group, GDN_DV)]
                oh = oh * lax.rsqrt(jnp.mean(oh * oh, axis=-1, keepdims=True) + EPS) * og
                o_ref[u[0], rows(c), lanes(0, h, GDN_DV)] = oh.astype(o_ref.dtype)
            yield

    units = [(b, c) for b in range(bt) for c in range(nchunks)]
    per_wave = CHUNKS_PER_WAVE if nchunks > 1 else SEQS_PER_WAVE
    waves = [units[i:i + per_wave] for i in range(0, len(units), per_wave)]
    res = {}
    prev = None
    for wave in waves:
        gens = [phase1(wave, res)]
        if prev is not None:
            gens.append(phase2(prev, res))
        yield from _interleaved(gens)
        prev = wave
    yield from phase2(prev, res)


def _gdn_kernel(qkv_ref, ba_ref, s0_ref, hist_ref, cw_ref, alog_ref, dtb_ref, og_ref,
                o_ref, s_ref, carry, *, chunk, group):
    @pl.when(pl.program_id(1) == 0)
    def _():
        s_ref[...] = s0_ref[...]

    _init_conv_carry(carry, hist_ref)
    _round_robin([_gdn_tile(qkv_ref, ba_ref, cw_ref, alog_ref, dtb_ref, og_ref, o_ref, s_ref,
                            carry, chunk=chunk, group=group)])


def _gdn(qkv, ba, s0, hist, cw, alog_pad, dtb_pad, og, *, chunk, bt, lt):
    group = 2 if 2 * chunk == LANES else 1
    bn, seq, _ = qkv.shape
    width, cc = cw.shape
    s_spec = pl.BlockSpec((bt, GDN_HEADS, GDN_DK, GDN_DV), lambda b, l: (b, 0, 0, 0))
    hist_spec = pl.BlockSpec((bt, width - 1, cc), lambda b, l: (b, 0, 0))
    vec = pl.BlockSpec((1, LANES), lambda b, l: (0, 0))
    return pl.pallas_call(
        functools.partial(_gdn_kernel, chunk=chunk, group=group),
        grid=(bn // bt, seq // lt),
        in_specs=[
            pl.BlockSpec((bt, lt, QKV_WIDTH), lambda b, l: (b, l, 0)),
            pl.BlockSpec((bt, lt, BA_PAD), lambda b, l: (b, l, 0)),
            s_spec, hist_spec,
            pl.BlockSpec((width, cc), lambda b, l: (0, 0)),
            vec, vec, vec,
        ],
        out_specs=[
            pl.BlockSpec((bt, lt, GDN_WIDTH), lambda b, l: (b, l, 0)),
            s_spec,
        ],
        out_shape=[
            jax.ShapeDtypeStruct((bn, seq, GDN_WIDTH), qkv.dtype),
            jax.ShapeDtypeStruct(s0.shape, F32),
        ],
        scratch_shapes=[pltpu.VMEM((bt, SUBLANES, cc), F32)],
        compiler_params=_cparams("arbitrary", "arbitrary"),
        name="gdn",
    )(qkv, ba, s0, hist, cw, alog_pad, dtb_pad, og)


PROJ_ROUNDS = ((0, 5), (14, 5), (28, 1), (30, 1), (32, 1), (34, 1), (36, 1))


def _gdn_layer_kernel(x_ref, g_ref, w_qkv, w_gate, w_xq, w_ba, s0_ref, hist_ref, cw_ref,
                      alog_ref, dtb_ref, og_ref, gate_out, xq_out, hist_out, o_ref, s_ref,
                      qkv_scr, ba_scr, carry, *, lt, tiles_per_seq, chunk, group):
    i = pl.program_id(0)
    slot = i % 2
    prev = jnp.maximum(i - 1, 0)

    @pl.when(i == 0)
    def _():
        qkv_scr[1] = jnp.zeros(qkv_scr.shape[1:], F32)
        ba_scr[1] = jnp.zeros(ba_scr.shape[1:], F32)

    @pl.when(prev % tiles_per_seq == 0)
    def _():
        s_ref[...] = s0_ref[...]
        carry[:, SUBLANES - hist_ref.shape[1]:, :] = hist_ref[...]

    proj = _inproj_chunks(
        x_ref, g_ref, (w_qkv, w_gate, w_xq, w_ba, qkv_scr.at[slot], gate_out, xq_out,
                       ba_scr.at[slot], hist_out), mixer="gdn", bt=1, lt=lt)
    rule = _gdn_tile(qkv_scr.at[1 - slot], ba_scr.at[1 - slot], cw_ref, alog_ref, dtb_ref,
                     og_ref, o_ref, s_ref, carry, chunk=chunk, group=group)
    starts = [r for w, n in PROJ_ROUNDS for r in range(w, w + n)]
    _round_robin([rule] + [_delayed(p, starts[k]) for k, p in enumerate(proj)])


def _gdn_layer(x, g, ws, s0, hist, cw, alog_pad, dtb_pad, og, *, chunk, lt, act_dtype):
    group = 2 if 2 * chunk == LANES else 1
    bn, seq, _ = x.shape
    nl = seq // lt
    tiles = bn * nl
    width, cc = cw.shape

    def cur(i):
        return jnp.minimum(i, tiles - 1)

    def prev(i):
        return jnp.maximum(i - 1, 0)

    def tile(n, which):
        return pl.BlockSpec((1, lt, n), lambda i: (which(i), 0, 0))

    def per_seq(shape, which):
        return pl.BlockSpec((1,) + shape, lambda i: (which(i) // nl,) + (0,) * len(shape))

    def whole(a):
        return pl.BlockSpec(a.shape, lambda i: (0,) * a.ndim)

    state = (GDN_HEADS, GDN_DK, GDN_DV)
    gate, xq, hist_new, tok, s_new = pl.pallas_call(
        functools.partial(_gdn_layer_kernel, lt=lt, tiles_per_seq=nl, chunk=chunk, group=group),
        grid=(tiles + 1,),
        in_specs=[tile(D_MODEL, cur), whole(g)] + [whole(w) for w in ws] + [
            per_seq(state, prev), per_seq((width - 1, cc), prev), whole(cw),
            whole(alog_pad), whole(dtb_pad), whole(og)],
        out_specs=[
            tile(BRANCH_WIDTH, cur), tile(X_WIDTH, cur), per_seq((width - 1, cc), cur),
            tile(GDN_WIDTH, prev), per_seq(state, prev)],
        out_shape=[
            jax.ShapeDtypeStruct((tiles, lt, BRANCH_WIDTH), act_dtype),
            jax.ShapeDtypeStruct((tiles, lt, X_WIDTH), act_dtype),
            jax.ShapeDtypeStruct(hist.shape, F32),
            jax.ShapeDtypeStruct((tiles, lt, GDN_WIDTH), act_dtype),
            jax.ShapeDtypeStruct(s0.shape, F32)],
        scratch_shapes=[
            pltpu.VMEM((2, 1, lt, cc), F32),
            pltpu.VMEM((2, 1, lt, BA_PAD), F32),
            pltpu.VMEM((1, SUBLANES, cc), F32)],
        compiler_params=_cparams("arbitrary"),
        name="gdn_layer",
    )(x.reshape(tiles, lt, D_MODEL), g, *ws, s0, hist, cw, alog_pad, dtb_pad, og)
    return (gate.reshape(bn, seq, -1), xq.reshape(bn, seq, -1), tok.reshape(bn, seq, -1),
            s_new, hist_new)


ATTN_ROWS = 256
ATTN_SEQS = 8


def _attn_out_tile(tok_ref, xq_ref, gate_ref, x_ref, kt_ref, vt_ref, w_ref, fg_ref,
                   y_ref, xo_scr, *, bt, lt, final, next_norm=None):
    lane_head = lax.broadcasted_iota(jnp.int32, (1, X_WIDTH), 1) // X_HEAD_DIM
    scale = X_HEAD_DIM ** -0.5
    tm = bt * lt

    def attend(b, rs):
        r = rs.stop - rs.start
        q = xq_ref[b, rs, :]
        q = q * jnp.asarray(scale, q.dtype)
        qx = jnp.concatenate([jnp.where(lane_head == h, q, jnp.zeros_like(q))
                              for h in range(X_HEADS)], axis=0)
        s = _dot(qx, kt_ref[b])
        yield
        e = jnp.exp(s - jnp.max(s, axis=-1, keepdims=True))
        p = e / jnp.sum(e, axis=-1, keepdims=True)
        o4 = _dot_nt(p, vt_ref[b])
        yield
        xo = jnp.where(lane_head == 0, o4[0:r], 0.0)
        for h in range(1, X_HEADS):
            xo = xo + jnp.where(lane_head == h, o4[h * r:(h + 1) * r], 0.0)
        xo_scr[b, rs, :] = xo
        yield

    acc = {}

    def tok_proj():
        sg = _silu(gate_ref[:, :, :GDN_WIDTH].astype(F32).reshape(tm, GDN_WIDTH))
        br = (tok_ref[...].astype(F32).reshape(tm, GDN_WIDTH) * sg).astype(BF16)
        for cols in _col_chunks(D_MODEL):
            acc[cols.start] = (x_ref[:, :, cols].reshape(tm, -1)
                               + _dot(br, w_ref[:GDN_WIDTH, cols]))
            yield

    if bt == 1:
        blocks = [slice(r, r + ATTN_ROWS) for r in range(0, lt, ATTN_ROWS)]
        yield from _interleaved([attend(0, rs) for rs in blocks] + [tok_proj()])
    else:
        def seq_group(i, carry):
            _round_robin([attend(i * ATTN_SEQS + j, slice(0, lt)) for j in range(ATTN_SEQS)])
            return carry

        lax.fori_loop(0, bt // ATTN_SEQS, seq_group, 0)
        yield from tok_proj()

    sg_x = _silu(gate_ref[:, :, GDN_WIDTH:].astype(F32).reshape(tm, X_WIDTH))
    br_x = (xo_scr[...].reshape(tm, X_WIDTH) * sg_x).astype(BF16)
    y = [acc[cols.start] + _dot(br_x, w_ref[GDN_WIDTH:, cols]) for cols in _col_chunks(D_MODEL)]
    if final or next_norm is not None:
        ms = sum(jnp.sum(c * c, axis=-1, keepdims=True) for c in y) * (1.0 / D_MODEL)
        r = lax.rsqrt(ms + EPS)
    if next_norm is not None:
        h_ref, g_ref = next_norm
        for c, cols in zip(y, _col_chunks(D_MODEL)):
            h_ref[:, :, cols] = (c * r * g_ref[:, cols]).astype(h_ref.dtype).reshape(bt, lt, -1)
    if final:
        y = [c * r * fg_ref[:, cols] for c, cols in zip(y, _col_chunks(D_MODEL))]
    for c, cols in zip(y, _col_chunks(D_MODEL)):
        y_ref[:, :, cols] = c.reshape(bt, lt, -1)
    yield


def _attn_out_kernel(tok_ref, xq_ref, gate_ref, x_ref, kt_ref, vt_ref, w_ref, fg_ref,
                     y_ref, xo_scr, *, bt, lt, final):
    _round_robin([_attn_out_tile(tok_ref, xq_ref, gate_ref, x_ref, kt_ref, vt_ref, w_ref,
                                 fg_ref, y_ref, xo_scr, bt=bt, lt=lt, final=final)])


def _attn_out(tok, xq, gate, x, mk, mv, w_bf16, fg, *, layer, bt, lt, final):
    bn, seq, _ = x.shape

    def act(n):
        return pl.BlockSpec((bt, lt, n), lambda b, l: (b, l, 0))

    mem = pl.BlockSpec((None, bt, X_WIDTH, N_MEM), lambda b, l: (layer, b, 0, 0))
    return pl.pallas_call(
        functools.partial(_attn_out_kernel, bt=bt, lt=lt, final=final),
        grid=(bn // bt, seq // lt),
        in_specs=[
            act(GDN_WIDTH), act(X_WIDTH), act(BRANCH_WIDTH), act(D_MODEL), mem, mem,
            pl.BlockSpec((BRANCH_WIDTH, D_MODEL), lambda b, l: (0, 0)),
            pl.BlockSpec((1, D_MODEL), lambda b, l: (0, 0)),
        ],
        out_specs=act(D_MODEL),
        out_shape=jax.ShapeDtypeStruct(x.shape, F32),
        scratch_shapes=[pltpu.VMEM((bt, lt, X_WIDTH), F32)],
        compiler_params=_cparams("arbitrary", "arbitrary"),
        name="attn_out",
    )(tok, xq, gate, x, mk, mv, w_bf16, fg)


SCONV_PROJ_ROUNDS = (0, 0, 1, 1, 2, 3, 4, 4)
TAIL_ATTN1_START = 3


def _tail_kernel(tok_ref, xq_ref, gate_ref, x_ref, kt0_ref, vt0_ref, kt1_ref, vt1_ref,
                 w_out0_ref, w_out1_ref, fg_ref, g_ref, w_in_ref, hist_ref, cw_ref,
                 y_ref, hist_out,
                 xo0_scr, xo1_scr, h_scr, x1_scr, tok1_scr, gate1_scr, xq1_scr, carry,
                 *, lt, tiles_per_seq):
    i = pl.program_id(0)
    a2, b2, c2 = i % 2, (i + 1) % 2, i % 2
    a3, c3 = i % 3, (i + 1) % 3
    prev = jnp.maximum(i - 1, 0)

    @pl.when(i == 0)
    def _():
        h_scr[1] = jnp.zeros(h_scr.shape[1:], BF16)
        x1_scr[1] = jnp.zeros(x1_scr.shape[1:], F32)
        x1_scr[2] = jnp.zeros(x1_scr.shape[1:], F32)
        tok1_scr[0] = jnp.zeros(tok1_scr.shape[1:], BF16)
        gate1_scr[0] = jnp.zeros(gate1_scr.shape[1:], BF16)
        xq1_scr[0] = jnp.zeros(xq1_scr.shape[1:], BF16)

    @pl.when(prev % tiles_per_seq == 0)
    def _():
        carry[:, SUBLANES - hist_ref.shape[1]:, :] = hist_ref[...]

    attn0 = _attn_out_tile(tok_ref, xq_ref, gate_ref, x_ref, kt0_ref, vt0_ref, w_out0_ref,
                           fg_ref, x1_scr.at[a3], xo0_scr, bt=1, lt=lt, final=False,
                           next_norm=(h_scr.at[a2], g_ref))
    proj = _inproj_chunks(h_scr.at[1 - a2], None,
                          (w_in_ref, hist_ref, cw_ref, tok1_scr.at[b2], gate1_scr.at[b2],
                           xq1_scr.at[b2], hist_out, carry), mixer="sconv", bt=1, lt=lt)
    attn1 = _attn_out_tile(tok1_scr.at[c2], xq1_scr.at[c2], gate1_scr.at[c2], x1_scr.at[c3],
                           kt1_ref, vt1_ref, w_out1_ref, fg_ref, y_ref, xo1_scr,
                           bt=1, lt=lt, final=True)
    _round_robin([attn0, _delayed(attn1, TAIL_ATTN1_START)]
                 + [_delayed(p, SCONV_PROJ_ROUNDS[k]) for k, p in enumerate(proj)])


def _tail(tok, xq, gate, x, mk, mv, w_out_bf16, fg, g, w_in, hist, cw, *, lt):
    bn, seq, _ = x.shape
    nl = seq // lt
    tiles = bn * nl
    width, cc = cw.shape

    def clamp(i, lag):
        return jnp.clip(i - lag, 0, tiles - 1)

    def tile(n, lag):
        return pl.BlockSpec((1, lt, n), lambda i: (clamp(i, lag), 0, 0))

    def whole(a):
        return pl.BlockSpec(a.shape, lambda i: (0,) * a.ndim, pipeline_mode=pl.Buffered(1))

    def mem(layer, lag):
        return pl.BlockSpec((None, 1, X_WIDTH, N_MEM),
                            lambda i: (layer, clamp(i, lag) // nl, 0, 0))

    def tiled(a):
        return a.reshape(tiles, lt, a.shape[-1])

    def slots(n, width_, dtype):
        return pltpu.VMEM((n, 1, lt, width_), dtype)

    hist_spec = pl.BlockSpec((1, width - 1, cc), lambda i: (clamp(i, 1) // nl, 0, 0))
    y, hist_new = pl.pallas_call(
        functools.partial(_tail_kernel, lt=lt, tiles_per_seq=nl),
        grid=(tiles + 2,),
        in_specs=[
            tile(GDN_WIDTH, 0), tile(X_WIDTH, 0), tile(BRANCH_WIDTH, 0), tile(D_MODEL, 0),
            mem(0, 0), mem(0, 0), mem(1, 2), mem(1, 2),
            whole(w_out_bf16[0]), whole(w_out_bf16[1]), whole(fg), whole(g), whole(w_in),
            hist_spec, whole(cw)],
        out_specs=[tile(D_MODEL, 2), hist_spec],
        out_shape=[
            jax.ShapeDtypeStruct((tiles, lt, D_MODEL), F32),
            jax.ShapeDtypeStruct(hist.shape, F32)],
        scratch_shapes=[
            pltpu.VMEM((1, lt, X_WIDTH), F32), pltpu.VMEM((1, lt, X_WIDTH), F32),
            slots(2, D_MODEL, BF16), slots(3, D_MODEL, F32),
            slots(2, cc, BF16), slots(2, BRANCH_WIDTH, BF16), slots(2, X_WIDTH, BF16),
            pltpu.VMEM((1, SUBLANES, cc), F32)],
        compiler_params=_cparams("arbitrary"),
        name="tail",
    )(tiled(tok), tiled(xq), tiled(gate), tiled(x), mk, mv, mk, mv,
      w_out_bf16[0], w_out_bf16[1], fg, g, w_in, hist, cw)
    return y.reshape(x.shape), hist_new


def _trunk(x, mem_k, mem_v, gdn_s, gdn_conv, sc_conv, p, *, bt, lt, chunk, gdn_bt, gdn_lt,
           attn_bt, attn_lt, act_dtype):
    if gdn_bt == 1 and bt == 1:
        gate, xq, tok, s_new, gconv_new = _gdn_layer(
            x, p["norm_g"][0:1], p["w_in_a"], gdn_s, gdn_conv, p["conv_w_a"], p["alog_pad"],
            p["dtb_pad"], p["o_norm_g"], chunk=chunk, lt=gdn_lt, act_dtype=act_dtype)
    else:
        qkv, gate, xq, ba, gconv_new = _inproj(
            x, p["norm_g"][0:1], p["w_in_a"], None, p["conv_w_a"], mixer="gdn", bt=bt, lt=lt,
            act_dtype=act_dtype)
        tok, s_new = _gdn(qkv, ba, gdn_s, gdn_conv, p["conv_w_a"], p["alog_pad"],
                          p["dtb_pad"], p["o_norm_g"], chunk=chunk, bt=gdn_bt, lt=gdn_lt)
    if attn_bt == 1 and bt == 1:
        y, sconv_new = _tail(tok, xq, gate, x, mem_k, mem_v, p["w_out"], p["final_norm_g"],
                             p["norm_g"][1:2], p["w_in_b"][0], sc_conv, p["conv_w_b"],
                             lt=attn_lt)
    else:
        x = _attn_out(tok, xq, gate, x, mem_k, mem_v, p["w_out"][0], p["final_norm_g"],
                      layer=0, bt=attn_bt, lt=attn_lt, final=False)
        tok, gate, xq, sconv_new = _inproj(
            x, p["norm_g"][1:2], p["w_in_b"], sc_conv, p["conv_w_b"], mixer="sconv", bt=bt,
            lt=lt, act_dtype=act_dtype)
        y = _attn_out(tok, xq, gate, x, mem_k, mem_v, p["w_out"][1], p["final_norm_g"],
                      layer=1, bt=attn_bt, lt=attn_lt, final=True)
    return y, s_new[None], gconv_new[None], sconv_new[None]


def kernel(x_prompt, x_sample, mem_prompt, state_gdn, state_gdn_conv, state_sconv, cache_mem_k, cache_mem_v, norm_g, w_in_a, conv_w_a, a_log, dt_bias, o_norm_g, w_in_b, conv_w_b, mem_norm_g, w_mem_kv, w_out, final_norm_g):
    bp = x_prompt.shape[0]

    wa = jnp.transpose(w_in_a[0])
    c_b = QKV_WIDTH
    c_g = c_b + 2 * GDN_HEADS
    c_x = c_g + BRANCH_WIDTH
    wa = [wa[:c_b], wa[c_g:c_x], wa[c_x:],
          jnp.concatenate([wa[c_b:c_g], jnp.zeros((BA_PAD - 2 * GDN_HEADS, D_MODEL), wa.dtype)])]
    wa = [w.astype(BF16) for w in wa]
    pad_lo = jnp.zeros((GDN_HEADS,), F32)
    pad_hi = jnp.zeros((LANES - 2 * GDN_HEADS,), F32)
    params = {
        "norm_g": norm_g,
        "w_in_a": wa,
        "conv_w_a": conv_w_a[0],
        "alog_pad": jnp.concatenate([pad_lo, a_log[0], pad_hi])[None],
        "dtb_pad": jnp.concatenate([pad_lo, dt_bias[0], pad_hi])[None],
        "o_norm_g": o_norm_g,
        "w_in_b": [w_in_b[0].astype(BF16)],
        "conv_w_b": conv_w_b[0],
        "w_out": w_out.astype(BF16),
        "final_norm_g": final_norm_g[None],
    }

    def to_cache(t):
        t = t.reshape(t.shape[0], t.shape[1], X_HEADS, X_HEAD_DIM, t.shape[3])
        return jnp.transpose(t, (0, 1, 4, 2, 3))

    def from_cache(t):
        t = jnp.transpose(t, (0, 1, 3, 4, 2))
        return t.reshape(t.shape[0], t.shape[1], X_WIDTH, t.shape[4])

    mem_kt, mem_vt = _memkv(mem_prompt, mem_norm_g[None], w_mem_kv.astype(BF16))
    mem_k_p = to_cache(mem_kt)
    mem_v_p = to_cache(mem_vt)

    s0_p = jnp.zeros((bp,) + state_gdn.shape[2:], F32)
    gc0_p = jnp.zeros((bp,) + state_gdn_conv.shape[2:], F32)
    sc0_p = jnp.zeros((bp,) + state_sconv.shape[2:], F32)
    y_p, s_p, gc_p, sc_p = _trunk(x_prompt, mem_kt, mem_vt, s0_p, gc0_p, sc0_p,
                                  params, bt=1, lt=1024, chunk=GDN_CHUNK, gdn_bt=1, gdn_lt=512,
                                  attn_bt=1, attn_lt=512, act_dtype=BF16)
    dec_seq = x_sample.shape[1]
    y_s, s_s, gc_s, sc_s = _trunk(x_sample, from_cache(cache_mem_k), from_cache(cache_mem_v),
                                  state_gdn[0], state_gdn_conv[0], state_sconv[0], params,
                                  bt=32, lt=dec_seq, chunk=dec_seq, gdn_bt=16, gdn_lt=dec_seq,
                                  attn_bt=16, attn_lt=dec_seq, act_dtype=F32)
    return (y_p, y_s, s_p, gc_p, sc_p, mem_k_p, mem_v_p, s_s, gc_s, sc_s)
```

```python
import functools

import jax
import jax.numpy as jnp
from jax import lax
from jax.experimental import pallas as pl
from jax.experimental.pallas import tpu as pltpu

F32 = jnp.float32
BF16 = jnp.bfloat16

D_MODEL = 1024
N_MEM = 256
X_WIDTH = 256
X_HEADS = 4
X_HEAD_DIM = 64
GDN_HEADS = 6
GDN_DK = 128
GDN_DV = 128
GDN_WIDTH = GDN_HEADS * GDN_DV
QK_WIDTH = GDN_HEADS * GDN_DK
QKV_WIDTH = 2 * QK_WIDTH + GDN_WIDTH
GDN_CHUNK = 64
SC_WIDTH = 768
BRANCH_WIDTH = 1024
EPS = 1e-6

LANES = 128
SUBLANES = 8
COL_CHUNK = 256
BA_PAD = LANES
VMEM_LIMIT = 56 * 1024 * 1024

HIGHEST = lax.Precision.HIGHEST


def _cparams(*sem):
    return pltpu.CompilerParams(dimension_semantics=sem, vmem_limit_bytes=VMEM_LIMIT)


def _rms_rows(x, g):
    r = lax.rsqrt(jnp.mean(x * x, axis=-1, keepdims=True) + EPS)
    return x * r * g


def _silu(x):
    h = 0.5 * x
    return h + h * jnp.tanh(h)


def _dot(a, b):
    return jnp.dot(a.astype(BF16), b.astype(BF16), preferred_element_type=F32)


def _dot_nt(a, b):
    return lax.dot_general(a.astype(BF16), b.astype(BF16), (((1,), (1,)), ((), ())),
                           preferred_element_type=F32)


def _dot_tn(a, b):
    return lax.dot_general(a.astype(BF16), b.astype(BF16), (((0,), (0,)), ((), ())),
                           preferred_element_type=F32)


def _dot_f32(a, b):
    return jnp.dot(a, b, preferred_element_type=F32, precision=HIGHEST)


def _col_chunks(n):
    return [slice(c, min(c + COL_CHUNK, n)) for c in range(0, n, COL_CHUNK)]


def _interleaved(gens):
    gens = list(gens)
    while gens:
        alive = []
        for g in gens:
            try:
                next(g)
                alive.append(g)
            except StopIteration:
                pass
        gens = alive
        yield


def _round_robin(gens):
    for _ in _interleaved(gens):
        pass


def _delayed(gen, rounds):
    for _ in range(rounds):
        yield
    yield from gen


def _memkv_kernel(m_ref, g_ref, w_ref, kt_ref, vt_ref):
    h = _rms_rows(m_ref[...], g_ref[...]).astype(BF16)
    for layer in range(w_ref.shape[0]):
        kv = jnp.dot(h, w_ref[layer], preferred_element_type=F32)
        kt_ref[layer] = kv[:, :X_WIDTH].T
        vt_ref[layer] = kv[:, X_WIDTH:].T


def _memkv(mem, g, w_bf16):
    bn, n_mem, _ = mem.shape
    depth = w_bf16.shape[0]
    out = jax.ShapeDtypeStruct((depth, bn, X_WIDTH, n_mem), F32)
    out_spec = pl.BlockSpec((depth, None, X_WIDTH, n_mem), lambda b: (0, b, 0, 0))
    return pl.pallas_call(
        _memkv_kernel,
        grid=(bn,),
        in_specs=[
            pl.BlockSpec((None, n_mem, D_MODEL), lambda b: (b, 0, 0)),
            pl.BlockSpec((1, D_MODEL), lambda b: (0, 0)),
            pl.BlockSpec(w_bf16.shape, lambda b: (0, 0, 0)),
        ],
        out_specs=[out_spec, out_spec],
        out_shape=[out, out],
        compiler_params=_cparams("arbitrary"),
        name="memkv",
    )(mem, g, w_bf16)


def _init_conv_carry(carry, hist_ref):
    @pl.when(pl.program_id(1) == 0)
    def _():
        carry[:, SUBLANES - hist_ref.shape[1]:, :] = hist_ref[...]


def _causal_conv_cols(x, cols, carry, hist_out, cw_ref, seqs=slice(None)):
    width = cw_ref.shape[0]
    bt, lt, n = x.shape
    g = lt // SUBLANES
    xe = jnp.concatenate([carry[seqs, :, cols], x], axis=1).reshape(bt, g + 1, SUBLANES, n)
    sub = lax.broadcasted_iota(jnp.int32, (1, 1, SUBLANES, n), 2)
    y = x.reshape(bt, g, SUBLANES, n) * cw_ref[width - 1:width, cols]
    for j in range(width - 1):
        s = width - 1 - j
        r = pltpu.roll(xe, s, axis=2)
        y = y + jnp.where(sub < s, r[:, :g], r[:, 1:]) * cw_ref[j:j + 1, cols]
    carry[seqs, :, cols] = x[:, lt - SUBLANES:, :]
    if hist_out is not None:
        hist_out[seqs, :, cols] = x[:, lt - (width - 1):, :]
    return y.reshape(bt, lt, n)


def _inproj_chunks(x_ref, g_ref, rest, *, mixer, bt, lt):
    tm = bt * lt
    if g_ref is None:
        h = x_ref[...].reshape(tm, D_MODEL).astype(BF16)
    else:
        h = _rms_rows(x_ref[...].reshape(tm, D_MODEL), g_ref[...]).astype(BF16)
    if mixer == "gdn":
        n_out = (len(rest) - 1) // 2
        w_refs, (mix_out, *plain_outs, hist_out) = rest[:n_out], rest[n_out:]

        def proj(i, cols):
            return _dot_nt(h, w_refs[i][cols, :]).reshape(bt, lt, -1)
    else:
        w_ref, _, cw_ref, mix_out, *plain_outs, hist_out, carry = rest
        bases = [0, 3 * SC_WIDTH]
        for out in plain_outs[:-1]:
            bases.append(bases[-1] + out.shape[-1])

        def proj(i, cols, offset=0):
            cols = slice(bases[i] + offset + cols.start, bases[i] + offset + cols.stop)
            return jnp.dot(h, w_ref[:, cols], preferred_element_type=F32).reshape(bt, lt, -1)

    def mixer_chunk(cols):
        if mixer == "gdn":
            y = proj(0, cols)
            hist_out[:, :, cols] = y[:, lt - hist_out.shape[1]:, :]
        else:
            gate_b = proj(0, cols)
            pre = proj(0, cols, SC_WIDTH) * proj(0, cols, 2 * SC_WIDTH)
            yield
            y = gate_b * _causal_conv_cols(pre, cols, carry, hist_out, cw_ref)
        mix_out[:, :, cols] = y.astype(mix_out.dtype)
        yield

    def plain_chunk(i, out, cols):
        out[:, :, cols] = proj(i, cols).astype(out.dtype)
        yield

    gens = [mixer_chunk(cols) for cols in _col_chunks(mix_out.shape[-1])]
    for i, out in enumerate(plain_outs):
        gens += [plain_chunk(i + 1, out, cols) for cols in _col_chunks(out.shape[-1])]
    return gens


def _inproj_kernel(x_ref, g_ref, *rest, mixer, bt, lt):
    if mixer == "sconv":
        _init_conv_carry(rest[-1], rest[1])
    gens = _inproj_chunks(x_ref, g_ref, rest, mixer=mixer, bt=bt, lt=lt)
    _round_robin([_delayed(g, i) for i, g in enumerate(gens)])


def _inproj(x, g, ws, hist, cw, *, mixer, bt, lt, act_dtype):
    bn, seq, _ = x.shape
    width, cc = cw.shape

    def act(n, dtype=act_dtype):
        return (pl.BlockSpec((bt, lt, n), lambda b, l: (b, l, 0)),
                jax.ShapeDtypeStruct((bn, seq, n), dtype))

    hist_spec = pl.BlockSpec((bt, width - 1, cc), lambda b, l: (b, 0, 0))
    args = [x, g, *ws]
    in_specs = [
        pl.BlockSpec((bt, lt, D_MODEL), lambda b, l: (b, l, 0)),
        pl.BlockSpec((1, D_MODEL), lambda b, l: (0, 0)),
    ] + [pl.BlockSpec(w.shape, lambda b, l: (0, 0)) for w in ws]
    outs = [act(cc), act(BRANCH_WIDTH), act(X_WIDTH)]
    scratch = []
    if mixer == "gdn":
        outs.append(act(BA_PAD, F32))
    else:
        args += [hist, cw]
        in_specs += [hist_spec, pl.BlockSpec((width, cc), lambda b, l: (0, 0))]
        scratch.append(pltpu.VMEM((bt, SUBLANES, cc), F32))
    outs.append((hist_spec, jax.ShapeDtypeStruct((bn, width - 1, cc), F32)))
    return pl.pallas_call(
        functools.partial(_inproj_kernel, mixer=mixer, bt=bt, lt=lt),
        grid=(bn // bt, seq // lt),
        in_specs=in_specs,
        out_specs=[o[0] for o in outs],
        out_shape=[o[1] for o in outs],
        scratch_shapes=scratch,
        compiler_params=_cparams("arbitrary", "arbitrary"),
        name="inproj_" + mixer,
    )(*args)


CHUNKS_PER_WAVE = 4
SEQS_PER_WAVE = 8


def _gdn_tile(qkv_ref, ba_ref, cw_ref, alog_ref, dtb_ref, og_ref, o_ref, s_ref, carry,
              *, chunk, group):
    bt, lt, _ = qkv_ref.shape
    nchunks = lt // chunk
    heads = range(GDN_HEADS)
    groups = [tuple(range(g, g + group)) for g in range(0, GDN_HEADS, group)]
    gw = group * chunk
    row = lax.broadcasted_iota(jnp.int32, (chunk, gw), 0)
    col = lax.broadcasted_iota(jnp.int32, (chunk, gw), 1) % chunk
    tril = row >= col
    strict = row > col
    lane_blk = lax.broadcasted_iota(jnp.int32, (1, gw), 1) // chunk
    ones_tril = (lax.broadcasted_iota(jnp.int32, (chunk, chunk), 0)
                 >= lax.broadcasted_iota(jnp.int32, (chunk, chunk), 1)).astype(F32)
    neg_a = -jnp.exp(alog_ref[...])
    dtb = dtb_ref[...]
    og = og_ref[...]

    def rows(c):
        return slice(c * chunk, (c + 1) * chunk)

    def lanes(base, h, n):
        return slice(base + h * n, base + (h + 1) * n)

    def cat(xs, axis):
        return xs[0] if len(xs) == 1 else jnp.concatenate(xs, axis=axis)

    def pick(xs):
        out = xs[0]
        for j in range(1, len(xs)):
            out = jnp.where(lane_blk == j, xs[j], out)
        return out

    def blockdiag(xs):
        z = jnp.zeros_like(xs[0])
        return cat([cat([x if i == j else z for i in range(len(xs))], 1)
                    for j, x in enumerate(xs)], 0)

    def blockdiag_of_lane_blocks(x):
        if group == 1:
            return x
        return cat([jnp.where(lane_blk == j, x, 0.0) for j in range(group)], 0)

    def conv_silu(u, cols):
        b, c = u
        x = qkv_ref[b, rows(c), cols].astype(F32)
        y = _causal_conv_cols(x[None], cols, carry, None, cw_ref, seqs=slice(b, b + 1))
        return _silu(y[0])

    def phase1(units, res):
        uh = [(u, h) for u in units for h in heads]
        ug = [(u, g) for u in units for g in range(len(groups))]
        gcum, gcum_t, beta_all = {}, {}, {}
        for u in units:
            ba = ba_ref[u[0], rows(u[1]), :]
            beta_all[u] = 1.0 / (1.0 + jnp.exp(-ba))
            z = ba + dtb
            softplus = jnp.maximum(z, 0.0) + jnp.log1p(jnp.exp(-jnp.abs(z)))
            gcum[u] = _dot_f32(ones_tril, neg_a * softplus)
        q, k, v = {}, {}, {}
        for u in units:
            for h in heads:
                q[u, h] = conv_silu(u, lanes(0, h, GDN_DK))
                k[u, h] = conv_silu(u, lanes(QK_WIDTH, h, GDN_DK))
                v[u, h] = conv_silu(u, lanes(2 * QK_WIDTH, h, GDN_DV))
            yield
        for key in uh:
            x = q[key]
            q[key] = x * (lax.rsqrt(jnp.sum(x * x, axis=-1, keepdims=True) + EPS)
                          * (GDN_DK ** -0.5))
            x = k[key]
            k[key] = x * lax.rsqrt(jnp.sum(x * x, axis=-1, keepdims=True) + EPS)
        yield
        for u in units:
            gcum_t[u] = cat([gcum[u]] * group, 0).T
        gc = {(u, h): jnp.broadcast_to(gcum[u][:, GDN_HEADS + h:GDN_HEADS + h + 1],
                                       (chunk, GDN_DK)) for u, h in uh}
        gl = {(u, h): gcum[u][chunk - 1:chunk, GDN_HEADS + h:GDN_HEADS + h + 1]
              for u, h in uh}
        beta = {(u, h): jnp.broadcast_to(beta_all[u][:, h:h + 1], (chunk, GDN_DK))
                for u, h in uh}
        eg = {key: jnp.exp(gc[key]) for key in uh}
        kb = {key: k[key] * beta[key] for key in uh}
        kkqk = {}
        for u, g in ug:
            grp = groups[g]
            kkqk[u, g] = _dot_nt(
                cat([cat([kb[u, h] for h in grp], 1), cat([q[u, h] for h in grp], 1)], 0),
                blockdiag([k[u, h] for h in grp]))
        yield
        n, p, aqk = {}, {}, {}
        for u, g in ug:
            grp = groups[g]
            g_col = pick([gc[u, h][:, :gw] for h in grp])
            g_row = pick([gcum_t[u][GDN_HEADS + h:GDN_HEADS + h + 1, :] for h in grp])
            decay = jnp.exp(jnp.where(tril, g_col - g_row, -jnp.inf))
            a = jnp.where(strict, kkqk[u, g][:chunk] * decay, 0.0)
            aqk[u, g] = (kkqk[u, g][chunk:] * decay).astype(BF16)
            n[u, g] = -a
            p[u, g] = _dot(a, blockdiag_of_lane_blocks(a))
        yield
        span = 2
        while 2 * span < chunk:
            for key in ug:
                both = _dot(cat([n[key], p[key]], 0), blockdiag_of_lane_blocks(p[key]))
                n[key] = n[key] + p[key] + both[:chunk]
                p[key] = both[chunk:]
            span *= 2
            yield
        for key in ug:
            n[key] = n[key] + p[key] + _dot(n[key], blockdiag_of_lane_blocks(p[key]))
        yield
        wu = {}
        for u, h in uh:
            rhs = jnp.concatenate([kb[u, h] * eg[u, h], v[u, h] * beta[u, h]], axis=-1)
            wu[u, h] = rhs + _dot(n[u, h // group][:, lanes(0, h % group, chunk)], rhs)
        yield
        for u in units:
            res[u] = dict(
                w_qd=[jnp.concatenate([wu[u, h][:, :GDN_DK], q[u, h] * eg[u, h]],
                                      axis=0).astype(BF16) for h in heads],
                u=[wu[u, h][:, GDN_DK:] for h in heads],
                kd=[(k[u, h] * jnp.exp(gl[u, h] - gc[u, h])).astype(BF16) for h in heads],
                aqk=[aqk[u, g] for g in range(len(groups))],
                egl=[jnp.exp(gl[u, h]) for h in heads])
        yield

    def phase2(units, res):
        for c in sorted({c for _, c in units}):
            us = [u for u in units if u[1] == c]
            uh = [(u, h) for u in us for h in heads]
            s = {(u, h): s_ref[u[0], h] for u, h in uh}
            ws_qs = {(u, h): _dot(res[u]["w_qd"][h], s[u, h]) for u, h in uh}
            yield
            v_new = {(u, h): res[u]["u"][h] - ws_qs[u, h][:chunk] for u, h in uh}
            o_grp = {(u, g): _dot(res[u]["aqk"][g], blockdiag([v_new[u, h] for h in grp]))
                     for u in us for g, grp in enumerate(groups)}
            for u, h in uh:
                s_ref[u[0], h] = (s[u, h] * res[u]["egl"][h]
                                  + _dot_tn(res[u]["kd"][h], v_new[u, h]))
            yield
            for u, h in uh:
                oh = ws_qs[u, h][chunk:] + o_grp[u, h // group][:, lanes(0, h % group, GDN_DV)]
                oh = oh * lax.rsqrt(jnp.mean(oh * oh, axis=-1, keepdims=True) + EPS) * og
                o_ref[u[0], rows(c), lanes(0, h, GDN_DV)] = oh.astype(o_ref.dtype)
            yield

    units = [(b, c) for b in range(bt) for c in range(nchunks)]
    per_wave = CHUNKS_PER_WAVE if nchunks > 1 else SEQS_PER_WAVE
    waves = [units[i:i + per_wave] for i in range(0, len(units), per_wave)]
    res = {}
    prev = None
    for wave in waves:
        gens = [phase1(wave, res)]
        if prev is not None:
            gens.append(phase2(prev, res))
        yield from _interleaved(gens)
        prev = wave
    yield from phase2(prev, res)


def _gdn_kernel(qkv_ref, ba_ref, s0_ref, hist_ref, cw_ref, alog_ref, dtb_ref, og_ref,
                o_ref, s_ref, carry, *, chunk, group):
    @pl.when(pl.program_id(1) == 0)
    def _():
        s_ref[...] = s0_ref[...]

    _init_conv_carry(carry, hist_ref)
    _round_robin([_gdn_tile(qkv_ref, ba_ref, cw_ref, alog_ref, dtb_ref, og_ref, o_ref, s_ref,
                            carry, chunk=chunk, group=group)])


def _gdn(qkv, ba, s0, hist, cw, alog_pad, dtb_pad, og, *, chunk, bt, lt):
    group = 2 if 2 * chunk == LANES else 1
    bn, seq, _ = qkv.shape
    width, cc = cw.shape
    s_spec = pl.BlockSpec((bt, GDN_HEADS, GDN_DK, GDN_DV), lambda b, l: (b, 0, 0, 0))
    hist_spec = pl.BlockSpec((bt, width - 1, cc), lambda b, l: (b, 0, 0))
    vec = pl.BlockSpec((1, LANES), lambda b, l: (0, 0))
    return pl.pallas_call(
        functools.partial(_gdn_kernel, chunk=chunk, group=group),
        grid=(bn // bt, seq // lt),
        in_specs=[
            pl.BlockSpec((bt, lt, QKV_WIDTH), lambda b, l: (b, l, 0)),
            pl.BlockSpec((bt, lt, BA_PAD), lambda b, l: (b, l, 0)),
            s_spec, hist_spec,
            pl.BlockSpec((width, cc), lambda b, l: (0, 0)),
            vec, vec, vec,
        ],
        out_specs=[
            pl.BlockSpec((bt, lt, GDN_WIDTH), lambda b, l: (b, l, 0)),
            s_spec,
        ],
        out_shape=[
            jax.ShapeDtypeStruct((bn, seq, GDN_WIDTH), qkv.dtype),
            jax.ShapeDtypeStruct(s0.shape, F32),
        ],
        scratch_shapes=[pltpu.VMEM((bt, SUBLANES, cc), F32)],
        compiler_params=_cparams("arbitrary", "arbitrary"),
        name="gdn",
    )(qkv, ba, s0, hist, cw, alog_pad, dtb_pad, og)


PROJ_ROUNDS = ((0, 5), (14, 5), (28, 1), (30, 1), (32, 1), (34, 1), (36, 1))


def _gdn_layer_kernel(x_ref, g_ref, w_qkv, w_gate, w_xq, w_ba, s0_ref, hist_ref, cw_ref,
                      alog_ref, dtb_ref, og_ref, gate_out, xq_out, hist_out, o_ref, s_ref,
                      qkv_scr, ba_scr, carry, *, lt, tiles_per_seq, chunk, group):
    i = pl.program_id(0)
    slot = i % 2
    prev = jnp.maximum(i - 1, 0)

    @pl.when(i == 0)
    def _():
        qkv_scr[1] = jnp.zeros(qkv_scr.shape[1:], F32)
        ba_scr[1] = jnp.zeros(ba_scr.shape[1:], F32)

    @pl.when(prev % tiles_per_seq == 0)
    def _():
        s_ref[...] = s0_ref[...]
        carry[:, SUBLANES - hist_ref.shape[1]:, :] = hist_ref[...]

    proj = _inproj_chunks(
        x_ref, g_ref, (w_qkv, w_gate, w_xq, w_ba, qkv_scr.at[slot], gate_out, xq_out,
                       ba_scr.at[slot], hist_out), mixer="gdn", bt=1, lt=lt)
    rule = _gdn_tile(qkv_scr.at[1 - slot], ba_scr.at[1 - slot], cw_ref, alog_ref, dtb_ref,
                     og_ref, o_ref, s_ref, carry, chunk=chunk, group=group)
    starts = [r for w, n in PROJ_ROUNDS for r in range(w, w + n)]
    _round_robin([rule] + [_delayed(p, starts[k]) for k, p in enumerate(proj)])


def _gdn_layer(x, g, ws, s0, hist, cw, alog_pad, dtb_pad, og, *, chunk, lt, act_dtype):
    group = 2 if 2 * chunk == LANES else 1
    bn, seq, _ = x.shape
    nl = seq // lt
    tiles = bn * nl
    width, cc = cw.shape

    def cur(i):
        return jnp.minimum(i, tiles - 1)

    def prev(i):
        return jnp.maximum(i - 1, 0)

    def tile(n, which):
        return pl.BlockSpec((1, lt, n), lambda i: (which(i), 0, 0))

    def per_seq(shape, which):
        return pl.BlockSpec((1,) + shape, lambda i: (which(i) // nl,) + (0,) * len(shape))

    def whole(a):
        return pl.BlockSpec(a.shape, lambda i: (0,) * a.ndim)

    state = (GDN_HEADS, GDN_DK, GDN_DV)
    gate, xq, hist_new, tok, s_new = pl.pallas_call(
        functools.partial(_gdn_layer_kernel, lt=lt, tiles_per_seq=nl, chunk=chunk, group=group),
        grid=(tiles + 1,),
        in_specs=[tile(D_MODEL, cur), whole(g)] + [whole(w) for w in ws] + [
            per_seq(state, prev), per_seq((width - 1, cc), prev), whole(cw),
            whole(alog_pad), whole(dtb_pad), whole(og)],
        out_specs=[
            tile(BRANCH_WIDTH, cur), tile(X_WIDTH, cur), per_seq((width - 1, cc), cur),
            tile(GDN_WIDTH, prev), per_seq(state, prev)],
        out_shape=[
            jax.ShapeDtypeStruct((tiles, lt, BRANCH_WIDTH), act_dtype),
            jax.ShapeDtypeStruct((tiles, lt, X_WIDTH), act_dtype),
            jax.ShapeDtypeStruct(hist.shape, F32),
            jax.ShapeDtypeStruct((tiles, lt, GDN_WIDTH), act_dtype),
            jax.ShapeDtypeStruct(s0.shape, F32)],
        scratch_shapes=[
            pltpu.VMEM((2, 1, lt, cc), F32),
            pltpu.VMEM((2, 1, lt, BA_PAD), F32),
            pltpu.VMEM((1, SUBLANES, cc), F32)],
        compiler_params=_cparams("arbitrary"),
        name="gdn_layer",
    )(x.reshape(tiles, lt, D_MODEL), g, *ws, s0, hist, cw, alog_pad, dtb_pad, og)
    return (gate.reshape(bn, seq, -1), xq.reshape(bn, seq, -1), tok.reshape(bn, seq, -1),
            s_new, hist_new)


ATTN_ROWS = 256
ATTN_SEQS = 8


def _attn_out_tile(tok_ref, xq_ref, gate_ref, x_ref, kt_ref, vt_ref, w_ref, fg_ref,
                   y_ref, xo_scr, *, bt, lt, final, next_norm=None):
    lane_head = lax.broadcasted_iota(jnp.int32, (1, X_WIDTH), 1) // X_HEAD_DIM
    scale = X_HEAD_DIM ** -0.5
    tm = bt * lt

    def attend(b, rs):
        r = rs.stop - rs.start
        q = xq_ref[b, rs, :]
        q = q * jnp.asarray(scale, q.dtype)
        qx = jnp.concatenate([jnp.where(lane_head == h, q, jnp.zeros_like(q))
                              for h in range(X_HEADS)], axis=0)
        s = _dot(qx, kt_ref[b])
        yield
        e = jnp.exp(s - jnp.max(s, axis=-1, keepdims=True))
        p = e / jnp.sum(e, axis=-1, keepdims=True)
        o4 = _dot_nt(p, vt_ref[b])
        yield
        xo = jnp.where(lane_head == 0, o4[0:r], 0.0)
        for h in range(1, X_HEADS):
            xo = xo + jnp.where(lane_head == h, o4[h * r:(h + 1) * r], 0.0)
        xo_scr[b, rs, :] = xo
        yield

    acc = {}

    def tok_proj():
        sg = _silu(gate_ref[:, :, :GDN_WIDTH].astype(F32).reshape(tm, GDN_WIDTH))
        br = (tok_ref[...].astype(F32).reshape(tm, GDN_WIDTH) * sg).astype(BF16)
        for cols in _col_chunks(D_MODEL):
            acc[cols.start] = (x_ref[:, :, cols].reshape(tm, -1)
                               + _dot(br, w_ref[:GDN_WIDTH, cols]))
            yield

    if bt == 1:
        blocks = [slice(r, r + ATTN_ROWS) for r in range(0, lt, ATTN_ROWS)]
        yield from _interleaved([attend(0, rs) for rs in blocks] + [tok_proj()])
    else:
        def seq_group(i, carry):
            _round_robin([attend(i * ATTN_SEQS + j, slice(0, lt)) for j in range(ATTN_SEQS)])
            return carry

        lax.fori_loop(0, bt // ATTN_SEQS, seq_group, 0)
        yield from tok_proj()

    sg_x = _silu(gate_ref[:, :, GDN_WIDTH:].astype(F32).reshape(tm, X_WIDTH))
    br_x = (xo_scr[...].reshape(tm, X_WIDTH) * sg_x).astype(BF16)
    y = [acc[cols.start] + _dot(br_x, w_ref[GDN_WIDTH:, cols]) for cols in _col_chunks(D_MODEL)]
    if final or next_norm is not None:
        ms = sum(jnp.sum(c * c, axis=-1, keepdims=True) for c in y) * (1.0 / D_MODEL)
        r = lax.rsqrt(ms + EPS)
    if next_norm is not None:
        h_ref, g_ref = next_norm
        for c, cols in zip(y, _col_chunks(D_MODEL)):
            h_ref[:, :, cols] = (c * r * g_ref[:, cols]).astype(h_ref.dtype).reshape(bt, lt, -1)
    if final:
        y = [c * r * fg_ref[:, cols] for c, cols in zip(y, _col_chunks(D_MODEL))]
    for c, cols in zip(y, _col_chunks(D_MODEL)):
        y_ref[:, :, cols] = c.reshape(bt, lt, -1)
    yield


def _attn_out_kernel(tok_ref, xq_ref, gate_ref, x_ref, kt_ref, vt_ref, w_ref, fg_ref,
                     y_ref, xo_scr, *, bt, lt, final):
    _round_robin([_attn_out_tile(tok_ref, xq_ref, gate_ref, x_ref, kt_ref, vt_ref, w_ref,
                                 fg_ref, y_ref, xo_scr, bt=bt, lt=lt, final=final)])


def _attn_out(tok, xq, gate, x, mk, mv, w_bf16, fg, *, layer, bt, lt, final):
    bn, seq, _ = x.shape

    def act(n):
        return pl.BlockSpec((bt, lt, n), lambda b, l: (b, l, 0))

    mem = pl.BlockSpec((None, bt, X_WIDTH, N_MEM), lambda b, l: (layer, b, 0, 0))
    return pl.pallas_call(
        functools.partial(_attn_out_kernel, bt=bt, lt=lt, final=final),
        grid=(bn // bt, seq // lt),
        in_specs=[
            act(GDN_WIDTH), act(X_WIDTH), act(BRANCH_WIDTH), act(D_MODEL), mem, mem,
            pl.BlockSpec((BRANCH_WIDTH, D_MODEL), lambda b, l: (0, 0)),
            pl.BlockSpec((1, D_MODEL), lambda b, l: (0, 0)),
        ],
        out_specs=act(D_MODEL),
        out_shape=jax.ShapeDtypeStruct(x.shape, F32),
        scratch_shapes=[pltpu.VMEM((bt, lt, X_WIDTH), F32)],
        compiler_params=_cparams("arbitrary", "arbitrary"),
        name="attn_out",
    )(tok, xq, gate, x, mk, mv, w_bf16, fg)


SCONV_PROJ_ROUNDS = (0, 0, 1, 1, 2, 3, 4, 4)
TAIL_ATTN1_START = 3


def _tail_kernel(tok_ref, xq_ref, gate_ref, x_ref, kt0_ref, vt0_ref, kt1_ref, vt1_ref,
                 w_out0_ref, w_out1_ref, fg_ref, g_ref, w_in_ref, hist_ref, cw_ref,
                 y_ref, hist_out,
                 xo0_scr, xo1_scr, h_scr, x1_scr, tok1_scr, gate1_scr, xq1_scr, carry,
                 *, bt, lt, tiles_per_seq):
    i = pl.program_id(0)
    a2, b2, c2 = i % 2, (i + 1) % 2, i % 2
    a3, c3 = i % 3, (i + 1) % 3
    prev = jnp.maximum(i - 1, 0)

    @pl.when(i == 0)
    def _():
        for scr, slot in ((h_scr, 1), (x1_scr, 1), (x1_scr, 2), (tok1_scr, 0), (gate1_scr, 0),
                          (xq1_scr, 0)):
            scr[slot] = jnp.zeros(scr.shape[1:], scr.dtype)

    @pl.when(prev % tiles_per_seq == 0)
    def _():
        carry[:, SUBLANES - hist_ref.shape[1]:, :] = hist_ref[...]

    attn0 = _attn_out_tile(tok_ref, xq_ref, gate_ref, x_ref, kt0_ref, vt0_ref, w_out0_ref,
                           fg_ref, x1_scr.at[a3], xo0_scr, bt=bt, lt=lt, final=False,
                           next_norm=(h_scr.at[a2], g_ref))
    proj = _inproj_chunks(h_scr.at[1 - a2], None,
                          (w_in_ref, hist_ref, cw_ref, tok1_scr.at[b2], gate1_scr.at[b2],
                           xq1_scr.at[b2], hist_out, carry), mixer="sconv", bt=bt, lt=lt)
    attn1 = _attn_out_tile(tok1_scr.at[c2], xq1_scr.at[c2], gate1_scr.at[c2], x1_scr.at[c3],
                           kt1_ref, vt1_ref, w_out1_ref, fg_ref, y_ref, xo1_scr,
                           bt=bt, lt=lt, final=True)
    _round_robin([attn0, _delayed(attn1, TAIL_ATTN1_START)]
                 + [_delayed(p, SCONV_PROJ_ROUNDS[k]) for k, p in enumerate(proj)])


def _tail(tok, xq, gate, x, mk, mv, w_out_bf16, fg, g, w_in, hist, cw, *, bt, lt, slot_dtype):
    bn, seq, _ = x.shape
    nl = seq // lt
    assert bt == 1 or nl == 1
    tiles = bn * nl // bt
    width, cc = cw.shape

    def clamp(i, lag):
        return jnp.clip(i - lag, 0, tiles - 1)

    def tile(n, lag):
        return pl.BlockSpec((bt, lt, n), lambda i: (clamp(i, lag), 0, 0))

    def whole(a):
        return pl.BlockSpec(a.shape, lambda i: (0,) * a.ndim, pipeline_mode=pl.Buffered(1))

    def mem(layer, lag):
        return pl.BlockSpec((None, bt, X_WIDTH, N_MEM),
                            lambda i: (layer, clamp(i, lag) // nl, 0, 0))

    def tiled(a):
        return a.reshape(tiles * bt, lt, a.shape[-1])

    def slots(n, width_, dtype):
        return pltpu.VMEM((n, bt, lt, width_), dtype)

    hist_spec = pl.BlockSpec((bt, width - 1, cc), lambda i: (clamp(i, 1) // nl, 0, 0))
    y, hist_new = pl.pallas_call(
        functools.partial(_tail_kernel, bt=bt, lt=lt, tiles_per_seq=nl),
        grid=(tiles + 2,),
        in_specs=[
            tile(GDN_WIDTH, 0), tile(X_WIDTH, 0), tile(BRANCH_WIDTH, 0), tile(D_MODEL, 0),
            mem(0, 0), mem(0, 0), mem(1, 2), mem(1, 2),
            whole(w_out_bf16[0]), whole(w_out_bf16[1]), whole(fg), whole(g), whole(w_in),
            hist_spec, whole(cw)],
        out_specs=[tile(D_MODEL, 2), hist_spec],
        out_shape=[
            jax.ShapeDtypeStruct((tiles * bt, lt, D_MODEL), F32),
            jax.ShapeDtypeStruct(hist.shape, F32)],
        scratch_shapes=[
            pltpu.VMEM((bt, lt, X_WIDTH), F32), pltpu.VMEM((bt, lt, X_WIDTH), F32),
            slots(2, D_MODEL, slot_dtype), slots(3, D_MODEL, F32),
            slots(2, cc, slot_dtype), slots(2, BRANCH_WIDTH, slot_dtype),
            slots(2, X_WIDTH, slot_dtype),
            pltpu.VMEM((bt, SUBLANES, cc), F32)],
        compiler_params=_cparams("arbitrary"),
        name="tail",
    )(tiled(tok), tiled(xq), tiled(gate), tiled(x), mk, mv, mk, mv,
      w_out_bf16[0], w_out_bf16[1], fg, g, w_in, hist, cw)
    return y.reshape(x.shape), hist_new


def _trunk(x, mem_k, mem_v, gdn_s, gdn_conv, sc_conv, p, *, bt, lt, chunk, gdn_bt, gdn_lt,
           attn_bt, attn_lt, act_dtype):
    if gdn_bt == 1 and bt == 1:
        gate, xq, tok, s_new, gconv_new = _gdn_layer(
            x, p["norm_g"][0:1], p["w_in_a"], gdn_s, gdn_conv, p["conv_w_a"], p["alog_pad"],
            p["dtb_pad"], p["o_norm_g"], chunk=chunk, lt=gdn_lt, act_dtype=act_dtype)
    else:
        qkv, gate, xq, ba, gconv_new = _inproj(
            x, p["norm_g"][0:1], p["w_in_a"], None, p["conv_w_a"], mixer="gdn", bt=bt, lt=lt,
            act_dtype=act_dtype)
        tok, s_new = _gdn(qkv, ba, gdn_s, gdn_conv, p["conv_w_a"], p["alog_pad"],
                          p["dtb_pad"], p["o_norm_g"], chunk=chunk, bt=gdn_bt, lt=gdn_lt)
    y, sconv_new = _tail(tok, xq, gate, x, mem_k, mem_v, p["w_out"], p["final_norm_g"],
                         p["norm_g"][1:2], p["w_in_b"][0], sc_conv, p["conv_w_b"],
                         bt=attn_bt, lt=attn_lt, slot_dtype=act_dtype)
    return y, s_new[None], gconv_new[None], sconv_new[None]


def kernel(x_prompt, x_sample, mem_prompt, state_gdn, state_gdn_conv, state_sconv, cache_mem_k, cache_mem_v, norm_g, w_in_a, conv_w_a, a_log, dt_bias, o_norm_g, w_in_b, conv_w_b, mem_norm_g, w_mem_kv, w_out, final_norm_g):
    bp = x_prompt.shape[0]

    wa = jnp.transpose(w_in_a[0])
    c_b = QKV_WIDTH
    c_g = c_b + 2 * GDN_HEADS
    c_x = c_g + BRANCH_WIDTH
    wa = [wa[:c_b], wa[c_g:c_x], wa[c_x:],
          jnp.concatenate([wa[c_b:c_g], jnp.zeros((BA_PAD - 2 * GDN_HEADS, D_MODEL), wa.dtype)])]
    wa = [w.astype(BF16) for w in wa]
    pad_lo = jnp.zeros((GDN_HEADS,), F32)
    pad_hi = jnp.zeros((LANES - 2 * GDN_HEADS,), F32)
    params = {
        "norm_g": norm_g,
        "w_in_a": wa,
        "conv_w_a": conv_w_a[0],
        "alog_pad": jnp.concatenate([pad_lo, a_log[0], pad_hi])[None],
        "dtb_pad": jnp.concatenate([pad_lo, dt_bias[0], pad_hi])[None],
        "o_norm_g": o_norm_g,
        "w_in_b": [w_in_b[0].astype(BF16)],
        "conv_w_b": conv_w_b[0],
        "w_out": w_out.astype(BF16),
        "final_norm_g": final_norm_g[None],
    }

    def to_cache(t):
        t = t.reshape(t.shape[0], t.shape[1], X_HEADS, X_HEAD_DIM, t.shape[3])
        return jnp.transpose(t, (0, 1, 4, 2, 3))

    def from_cache(t):
        t = jnp.transpose(t, (0, 1, 3, 4, 2))
        return t.reshape(t.shape[0], t.shape[1], X_WIDTH, t.shape[4])

    mem_kt, mem_vt = _memkv(mem_prompt, mem_norm_g[None], w_mem_kv.astype(BF16))
    mem_k_p = to_cache(mem_kt)
    mem_v_p = to_cache(mem_vt)

    s0_p = jnp.zeros((bp,) + state_gdn.shape[2:], F32)
    gc0_p = jnp.zeros((bp,) + state_gdn_conv.shape[2:], F32)
    sc0_p = jnp.zeros((bp,) + state_sconv.shape[2:], F32)
    y_p, s_p, gc_p, sc_p = _trunk(x_prompt, mem_kt, mem_vt, s0_p, gc0_p, sc0_p,
                                  params, bt=1, lt=1024, chunk=GDN_CHUNK, gdn_bt=1, gdn_lt=512,
                                  attn_bt=1, attn_lt=512, act_dtype=BF16)
    dec_seq = x_sample.shape[1]
    y_s, s_s, gc_s, sc_s = _trunk(x_sample, from_cache(cache_mem_k), from_cache(cache_mem_v),
                                  state_gdn[0], state_gdn_conv[0], state_sconv[0], params,
                                  bt=32, lt=dec_seq, chunk=dec_seq, gdn_bt=16, gdn_lt=dec_seq,
                                  attn_bt=8, attn_lt=dec_seq, act_dtype=F32)
    return (y_p, y_s, s_p, gc_p, sc_p, mem_k_p, mem_v_p, s_s, gc_s, sc_s)
```

```python
import functools

import jax
import jax.numpy as jnp
from jax import lax
from jax.experimental import pallas as pl
from jax.experimental.pallas import tpu as pltpu

F32 = jnp.float32
BF16 = jnp.bfloat16

D_MODEL = 1024
N_MEM = 256
X_WIDTH = 256
X_HEADS = 4
X_HEAD_DIM = 64
GDN_HEADS = 6
GDN_DK = 128
GDN_DV = 128
GDN_WIDTH = GDN_HEADS * GDN_DV
QK_WIDTH = GDN_HEADS * GDN_DK
QKV_WIDTH = 2 * QK_WIDTH + GDN_WIDTH
GDN_CHUNK = 64
SC_WIDTH = 768
BRANCH_WIDTH = 1024
EPS = 1e-6

LANES = 128
SUBLANES = 8
COL_CHUNK = 256
BA_PAD = LANES
VMEM_LIMIT = 56 * 1024 * 1024

HIGHEST = lax.Precision.HIGHEST


def _cparams(*sem):
    return pltpu.CompilerParams(dimension_semantics=sem, vmem_limit_bytes=VMEM_LIMIT)


def _rms_rows(x, g):
    r = lax.rsqrt(jnp.mean(x * x, axis=-1, keepdims=True) + EPS)
    return x * r * g


def _silu(x):
    h = 0.5 * x
    return h + h * jnp.tanh(h)


def _dot(a, b):
    return jnp.dot(a.astype(BF16), b.astype(BF16), preferred_element_type=F32)


def _dot_nt(a, b):
    return lax.dot_general(a.astype(BF16), b.astype(BF16), (((1,), (1,)), ((), ())),
                           preferred_element_type=F32)


def _dot_tn(a, b):
    return lax.dot_general(a.astype(BF16), b.astype(BF16), (((0,), (0,)), ((), ())),
                           preferred_element_type=F32)


def _dot_f32(a, b):
    return jnp.dot(a, b, preferred_element_type=F32, precision=HIGHEST)


def _col_chunks(n):
    return [slice(c, min(c + COL_CHUNK, n)) for c in range(0, n, COL_CHUNK)]


def _interleaved(gens):
    gens = list(gens)
    while gens:
        alive = []
        for g in gens:
            try:
                next(g)
                alive.append(g)
            except StopIteration:
                pass
        gens = alive
        yield


def _round_robin(gens):
    for _ in _interleaved(gens):
        pass


def _delayed(gen, rounds):
    for _ in range(rounds):
        yield
    yield from gen


def _memkv_kernel(m_ref, g_ref, w_ref, kt_ref, vt_ref):
    h = _rms_rows(m_ref[...], g_ref[...]).astype(BF16)
    for layer in range(w_ref.shape[0]):
        kv = jnp.dot(h, w_ref[layer], preferred_element_type=F32)
        kt_ref[layer] = kv[:, :X_WIDTH].T
        vt_ref[layer] = kv[:, X_WIDTH:].T


def _memkv(mem, g, w_bf16):
    bn, n_mem, _ = mem.shape
    depth = w_bf16.shape[0]
    out = jax.ShapeDtypeStruct((depth, bn, X_WIDTH, n_mem), F32)
    out_spec = pl.BlockSpec((depth, None, X_WIDTH, n_mem), lambda b: (0, b, 0, 0))
    return pl.pallas_call(
        _memkv_kernel,
        grid=(bn,),
        in_specs=[
            pl.BlockSpec((None, n_mem, D_MODEL), lambda b: (b, 0, 0)),
            pl.BlockSpec((1, D_MODEL), lambda b: (0, 0)),
            pl.BlockSpec(w_bf16.shape, lambda b: (0, 0, 0)),
        ],
        out_specs=[out_spec, out_spec],
        out_shape=[out, out],
        compiler_params=_cparams("arbitrary"),
        name="memkv",
    )(mem, g, w_bf16)


def _init_conv_carry(carry, hist_ref):
    @pl.when(pl.program_id(1) == 0)
    def _():
        carry[:, SUBLANES - hist_ref.shape[1]:, :] = hist_ref[...]


def _causal_conv_cols(x, cols, carry, hist_out, cw_ref, seqs=slice(None)):
    width = cw_ref.shape[0]
    bt, lt, n = x.shape
    g = lt // SUBLANES
    xe = jnp.concatenate([carry[seqs, :, cols], x], axis=1).reshape(bt, g + 1, SUBLANES, n)
    sub = lax.broadcasted_iota(jnp.int32, (1, 1, SUBLANES, n), 2)
    y = x.reshape(bt, g, SUBLANES, n) * cw_ref[width - 1:width, cols]
    for j in range(width - 1):
        s = width - 1 - j
        r = pltpu.roll(xe, s, axis=2)
        y = y + jnp.where(sub < s, r[:, :g], r[:, 1:]) * cw_ref[j:j + 1, cols]
    carry[seqs, :, cols] = x[:, lt - SUBLANES:, :]
    if hist_out is not None:
        hist_out[seqs, :, cols] = x[:, lt - (width - 1):, :]
    return y.reshape(bt, lt, n)


def _inproj_chunks(x_ref, g_ref, rest, *, mixer, bt, lt):
    tm = bt * lt
    if g_ref is None:
        h = x_ref[...].reshape(tm, D_MODEL)
    else:
        h = _rms_rows(x_ref[...].reshape(tm, D_MODEL), g_ref[...]).astype(BF16)
    if mixer == "gdn":
        n_out = (len(rest) - 1) // 2
        w_refs, (mix_out, *plain_outs, hist_out) = rest[:n_out], rest[n_out:]

        def proj(i, cols):
            return _dot_nt(h, w_refs[i][cols, :]).reshape(bt, lt, -1)
    else:
        w_ref, _, cw_ref, mix_out, *plain_outs, hist_out, carry = rest
        bases = [0, 3 * SC_WIDTH]
        for out in plain_outs[:-1]:
            bases.append(bases[-1] + out.shape[-1])

        def proj(i, cols, offset=0):
            cols = slice(bases[i] + offset + cols.start, bases[i] + offset + cols.stop)
            return jnp.dot(h, w_ref[:, cols], preferred_element_type=F32).reshape(bt, lt, -1)

    def mixer_chunk(cols):
        if mixer == "gdn":
            y = proj(0, cols)
            hist_out[:, :, cols] = y[:, lt - hist_out.shape[1]:, :]
        else:
            gate_b = proj(0, cols)
            pre = proj(0, cols, SC_WIDTH) * proj(0, cols, 2 * SC_WIDTH)
            yield
            y = gate_b * _causal_conv_cols(pre, cols, carry, hist_out, cw_ref)
        mix_out[:, :, cols] = y.astype(mix_out.dtype)
        yield

    def plain_chunk(i, out, cols):
        out[:, :, cols] = proj(i, cols).astype(out.dtype)
        yield

    gens = [mixer_chunk(cols) for cols in _col_chunks(mix_out.shape[-1])]
    for i, out in enumerate(plain_outs):
        gens += [plain_chunk(i + 1, out, cols) for cols in _col_chunks(out.shape[-1])]
    return gens


def _inproj_kernel(x_ref, g_ref, *rest, mixer, bt, lt):
    if mixer == "sconv":
        _init_conv_carry(rest[-1], rest[1])
    gens = _inproj_chunks(x_ref, g_ref, rest, mixer=mixer, bt=bt, lt=lt)
    _round_robin([_delayed(g, i) for i, g in enumerate(gens)])


def _inproj(x, g, ws, hist, cw, *, mixer, bt, lt, act_dtype):
    bn, seq, _ = x.shape
    width, cc = cw.shape

    def act(n, dtype=act_dtype):
        return (pl.BlockSpec((bt, lt, n), lambda b, l: (b, l, 0)),
                jax.ShapeDtypeStruct((bn, seq, n), dtype))

    hist_spec = pl.BlockSpec((bt, width - 1, cc), lambda b, l: (b, 0, 0))
    args = [x, g, *ws]
    in_specs = [
        pl.BlockSpec((bt, lt, D_MODEL), lambda b, l: (b, l, 0)),
        pl.BlockSpec((1, D_MODEL), lambda b, l: (0, 0)),
    ] + [pl.BlockSpec(w.shape, lambda b, l: (0, 0)) for w in ws]
    outs = [act(cc), act(BRANCH_WIDTH), act(X_WIDTH)]
    scratch = []
    if mixer == "gdn":
        outs.append(act(BA_PAD, F32))
    else:
        args += [hist, cw]
        in_specs += [hist_spec, pl.BlockSpec((width, cc), lambda b, l: (0, 0))]
        scratch.append(pltpu.VMEM((bt, SUBLANES, cc), F32))
    outs.append((hist_spec, jax.ShapeDtypeStruct((bn, width - 1, cc), F32)))
    return pl.pallas_call(
        functools.partial(_inproj_kernel, mixer=mixer, bt=bt, lt=lt),
        grid=(bn // bt, seq // lt),
        in_specs=in_specs,
        out_specs=[o[0] for o in outs],
        out_shape=[o[1] for o in outs],
        scratch_shapes=scratch,
        compiler_params=_cparams("arbitrary", "arbitrary"),
        name="inproj_" + mixer,
    )(*args)


CHUNKS_PER_WAVE = 4
SEQS_PER_WAVE = 8


def _gdn_tile(qkv_ref, ba_ref, cw_ref, alog_ref, dtb_ref, og_ref, o_ref, s_ref, carry,
              *, chunk, group):
    bt, lt, _ = qkv_ref.shape
    nchunks = lt // chunk
    heads = range(GDN_HEADS)
    groups = [tuple(range(g, g + group)) for g in range(0, GDN_HEADS, group)]
    gw = group * chunk
    row = lax.broadcasted_iota(jnp.int32, (chunk, gw), 0)
    col = lax.broadcasted_iota(jnp.int32, (chunk, gw), 1) % chunk
    tril = row >= col
    strict = row > col
    lane_blk = lax.broadcasted_iota(jnp.int32, (1, gw), 1) // chunk
    ones_tril = (lax.broadcasted_iota(jnp.int32, (chunk, chunk), 0)
                 >= lax.broadcasted_iota(jnp.int32, (chunk, chunk), 1)).astype(F32)
    neg_a = -jnp.exp(alog_ref[...])
    dtb = dtb_ref[...]
    og = og_ref[...]

    def rows(c):
        return slice(c * chunk, (c + 1) * chunk)

    def lanes(base, h, n):
        return slice(base + h * n, base + (h + 1) * n)

    def cat(xs, axis):
        return xs[0] if len(xs) == 1 else jnp.concatenate(xs, axis=axis)

    def pick(xs):
        out = xs[0]
        for j in range(1, len(xs)):
            out = jnp.where(lane_blk == j, xs[j], out)
        return out

    def blockdiag(xs):
        z = jnp.zeros_like(xs[0])
        return cat([cat([x if i == j else z for i in range(len(xs))], 1)
                    for j, x in enumerate(xs)], 0)

    def blockdiag_of_lane_blocks(x):
        if group == 1:
            return x
        return cat([jnp.where(lane_blk == j, x, 0.0) for j in range(group)], 0)

    def conv_silu(u, cols):
        b, c = u
        x = qkv_ref[b, rows(c), cols].astype(F32)
        y = _causal_conv_cols(x[None], cols, carry, None, cw_ref, seqs=slice(b, b + 1))
        return _silu(y[0])

    def phase1(units, res):
        uh = [(u, h) for u in units for h in heads]
        ug = [(u, g) for u in units for g in range(len(groups))]
        gcum, gcum_t, beta_all = {}, {}, {}
        for u in units:
            ba = ba_ref[u[0], rows(u[1]), :]
            beta_all[u] = 1.0 / (1.0 + jnp.exp(-ba))
            z = ba + dtb
            softplus = jnp.maximum(z, 0.0) + jnp.log1p(jnp.exp(-jnp.abs(z)))
            gcum[u] = _dot_f32(ones_tril, neg_a * softplus)
        q, k, v = {}, {}, {}
        for u in units:
            for h in heads:
                q[u, h] = conv_silu(u, lanes(0, h, GDN_DK))
                k[u, h] = conv_silu(u, lanes(QK_WIDTH, h, GDN_DK))
                v[u, h] = conv_silu(u, lanes(2 * QK_WIDTH, h, GDN_DV))
            yield
        for key in uh:
            x = q[key]
            q[key] = x * (lax.rsqrt(jnp.sum(x * x, axis=-1, keepdims=True) + EPS)
                          * (GDN_DK ** -0.5))
            x = k[key]
            k[key] = x * lax.rsqrt(jnp.sum(x * x, axis=-1, keepdims=True) + EPS)
        yield
        for u in units:
            gcum_t[u] = cat([gcum[u]] * group, 0).T
        gc = {(u, h): jnp.broadcast_to(gcum[u][:, GDN_HEADS + h:GDN_HEADS + h + 1],
                                       (chunk, GDN_DK)) for u, h in uh}
        gl = {(u, h): gcum[u][chunk - 1:chunk, GDN_HEADS + h:GDN_HEADS + h + 1]
              for u, h in uh}
        beta = {(u, h): jnp.broadcast_to(beta_all[u][:, h:h + 1], (chunk, GDN_DK))
                for u, h in uh}
        eg = {key: jnp.exp(gc[key]) for key in uh}
        kb = {key: k[key] * beta[key] for key in uh}
        kkqk = {}
        for u, g in ug:
            grp = groups[g]
            kkqk[u, g] = _dot_nt(
                cat([cat([kb[u, h] for h in grp], 1), cat([q[u, h] for h in grp], 1)], 0),
                blockdiag([k[u, h] for h in grp]))
        yield
        n, p, aqk = {}, {}, {}
        for u, g in ug:
            grp = groups[g]
            g_col = pick([gc[u, h][:, :gw] for h in grp])
            g_row = pick([gcum_t[u][GDN_HEADS + h:GDN_HEADS + h + 1, :] for h in grp])
            decay = jnp.exp(jnp.where(tril, g_col - g_row, -jnp.inf))
            a = jnp.where(strict, kkqk[u, g][:chunk] * decay, 0.0)
            aqk[u, g] = (kkqk[u, g][chunk:] * decay).astype(BF16)
            n[u, g] = -a
            p[u, g] = _dot(a, blockdiag_of_lane_blocks(a))
        yield
        span = 2
        while 2 * span < chunk:
            for key in ug:
                both = _dot(cat([n[key], p[key]], 0), blockdiag_of_lane_blocks(p[key]))
                n[key] = n[key] + p[key] + both[:chunk]
                p[key] = both[chunk:]
            span *= 2
            yield
        for key in ug:
            n[key] = n[key] + p[key] + _dot(n[key], blockdiag_of_lane_blocks(p[key]))
        yield
        wu = {}
        for u, h in uh:
            rhs = jnp.concatenate([kb[u, h] * eg[u, h], v[u, h] * beta[u, h]], axis=-1)
            wu[u, h] = rhs + _dot(n[u, h // group][:, lanes(0, h % group, chunk)], rhs)
        yield
        for u in units:
            res[u] = dict(
                w_qd=[jnp.concatenate([wu[u, h][:, :GDN_DK], q[u, h] * eg[u, h]],
                                      axis=0).astype(BF16) for h in heads],
                u=[wu[u, h][:, GDN_DK:] for h in heads],
                kd=[(k[u, h] * jnp.exp(gl[u, h] - gc[u, h])).astype(BF16) for h in heads],
                aqk=[aqk[u, g] for g in range(len(groups))],
                egl=[jnp.exp(gl[u, h]) for h in heads])
        yield

    def phase2(units, res):
        for c in sorted({c for _, c in units}):
            us = [u for u in units if u[1] == c]
            uh = [(u, h) for u in us for h in heads]
            s = {(u, h): s_ref[u[0], h] for u, h in uh}
            ws_qs = {(u, h): _dot(res[u]["w_qd"][h], s[u, h]) for u, h in uh}
            yield
            v_new = {(u, h): res[u]["u"][h] - ws_qs[u, h][:chunk] for u, h in uh}
            o_grp = {(u, g): _dot(res[u]["aqk"][g], blockdiag([v_new[u, h] for h in grp]))
                     for u in us for g, grp in enumerate(groups)}
            for u, h in uh:
                s_ref[u[0], h] = (s[u, h] * res[u]["egl"][h]
                                  + _dot_tn(res[u]["kd"][h], v_new[u, h]))
            yield
            for u, h in uh:
                oh = ws_qs[u, h][chunk:] + o_grp[u, h // group][:, lanes(0, h % group, GDN_DV)]
                oh = oh * lax.rsqrt(jnp.mean(oh * oh, axis=-1, keepdims=True) + EPS) * og
                o_ref[u[0], rows(c), lanes(0, h, GDN_DV)] = oh.astype(o_ref.dtype)
            yield

    units = [(b, c) for b in range(bt) for c in range(nchunks)]
    per_wave = CHUNKS_PER_WAVE if nchunks > 1 else SEQS_PER_WAVE
    waves = [units[i:i + per_wave] for i in range(0, len(units), per_wave)]
    res = {}
    prev = None
    for wave in waves:
        gens = [phase1(wave, res)]
        if prev is not None:
            gens.append(phase2(prev, res))
        yield from _interleaved(gens)
        prev = wave
    yield from phase2(prev, res)


def _gdn_kernel(qkv_ref, ba_ref, s0_ref, hist_ref, cw_ref, alog_ref, dtb_ref, og_ref,
                o_ref, s_ref, carry, *, chunk, group):
    @pl.when(pl.program_id(1) == 0)
    def _():
        s_ref[...] = s0_ref[...]

    _init_conv_carry(carry, hist_ref)
    _round_robin([_gdn_tile(qkv_ref, ba_ref, cw_ref, alog_ref, dtb_ref, og_ref, o_ref, s_ref,
                            carry, chunk=chunk, group=group)])


def _gdn(qkv, ba, s0, hist, cw, alog_pad, dtb_pad, og, *, chunk, bt, lt):
    group = 2 if 2 * chunk == LANES else 1
    bn, seq, _ = qkv.shape
    width, cc = cw.shape
    s_spec = pl.BlockSpec((bt, GDN_HEADS, GDN_DK, GDN_DV), lambda b, l: (b, 0, 0, 0))
    hist_spec = pl.BlockSpec((bt, width - 1, cc), lambda b, l: (b, 0, 0))
    vec = pl.BlockSpec((1, LANES), lambda b, l: (0, 0))
    return pl.pallas_call(
        functools.partial(_gdn_kernel, chunk=chunk, group=group),
        grid=(bn // bt, seq // lt),
        in_specs=[
            pl.BlockSpec((bt, lt, QKV_WIDTH), lambda b, l: (b, l, 0)),
            pl.BlockSpec((bt, lt, BA_PAD), lambda b, l: (b, l, 0)),
            s_spec, hist_spec,
            pl.BlockSpec((width, cc), lambda b, l: (0, 0)),
            vec, vec, vec,
        ],
        out_specs=[
            pl.BlockSpec((bt, lt, GDN_WIDTH), lambda b, l: (b, l, 0)),
            s_spec,
        ],
        out_shape=[
            jax.ShapeDtypeStruct((bn, seq, GDN_WIDTH), qkv.dtype),
            jax.ShapeDtypeStruct(s0.shape, F32),
        ],
        scratch_shapes=[pltpu.VMEM((bt, SUBLANES, cc), F32)],
        compiler_params=_cparams("arbitrary", "arbitrary"),
        name="gdn",
    )(qkv, ba, s0, hist, cw, alog_pad, dtb_pad, og)


PROJ_ROUNDS = ((0, 5), (14, 5), (28, 1), (30, 1), (32, 1), (34, 1), (36, 1))


def _gdn_layer_kernel(x_ref, g_ref, w_qkv, w_gate, w_xq, w_ba, s0_ref, hist_ref, cw_ref,
                      alog_ref, dtb_ref, og_ref, gate_out, xq_out, hist_out, o_ref, s_ref,
                      qkv_scr, ba_scr, carry, *, bt, lt, tiles_per_seq, chunk, group):
    i = pl.program_id(0)
    slot = i % 2
    prev = jnp.maximum(i - 1, 0)

    @pl.when(i == 0)
    def _():
        qkv_scr[1] = jnp.zeros(qkv_scr.shape[1:], F32)
        ba_scr[1] = jnp.zeros(ba_scr.shape[1:], F32)

    @pl.when(prev % tiles_per_seq == 0)
    def _():
        s_ref[...] = s0_ref[...]
        carry[:, SUBLANES - hist_ref.shape[1]:, :] = hist_ref[...]

    proj = _inproj_chunks(
        x_ref, g_ref, (w_qkv, w_gate, w_xq, w_ba, qkv_scr.at[slot], gate_out, xq_out,
                       ba_scr.at[slot], hist_out), mixer="gdn", bt=bt, lt=lt)
    rule = _gdn_tile(qkv_scr.at[1 - slot], ba_scr.at[1 - slot], cw_ref, alog_ref, dtb_ref,
                     og_ref, o_ref, s_ref, carry, chunk=chunk, group=group)
    starts = [r for w, n in PROJ_ROUNDS for r in range(w, w + n)]
    _round_robin([rule] + [_delayed(p, starts[k]) for k, p in enumerate(proj)])


def _gdn_layer(x, g, ws, s0, hist, cw, alog_pad, dtb_pad, og, *, chunk, bt, lt, act_dtype):
    group = 2 if 2 * chunk == LANES else 1
    bn, seq, _ = x.shape
    nl = seq // lt
    assert bt == 1 or nl == 1
    tiles = bn * nl // bt
    width, cc = cw.shape

    def cur(i):
        return jnp.minimum(i, tiles - 1)

    def prev(i):
        return jnp.maximum(i - 1, 0)

    def tile(n, which):
        return pl.BlockSpec((bt, lt, n), lambda i: (which(i), 0, 0))

    def per_seq(shape, which):
        return pl.BlockSpec((bt,) + shape, lambda i: (which(i) // nl,) + (0,) * len(shape))

    def whole(a):
        return pl.BlockSpec(a.shape, lambda i: (0,) * a.ndim)

    state = (GDN_HEADS, GDN_DK, GDN_DV)
    gate, xq, hist_new, tok, s_new = pl.pallas_call(
        functools.partial(_gdn_layer_kernel, bt=bt, lt=lt, tiles_per_seq=nl, chunk=chunk,
                          group=group),
        grid=(tiles + 1,),
        in_specs=[tile(D_MODEL, cur), whole(g)] + [whole(w) for w in ws] + [
            per_seq(state, prev), per_seq((width - 1, cc), prev), whole(cw),
            whole(alog_pad), whole(dtb_pad), whole(og)],
        out_specs=[
            tile(BRANCH_WIDTH, cur), tile(X_WIDTH, cur), per_seq((width - 1, cc), cur),
            tile(GDN_WIDTH, prev), per_seq(state, prev)],
        out_shape=[
            jax.ShapeDtypeStruct((tiles * bt, lt, BRANCH_WIDTH), act_dtype),
            jax.ShapeDtypeStruct((tiles * bt, lt, X_WIDTH), act_dtype),
            jax.ShapeDtypeStruct(hist.shape, F32),
            jax.ShapeDtypeStruct((tiles * bt, lt, GDN_WIDTH), act_dtype),
            jax.ShapeDtypeStruct(s0.shape, F32)],
        scratch_shapes=[
            pltpu.VMEM((2, bt, lt, cc), F32),
            pltpu.VMEM((2, bt, lt, BA_PAD), F32),
            pltpu.VMEM((bt, SUBLANES, cc), F32)],
        compiler_params=_cparams("arbitrary"),
        name="gdn_layer",
    )(x.reshape(tiles * bt, lt, D_MODEL), g, *ws, s0, hist, cw, alog_pad, dtb_pad, og)
    return (gate.reshape(bn, seq, -1), xq.reshape(bn, seq, -1), tok.reshape(bn, seq, -1),
            s_new, hist_new)


ATTN_ROWS = 256
ATTN_SEQS = 16


def _attn_out_tile(tok_ref, xq_ref, gate_ref, x_ref, kt_ref, vt_ref, w_ref, fg_ref,
                   y_ref, xo_scr, *, bt, lt, final, next_norm=None):
    lane_head = lax.broadcasted_iota(jnp.int32, (1, X_WIDTH), 1) // X_HEAD_DIM
    scale = X_HEAD_DIM ** -0.5
    tm = bt * lt

    def attend(b, rs):
        r = rs.stop - rs.start
        q = xq_ref[b, rs, :]
        q = q * jnp.asarray(scale, q.dtype)
        qx = jnp.concatenate([jnp.where(lane_head == h, q, jnp.zeros_like(q))
                              for h in range(X_HEADS)], axis=0)
        s = _dot(qx, kt_ref[b])
        yield
        e = jnp.exp(s - jnp.max(s, axis=-1, keepdims=True))
        p = e / jnp.sum(e, axis=-1, keepdims=True)
        o4 = _dot_nt(p, vt_ref[b])
        yield
        xo = jnp.where(lane_head == 0, o4[0:r], 0.0)
        for h in range(1, X_HEADS):
            xo = xo + jnp.where(lane_head == h, o4[h * r:(h + 1) * r], 0.0)
        xo_scr[b, rs, :] = xo
        yield

    acc = {}

    def tok_proj():
        sg = _silu(gate_ref[:, :, :GDN_WIDTH].astype(F32).reshape(tm, GDN_WIDTH))
        br = (tok_ref[...].astype(F32).reshape(tm, GDN_WIDTH) * sg).astype(BF16)
        for cols in _col_chunks(D_MODEL):
            acc[cols.start] = (x_ref[:, :, cols].reshape(tm, -1)
                               + _dot(br, w_ref[:GDN_WIDTH, cols]))
            yield

    if bt == 1:
        blocks = [slice(r, r + ATTN_ROWS) for r in range(0, lt, ATTN_ROWS)]
        yield from _interleaved([attend(0, rs) for rs in blocks] + [tok_proj()])
    else:
        def seq_group(i, carry):
            _round_robin([attend(i * ATTN_SEQS + j, slice(0, lt)) for j in range(ATTN_SEQS)])
            return carry

        lax.fori_loop(0, bt // ATTN_SEQS, seq_group, 0)
        yield from tok_proj()

    sg_x = _silu(gate_ref[:, :, GDN_WIDTH:].astype(F32).reshape(tm, X_WIDTH))
    br_x = (xo_scr[...].reshape(tm, X_WIDTH) * sg_x).astype(BF16)
    y = [acc[cols.start] + _dot(br_x, w_ref[GDN_WIDTH:, cols]) for cols in _col_chunks(D_MODEL)]
    if final or next_norm is not None:
        ms = sum(jnp.sum(c * c, axis=-1, keepdims=True) for c in y) * (1.0 / D_MODEL)
        r = lax.rsqrt(ms + EPS)
    if next_norm is not None:
        h_ref, g_ref = next_norm
        for c, cols in zip(y, _col_chunks(D_MODEL)):
            h_ref[:, :, cols] = (c * r * g_ref[:, cols]).astype(h_ref.dtype).reshape(bt, lt, -1)
    if final:
        y = [c * r * fg_ref[:, cols] for c, cols in zip(y, _col_chunks(D_MODEL))]
    for c, cols in zip(y, _col_chunks(D_MODEL)):
        y_ref[:, :, cols] = c.reshape(bt, lt, -1)
    yield


def _attn_out_kernel(tok_ref, xq_ref, gate_ref, x_ref, kt_ref, vt_ref, w_ref, fg_ref,
                     y_ref, xo_scr, *, bt, lt, final):
    _round_robin([_attn_out_tile(tok_ref, xq_ref, gate_ref, x_ref, kt_ref, vt_ref, w_ref,
                                 fg_ref, y_ref, xo_scr, bt=bt, lt=lt, final=final)])


def _attn_out(tok, xq, gate, x, mk, mv, w_bf16, fg, *, layer, bt, lt, final):
    bn, seq, _ = x.shape

    def act(n):
        return pl.BlockSpec((bt, lt, n), lambda b, l: (b, l, 0))

    mem = pl.BlockSpec((None, bt, X_WIDTH, N_MEM), lambda b, l: (layer, b, 0, 0))
    return pl.pallas_call(
        functools.partial(_attn_out_kernel, bt=bt, lt=lt, final=final),
        grid=(bn // bt, seq // lt),
        in_specs=[
            act(GDN_WIDTH), act(X_WIDTH), act(BRANCH_WIDTH), act(D_MODEL), mem, mem,
            pl.BlockSpec((BRANCH_WIDTH, D_MODEL), lambda b, l: (0, 0)),
            pl.BlockSpec((1, D_MODEL), lambda b, l: (0, 0)),
        ],
        out_specs=act(D_MODEL),
        out_shape=jax.ShapeDtypeStruct(x.shape, F32),
        scratch_shapes=[pltpu.VMEM((bt, lt, X_WIDTH), F32)],
        compiler_params=_cparams("arbitrary", "arbitrary"),
        name="attn_out",
    )(tok, xq, gate, x, mk, mv, w_bf16, fg)


SCONV_PROJ_ROUNDS = (0, 0, 1, 1, 2, 3, 4, 4)
TAIL_ATTN1_START = 3


def _tail_kernel(tok_ref, xq_ref, gate_ref, x_ref, kt0_ref, vt0_ref, kt1_ref, vt1_ref,
                 w_out0_ref, w_out1_ref, fg_ref, g_ref, w_in_ref, hist_ref, cw_ref,
                 y_ref, hist_out,
                 xo0_scr, xo1_scr, h_scr, x1_scr, tok1_scr, gate1_scr, xq1_scr, carry,
                 *, lt, tiles_per_seq):
    i = pl.program_id(0)
    a2, b2, c2 = i % 2, (i + 1) % 2, i % 2
    a3, c3 = i % 3, (i + 1) % 3
    prev = jnp.maximum(i - 1, 0)

    @pl.when(i == 0)
    def _():
        h_scr[1] = jnp.zeros(h_scr.shape[1:], BF16)
        x1_scr[1] = jnp.zeros(x1_scr.shape[1:], F32)
        x1_scr[2] = jnp.zeros(x1_scr.shape[1:], F32)
        tok1_scr[0] = jnp.zeros(tok1_scr.shape[1:], BF16)
        gate1_scr[0] = jnp.zeros(gate1_scr.shape[1:], BF16)
        xq1_scr[0] = jnp.zeros(xq1_scr.shape[1:], BF16)

    @pl.when(prev % tiles_per_seq == 0)
    def _():
        carry[:, SUBLANES - hist_ref.shape[1]:, :] = hist_ref[...]

    attn0 = _attn_out_tile(tok_ref, xq_ref, gate_ref, x_ref, kt0_ref, vt0_ref, w_out0_ref,
                           fg_ref, x1_scr.at[a3], xo0_scr, bt=1, lt=lt, final=False,
                           next_norm=(h_scr.at[a2], g_ref))
    proj = _inproj_chunks(h_scr.at[1 - a2], None,
                          (w_in_ref, hist_ref, cw_ref, tok1_scr.at[b2], gate1_scr.at[b2],
                           xq1_scr.at[b2], hist_out, carry), mixer="sconv", bt=1, lt=lt)
    attn1 = _attn_out_tile(tok1_scr.at[c2], xq1_scr.at[c2], gate1_scr.at[c2], x1_scr.at[c3],
                           kt1_ref, vt1_ref, w_out1_ref, fg_ref, y_ref, xo1_scr,
                           bt=1, lt=lt, final=True)
    _round_robin([attn0, _delayed(attn1, TAIL_ATTN1_START)]
                 + [_delayed(p, SCONV_PROJ_ROUNDS[k]) for k, p in enumerate(proj)])


def _tail(tok, xq, gate, x, mk, mv, w_out_bf16, fg, g, w_in, hist, cw, *, lt):
    bn, seq, _ = x.shape
    nl = seq // lt
    tiles = bn * nl
    width, cc = cw.shape

    def clamp(i, lag):
        return jnp.clip(i - lag, 0, tiles - 1)

    def tile(n, lag):
        return pl.BlockSpec((1, lt, n), lambda i: (clamp(i, lag), 0, 0))

    def whole(a):
        return pl.BlockSpec(a.shape, lambda i: (0,) * a.ndim, pipeline_mode=pl.Buffered(1))

    def mem(layer, lag):
        return pl.BlockSpec((None, 1, X_WIDTH, N_MEM),
                            lambda i: (layer, clamp(i, lag) // nl, 0, 0))

    def tiled(a):
        return a.reshape(tiles, lt, a.shape[-1])

    def slots(n, width_, dtype):
        return pltpu.VMEM((n, 1, lt, width_), dtype)

    hist_spec = pl.BlockSpec((1, width - 1, cc), lambda i: (clamp(i, 1) // nl, 0, 0))
    y, hist_new = pl.pallas_call(
        functools.partial(_tail_kernel, lt=lt, tiles_per_seq=nl),
        grid=(tiles + 2,),
        in_specs=[
            tile(GDN_WIDTH, 0), tile(X_WIDTH, 0), tile(BRANCH_WIDTH, 0), tile(D_MODEL, 0),
            mem(0, 0), mem(0, 0), mem(1, 2), mem(1, 2),
            whole(w_out_bf16[0]), whole(w_out_bf16[1]), whole(fg), whole(g), whole(w_in),
            hist_spec, whole(cw)],
        out_specs=[tile(D_MODEL, 2), hist_spec],
        out_shape=[
            jax.ShapeDtypeStruct((tiles, lt, D_MODEL), F32),
            jax.ShapeDtypeStruct(hist.shape, F32)],
        scratch_shapes=[
            pltpu.VMEM((1, lt, X_WIDTH), F32), pltpu.VMEM((1, lt, X_WIDTH), F32),
            slots(2, D_MODEL, BF16), slots(3, D_MODEL, F32),
            slots(2, cc, BF16), slots(2, BRANCH_WIDTH, BF16), slots(2, X_WIDTH, BF16),
            pltpu.VMEM((1, SUBLANES, cc), F32)],
        compiler_params=_cparams("arbitrary"),
        name="tail",
    )(tiled(tok), tiled(xq), tiled(gate), tiled(x), mk, mv, mk, mv,
      w_out_bf16[0], w_out_bf16[1], fg, g, w_in, hist, cw)
    return y.reshape(x.shape), hist_new


def _trunk(x, mem_k, mem_v, gdn_s, gdn_conv, sc_conv, p, *, bt, lt, chunk, gdn_bt, gdn_lt,
           attn_bt, attn_lt, act_dtype):
    if gdn_lt == lt or bt == 1:
        gate, xq, tok, s_new, gconv_new = _gdn_layer(
            x, p["norm_g"][0:1], p["w_in_a"], gdn_s, gdn_conv, p["conv_w_a"], p["alog_pad"],
            p["dtb_pad"], p["o_norm_g"], chunk=chunk, bt=gdn_bt, lt=gdn_lt, act_dtype=act_dtype)
    else:
        qkv, gate, xq, ba, gconv_new = _inproj(
            x, p["norm_g"][0:1], p["w_in_a"], None, p["conv_w_a"], mixer="gdn", bt=bt, lt=lt,
            act_dtype=act_dtype)
        tok, s_new = _gdn(qkv, ba, gdn_s, gdn_conv, p["conv_w_a"], p["alog_pad"],
                          p["dtb_pad"], p["o_norm_g"], chunk=chunk, bt=gdn_bt, lt=gdn_lt)
    if attn_bt == 1 and bt == 1:
        y, sconv_new = _tail(tok, xq, gate, x, mem_k, mem_v, p["w_out"], p["final_norm_g"],
                             p["norm_g"][1:2], p["w_in_b"][0], sc_conv, p["conv_w_b"],
                             lt=attn_lt)
    else:
        x = _attn_out(tok, xq, gate, x, mem_k, mem_v, p["w_out"][0], p["final_norm_g"],
                      layer=0, bt=attn_bt, lt=attn_lt, final=False)
        tok, gate, xq, sconv_new = _inproj(
            x, p["norm_g"][1:2], p["w_in_b"], sc_conv, p["conv_w_b"], mixer="sconv", bt=bt,
            lt=lt, act_dtype=act_dtype)
        y = _attn_out(tok, xq, gate, x, mem_k, mem_v, p["w_out"][1], p["final_norm_g"],
                      layer=1, bt=attn_bt, lt=attn_lt, final=True)
    return y, s_new[None], gconv_new[None], sconv_new[None]


def kernel(x_prompt, x_sample, mem_prompt, state_gdn, state_gdn_conv, state_sconv, cache_mem_k, cache_mem_v, norm_g, w_in_a, conv_w_a, a_log, dt_bias, o_norm_g, w_in_b, conv_w_b, mem_norm_g, w_mem_kv, w_out, final_norm_g):
    bp = x_prompt.shape[0]

    wa = jnp.transpose(w_in_a[0])
    c_b = QKV_WIDTH
    c_g = c_b + 2 * GDN_HEADS
    c_x = c_g + BRANCH_WIDTH
    wa = [wa[:c_b], wa[c_g:c_x], wa[c_x:],
          jnp.concatenate([wa[c_b:c_g], jnp.zeros((BA_PAD - 2 * GDN_HEADS, D_MODEL), wa.dtype)])]
    wa = [w.astype(BF16) for w in wa]
    pad_lo = jnp.zeros((GDN_HEADS,), F32)
    pad_hi = jnp.zeros((LANES - 2 * GDN_HEADS,), F32)
    params = {
        "norm_g": norm_g,
        "w_in_a": wa,
        "conv_w_a": conv_w_a[0],
        "alog_pad": jnp.concatenate([pad_lo, a_log[0], pad_hi])[None],
        "dtb_pad": jnp.concatenate([pad_lo, dt_bias[0], pad_hi])[None],
        "o_norm_g": o_norm_g,
        "w_in_b": [w_in_b[0].astype(BF16)],
        "conv_w_b": conv_w_b[0],
        "w_out": w_out.astype(BF16),
        "final_norm_g": final_norm_g[None],
    }

    def to_cache(t):
        t = t.reshape(t.shape[0], t.shape[1], X_HEADS, X_HEAD_DIM, t.shape[3])
        return jnp.transpose(t, (0, 1, 4, 2, 3))

    def from_cache(t):
        t = jnp.transpose(t, (0, 1, 3, 4, 2))
        return t.reshape(t.shape[0], t.shape[1], X_WIDTH, t.shape[4])

    mem_kt, mem_vt = _memkv(mem_prompt, mem_norm_g[None], w_mem_kv.astype(BF16))
    mem_k_p = to_cache(mem_kt)
    mem_v_p = to_cache(mem_vt)

    s0_p = jnp.zeros((bp,) + state_gdn.shape[2:], F32)
    gc0_p = jnp.zeros((bp,) + state_gdn_conv.shape[2:], F32)
    sc0_p = jnp.zeros((bp,) + state_sconv.shape[2:], F32)
    y_p, s_p, gc_p, sc_p = _trunk(x_prompt, mem_kt, mem_vt, s0_p, gc0_p, sc0_p,
                                  params, bt=1, lt=1024, chunk=GDN_CHUNK, gdn_bt=1, gdn_lt=512,
                                  attn_bt=1, attn_lt=512, act_dtype=BF16)
    dec_seq = x_sample.shape[1]
    y_s, s_s, gc_s, sc_s = _trunk(x_sample, from_cache(cache_mem_k), from_cache(cache_mem_v),
                                  state_gdn[0], state_gdn_conv[0], state_sconv[0], params,
                                  bt=32, lt=dec_seq, chunk=dec_seq, gdn_bt=16, gdn_lt=dec_seq,
                                  attn_bt=16, attn_lt=dec_seq, act_dtype=F32)
    return (y_p, y_s, s_p, gc_p, sc_p, mem_k_p, mem_v_p, s_s, gc_s, sc_s)
```

```python
import functools

import jax
import jax.numpy as jnp
from jax import lax
from jax.experimental import pallas as pl
from jax.experimental.pallas import tpu as pltpu

F32 = jnp.float32
BF16 = jnp.bfloat16

D_MODEL = 1024
N_MEM = 256
X_WIDTH = 256
X_HEADS = 4
X_HEAD_DIM = 64
GDN_HEADS = 6
GDN_DK = 128
GDN_DV = 128
GDN_WIDTH = GDN_HEADS * GDN_DV
QK_WIDTH = GDN_HEADS * GDN_DK
QKV_WIDTH = 2 * QK_WIDTH + GDN_WIDTH
GDN_CHUNK = 64
SC_WIDTH = 768
BRANCH_WIDTH = 1024
EPS = 1e-6

LANES = 128
SUBLANES = 8
COL_CHUNK = 256
BA_PAD = LANES
VMEM_LIMIT = 56 * 1024 * 1024

HIGHEST = lax.Precision.HIGHEST


def _cparams(*sem):
    return pltpu.CompilerParams(dimension_semantics=sem, vmem_limit_bytes=VMEM_LIMIT)


def _rms_rows(x, g):
    r = lax.rsqrt(jnp.mean(x * x, axis=-1, keepdims=True) + EPS)
    return x * r * g


def _silu(x):
    h = 0.5 * x
    return h + h * jnp.tanh(h)


def _dot(a, b):
    return jnp.dot(a.astype(BF16), b.astype(BF16), preferred_element_type=F32)


def _dot_nt(a, b):
    return lax.dot_general(a.astype(BF16), b.astype(BF16), (((1,), (1,)), ((), ())),
                           preferred_element_type=F32)


def _dot_tn(a, b):
    return lax.dot_general(a.astype(BF16), b.astype(BF16), (((0,), (0,)), ((), ())),
                           preferred_element_type=F32)


def _dot_f32(a, b):
    return jnp.dot(a, b, preferred_element_type=F32, precision=HIGHEST)


def _col_chunks(n):
    return [slice(c, min(c + COL_CHUNK, n)) for c in range(0, n, COL_CHUNK)]


def _interleaved(gens):
    gens = list(gens)
    while gens:
        alive = []
        for g in gens:
            try:
                next(g)
                alive.append(g)
            except StopIteration:
                pass
        gens = alive
        yield


def _round_robin(gens):
    for _ in _interleaved(gens):
        pass


def _delayed(gen, rounds):
    for _ in range(rounds):
        yield
    yield from gen


def _memkv_kernel(m_ref, g_ref, w_ref, kt_ref, vt_ref):
    h = _rms_rows(m_ref[...], g_ref[...]).astype(BF16)
    for layer in range(w_ref.shape[0]):
        kv = jnp.dot(h, w_ref[layer], preferred_element_type=F32)
        kt_ref[layer] = kv[:, :X_WIDTH].T
        vt_ref[layer] = kv[:, X_WIDTH:].T


def _memkv(mem, g, w_bf16):
    bn, n_mem, _ = mem.shape
    depth = w_bf16.shape[0]
    out = jax.ShapeDtypeStruct((depth, bn, X_WIDTH, n_mem), F32)
    out_spec = pl.BlockSpec((depth, None, X_WIDTH, n_mem), lambda b: (0, b, 0, 0))
    return pl.pallas_call(
        _memkv_kernel,
        grid=(bn,),
        in_specs=[
            pl.BlockSpec((None, n_mem, D_MODEL), lambda b: (b, 0, 0)),
            pl.BlockSpec((1, D_MODEL), lambda b: (0, 0)),
            pl.BlockSpec(w_bf16.shape, lambda b: (0, 0, 0)),
        ],
        out_specs=[out_spec, out_spec],
        out_shape=[out, out],
        compiler_params=_cparams("arbitrary"),
        name="memkv",
    )(mem, g, w_bf16)


def _init_conv_carry(carry, hist_ref):
    @pl.when(pl.program_id(1) == 0)
    def _():
        carry[:, SUBLANES - hist_ref.shape[1]:, :] = hist_ref[...]


def _causal_conv_cols(x, cols, carry, hist_out, cw_ref, seqs=slice(None)):
    width = cw_ref.shape[0]
    bt, lt, n = x.shape
    g = lt // SUBLANES
    xe = jnp.concatenate([carry[seqs, :, cols], x], axis=1).reshape(bt, g + 1, SUBLANES, n)
    sub = lax.broadcasted_iota(jnp.int32, (1, 1, SUBLANES, n), 2)
    y = x.reshape(bt, g, SUBLANES, n) * cw_ref[width - 1:width, cols]
    for j in range(width - 1):
        s = width - 1 - j
        r = pltpu.roll(xe, s, axis=2)
        y = y + jnp.where(sub < s, r[:, :g], r[:, 1:]) * cw_ref[j:j + 1, cols]
    carry[seqs, :, cols] = x[:, lt - SUBLANES:, :]
    if hist_out is not None:
        hist_out[seqs, :, cols] = x[:, lt - (width - 1):, :]
    return y.reshape(bt, lt, n)


def _inproj_chunks(x_ref, g_ref, rest, *, mixer, bt, lt):
    tm = bt * lt
    if g_ref is None:
        h = x_ref[...].reshape(tm, D_MODEL)
    else:
        h = _rms_rows(x_ref[...].reshape(tm, D_MODEL), g_ref[...]).astype(BF16)
    if mixer == "gdn":
        n_out = (len(rest) - 1) // 2
        w_refs, (mix_out, *plain_outs, hist_out) = rest[:n_out], rest[n_out:]

        def proj(i, cols):
            return _dot_nt(h, w_refs[i][cols, :]).reshape(bt, lt, -1)
    else:
        w_ref, _, cw_ref, mix_out, *plain_outs, hist_out, carry = rest
        bases = [0, 3 * SC_WIDTH]
        for out in plain_outs[:-1]:
            bases.append(bases[-1] + out.shape[-1])

        def proj(i, cols, offset=0):
            cols = slice(bases[i] + offset + cols.start, bases[i] + offset + cols.stop)
            return jnp.dot(h, w_ref[:, cols], preferred_element_type=F32).reshape(bt, lt, -1)

    def mixer_chunk(cols):
        if mixer == "gdn":
            y = proj(0, cols)
            hist_out[:, :, cols] = y[:, lt - hist_out.shape[1]:, :]
        else:
            gate_b = proj(0, cols)
            pre = proj(0, cols, SC_WIDTH) * proj(0, cols, 2 * SC_WIDTH)
            yield
            y = gate_b * _causal_conv_cols(pre, cols, carry, hist_out, cw_ref)
        mix_out[:, :, cols] = y.astype(mix_out.dtype)
        yield

    def plain_chunk(i, out, cols):
        out[:, :, cols] = proj(i, cols).astype(out.dtype)
        yield

    gens = [mixer_chunk(cols) for cols in _col_chunks(mix_out.shape[-1])]
    for i, out in enumerate(plain_outs):
        gens += [plain_chunk(i + 1, out, cols) for cols in _col_chunks(out.shape[-1])]
    return gens


def _inproj_kernel(x_ref, g_ref, *rest, mixer, bt, lt):
    if mixer == "sconv":
        _init_conv_carry(rest[-1], rest[1])
    gens = _inproj_chunks(x_ref, g_ref, rest, mixer=mixer, bt=bt, lt=lt)
    _round_robin([_delayed(g, i) for i, g in enumerate(gens)])


def _inproj(x, g, ws, hist, cw, *, mixer, bt, lt, act_dtype):
    bn, seq, _ = x.shape
    width, cc = cw.shape

    def act(n, dtype=act_dtype):
        return (pl.BlockSpec((bt, lt, n), lambda b, l: (b, l, 0)),
                jax.ShapeDtypeStruct((bn, seq, n), dtype))

    hist_spec = pl.BlockSpec((bt, width - 1, cc), lambda b, l: (b, 0, 0))
    args = [x, g, *ws]
    in_specs = [
        pl.BlockSpec((bt, lt, D_MODEL), lambda b, l: (b, l, 0)),
        pl.BlockSpec((1, D_MODEL), lambda b, l: (0, 0)),
    ] + [pl.BlockSpec(w.shape, lambda b, l: (0, 0)) for w in ws]
    outs = [act(cc), act(BRANCH_WIDTH), act(X_WIDTH)]
    scratch = []
    if mixer == "gdn":
        outs.append(act(BA_PAD, F32))
    else:
        args += [hist, cw]
        in_specs += [hist_spec, pl.BlockSpec((width, cc), lambda b, l: (0, 0))]
        scratch.append(pltpu.VMEM((bt, SUBLANES, cc), F32))
    outs.append((hist_spec, jax.ShapeDtypeStruct((bn, width - 1, cc), F32)))
    return pl.pallas_call(
        functools.partial(_inproj_kernel, mixer=mixer, bt=bt, lt=lt),
        grid=(bn // bt, seq // lt),
        in_specs=in_specs,
        out_specs=[o[0] for o in outs],
        out_shape=[o[1] for o in outs],
        scratch_shapes=scratch,
        compiler_params=_cparams("arbitrary", "arbitrary"),
        name="inproj_" + mixer,
    )(*args)


CHUNKS_PER_WAVE = 4
SEQS_PER_WAVE = 8


def _gdn_tile(qkv_ref, ba_ref, cw_ref, alog_ref, dtb_ref, og_ref, o_ref, s_ref, carry,
              *, chunk, group):
    bt, lt, _ = qkv_ref.shape
    nchunks = lt // chunk
    heads = range(GDN_HEADS)
    groups = [tuple(range(g, g + group)) for g in range(0, GDN_HEADS, group)]
    gw = group * chunk
    row = lax.broadcasted_iota(jnp.int32, (chunk, gw), 0)
    col = lax.broadcasted_iota(jnp.int32, (chunk, gw), 1) % chunk
    tril = row >= col
    strict = row > col
    lane_blk = lax.broadcasted_iota(jnp.int32, (1, gw), 1) // chunk
    ones_tril = (lax.broadcasted_iota(jnp.int32, (chunk, chunk), 0)
                 >= lax.broadcasted_iota(jnp.int32, (chunk, chunk), 1)).astype(F32)
    neg_a = -jnp.exp(alog_ref[...])
    dtb = dtb_ref[...]
    og = og_ref[...]

    def rows(c):
        return slice(c * chunk, (c + 1) * chunk)

    def lanes(base, h, n):
        return slice(base + h * n, base + (h + 1) * n)

    def cat(xs, axis):
        return xs[0] if len(xs) == 1 else jnp.concatenate(xs, axis=axis)

    def pick(xs):
        out = xs[0]
        for j in range(1, len(xs)):
            out = jnp.where(lane_blk == j, xs[j], out)
        return out

    def blockdiag(xs):
        z = jnp.zeros_like(xs[0])
        return cat([cat([x if i == j else z for i in range(len(xs))], 1)
                    for j, x in enumerate(xs)], 0)

    def blockdiag_of_lane_blocks(x):
        if group == 1:
            return x
        return cat([jnp.where(lane_blk == j, x, 0.0) for j in range(group)], 0)

    def conv_silu(u, cols):
        b, c = u
        x = qkv_ref[b, rows(c), cols].astype(F32)
        y = _causal_conv_cols(x[None], cols, carry, None, cw_ref, seqs=slice(b, b + 1))
        return _silu(y[0])

    def phase1(units, res):
        uh = [(u, h) for u in units for h in heads]
        ug = [(u, g) for u in units for g in range(len(groups))]
        gcum, gcum_t, beta_all = {}, {}, {}
        for u in units:
            ba = ba_ref[u[0], rows(u[1]), :]
            beta_all[u] = 1.0 / (1.0 + jnp.exp(-ba))
            z = ba + dtb
            softplus = jnp.maximum(z, 0.0) + jnp.log1p(jnp.exp(-jnp.abs(z)))
            gcum[u] = _dot_f32(ones_tril, neg_a * softplus)
        q, k, v = {}, {}, {}
        for u in units:
            for h in heads:
                q[u, h] = conv_silu(u, lanes(0, h, GDN_DK))
                k[u, h] = conv_silu(u, lanes(QK_WIDTH, h, GDN_DK))
                v[u, h] = conv_silu(u, lanes(2 * QK_WIDTH, h, GDN_DV))
            yield
        for key in uh:
            x = q[key]
            q[key] = x * (lax.rsqrt(jnp.sum(x * x, axis=-1, keepdims=True) + EPS)
                          * (GDN_DK ** -0.5))
            x = k[key]
            k[key] = x * lax.rsqrt(jnp.sum(x * x, axis=-1, keepdims=True) + EPS)
        yield
        for u in units:
            gcum_t[u] = cat([gcum[u]] * group, 0).T
        gc = {(u, h): jnp.broadcast_to(gcum[u][:, GDN_HEADS + h:GDN_HEADS + h + 1],
                                       (chunk, GDN_DK)) for u, h in uh}
        gl = {(u, h): gcum[u][chunk - 1:chunk, GDN_HEADS + h:GDN_HEADS + h + 1]
              for u, h in uh}
        beta = {(u, h): jnp.broadcast_to(beta_all[u][:, h:h + 1], (chunk, GDN_DK))
                for u, h in uh}
        eg = {key: jnp.exp(gc[key]) for key in uh}
        kb = {key: k[key] * beta[key] for key in uh}
        kkqk = {}
        for u, g in ug:
            grp = groups[g]
            kkqk[u, g] = _dot_nt(
                cat([cat([kb[u, h] for h in grp], 1), cat([q[u, h] for h in grp], 1)], 0),
                blockdiag([k[u, h] for h in grp]))
        yield
        n, p, aqk = {}, {}, {}
        for u, g in ug:
            grp = groups[g]
            g_col = pick([gc[u, h][:, :gw] for h in grp])
            g_row = pick([gcum_t[u][GDN_HEADS + h:GDN_HEADS + h + 1, :] for h in grp])
            decay = jnp.exp(jnp.where(tril, g_col - g_row, -jnp.inf))
            a = jnp.where(strict, kkqk[u, g][:chunk] * decay, 0.0)
            aqk[u, g] = (kkqk[u, g][chunk:] * decay).astype(BF16)
            n[u, g] = -a
            p[u, g] = _dot(a, blockdiag_of_lane_blocks(a))
        yield
        span = 2
        while 2 * span < chunk:
            for key in ug:
                both = _dot(cat([n[key], p[key]], 0), blockdiag_of_lane_blocks(p[key]))
                n[key] = n[key] + p[key] + both[:chunk]
                p[key] = both[chunk:]
            span *= 2
            yield
        for key in ug:
            n[key] = n[key] + p[key] + _dot(n[key], blockdiag_of_lane_blocks(p[key]))
        yield
        wu = {}
        for u, h in uh:
            rhs = jnp.concatenate([kb[u, h] * eg[u, h], v[u, h] * beta[u, h]], axis=-1)
            wu[u, h] = rhs + _dot(n[u, h // group][:, lanes(0, h % group, chunk)], rhs)
        yield
        for u in units:
            res[u] = dict(
                w_qd=[jnp.concatenate([wu[u, h][:, :GDN_DK], q[u, h] * eg[u, h]],
                                      axis=0).astype(BF16) for h in heads],
                u=[wu[u, h][:, GDN_DK:] for h in heads],
                kd=[(k[u, h] * jnp.exp(gl[u, h] - gc[u, h])).astype(BF16) for h in heads],
                aqk=[aqk[u, g] for g in range(len(groups))],
                egl=[jnp.exp(gl[u, h]) for h in heads])
        yield

    def phase2(units, res):
        for c in sorted({c for _, c in units}):
            us = [u for u in units if u[1] == c]
            uh = [(u, h) for u in us for h in heads]
            s = {(u, h): s_ref[u[0], h] for u, h in uh}
            ws_qs = {(u, h): _dot(res[u]["w_qd"][h], s[u, h]) for u, h in uh}
            yield
            v_new = {(u, h): res[u]["u"][h] - ws_qs[u, h][:chunk] for u, h in uh}
            o_grp = {(u, g): _dot(res[u]["aqk"][g], blockdiag([v_new[u, h] for h in grp]))
                     for u in us for g, grp in enumerate(groups)}
            for u, h in uh:
                s_ref[u[0], h] = (s[u, h] * res[u]["egl"][h]
                                  + _dot_tn(res[u]["kd"][h], v_new[u, h]))
            yield
            for u, h in uh:
                oh = ws_qs[u, h][chunk:] + o_grp[u, h // group][:, lanes(0, h % group, GDN_DV)]
                oh = oh * lax.rsqrt(jnp.mean(oh * oh, axis=-1, keepdims=True) + EPS) * og
                o_ref[u[0], rows(c), lanes(0, h, GDN_DV)] = oh.astype(o_ref.dtype)
            yield

    units = [(b, c) for b in range(bt) for c in range(nchunks)]
    per_wave = CHUNKS_PER_WAVE if nchunks > 1 else SEQS_PER_WAVE
    waves = [units[i:i + per_wave] for i in range(0, len(units), per_wave)]
    res = {}
    prev = None
    for wave in waves:
        gens = [phase1(wave, res)]
        if prev is not None:
            gens.append(phase2(prev, res))
        yield from _interleaved(gens)
        prev = wave
    yield from phase2(prev, res)


def _gdn_kernel(qkv_ref, ba_ref, s0_ref, hist_ref, cw_ref, alog_ref, dtb_ref, og_ref,
                o_ref, s_ref, carry, *, chunk, group):
    @pl.when(pl.program_id(1) == 0)
    def _():
        s_ref[...] = s0_ref[...]

    _init_conv_carry(carry, hist_ref)
    _round_robin([_gdn_tile(qkv_ref, ba_ref, cw_ref, alog_ref, dtb_ref, og_ref, o_ref, s_ref,
                            carry, chunk=chunk, group=group)])


def _gdn(qkv, ba, s0, hist, cw, alog_pad, dtb_pad, og, *, chunk, bt, lt):
    group = 2 if 2 * chunk == LANES else 1
    bn, seq, _ = qkv.shape
    width, cc = cw.shape
    s_spec = pl.BlockSpec((bt, GDN_HEADS, GDN_DK, GDN_DV), lambda b, l: (b, 0, 0, 0))
    hist_spec = pl.BlockSpec((bt, width - 1, cc), lambda b, l: (b, 0, 0))
    vec = pl.BlockSpec((1, LANES), lambda b, l: (0, 0))
    return pl.pallas_call(
        functools.partial(_gdn_kernel, chunk=chunk, group=group),
        grid=(bn // bt, seq // lt),
        in_specs=[
            pl.BlockSpec((bt, lt, QKV_WIDTH), lambda b, l: (b, l, 0)),
            pl.BlockSpec((bt, lt, BA_PAD), lambda b, l: (b, l, 0)),
            s_spec, hist_spec,
            pl.BlockSpec((width, cc), lambda b, l: (0, 0)),
            vec, vec, vec,
        ],
        out_specs=[
            pl.BlockSpec((bt, lt, GDN_WIDTH), lambda b, l: (b, l, 0)),
            s_spec,
        ],
        out_shape=[
            jax.ShapeDtypeStruct((bn, seq, GDN_WIDTH), qkv.dtype),
            jax.ShapeDtypeStruct(s0.shape, F32),
        ],
        scratch_shapes=[pltpu.VMEM((bt, SUBLANES, cc), F32)],
        compiler_params=_cparams("arbitrary", "arbitrary"),
        name="gdn",
    )(qkv, ba, s0, hist, cw, alog_pad, dtb_pad, og)


PROJ_ROUNDS = ((0, 5), (14, 5), (28, 1), (30, 1), (32, 1), (34, 1), (36, 1))


def _gdn_layer_kernel(x_ref, g_ref, w_qkv, w_gate, w_xq, w_ba, s0_ref, hist_ref, cw_ref,
                      alog_ref, dtb_ref, og_ref, gate_out, xq_out, hist_out, o_ref, s_ref,
                      qkv_scr, ba_scr, carry, *, lt, tiles_per_seq, chunk, group):
    i = pl.program_id(0)
    slot = i % 2
    prev = jnp.maximum(i - 1, 0)

    @pl.when(i == 0)
    def _():
        qkv_scr[1] = jnp.zeros(qkv_scr.shape[1:], F32)
        ba_scr[1] = jnp.zeros(ba_scr.shape[1:], F32)

    @pl.when(prev % tiles_per_seq == 0)
    def _():
        s_ref[...] = s0_ref[...]
        carry[:, SUBLANES - hist_ref.shape[1]:, :] = hist_ref[...]

    proj = _inproj_chunks(
        x_ref, g_ref, (w_qkv, w_gate, w_xq, w_ba, qkv_scr.at[slot], gate_out, xq_out,
                       ba_scr.at[slot], hist_out), mixer="gdn", bt=1, lt=lt)
    rule = _gdn_tile(qkv_scr.at[1 - slot], ba_scr.at[1 - slot], cw_ref, alog_ref, dtb_ref,
                     og_ref, o_ref, s_ref, carry, chunk=chunk, group=group)
    starts = [r for w, n in PROJ_ROUNDS for r in range(w, w + n)]
    _round_robin([rule] + [_delayed(p, starts[k]) for k, p in enumerate(proj)])


def _gdn_layer(x, g, ws, s0, hist, cw, alog_pad, dtb_pad, og, *, chunk, lt, act_dtype):
    group = 2 if 2 * chunk == LANES else 1
    bn, seq, _ = x.shape
    nl = seq // lt
    tiles = bn * nl
    width, cc = cw.shape

    def cur(i):
        return jnp.minimum(i, tiles - 1)

    def prev(i):
        return jnp.maximum(i - 1, 0)

    def tile(n, which):
        return pl.BlockSpec((1, lt, n), lambda i: (which(i), 0, 0))

    def per_seq(shape, which):
        return pl.BlockSpec((1,) + shape, lambda i: (which(i) // nl,) + (0,) * len(shape))

    def whole(a):
        return pl.BlockSpec(a.shape, lambda i: (0,) * a.ndim)

    state = (GDN_HEADS, GDN_DK, GDN_DV)
    gate, xq, hist_new, tok, s_new = pl.pallas_call(
        functools.partial(_gdn_layer_kernel, lt=lt, tiles_per_seq=nl, chunk=chunk, group=group),
        grid=(tiles + 1,),
        in_specs=[tile(D_MODEL, cur), whole(g)] + [whole(w) for w in ws] + [
            per_seq(state, prev), per_seq((width - 1, cc), prev), whole(cw),
            whole(alog_pad), whole(dtb_pad), whole(og)],
        out_specs=[
            tile(BRANCH_WIDTH, cur), tile(X_WIDTH, cur), per_seq((width - 1, cc), cur),
            tile(GDN_WIDTH, prev), per_seq(state, prev)],
        out_shape=[
            jax.ShapeDtypeStruct((tiles, lt, BRANCH_WIDTH), act_dtype),
            jax.ShapeDtypeStruct((tiles, lt, X_WIDTH), act_dtype),
            jax.ShapeDtypeStruct(hist.shape, F32),
            jax.ShapeDtypeStruct((tiles, lt, GDN_WIDTH), act_dtype),
            jax.ShapeDtypeStruct(s0.shape, F32)],
        scratch_shapes=[
            pltpu.VMEM((2, 1, lt, cc), F32),
            pltpu.VMEM((2, 1, lt, BA_PAD), F32),
            pltpu.VMEM((1, SUBLANES, cc), F32)],
        compiler_params=_cparams("arbitrary"),
        name="gdn_layer",
    )(x.reshape(tiles, lt, D_MODEL), g, *ws, s0, hist, cw, alog_pad, dtb_pad, og)
    return (gate.reshape(bn, seq, -1), xq.reshape(bn, seq, -1), tok.reshape(bn, seq, -1),
            s_new, hist_new)


ATTN_ROWS = 256
ATTN_SEQS = 16


def _attn_out_tile(tok_ref, xq_ref, gate_ref, x_ref, kt_ref, vt_ref, w_ref, fg_ref,
                   y_ref, xo_scr, *, bt, lt, final, next_norm=None):
    lane_head = lax.broadcasted_iota(jnp.int32, (1, X_WIDTH), 1) // X_HEAD_DIM
    scale = X_HEAD_DIM ** -0.5
    tm = bt * lt

    def attend(b, rs):
        r = rs.stop - rs.start
        q = xq_ref[b, rs, :]
        q = q * jnp.asarray(scale, q.dtype)
        qx = jnp.concatenate([jnp.where(lane_head == h, q, jnp.zeros_like(q))
                              for h in range(X_HEADS)], axis=0)
        s = _dot(qx, kt_ref[b])
        yield
        e = jnp.exp(s - jnp.max(s, axis=-1, keepdims=True))
        p = e / jnp.sum(e, axis=-1, keepdims=True)
        o4 = _dot_nt(p, vt_ref[b])
        yield
        xo = jnp.where(lane_head == 0, o4[0:r], 0.0)
        for h in range(1, X_HEADS):
            xo = xo + jnp.where(lane_head == h, o4[h * r:(h + 1) * r], 0.0)
        xo_scr[b, rs, :] = xo
        yield

    acc = {}

    def tok_proj():
        sg = _silu(gate_ref[:, :, :GDN_WIDTH].astype(F32).reshape(tm, GDN_WIDTH))
        br = (tok_ref[...].astype(F32).reshape(tm, GDN_WIDTH) * sg).astype(BF16)
        for cols in _col_chunks(D_MODEL):
            acc[cols.start] = (x_ref[:, :, cols].reshape(tm, -1)
                               + _dot(br, w_ref[:GDN_WIDTH, cols]))
            yield

    if bt == 1:
        blocks = [slice(r, r + ATTN_ROWS) for r in range(0, lt, ATTN_ROWS)]
        yield from _interleaved([attend(0, rs) for rs in blocks] + [tok_proj()])
    else:
        def seq_group(i, carry):
            _round_robin([attend(i * ATTN_SEQS + j, slice(0, lt)) for j in range(ATTN_SEQS)])
            return carry

        lax.fori_loop(0, bt // ATTN_SEQS, seq_group, 0)
        yield from tok_proj()

    sg_x = _silu(gate_ref[:, :, GDN_WIDTH:].astype(F32).reshape(tm, X_WIDTH))
    br_x = (xo_scr[...].reshape(tm, X_WIDTH) * sg_x).astype(BF16)
    y = [acc[cols.start] + _dot(br_x, w_ref[GDN_WIDTH:, cols]) for cols in _col_chunks(D_MODEL)]
    if final or next_norm is not None:
        ms = sum(jnp.sum(c * c, axis=-1, keepdims=True) for c in y) * (1.0 / D_MODEL)
        r = lax.rsqrt(ms + EPS)
    if next_norm is not None:
        h_ref, g_ref = next_norm
        for c, cols in zip(y, _col_chunks(D_MODEL)):
            h_ref[:, :, cols] = (c * r * g_ref[:, cols]).astype(h_ref.dtype).reshape(bt, lt, -1)
    if final:
        y = [c * r * fg_ref[:, cols] for c, cols in zip(y, _col_chunks(D_MODEL))]
    for c, cols in zip(y, _col_chunks(D_MODEL)):
        y_ref[:, :, cols] = c.reshape(bt, lt, -1)
    yield


def _attn_out_kernel(tok_ref, xq_ref, gate_ref, x_ref, kt_ref, vt_ref, w_ref, fg_ref,
                     y_ref, xo_scr, *, bt, lt, final):
    _round_robin([_attn_out_tile(tok_ref, xq_ref, gate_ref, x_ref, kt_ref, vt_ref, w_ref,
                                 fg_ref, y_ref, xo_scr, bt=bt, lt=lt, final=final)])


def _attn_out(tok, xq, gate, x, mk, mv, w_bf16, fg, *, layer, bt, lt, final):
    bn, seq, _ = x.shape

    def act(n):
        return pl.BlockSpec((bt, lt, n), lambda b, l: (b, l, 0))

    mem = pl.BlockSpec((None, bt, X_WIDTH, N_MEM), lambda b, l: (layer, b, 0, 0))
    return pl.pallas_call(
        functools.partial(_attn_out_kernel, bt=bt, lt=lt, final=final),
        grid=(bn // bt, seq // lt),
        in_specs=[
            act(GDN_WIDTH), act(X_WIDTH), act(BRANCH_WIDTH), act(D_MODEL), mem, mem,
            pl.BlockSpec((BRANCH_WIDTH, D_MODEL), lambda b, l: (0, 0)),
            pl.BlockSpec((1, D_MODEL), lambda b, l: (0, 0)),
        ],
        out_specs=act(D_MODEL),
        out_shape=jax.ShapeDtypeStruct(x.shape, F32),
        scratch_shapes=[pltpu.VMEM((bt, lt, X_WIDTH), F32)],
        compiler_params=_cparams("arbitrary", "arbitrary"),
        name="attn_out",
    )(tok, xq, gate, x, mk, mv, w_bf16, fg)


SCONV_PROJ_ROUNDS = (0, 0, 1, 1, 2, 3, 4, 4)
TAIL_ATTN1_START = 3


def _tail_kernel(tok_ref, xq_ref, gate_ref, x_ref, kt0_ref, vt0_ref, kt1_ref, vt1_ref,
                 w_out0_ref, w_out1_ref, fg_ref, g_ref, w_in_ref, hist_ref, cw_ref,
                 y_ref, hist_out,
                 xo0_scr, xo1_scr, h_scr, x1_scr, tok1_scr, gate1_scr, xq1_scr, carry,
                 *, lt, tiles_per_seq):
    i = pl.program_id(0)
    a2, b2, c2 = i % 2, (i + 1) % 2, i % 2
    a3, c3 = i % 3, (i + 1) % 3
    prev = jnp.maximum(i - 1, 0)

    @pl.when(i == 0)
    def _():
        h_scr[1] = jnp.zeros(h_scr.shape[1:], BF16)
        x1_scr[1] = jnp.zeros(x1_scr.shape[1:], F32)
        x1_scr[2] = jnp.zeros(x1_scr.shape[1:], F32)
        tok1_scr[0] = jnp.zeros(tok1_scr.shape[1:], BF16)
        gate1_scr[0] = jnp.zeros(gate1_scr.shape[1:], BF16)
        xq1_scr[0] = jnp.zeros(xq1_scr.shape[1:], BF16)

    @pl.when(prev % tiles_per_seq == 0)
    def _():
        carry[:, SUBLANES - hist_ref.shape[1]:, :] = hist_ref[...]

    attn0 = _attn_out_tile(tok_ref, xq_ref, gate_ref, x_ref, kt0_ref, vt0_ref, w_out0_ref,
                           fg_ref, x1_scr.at[a3], xo0_scr, bt=1, lt=lt, final=False,
                           next_norm=(h_scr.at[a2], g_ref))
    proj = _inproj_chunks(h_scr.at[1 - a2], None,
                          (w_in_ref, hist_ref, cw_ref, tok1_scr.at[b2], gate1_scr.at[b2],
                           xq1_scr.at[b2], hist_out, carry), mixer="sconv", bt=1, lt=lt)
    attn1 = _attn_out_tile(tok1_scr.at[c2], xq1_scr.at[c2], gate1_scr.at[c2], x1_scr.at[c3],
                           kt1_ref, vt1_ref, w_out1_ref, fg_ref, y_ref, xo1_scr,
                           bt=1, lt=lt, final=True)
    _round_robin([attn0, _delayed(attn1, TAIL_ATTN1_START)]
                 + [_delayed(p, SCONV_PROJ_ROUNDS[k]) for k, p in enumerate(proj)])


def _tail(tok, xq, gate, x, mk, mv, w_out_bf16, fg, g, w_in, hist, cw, *, lt):
    bn, seq, _ = x.shape
    nl = seq // lt
    tiles = bn * nl
    width, cc = cw.shape

    def clamp(i, lag):
        return jnp.clip(i - lag, 0, tiles - 1)

    def tile(n, lag):
        return pl.BlockSpec((1, lt, n), lambda i: (clamp(i, lag), 0, 0))

    def whole(a):
        return pl.BlockSpec(a.shape, lambda i: (0,) * a.ndim, pipeline_mode=pl.Buffered(1))

    def mem(layer, lag):
        return pl.BlockSpec((None, 1, X_WIDTH, N_MEM),
                            lambda i: (layer, clamp(i, lag) // nl, 0, 0))

    def tiled(a):
        return a.reshape(tiles, lt, a.shape[-1])

    def slots(n, width_, dtype):
        return pltpu.VMEM((n, 1, lt, width_), dtype)

    hist_spec = pl.BlockSpec((1, width - 1, cc), lambda i: (clamp(i, 1) // nl, 0, 0))
    y, hist_new = pl.pallas_call(
        functools.partial(_tail_kernel, lt=lt, tiles_per_seq=nl),
        grid=(tiles + 2,),
        in_specs=[
            tile(GDN_WIDTH, 0), tile(X_WIDTH, 0), tile(BRANCH_WIDTH, 0), tile(D_MODEL, 0),
            mem(0, 0), mem(0, 0), mem(1, 2), mem(1, 2),
            whole(w_out_bf16[0]), whole(w_out_bf16[1]), whole(fg), whole(g), whole(w_in),
            hist_spec, whole(cw)],
        out_specs=[tile(D_MODEL, 2), hist_spec],
        out_shape=[
            jax.ShapeDtypeStruct((tiles, lt, D_MODEL), F32),
            jax.ShapeDtypeStruct(hist.shape, F32)],
        scratch_shapes=[
            pltpu.VMEM((1, lt, X_WIDTH), F32), pltpu.VMEM((1, lt, X_WIDTH), F32),
            slots(2, D_MODEL, BF16), slots(3, D_MODEL, F32),
            slots(2, cc, BF16), slots(2, BRANCH_WIDTH, BF16), slots(2, X_WIDTH, BF16),
            pltpu.VMEM((1, SUBLANES, cc), F32)],
        compiler_params=_cparams("arbitrary"),
        name="tail",
    )(tiled(tok), tiled(xq), tiled(gate), tiled(x), mk, mv, mk, mv,
      w_out_bf16[0], w_out_bf16[1], fg, g, w_in, hist, cw)
    return y.reshape(x.shape), hist_new


def _trunk(x, mem_k, mem_v, gdn_s, gdn_conv, sc_conv, p, *, bt, lt, chunk, gdn_bt, gdn_lt,
           attn_bt, attn_lt, act_dtype):
    if gdn_bt == 1 and bt == 1:
        gate, xq, tok, s_new, gconv_new = _gdn_layer(
            x, p["norm_g"][0:1], p["w_in_a"], gdn_s, gdn_conv, p["conv_w_a"], p["alog_pad"],
            p["dtb_pad"], p["o_norm_g"], chunk=chunk, lt=gdn_lt, act_dtype=act_dtype)
    else:
        qkv, gate, xq, ba, gconv_new = _inproj(
            x, p["norm_g"][0:1], p["w_in_a"], None, p["conv_w_a"], mixer="gdn", bt=bt, lt=lt,
            act_dtype=act_dtype)
        tok, s_new = _gdn(qkv, ba, gdn_s, gdn_conv, p["conv_w_a"], p["alog_pad"],
                          p["dtb_pad"], p["o_norm_g"], chunk=chunk, bt=gdn_bt, lt=gdn_lt)
    if attn_bt == 1 and bt == 1:
        y, sconv_new = _tail(tok, xq, gate, x, mem_k, mem_v, p["w_out"], p["final_norm_g"],
                             p["norm_g"][1:2], p["w_in_b"][0], sc_conv, p["conv_w_b"],
                             lt=attn_lt)
    else:
        x = _attn_out(tok, xq, gate, x, mem_k, mem_v, p["w_out"][0], p["final_norm_g"],
                      layer=0, bt=attn_bt, lt=attn_lt, final=False)
        tok, gate, xq, sconv_new = _inproj(
            x, p["norm_g"][1:2], p["w_in_b"], sc_conv, p["conv_w_b"], mixer="sconv", bt=bt,
            lt=lt, act_dtype=act_dtype)
        y = _attn_out(tok, xq, gate, x, mem_k, mem_v, p["w_out"][1], p["final_norm_g"],
                      layer=1, bt=attn_bt, lt=attn_lt, final=True)
    return y, s_new[None], gconv_new[None], sconv_new[None]


def kernel(x_prompt, x_sample, mem_prompt, state_gdn, state_gdn_conv, state_sconv, cache_mem_k, cache_mem_v, norm_g, w_in_a, conv_w_a, a_log, dt_bias, o_norm_g, w_in_b, conv_w_b, mem_norm_g, w_mem_kv, w_out, final_norm_g):
    bp = x_prompt.shape[0]

    wa = jnp.transpose(w_in_a[0])
    c_b = QKV_WIDTH
    c_g = c_b + 2 * GDN_HEADS
    c_x = c_g + BRANCH_WIDTH
    wa = [wa[:c_b], wa[c_g:c_x], wa[c_x:],
          jnp.concatenate([wa[c_b:c_g], jnp.zeros((BA_PAD - 2 * GDN_HEADS, D_MODEL), wa.dtype)])]
    wa = [w.astype(BF16) for w in wa]
    pad_lo = jnp.zeros((GDN_HEADS,), F32)
    pad_hi = jnp.zeros((LANES - 2 * GDN_HEADS,), F32)
    params = {
        "norm_g": norm_g,
        "w_in_a": wa,
        "conv_w_a": conv_w_a[0],
        "alog_pad": jnp.concatenate([pad_lo, a_log[0], pad_hi])[None],
        "dtb_pad": jnp.concatenate([pad_lo, dt_bias[0], pad_hi])[None],
        "o_norm_g": o_norm_g,
        "w_in_b": [w_in_b[0].astype(BF16)],
        "conv_w_b": conv_w_b[0],
        "w_out": w_out.astype(BF16),
        "final_norm_g": final_norm_g[None],
    }

    def to_cache(t):
        t = t.reshape(t.shape[0], t.shape[1], X_HEADS, X_HEAD_DIM, t.shape[3])
        return jnp.transpose(t, (0, 1, 4, 2, 3))

    def from_cache(t):
        t = jnp.transpose(t, (0, 1, 3, 4, 2))
        return t.reshape(t.shape[0], t.shape[1], X_WIDTH, t.shape[4])

    mem_kt, mem_vt = _memkv(mem_prompt, mem_norm_g[None], w_mem_kv.astype(BF16))
    mem_k_p = to_cache(mem_kt)
    mem_v_p = to_cache(mem_vt)

    s0_p = jnp.zeros((bp,) + state_gdn.shape[2:], F32)
    gc0_p = jnp.zeros((bp,) + state_gdn_conv.shape[2:], F32)
    sc0_p = jnp.zeros((bp,) + state_sconv.shape[2:], F32)
    y_p, s_p, gc_p, sc_p = _trunk(x_prompt, mem_kt, mem_vt, s0_p, gc0_p, sc0_p,
                                  params, bt=1, lt=1024, chunk=GDN_CHUNK, gdn_bt=1, gdn_lt=512,
                                  attn_bt=1, attn_lt=512, act_dtype=BF16)
    dec_seq = x_sample.shape[1]
    y_s, s_s, gc_s, sc_s = _trunk(x_sample, from_cache(cache_mem_k), from_cache(cache_mem_v),
                                  state_gdn[0], state_gdn_conv[0], state_sconv[0], params,
                                  bt=64, lt=dec_seq, chunk=dec_seq, gdn_bt=16, gdn_lt=dec_seq,
                                  attn_bt=16, attn_lt=dec_seq, act_dtype=F32)
    return (y_p, y_s, s_p, gc_p, sc_p, mem_k_p, mem_v_p, s_s, gc_s, sc_s)
```

```python
import functools

import jax
import jax.numpy as jnp
from jax import lax
from jax.experimental import pallas as pl
from jax.experimental.pallas import tpu as pltpu

F32 = jnp.float32
BF16 = jnp.bfloat16

D_MODEL = 1024
N_MEM = 256
X_WIDTH = 256
X_HEADS = 4
X_HEAD_DIM = 64
GDN_HEADS = 6
GDN_DK = 128
GDN_DV = 128
GDN_WIDTH = GDN_HEADS * GDN_DV
QK_WIDTH = GDN_HEADS * GDN_DK
QKV_WIDTH = 2 * QK_WIDTH + GDN_WIDTH
GDN_CHUNK = 64
SC_WIDTH = 768
BRANCH_WIDTH = 1024
EPS = 1e-6

LANES = 128
SUBLANES = 8
COL_CHUNK = 256
BA_PAD = LANES
VMEM_LIMIT = 56 * 1024 * 1024

HIGHEST = lax.Precision.HIGHEST


def _cparams(*sem):
    return pltpu.CompilerParams(dimension_semantics=sem, vmem_limit_bytes=VMEM_LIMIT)


def _rms_rows(x, g):
    r = lax.rsqrt(jnp.mean(x * x, axis=-1, keepdims=True) + EPS)
    return x * r * g


def _silu(x):
    h = 0.5 * x
    return h + h * jnp.tanh(h)


def _dot(a, b):
    return jnp.dot(a.astype(BF16), b.astype(BF16), preferred_element_type=F32)


def _dot_nt(a, b):
    return lax.dot_general(a.astype(BF16), b.astype(BF16), (((1,), (1,)), ((), ())),
                           preferred_element_type=F32)


def _dot_tn(a, b):
    return lax.dot_general(a.astype(BF16), b.astype(BF16), (((0,), (0,)), ((), ())),
                           preferred_element_type=F32)


def _dot_f32(a, b):
    return jnp.dot(a, b, preferred_element_type=F32, precision=HIGHEST)


def _col_chunks(n):
    return [slice(c, min(c + COL_CHUNK, n)) for c in range(0, n, COL_CHUNK)]


def _interleaved(gens):
    gens = list(gens)
    while gens:
        alive = []
        for g in gens:
            try:
                next(g)
                alive.append(g)
            except StopIteration:
                pass
        gens = alive
        yield


def _round_robin(gens):
    for _ in _interleaved(gens):
        pass


def _delayed(gen, rounds):
    for _ in range(rounds):
        yield
    yield from gen


def _memkv_kernel(m_ref, g_ref, w_ref, kt_ref, vt_ref):
    h = _rms_rows(m_ref[...], g_ref[...]).astype(BF16)
    for layer in range(w_ref.shape[0]):
        kv = jnp.dot(h, w_ref[layer], preferred_element_type=F32)
        kt_ref[layer] = kv[:, :X_WIDTH].T
        vt_ref[layer] = kv[:, X_WIDTH:].T


def _memkv(mem, g, w_bf16):
    bn, n_mem, _ = mem.shape
    depth = w_bf16.shape[0]
    out = jax.ShapeDtypeStruct((depth, bn, X_WIDTH, n_mem), F32)
    out_spec = pl.BlockSpec((depth, None, X_WIDTH, n_mem), lambda b: (0, b, 0, 0))
    return pl.pallas_call(
        _memkv_kernel,
        grid=(bn,),
        in_specs=[
            pl.BlockSpec((None, n_mem, D_MODEL), lambda b: (b, 0, 0)),
            pl.BlockSpec((1, D_MODEL), lambda b: (0, 0)),
            pl.BlockSpec(w_bf16.shape, lambda b: (0, 0, 0)),
        ],
        out_specs=[out_spec, out_spec],
        out_shape=[out, out],
        compiler_params=_cparams("arbitrary"),
        name="memkv",
    )(mem, g, w_bf16)


def _init_conv_carry(carry, hist_ref):
    @pl.when(pl.program_id(1) == 0)
    def _():
        carry[:, SUBLANES - hist_ref.shape[1]:, :] = hist_ref[...]


def _causal_conv_cols(x, cols, carry, hist_out, cw_ref, seqs=slice(None)):
    width = cw_ref.shape[0]
    bt, lt, n = x.shape
    g = lt // SUBLANES
    xe = jnp.concatenate([carry[seqs, :, cols], x], axis=1).reshape(bt, g + 1, SUBLANES, n)
    sub = lax.broadcasted_iota(jnp.int32, (1, 1, SUBLANES, n), 2)
    y = x.reshape(bt, g, SUBLANES, n) * cw_ref[width - 1:width, cols]
    for j in range(width - 1):
        s = width - 1 - j
        r = pltpu.roll(xe, s, axis=2)
        y = y + jnp.where(sub < s, r[:, :g], r[:, 1:]) * cw_ref[j:j + 1, cols]
    carry[seqs, :, cols] = x[:, lt - SUBLANES:, :]
    if hist_out is not None:
        hist_out[seqs, :, cols] = x[:, lt - (width - 1):, :]
    return y.reshape(bt, lt, n)


def _inproj_chunks(x_ref, g_ref, rest, *, mixer, bt, lt):
    tm = bt * lt
    if g_ref is None:
        h = x_ref[...].reshape(tm, D_MODEL)
    else:
        h = _rms_rows(x_ref[...].reshape(tm, D_MODEL), g_ref[...]).astype(BF16)
    if mixer == "gdn":
        n_out = (len(rest) - 1) // 2
        w_refs, (mix_out, *plain_outs, hist_out) = rest[:n_out], rest[n_out:]

        def proj(i, cols):
            return _dot_nt(h, w_refs[i][cols, :]).reshape(bt, lt, -1)
    else:
        w_ref, _, cw_ref, mix_out, *plain_outs, hist_out, carry = rest
        bases = [0, 3 * SC_WIDTH]
        for out in plain_outs[:-1]:
            bases.append(bases[-1] + out.shape[-1])

        def proj(i, cols, offset=0):
            cols = slice(bases[i] + offset + cols.start, bases[i] + offset + cols.stop)
            return jnp.dot(h, w_ref[:, cols], preferred_element_type=F32).reshape(bt, lt, -1)

    def mixer_chunk(cols):
        if mixer == "gdn":
            y = proj(0, cols)
            hist_out[:, :, cols] = y[:, lt - hist_out.shape[1]:, :]
        else:
            gate_b = proj(0, cols)
            pre = proj(0, cols, SC_WIDTH) * proj(0, cols, 2 * SC_WIDTH)
            yield
            y = gate_b * _causal_conv_cols(pre, cols, carry, hist_out, cw_ref)
        mix_out[:, :, cols] = y.astype(mix_out.dtype)
        yield

    def plain_chunk(i, out, cols):
        out[:, :, cols] = proj(i, cols).astype(out.dtype)
        yield

    gens = [mixer_chunk(cols) for cols in _col_chunks(mix_out.shape[-1])]
    for i, out in enumerate(plain_outs):
        gens += [plain_chunk(i + 1, out, cols) for cols in _col_chunks(out.shape[-1])]
    return gens


def _inproj_kernel(x_ref, g_ref, *rest, mixer, bt, lt):
    if mixer == "sconv":
        _init_conv_carry(rest[-1], rest[1])
    gens = _inproj_chunks(x_ref, g_ref, rest, mixer=mixer, bt=bt, lt=lt)
    _round_robin([_delayed(g, i) for i, g in enumerate(gens)])


def _inproj(x, g, ws, hist, cw, *, mixer, bt, lt, act_dtype):
    bn, seq, _ = x.shape
    width, cc = cw.shape

    def act(n, dtype=act_dtype):
        return (pl.BlockSpec((bt, lt, n), lambda b, l: (b, l, 0)),
                jax.ShapeDtypeStruct((bn, seq, n), dtype))

    hist_spec = pl.BlockSpec((bt, width - 1, cc), lambda b, l: (b, 0, 0))
    args = [x, g, *ws]
    in_specs = [
        pl.BlockSpec((bt, lt, D_MODEL), lambda b, l: (b, l, 0)),
        pl.BlockSpec((1, D_MODEL), lambda b, l: (0, 0)),
    ] + [pl.BlockSpec(w.shape, lambda b, l: (0, 0)) for w in ws]
    outs = [act(cc), act(BRANCH_WIDTH), act(X_WIDTH)]
    scratch = []
    if mixer == "gdn":
        outs.append(act(BA_PAD, F32))
    else:
        args += [hist, cw]
        in_specs += [hist_spec, pl.BlockSpec((width, cc), lambda b, l: (0, 0))]
        scratch.append(pltpu.VMEM((bt, SUBLANES, cc), F32))
    outs.append((hist_spec, jax.ShapeDtypeStruct((bn, width - 1, cc), F32)))
    return pl.pallas_call(
        functools.partial(_inproj_kernel, mixer=mixer, bt=bt, lt=lt),
        grid=(bn // bt, seq // lt),
        in_specs=in_specs,
        out_specs=[o[0] for o in outs],
        out_shape=[o[1] for o in outs],
        scratch_shapes=scratch,
        compiler_params=_cparams("parallel", "arbitrary"),
        name="inproj_" + mixer,
    )(*args)


CHUNKS_PER_WAVE = 4
SEQS_PER_WAVE = 8


def _gdn_tile(qkv_ref, ba_ref, cw_ref, alog_ref, dtb_ref, og_ref, o_ref, s_ref, carry,
              *, chunk, group):
    bt, lt, _ = qkv_ref.shape
    nchunks = lt // chunk
    heads = range(GDN_HEADS)
    groups = [tuple(range(g, g + group)) for g in range(0, GDN_HEADS, group)]
    gw = group * chunk
    row = lax.broadcasted_iota(jnp.int32, (chunk, gw), 0)
    col = lax.broadcasted_iota(jnp.int32, (chunk, gw), 1) % chunk
    tril = row >= col
    strict = row > col
    lane_blk = lax.broadcasted_iota(jnp.int32, (1, gw), 1) // chunk
    ones_tril = (lax.broadcasted_iota(jnp.int32, (chunk, chunk), 0)
                 >= lax.broadcasted_iota(jnp.int32, (chunk, chunk), 1)).astype(F32)
    neg_a = -jnp.exp(alog_ref[...])
    dtb = dtb_ref[...]
    og = og_ref[...]

    def rows(c):
        return slice(c * chunk, (c + 1) * chunk)

    def lanes(base, h, n):
        return slice(base + h * n, base + (h + 1) * n)

    def cat(xs, axis):
        return xs[0] if len(xs) == 1 else jnp.concatenate(xs, axis=axis)

    def pick(xs):
        out = xs[0]
        for j in range(1, len(xs)):
            out = jnp.where(lane_blk == j, xs[j], out)
        return out

    def blockdiag(xs):
        z = jnp.zeros_like(xs[0])
        return cat([cat([x if i == j else z for i in range(len(xs))], 1)
                    for j, x in enumerate(xs)], 0)

    def blockdiag_of_lane_blocks(x):
        if group == 1:
            return x
        return cat([jnp.where(lane_blk == j, x, 0.0) for j in range(group)], 0)

    def conv_silu(u, cols):
        b, c = u
        x = qkv_ref[b, rows(c), cols].astype(F32)
        y = _causal_conv_cols(x[None], cols, carry, None, cw_ref, seqs=slice(b, b + 1))
        return _silu(y[0])

    def phase1(units, res):
        uh = [(u, h) for u in units for h in heads]
        ug = [(u, g) for u in units for g in range(len(groups))]
        gcum, gcum_t, beta_all = {}, {}, {}
        for u in units:
            ba = ba_ref[u[0], rows(u[1]), :]
            beta_all[u] = 1.0 / (1.0 + jnp.exp(-ba))
            z = ba + dtb
            softplus = jnp.maximum(z, 0.0) + jnp.log1p(jnp.exp(-jnp.abs(z)))
            gcum[u] = _dot_f32(ones_tril, neg_a * softplus)
        q, k, v = {}, {}, {}
        for u in units:
            for h in heads:
                q[u, h] = conv_silu(u, lanes(0, h, GDN_DK))
                k[u, h] = conv_silu(u, lanes(QK_WIDTH, h, GDN_DK))
                v[u, h] = conv_silu(u, lanes(2 * QK_WIDTH, h, GDN_DV))
            yield
        for key in uh:
            x = q[key]
            q[key] = x * (lax.rsqrt(jnp.sum(x * x, axis=-1, keepdims=True) + EPS)
                          * (GDN_DK ** -0.5))
            x = k[key]
            k[key] = x * lax.rsqrt(jnp.sum(x * x, axis=-1, keepdims=True) + EPS)
        yield
        for u in units:
            gcum_t[u] = cat([gcum[u]] * group, 0).T
        gc = {(u, h): jnp.broadcast_to(gcum[u][:, GDN_HEADS + h:GDN_HEADS + h + 1],
                                       (chunk, GDN_DK)) for u, h in uh}
        gl = {(u, h): gcum[u][chunk - 1:chunk, GDN_HEADS + h:GDN_HEADS + h + 1]
              for u, h in uh}
        beta = {(u, h): jnp.broadcast_to(beta_all[u][:, h:h + 1], (chunk, GDN_DK))
                for u, h in uh}
        eg = {key: jnp.exp(gc[key]) for key in uh}
        kb = {key: k[key] * beta[key] for key in uh}
        kkqk = {}
        for u, g in ug:
            grp = groups[g]
            kkqk[u, g] = _dot_nt(
                cat([cat([kb[u, h] for h in grp], 1), cat([q[u, h] for h in grp], 1)], 0),
                blockdiag([k[u, h] for h in grp]))
        yield
        n, p, aqk = {}, {}, {}
        for u, g in ug:
            grp = groups[g]
            g_col = pick([gc[u, h][:, :gw] for h in grp])
            g_row = pick([gcum_t[u][GDN_HEADS + h:GDN_HEADS + h + 1, :] for h in grp])
            decay = jnp.exp(jnp.where(tril, g_col - g_row, -jnp.inf))
            a = jnp.where(strict, kkqk[u, g][:chunk] * decay, 0.0)
            aqk[u, g] = (kkqk[u, g][chunk:] * decay).astype(BF16)
            n[u, g] = -a
            p[u, g] = _dot(a, blockdiag_of_lane_blocks(a))
        yield
        span = 2
        while 2 * span < chunk:
            for key in ug:
                both = _dot(cat([n[key], p[key]], 0), blockdiag_of_lane_blocks(p[key]))
                n[key] = n[key] + p[key] + both[:chunk]
                p[key] = both[chunk:]
            span *= 2
            yield
        for key in ug:
            n[key] = n[key] + p[key] + _dot(n[key], blockdiag_of_lane_blocks(p[key]))
        yield
        wu = {}
        for u, h in uh:
            rhs = jnp.concatenate([kb[u, h] * eg[u, h], v[u, h] * beta[u, h]], axis=-1)
            wu[u, h] = rhs + _dot(n[u, h // group][:, lanes(0, h % group, chunk)], rhs)
        yield
        for u in units:
            res[u] = dict(
                w_qd=[jnp.concatenate([wu[u, h][:, :GDN_DK], q[u, h] * eg[u, h]],
                                      axis=0).astype(BF16) for h in heads],
                u=[wu[u, h][:, GDN_DK:] for h in heads],
                kd=[(k[u, h] * jnp.exp(gl[u, h] - gc[u, h])).astype(BF16) for h in heads],
                aqk=[aqk[u, g] for g in range(len(groups))],
                egl=[jnp.exp(gl[u, h]) for h in heads])
        yield

    def phase2(units, res):
        for c in sorted({c for _, c in units}):
            us = [u for u in units if u[1] == c]
            uh = [(u, h) for u in us for h in heads]
            s = {(u, h): s_ref[u[0], h] for u, h in uh}
            ws_qs = {(u, h): _dot(res[u]["w_qd"][h], s[u, h]) for u, h in uh}
            yield
            v_new = {(u, h): res[u]["u"][h] - ws_qs[u, h][:chunk] for u, h in uh}
            o_grp = {(u, g): _dot(res[u]["aqk"][g], blockdiag([v_new[u, h] for h in grp]))
                     for u in us for g, grp in enumerate(groups)}
            for u, h in uh:
                s_ref[u[0], h] = (s[u, h] * res[u]["egl"][h]
                                  + _dot_tn(res[u]["kd"][h], v_new[u, h]))
            yield
            for u, h in uh:
                oh = ws_qs[u, h][chunk:] + o_grp[u, h // group][:, lanes(0, h % group, GDN_DV)]
                oh = oh * lax.rsqrt(jnp.mean(oh * oh, axis=-1, keepdims=True) + EPS) * og
                o_ref[u[0], rows(c), lanes(0, h, GDN_DV)] = oh.astype(o_ref.dtype)
            yield

    units = [(b, c) for b in range(bt) for c in range(nchunks)]
    per_wave = CHUNKS_PER_WAVE if nchunks > 1 else SEQS_PER_WAVE
    waves = [units[i:i + per_wave] for i in range(0, len(units), per_wave)]
    res = {}
    prev = None
    for wave in waves:
        gens = [phase1(wave, res)]
        if prev is not None:
            gens.append(phase2(prev, res))
        yield from _interleaved(gens)
        prev = wave
    yield from phase2(prev, res)


def _gdn_kernel(qkv_ref, ba_ref, s0_ref, hist_ref, cw_ref, alog_ref, dtb_ref, og_ref,
                o_ref, s_ref, carry, *, chunk, group):
    @pl.when(pl.program_id(1) == 0)
    def _():
        s_ref[...] = s0_ref[...]

    _init_conv_carry(carry, hist_ref)
    _round_robin([_gdn_tile(qkv_ref, ba_ref, cw_ref, alog_ref, dtb_ref, og_ref, o_ref, s_ref,
                            carry, chunk=chunk, group=group)])


def _gdn(qkv, ba, s0, hist, cw, alog_pad, dtb_pad, og, *, chunk, bt, lt):
    group = 2 if 2 * chunk == LANES else 1
    bn, seq, _ = qkv.shape
    width, cc = cw.shape
    s_spec = pl.BlockSpec((bt, GDN_HEADS, GDN_DK, GDN_DV), lambda b, l: (b, 0, 0, 0))
    hist_spec = pl.BlockSpec((bt, width - 1, cc), lambda b, l: (b, 0, 0))
    vec = pl.BlockSpec((1, LANES), lambda b, l: (0, 0))
    return pl.pallas_call(
        functools.partial(_gdn_kernel, chunk=chunk, group=group),
        grid=(bn // bt, seq // lt),
        in_specs=[
            pl.BlockSpec((bt, lt, QKV_WIDTH), lambda b, l: (b, l, 0)),
            pl.BlockSpec((bt, lt, BA_PAD), lambda b, l: (b, l, 0)),
            s_spec, hist_spec,
            pl.BlockSpec((width, cc), lambda b, l: (0, 0)),
            vec, vec, vec,
        ],
        out_specs=[
            pl.BlockSpec((bt, lt, GDN_WIDTH), lambda b, l: (b, l, 0)),
            s_spec,
        ],
        out_shape=[
            jax.ShapeDtypeStruct((bn, seq, GDN_WIDTH), qkv.dtype),
            jax.ShapeDtypeStruct(s0.shape, F32),
        ],
        scratch_shapes=[pltpu.VMEM((bt, SUBLANES, cc), F32)],
        compiler_params=_cparams("parallel", "arbitrary"),
        name="gdn",
    )(qkv, ba, s0, hist, cw, alog_pad, dtb_pad, og)


PROJ_ROUNDS = ((0, 5), (14, 5), (28, 1), (30, 1), (32, 1), (34, 1), (36, 1))


def _gdn_layer_kernel(x_ref, g_ref, w_qkv, w_gate, w_xq, w_ba, s0_ref, hist_ref, cw_ref,
                      alog_ref, dtb_ref, og_ref, gate_out, xq_out, hist_out, o_ref, s_ref,
                      qkv_scr, ba_scr, carry, *, lt, tiles_per_seq, chunk, group):
    i = pl.program_id(0)
    slot = i % 2
    prev = jnp.maximum(i - 1, 0)

    @pl.when(i == 0)
    def _():
        qkv_scr[1] = jnp.zeros(qkv_scr.shape[1:], F32)
        ba_scr[1] = jnp.zeros(ba_scr.shape[1:], F32)

    @pl.when(prev % tiles_per_seq == 0)
    def _():
        s_ref[...] = s0_ref[...]
        carry[:, SUBLANES - hist_ref.shape[1]:, :] = hist_ref[...]

    proj = _inproj_chunks(
        x_ref, g_ref, (w_qkv, w_gate, w_xq, w_ba, qkv_scr.at[slot], gate_out, xq_out,
                       ba_scr.at[slot], hist_out), mixer="gdn", bt=1, lt=lt)
    rule = _gdn_tile(qkv_scr.at[1 - slot], ba_scr.at[1 - slot], cw_ref, alog_ref, dtb_ref,
                     og_ref, o_ref, s_ref, carry, chunk=chunk, group=group)
    starts = [r for w, n in PROJ_ROUNDS for r in range(w, w + n)]
    _round_robin([rule] + [_delayed(p, starts[k]) for k, p in enumerate(proj)])


def _gdn_layer(x, g, ws, s0, hist, cw, alog_pad, dtb_pad, og, *, chunk, lt, act_dtype):
    group = 2 if 2 * chunk == LANES else 1
    bn, seq, _ = x.shape
    nl = seq // lt
    tiles = bn * nl
    width, cc = cw.shape

    def cur(i):
        return jnp.minimum(i, tiles - 1)

    def prev(i):
        return jnp.maximum(i - 1, 0)

    def tile(n, which):
        return pl.BlockSpec((1, lt, n), lambda i: (which(i), 0, 0))

    def per_seq(shape, which):
        return pl.BlockSpec((1,) + shape, lambda i: (which(i) // nl,) + (0,) * len(shape))

    def whole(a):
        return pl.BlockSpec(a.shape, lambda i: (0,) * a.ndim)

    state = (GDN_HEADS, GDN_DK, GDN_DV)
    gate, xq, hist_new, tok, s_new = pl.pallas_call(
        functools.partial(_gdn_layer_kernel, lt=lt, tiles_per_seq=nl, chunk=chunk, group=group),
        grid=(tiles + 1,),
        in_specs=[tile(D_MODEL, cur), whole(g)] + [whole(w) for w in ws] + [
            per_seq(state, prev), per_seq((width - 1, cc), prev), whole(cw),
            whole(alog_pad), whole(dtb_pad), whole(og)],
        out_specs=[
            tile(BRANCH_WIDTH, cur), tile(X_WIDTH, cur), per_seq((width - 1, cc), cur),
            tile(GDN_WIDTH, prev), per_seq(state, prev)],
        out_shape=[
            jax.ShapeDtypeStruct((tiles, lt, BRANCH_WIDTH), act_dtype),
            jax.ShapeDtypeStruct((tiles, lt, X_WIDTH), act_dtype),
            jax.ShapeDtypeStruct(hist.shape, F32),
            jax.ShapeDtypeStruct((tiles, lt, GDN_WIDTH), act_dtype),
            jax.ShapeDtypeStruct(s0.shape, F32)],
        scratch_shapes=[
            pltpu.VMEM((2, 1, lt, cc), F32),
            pltpu.VMEM((2, 1, lt, BA_PAD), F32),
            pltpu.VMEM((1, SUBLANES, cc), F32)],
        compiler_params=_cparams("arbitrary"),
        name="gdn_layer",
    )(x.reshape(tiles, lt, D_MODEL), g, *ws, s0, hist, cw, alog_pad, dtb_pad, og)
    return (gate.reshape(bn, seq, -1), xq.reshape(bn, seq, -1), tok.reshape(bn, seq, -1),
            s_new, hist_new)


ATTN_ROWS = 256
ATTN_SEQS = 16


def _attn_out_tile(tok_ref, xq_ref, gate_ref, x_ref, kt_ref, vt_ref, w_ref, fg_ref,
                   y_ref, xo_scr, *, bt, lt, final, next_norm=None):
    lane_head = lax.broadcasted_iota(jnp.int32, (1, X_WIDTH), 1) // X_HEAD_DIM
    scale = X_HEAD_DIM ** -0.5
    tm = bt * lt

    def attend(b, rs):
        r = rs.stop - rs.start
        q = xq_ref[b, rs, :]
        q = q * jnp.asarray(scale, q.dtype)
        qx = jnp.concatenate([jnp.where(lane_head == h, q, jnp.zeros_like(q))
                              for h in range(X_HEADS)], axis=0)
        s = _dot(qx, kt_ref[b])
        yield
        e = jnp.exp(s - jnp.max(s, axis=-1, keepdims=True))
        p = e / jnp.sum(e, axis=-1, keepdims=True)
        o4 = _dot_nt(p, vt_ref[b])
        yield
        xo = jnp.where(lane_head == 0, o4[0:r], 0.0)
        for h in range(1, X_HEADS):
            xo = xo + jnp.where(lane_head == h, o4[h * r:(h + 1) * r], 0.0)
        xo_scr[b, rs, :] = xo
        yield

    acc = {}

    def tok_proj():
        sg = _silu(gate_ref[:, :, :GDN_WIDTH].astype(F32).reshape(tm, GDN_WIDTH))
        br = (tok_ref[...].astype(F32).reshape(tm, GDN_WIDTH) * sg).astype(BF16)
        for cols in _col_chunks(D_MODEL):
            acc[cols.start] = (x_ref[:, :, cols].reshape(tm, -1)
                               + _dot(br, w_ref[:GDN_WIDTH, cols]))
            yield

    if bt == 1:
        blocks = [slice(r, r + ATTN_ROWS) for r in range(0, lt, ATTN_ROWS)]
        yield from _interleaved([attend(0, rs) for rs in blocks] + [tok_proj()])
    else:
        def seq_group(i, carry):
            _round_robin([attend(i * ATTN_SEQS + j, slice(0, lt)) for j in range(ATTN_SEQS)])
            return carry

        lax.fori_loop(0, bt // ATTN_SEQS, seq_group, 0)
        yield from tok_proj()

    sg_x = _silu(gate_ref[:, :, GDN_WIDTH:].astype(F32).reshape(tm, X_WIDTH))
    br_x = (xo_scr[...].reshape(tm, X_WIDTH) * sg_x).astype(BF16)
    y = [acc[cols.start] + _dot(br_x, w_ref[GDN_WIDTH:, cols]) for cols in _col_chunks(D_MODEL)]
    if final or next_norm is not None:
        ms = sum(jnp.sum(c * c, axis=-1, keepdims=True) for c in y) * (1.0 / D_MODEL)
        r = lax.rsqrt(ms + EPS)
    if next_norm is not None:
        h_ref, g_ref = next_norm
        for c, cols in zip(y, _col_chunks(D_MODEL)):
            h_ref[:, :, cols] = (c * r * g_ref[:, cols]).astype(h_ref.dtype).reshape(bt, lt, -1)
    if final:
        y = [c * r * fg_ref[:, cols] for c, cols in zip(y, _col_chunks(D_MODEL))]
    for c, cols in zip(y, _col_chunks(D_MODEL)):
        y_ref[:, :, cols] = c.reshape(bt, lt, -1)
    yield


def _attn_out_kernel(tok_ref, xq_ref, gate_ref, x_ref, kt_ref, vt_ref, w_ref, fg_ref,
                     y_ref, xo_scr, *, bt, lt, final):
    _round_robin([_attn_out_tile(tok_ref, xq_ref, gate_ref, x_ref, kt_ref, vt_ref, w_ref,
                                 fg_ref, y_ref, xo_scr, bt=bt, lt=lt, final=final)])


def _attn_out(tok, xq, gate, x, mk, mv, w_bf16, fg, *, layer, bt, lt, final):
    bn, seq, _ = x.shape

    def act(n):
        return pl.BlockSpec((bt, lt, n), lambda b, l: (b, l, 0))

    mem = pl.BlockSpec((None, bt, X_WIDTH, N_MEM), lambda b, l: (layer, b, 0, 0))
    return pl.pallas_call(
        functools.partial(_attn_out_kernel, bt=bt, lt=lt, final=final),
        grid=(bn // bt, seq // lt),
        in_specs=[
            act(GDN_WIDTH), act(X_WIDTH), act(BRANCH_WIDTH), act(D_MODEL), mem, mem,
            pl.BlockSpec((BRANCH_WIDTH, D_MODEL), lambda b, l: (0, 0)),
            pl.BlockSpec((1, D_MODEL), lambda b, l: (0, 0)),
        ],
        out_specs=act(D_MODEL),
        out_shape=jax.ShapeDtypeStruct(x.shape, F32),
        scratch_shapes=[pltpu.VMEM((bt, lt, X_WIDTH), F32)],
        compiler_params=_cparams("parallel", "arbitrary"),
        name="attn_out",
    )(tok, xq, gate, x, mk, mv, w_bf16, fg)


SCONV_PROJ_ROUNDS = (0, 0, 1, 1, 2, 3, 4, 4)
TAIL_ATTN1_START = 3


def _tail_kernel(tok_ref, xq_ref, gate_ref, x_ref, kt0_ref, vt0_ref, kt1_ref, vt1_ref,
                 w_out0_ref, w_out1_ref, fg_ref, g_ref, w_in_ref, hist_ref, cw_ref,
                 y_ref, hist_out,
                 xo0_scr, xo1_scr, h_scr, x1_scr, tok1_scr, gate1_scr, xq1_scr, carry,
                 *, lt, tiles_per_seq):
    i = pl.program_id(0)
    a2, b2, c2 = i % 2, (i + 1) % 2, i % 2
    a3, c3 = i % 3, (i + 1) % 3
    prev = jnp.maximum(i - 1, 0)

    @pl.when(i == 0)
    def _():
        h_scr[1] = jnp.zeros(h_scr.shape[1:], BF16)
        x1_scr[1] = jnp.zeros(x1_scr.shape[1:], F32)
        x1_scr[2] = jnp.zeros(x1_scr.shape[1:], F32)
        tok1_scr[0] = jnp.zeros(tok1_scr.shape[1:], BF16)
        gate1_scr[0] = jnp.zeros(gate1_scr.shape[1:], BF16)
        xq1_scr[0] = jnp.zeros(xq1_scr.shape[1:], BF16)

    @pl.when(prev % tiles_per_seq == 0)
    def _():
        carry[:, SUBLANES - hist_ref.shape[1]:, :] = hist_ref[...]

    attn0 = _attn_out_tile(tok_ref, xq_ref, gate_ref, x_ref, kt0_ref, vt0_ref, w_out0_ref,
                           fg_ref, x1_scr.at[a3], xo0_scr, bt=1, lt=lt, final=False,
                           next_norm=(h_scr.at[a2], g_ref))
    proj = _inproj_chunks(h_scr.at[1 - a2], None,
                          (w_in_ref, hist_ref, cw_ref, tok1_scr.at[b2], gate1_scr.at[b2],
                           xq1_scr.at[b2], hist_out, carry), mixer="sconv", bt=1, lt=lt)
    attn1 = _attn_out_tile(tok1_scr.at[c2], xq1_scr.at[c2], gate1_scr.at[c2], x1_scr.at[c3],
                           kt1_ref, vt1_ref, w_out1_ref, fg_ref, y_ref, xo1_scr,
                           bt=1, lt=lt, final=True)
    _round_robin([attn0, _delayed(attn1, TAIL_ATTN1_START)]
                 + [_delayed(p, SCONV_PROJ_ROUNDS[k]) for k, p in enumerate(proj)])


def _tail(tok, xq, gate, x, mk, mv, w_out_bf16, fg, g, w_in, hist, cw, *, lt):
    bn, seq, _ = x.shape
    nl = seq // lt
    tiles = bn * nl
    width, cc = cw.shape

    def clamp(i, lag):
        return jnp.clip(i - lag, 0, tiles - 1)

    def tile(n, lag):
        return pl.BlockSpec((1, lt, n), lambda i: (clamp(i, lag), 0, 0))

    def whole(a):
        return pl.BlockSpec(a.shape, lambda i: (0,) * a.ndim, pipeline_mode=pl.Buffered(1))

    def mem(layer, lag):
        return pl.BlockSpec((None, 1, X_WIDTH, N_MEM),
                            lambda i: (layer, clamp(i, lag) // nl, 0, 0))

    def tiled(a):
        return a.reshape(tiles, lt, a.shape[-1])

    def slots(n, width_, dtype):
        return pltpu.VMEM((n, 1, lt, width_), dtype)

    hist_spec = pl.BlockSpec((1, width - 1, cc), lambda i: (clamp(i, 1) // nl, 0, 0))
    y, hist_new = pl.pallas_call(
        functools.partial(_tail_kernel, lt=lt, tiles_per_seq=nl),
        grid=(tiles + 2,),
        in_specs=[
            tile(GDN_WIDTH, 0), tile(X_WIDTH, 0), tile(BRANCH_WIDTH, 0), tile(D_MODEL, 0),
            mem(0, 0), mem(0, 0), mem(1, 2), mem(1, 2),
            whole(w_out_bf16[0]), whole(w_out_bf16[1]), whole(fg), whole(g), whole(w_in),
            hist_spec, whole(cw)],
        out_specs=[tile(D_MODEL, 2), hist_spec],
        out_shape=[
            jax.ShapeDtypeStruct((tiles, lt, D_MODEL), F32),
            jax.ShapeDtypeStruct(hist.shape, F32)],
        scratch_shapes=[
            pltpu.VMEM((1, lt, X_WIDTH), F32), pltpu.VMEM((1, lt, X_WIDTH), F32),
            slots(2, D_MODEL, BF16), slots(3, D_MODEL, F32),
            slots(2, cc, BF16), slots(2, BRANCH_WIDTH, BF16), slots(2, X_WIDTH, BF16),
            pltpu.VMEM((1, SUBLANES, cc), F32)],
        compiler_params=_cparams("arbitrary"),
        name="tail",
    )(tiled(tok), tiled(xq), tiled(gate), tiled(x), mk, mv, mk, mv,
      w_out_bf16[0], w_out_bf16[1], fg, g, w_in, hist, cw)
    return y.reshape(x.shape), hist_new


def _trunk(x, mem_k, mem_v, gdn_s, gdn_conv, sc_conv, p, *, bt, lt, chunk, gdn_bt, gdn_lt,
           attn_bt, attn_lt, act_dtype):
    if gdn_bt == 1 and bt == 1:
        gate, xq, tok, s_new, gconv_new = _gdn_layer(
            x, p["norm_g"][0:1], p["w_in_a"], gdn_s, gdn_conv, p["conv_w_a"], p["alog_pad"],
            p["dtb_pad"], p["o_norm_g"], chunk=chunk, lt=gdn_lt, act_dtype=act_dtype)
    else:
        qkv, gate, xq, ba, gconv_new = _inproj(
            x, p["norm_g"][0:1], p["w_in_a"], None, p["conv_w_a"], mixer="gdn", bt=bt, lt=lt,
            act_dtype=act_dtype)
        tok, s_new = _gdn(qkv, ba, gdn_s, gdn_conv, p["conv_w_a"], p["alog_pad"],
                          p["dtb_pad"], p["o_norm_g"], chunk=chunk, bt=gdn_bt, lt=gdn_lt)
    if attn_bt == 1 and bt == 1:
        y, sconv_new = _tail(tok, xq, gate, x, mem_k, mem_v, p["w_out"], p["final_norm_g"],
                             p["norm_g"][1:2], p["w_in_b"][0], sc_conv, p["conv_w_b"],
                             lt=attn_lt)
    else:
        x = _attn_out(tok, xq, gate, x, mem_k, mem_v, p["w_out"][0], p["final_norm_g"],
                      layer=0, bt=attn_bt, lt=attn_lt, final=False)
        tok, gate, xq, sconv_new = _inproj(
            x, p["norm_g"][1:2], p["w_in_b"], sc_conv, p["conv_w_b"], mixer="sconv", bt=bt,
            lt=lt, act_dtype=act_dtype)
        y = _attn_out(tok, xq, gate, x, mem_k, mem_v, p["w_out"][1], p["final_norm_g"],
                      layer=1, bt=attn_bt, lt=attn_lt, final=True)
    return y, s_new[None], gconv_new[None], sconv_new[None]


def kernel(x_prompt, x_sample, mem_prompt, state_gdn, state_gdn_conv, state_sconv, cache_mem_k, cache_mem_v, norm_g, w_in_a, conv_w_a, a_log, dt_bias, o_norm_g, w_in_b, conv_w_b, mem_norm_g, w_mem_kv, w_out, final_norm_g):
    bp = x_prompt.shape[0]

    wa = jnp.transpose(w_in_a[0])
    c_b = QKV_WIDTH
    c_g = c_b + 2 * GDN_HEADS
    c_x = c_g + BRANCH_WIDTH
    wa = [wa[:c_b], wa[c_g:c_x], wa[c_x:],
          jnp.concatenate([wa[c_b:c_g], jnp.zeros((BA_PAD - 2 * GDN_HEADS, D_MODEL), wa.dtype)])]
    wa = [w.astype(BF16) for w in wa]
    pad_lo = jnp.zeros((GDN_HEADS,), F32)
    pad_hi = jnp.zeros((LANES - 2 * GDN_HEADS,), F32)
    params = {
        "norm_g": norm_g,
        "w_in_a": wa,
        "conv_w_a": conv_w_a[0],
        "alog_pad": jnp.concatenate([pad_lo, a_log[0], pad_hi])[None],
        "dtb_pad": jnp.concatenate([pad_lo, dt_bias[0], pad_hi])[None],
        "o_norm_g": o_norm_g,
        "w_in_b": [w_in_b[0].astype(BF16)],
        "conv_w_b": conv_w_b[0],
        "w_out": w_out.astype(BF16),
        "final_norm_g": final_norm_g[None],
    }

    def to_cache(t):
        t = t.reshape(t.shape[0], t.shape[1], X_HEADS, X_HEAD_DIM, t.shape[3])
        return jnp.transpose(t, (0, 1, 4, 2, 3))

    def from_cache(t):
        t = jnp.transpose(t, (0, 1, 3, 4, 2))
        return t.reshape(t.shape[0], t.shape[1], X_WIDTH, t.shape[4])

    mem_kt, mem_vt = _memkv(mem_prompt, mem_norm_g[None], w_mem_kv.astype(BF16))
    mem_k_p = to_cache(mem_kt)
    mem_v_p = to_cache(mem_vt)

    s0_p = jnp.zeros((bp,) + state_gdn.shape[2:], F32)
    gc0_p = jnp.zeros((bp,) + state_gdn_conv.shape[2:], F32)
    sc0_p = jnp.zeros((bp,) + state_sconv.shape[2:], F32)
    y_p, s_p, gc_p, sc_p = _trunk(x_prompt, mem_kt, mem_vt, s0_p, gc0_p, sc0_p,
                                  params, bt=1, lt=1024, chunk=GDN_CHUNK, gdn_bt=1, gdn_lt=512,
                                  attn_bt=1, attn_lt=512, act_dtype=BF16)
    dec_seq = x_sample.shape[1]
    y_s, s_s, gc_s, sc_s = _trunk(x_sample, from_cache(cache_mem_k), from_cache(cache_mem_v),
                                  state_gdn[0], state_gdn_conv[0], state_sconv[0], params,
                                  bt=32, lt=dec_seq, chunk=dec_seq, gdn_bt=16, gdn_lt=dec_seq,
                                  attn_bt=16, attn_lt=dec_seq, act_dtype=F32)
    return (y_p, y_s, s_p, gc_p, sc_p, mem_k_p, mem_v_p, s_s, gc_s, sc_s)
```
